```python
import jax
import jax.numpy as jnp
from jax import lax
import numpy as np

D_MODEL = 2048
BATCH = 4
SEQ = 2048
DEPTH = 1

GRID_W = 64
CTX_LEN = 256
HGRN_HEADS = 8
HGRN_HEAD_DIM = 128
HGRN_WIDTH = HGRN_HEADS * HGRN_HEAD_DIM
HGRN_CHUNK = 64
NA_HEADS = 8
NA_HEAD_DIM = 128
NA_WIDTH = NA_HEADS * NA_HEAD_DIM
WIN_R = 8
WIN_C = 16
ROPE_THETA = 10000.0
FFN_HIDDEN = 5632
CONV_W = 3
N_MOD = 6
EPS = 1e-6
IN_COLS = 5 * HGRN_WIDTH + 3 * NA_WIDTH + 2 * D_MODEL

kernel_name = 'hybrid_hgrn2_natten_convffn_dit'


def rmsnorm(x, g):
    xf = x.astype(jnp.float32)
    y = xf * lax.rsqrt(jnp.mean(xf * xf, axis=-1, keepdims=True) + EPS)
    return (y * g.astype(jnp.float32)).astype(x.dtype)


def modulate(h, shift, scale):
    return h * (1.0 + scale) + shift


def split_heads(t, n_heads):
    return t.reshape(t.shape[:-1] + (n_heads, t.shape[-1] // n_heads))


def split_columns(p):
    sizes = (HGRN_WIDTH,) * 5 + (NA_WIDTH,) * 3 + (D_MODEL, D_MODEL)
    outs, off = [], 0
    for s in sizes:
        outs.append(p[..., off:off + s])
        off += s
    return outs


def hgrn2_chunk_scan(q, logf, k, v, s0):
    bsz, L, nh, d = q.shape
    nc = L // HGRN_CHUNK

    def to_chunks(t):
        return t.reshape(bsz, nc, HGRN_CHUNK, nh, d).transpose(1, 0, 3, 2, 4)

    causal = jnp.tril(jnp.ones((HGRN_CHUNK, HGRN_CHUNK), dtype=bool))[:, :, None]

    def step(S, inp):
        qc, gc, kc, vc = inp
        cum = jnp.cumsum(gc, axis=2)
        o_inter = jnp.einsum('bhtk,bhkv->bhtv', qc * jnp.exp(cum), S)
        diff = cum[:, :, :, None, :] - cum[:, :, None, :, :]
        decay = jnp.where(causal, jnp.exp(jnp.where(causal, diff, 0.0)), 0.0)
        scores = jnp.einsum('bhtk,bhsk,bhtsk->bhts', qc, kc, decay)
        o_intra = jnp.einsum('bhts,bhsv->bhtv', scores, vc)
        last = cum[:, :, -1:, :]
        S_new = jnp.exp(last[:, :, 0, :])[..., None] * S + jnp.einsum('bhsk,bhsv->bhkv', kc * jnp.exp(last - cum), vc)
        return S_new, o_inter + o_intra

    s_fin, o = lax.scan(step, s0, (to_chunks(q), to_chunks(logf), to_chunks(k), to_chunks(v)))
    return o.transpose(1, 0, 3, 2, 4).reshape(bsz, L, nh, d), s_fin


def hgrn2_prep(q, f_logit, i_val, lb):
    f = lb + (1.0 - lb) * jax.nn.sigmoid(f_logit.astype(jnp.float32))
    return (split_heads(q.astype(jnp.float32), HGRN_HEADS),
            split_heads(jnp.log(f), HGRN_HEADS),
            split_heads(1.0 - f, HGRN_HEADS),
            split_heads(i_val.astype(jnp.float32), HGRN_HEADS))


def hgrn2_direction(ctx_in, lat_in, lb, reverse):
    ctx_t = hgrn2_prep(ctx_in[0], ctx_in[1], ctx_in[2], lb)
    lat_t = hgrn2_prep(lat_in[0], lat_in[1], lat_in[2], lb)
    if reverse:
        ctx_t = tuple(jnp.flip(t, axis=1) for t in ctx_t)
        lat_t = tuple(jnp.flip(t, axis=1) for t in lat_t)
    bsz = lat_t[0].shape[0]
    s0 = jnp.zeros((bsz, HGRN_HEADS, HGRN_HEAD_DIM, HGRN_HEAD_DIM), jnp.float32)
    o_ctx, s_ctx = hgrn2_chunk_scan(ctx_t[0], ctx_t[1], ctx_t[2], ctx_t[3], s0)
    o_lat, _ = hgrn2_chunk_scan(lat_t[0], lat_t[1], lat_t[2], lat_t[3], s_ctx)
    if reverse:
        o_ctx = jnp.flip(o_ctx, axis=1)
        o_lat = jnp.flip(o_lat, axis=1)
    return o_lat, o_ctx


def hgrn2_readout(o, g, norm_g, dtype):
    on = o * lax.rsqrt(jnp.mean(o * o, axis=-1, keepdims=True) + EPS) * norm_g.astype(jnp.float32)
    y = on.reshape(o.shape[:2] + (HGRN_WIDTH,)) * jax.nn.silu(g.astype(jnp.float32))
    return y.astype(dtype)


def qk_norm(t, g):
    tf = t.astype(jnp.float32)
    return tf * lax.rsqrt(jnp.mean(tf * tf, axis=-1, keepdims=True) + EPS) * g.astype(jnp.float32)


def axial_rope(t):
    L, d = t.shape[1], t.shape[-1]
    pos = jnp.arange(L, dtype=jnp.int32)
    row = (pos // GRID_W).astype(jnp.float32)
    col = (pos % GRID_W).astype(jnp.float32)
    half = d // 2
    nf = half // 2
    inv = ROPE_THETA ** (-jnp.arange(nf, dtype=jnp.float32) / nf)

    def rot(u, p):
        ang = p[:, None] * inv[None, :]
        cos = jnp.cos(ang)[None, :, None, :]
        sin = jnp.sin(ang)[None, :, None, :]
        u1, u2 = u[..., :nf], u[..., nf:]
        return jnp.concatenate([u1 * cos - u2 * sin, u1 * sin + u2 * cos], axis=-1)

    return jnp.concatenate([rot(t[..., :half], row), rot(t[..., half:], col)], axis=-1)


def neighbourhood_attention(q, k, v, k_ctx, v_ctx, rel_bias):
    bsz, L, nh, d = q.shape
    rows = L // GRID_W
    kr = min(WIN_R, rows)
    scale = d ** -0.5
    r = jnp.arange(rows)
    w = jnp.arange(GRID_W)
    row_start = jnp.clip(r - WIN_R // 2, 0, rows - kr)
    row_idx = row_start[:, None] + jnp.arange(kr)[None, :]
    col_start = jnp.clip(w - WIN_C // 2, 0, GRID_W - WIN_C)
    col_in = (w[None, :] >= col_start[:, None]) & (w[None, :] < col_start[:, None] + WIN_C)
    qg = q.reshape(bsz, rows, GRID_W, nh, d)
    kg = k.reshape(bsz, rows, GRID_W, nh, d)[:, row_idx]
    vg = v.reshape(bsz, rows, GRID_W, nh, d)[:, row_idx]
    s_band = jnp.einsum('brchd,brjwhd->bhrcjw', qg, kg).astype(jnp.float32) * scale
    dr = row_idx - r[:, None]
    dc = jnp.clip(w[None, :] - w[:, None], -(WIN_C - 1), WIN_C - 1)
    bias = rel_bias[:, (dr + WIN_R - 1)[:, None, :, None], (dc + WIN_C - 1)[None, :, None, :]]
    s_band = jnp.where(col_in[:, None, :], s_band + bias.astype(jnp.float32)[None], -jnp.inf)
    s_ctx = jnp.einsum('brchd,bnhd->bhrcn', qg, k_ctx).astype(jnp.float32) * scale
    n_band = kr * GRID_W
    s = jnp.concatenate([s_band.reshape(bsz, nh, rows, GRID_W, n_band), s_ctx], axis=-1)
    p = jax.nn.softmax(s, axis=-1)
    p_band = p[..., :n_band].reshape(bsz, nh, rows, GRID_W, kr, GRID_W)
    p_ctx = p[..., n_band:]
    o = jnp.einsum('bhrcjw,brjwhd->brchd', p_band, vg) + jnp.einsum('bhrcn,bnhd->brchd', p_ctx, v_ctx)
    return o.reshape(bsz, L, nh * d)


def context_attention(q, k, v):
    bsz, n, nh, d = q.shape
    s = jnp.einsum('bnhd,bmhd->bhnm', q, k).astype(jnp.float32) * (d ** -0.5)
    p = jax.nn.softmax(s, axis=-1)
    return jnp.einsum('bhnm,bmhd->bnhd', p, v).reshape(bsz, n, nh * d)


def branch_merge(y_a, y_b, gate_a, gate_b, w_a, w_b, w_o):
    z = jax.nn.sigmoid(gate_a) * (y_a @ w_a) + jax.nn.sigmoid(gate_b) * (y_b @ w_b)
    return z @ w_o


def dwconv_centred(u, w, b):
    L = u.shape[1]
    up = jnp.pad(u, ((0, 0), (1, 1), (0, 0)))
    return up[:, 0:L] * w[0] + up[:, 1:L + 1] * w[1] + up[:, 2:L + 2] * w[2] + b


def conv_ffn(h, w1, w3, cw, cb, w2):
    u = dwconv_centred(h @ w1, cw, cb)
    return (jax.nn.silu(u) * (h @ w3)) @ w2


def setup_inputs(seed: int = 0) -> dict:
    key = jax.random.key(seed)
    ks = jax.random.split(key, 24)

    def nrm(k, shape, scale):
        return jax.random.normal(k, shape, jnp.float32) * scale

    return {
        'x': nrm(ks[0], (BATCH, SEQ, D_MODEL), 1.0),
        'c': nrm(ks[1], (BATCH, D_MODEL), 1.0),
        'ctx': nrm(ks[2], (BATCH, CTX_LEN, D_MODEL), 1.0),
        'c_ctx': nrm(ks[3], (D_MODEL,), 1.0),
        'ada_w': nrm(ks[4], (DEPTH, D_MODEL, N_MOD * D_MODEL), D_MODEL ** -0.5),
        'ada_b': nrm(ks[5], (DEPTH, N_MOD * D_MODEL), 0.01),
        'norm1_g': 1.0 + nrm(ks[6], (DEPTH, D_MODEL), 0.02),
        'norm2_g': 1.0 + nrm(ks[7], (DEPTH, D_MODEL), 0.02),
        'w_in': nrm(ks[8], (DEPTH, D_MODEL, IN_COLS), D_MODEL ** -0.5),
        'hgrn_lb_logits': nrm(ks[9], (2, DEPTH + 1, HGRN_WIDTH), 0.5),
        'hgrn_norm_g': 1.0 + nrm(ks[10], (DEPTH, HGRN_HEAD_DIM), 0.02),
        'na_q_norm_g': 1.0 + nrm(ks[11], (DEPTH, NA_HEAD_DIM), 0.02),
        'na_k_norm_g': 1.0 + nrm(ks[12], (DEPTH, NA_HEAD_DIM), 0.02),
        'na_rel_bias': nrm(ks[13], (DEPTH, NA_HEADS, 2 * WIN_R - 1, 2 * WIN_C - 1), 0.1),
        'w_branch_a': nrm(ks[14], (DEPTH, HGRN_WIDTH, D_MODEL), HGRN_WIDTH ** -0.5),
        'w_branch_b': nrm(ks[15], (DEPTH, NA_WIDTH, D_MODEL), NA_WIDTH ** -0.5),
        'w_out': nrm(ks[16], (DEPTH, D_MODEL, D_MODEL), D_MODEL ** -0.5),
        'ffn_w1': nrm(ks[17], (DEPTH, D_MODEL, FFN_HIDDEN), D_MODEL ** -0.5),
        'ffn_w3': nrm(ks[18], (DEPTH, D_MODEL, FFN_HIDDEN), D_MODEL ** -0.5),
        'ffn_conv_w': nrm(ks[19], (DEPTH, CONV_W, FFN_HIDDEN), CONV_W ** -0.5),
        'ffn_conv_b': nrm(ks[20], (DEPTH, FFN_HIDDEN), 0.01),
        'ffn_w2': nrm(ks[21], (DEPTH, FFN_HIDDEN, D_MODEL), FFN_HIDDEN ** -0.5),
    }


def reference(x, c, ctx, c_ctx, ada_w, ada_b, norm1_g, norm2_g, w_in, hgrn_lb_logits, hgrn_norm_g,
              na_q_norm_g, na_k_norm_g, na_rel_bias, w_branch_a, w_branch_b, w_out,
              ffn_w1, ffn_w3, ffn_conv_w, ffn_conv_b, ffn_w2):
    lower_bounds = jnp.cumsum(jax.nn.softmax(hgrn_lb_logits.astype(jnp.float32), axis=1), axis=1)
    xc = ctx
    for l in range(DEPTH):
        last_layer = l == DEPTH - 1
        mod_l = jax.nn.silu(c) @ ada_w[l] + ada_b[l]
        mod_c = jax.nn.silu(c_ctx) @ ada_w[l] + ada_b[l]
        sh1, sc1, g1, sh2, sc2, g2 = [m[:, None, :] for m in jnp.split(mod_l, N_MOD, axis=-1)]
        sh1c, sc1c, g1c, sh2c, sc2c, g2c = jnp.split(mod_c, N_MOD, axis=-1)

        h = modulate(rmsnorm(x, norm1_g[l]), sh1, sc1)
        hc = modulate(rmsnorm(xc, norm1_g[l]), sh1c, sc1c)
        qa, fwa, fba, ia, ga, qn, kn, vn, gta, gtb = split_columns(h @ w_in[l])
        qa_c, fwa_c, fba_c, ia_c, ga_c, qn_c, kn_c, vn_c, gta_c, gtb_c = split_columns(hc @ w_in[l])

        o_lf, o_cf = hgrn2_direction((qa_c, fwa_c, ia_c), (qa, fwa, ia), lower_bounds[0, l], False)
        o_lb, o_cb = hgrn2_direction((qa_c, fba_c, ia_c), (qa, fba, ia), lower_bounds[1, l], True)
        y_a = hgrn2_readout(o_lf + o_lb, ga, hgrn_norm_g[l], x.dtype)

        q_n = axial_rope(qk_norm(split_heads(qn, NA_HEADS), na_q_norm_g[l]))
        k_n = axial_rope(qk_norm(split_heads(kn, NA_HEADS), na_k_norm_g[l]))
        v_n = split_heads(vn, NA_HEADS)
        k_c = qk_norm(split_heads(kn_c, NA_HEADS), na_k_norm_g[l])
        v_c = split_heads(vn_c, NA_HEADS)
        y_b = neighbourhood_attention(q_n, k_n, v_n, k_c, v_c, na_rel_bias[l]).astype(x.dtype)

        x_mid = x + g1 * branch_merge(y_a, y_b, gta, gtb, w_branch_a[l], w_branch_b[l], w_out[l])

        h2 = modulate(rmsnorm(x_mid, norm2_g[l]), sh2, sc2)
        x_new = x_mid + g2 * conv_ffn(h2, ffn_w1[l], ffn_w3[l], ffn_conv_w[l], ffn_conv_b[l], ffn_w2[l])

        if not last_layer:
            y_a_c = hgrn2_readout(o_cf + o_cb, ga_c, hgrn_norm_g[l], x.dtype)
            q_c = qk_norm(split_heads(qn_c, NA_HEADS), na_q_norm_g[l])
            y_b_c = context_attention(q_c, k_c, v_c).astype(x.dtype)
            xc_mid = xc + g1c * branch_merge(y_a_c, y_b_c, gta_c, gtb_c, w_branch_a[l], w_branch_b[l], w_out[l])
            h2c = modulate(rmsnorm(xc_mid, norm2_g[l]), sh2c, sc2c)
            xc = xc_mid + g2c * conv_ffn(h2c, ffn_w1[l], ffn_w3[l], ffn_conv_w[l], ffn_conv_b[l], ffn_w2[l])
        x = x_new
    return x
```

```python
import functools

import numpy as np
import jax
import jax.numpy as jnp
from jax import lax
from jax.experimental import pallas as pl
from jax.experimental.pallas import tpu as pltpu

GRID_W = 64
HEADS = 8
HEAD_DIM = 128
MIX_W = HEADS * HEAD_DIM
CHUNK = 64
N_LEVELS = 6
WIN_R = 8
WIN_C = 16
ROPE_THETA = 10000.0
N_MOD = 6
EPS = 1e-6
TOK = 256
MASKED = -1e30
V7X_VMEM_LIMIT = 56 * 1024 * 1024

BF16 = jnp.bfloat16
F32 = jnp.float32


def _dot(a, b):
    return jnp.dot(a, b, preferred_element_type=F32)


def _dot_nt(a, b):
    return lax.dot_general(a, b, (((1,), (1,)), ((), ())), preferred_element_type=F32)


def _sigmoid(t):
    return 1.0 / (1.0 + jnp.exp(-t))


def _params(sem):
    return pltpu.CompilerParams(dimension_semantics=sem, vmem_limit_bytes=V7X_VMEM_LIMIT)


def _mod_kernel(c_ref, w_ref, b_ref, o_ref):
    cv = c_ref[...]
    s = (cv * _sigmoid(cv)).astype(BF16)
    o_ref[...] = _dot(s, w_ref[...].astype(BF16)) + b_ref[...]


def _mod_call(c_all, w, b):
    d, n = w.shape
    tn = min(1024, d)
    return pl.pallas_call(
        _mod_kernel,
        grid=(n // tn,),
        in_specs=[pl.BlockSpec((8, d), lambda j: (0, 0)),
                  pl.BlockSpec((d, tn), lambda j: (0, j)),
                  pl.BlockSpec((1, tn), lambda j: (0, j))],
        out_specs=pl.BlockSpec((8, tn), lambda j: (0, j)),
        out_shape=jax.ShapeDtypeStruct((8, n), F32),
        compiler_params=_params(("arbitrary",)),
        name="mod",
    )(c_all, w, b)


def _prenorm_kernel(ctx_ref, x_ref, g_ref, ss_ref, o_ref, *, n_ctx_blk):
    t = pl.program_id(1)
    xv = jnp.where(t < n_ctx_blk, ctx_ref[...], x_ref[...])
    y = xv * lax.rsqrt(jnp.mean(xv * xv, axis=-1, keepdims=True) + EPS) * g_ref[...]
    o_ref[...] = (y * (1.0 + ss_ref[1:2, :]) + ss_ref[0:1, :]).astype(BF16)


def _prenorm_call(ctx, x, g, ss):
    b, l, d = x.shape
    ncb = ctx.shape[1] // TOK
    nt = ncb + l // TOK
    return pl.pallas_call(
        functools.partial(_prenorm_kernel, n_ctx_blk=ncb),
        grid=(b, nt),
        in_specs=[pl.BlockSpec((None, TOK, d), lambda i, t: (i, jnp.minimum(t, ncb - 1), 0)),
                  pl.BlockSpec((None, TOK, d), lambda i, t: (i, jnp.maximum(t - ncb, 0), 0)),
                  pl.BlockSpec((1, d), lambda i, t: (0, 0)),
                  pl.BlockSpec((None, None, 2, d), lambda i, t: (i, (t >= ncb).astype(jnp.int32), 0, 0))],
        out_specs=pl.BlockSpec((None, TOK, d), lambda i, t: (i, t, 0)),
        out_shape=jax.ShapeDtypeStruct((b, nt * TOK, d), BF16),
        compiler_params=_params(("parallel", "arbitrary")),
        name="prenorm",
    )(ctx, x, g, ss)


def _matmul_kernel(h_ref, w_ref, o_ref):
    o_ref[...] = _dot(h_ref[...], w_ref[...])


def _inproj_call(h, w):
    m, d = h.shape
    n = w.shape[1]
    tm = 1024 if m % 1024 == 0 else TOK
    tn = min(1024, d)
    return pl.pallas_call(
        _matmul_kernel,
        grid=(n // tn, m // tm),
        in_specs=[pl.BlockSpec((tm, d), lambda j, i: (i, 0)),
                  pl.BlockSpec((d, tn), lambda j, i: (0, j))],
        out_specs=pl.BlockSpec((tm, tn), lambda j, i: (i, j)),
        out_shape=jax.ShapeDtypeStruct((m, n), F32),
        compiler_params=_params(("parallel", "arbitrary")),
        name="inproj",
    )(h, w)


def _hgrn_constants():
    u = np.arange(CHUNK)
    mats, masks = [], []
    for rev in (False, True):
        pos = CHUNK - 1 - u if rev else u
        incl = (pos[None, :] <= pos[:, None]).astype(np.float32)
        blocks, lvl_masks = [incl], []
        for lev in range(N_LEVELS):
            half = CHUNK >> (lev + 1)
            ref = (pos // (2 * half)) * (2 * half) + half - 1
            upto_ref = (pos[None, :] <= ref[:, None]).astype(np.float32)
            blocks.append(incl - upto_ref)
            same = (pos[:, None] // (2 * half)) == (pos[None, :] // (2 * half))
            late_q = (pos[:, None] % (2 * half)) >= half
            early_k = (pos[None, :] % (2 * half)) < half
            lvl_masks.append((same & late_q & early_k).astype(np.float32))
        ones = np.ones((CHUNK, CHUNK), np.float32)
        blocks += [ones - incl, ones]
        lvl_masks.append(np.eye(CHUNK, dtype=np.float32))
        mats.append(np.concatenate(blocks, axis=0))
        masks.append(np.stack(lvl_masks))
    return np.stack(mats), np.stack(masks)


def _hgrn_kernel(lbl_ref, q_ref, f_ref, i_ref, gt_ref, cm_ref, mk_ref, ng_ref, y_ref, of_ref, st_ref,
                 *, n_ctx_chunks, n_lat_chunks):
    d = pl.program_id(2)
    la, lb_ = lbl_ref[0], lbl_ref[1]
    mx = jnp.maximum(la, lb_)
    ea, eb = jnp.exp(la - mx), jnp.exp(lb_ - mx)
    lower = ea / (ea + eb)
    st_ref[...] = jnp.zeros(st_ref.shape, F32)

    def chunk(pos, with_out):
        r0 = pl.multiple_of(pos * CHUNK, CHUNK)
        rows = pl.ds(r0, CHUNK)
        f = lower + (1.0 - lower) * _sigmoid(f_ref[rows, :])
        g = jnp.log(f)
        kk = 1.0 - f
        g1 = g.astype(BF16)
        r1 = g - g1.astype(F32)
        g2 = r1.astype(BF16)
        g3 = (r1 - g2.astype(F32)).astype(BF16)
        cm = cm_ref[...]
        sums = _dot(cm, g1) + _dot(cm, g2) + _dot(cm, g3)
        cum = sums[0:CHUNK]
        rest = sums[7 * CHUNK:8 * CHUNK]
        tot = sums[8 * CHUNK:8 * CHUNK + 1]
        v = i_ref[rows, :]
        st = st_ref[...]
        kdec = (kk * jnp.exp(rest)).astype(BF16)
        st_ref[...] = st * jnp.exp(tot) + _dot(v.T.astype(BF16), kdec)
        if not with_out:
            return
        q = q_ref[rows, :]
        vb = v.astype(BF16)
        o = _dot_nt((q * jnp.exp(cum)).astype(BF16), st.astype(BF16))
        sc = mk_ref[N_LEVELS] * _dot_nt(q.astype(BF16), kk.astype(BF16))
        for lev in range(N_LEVELS):
            dl = jnp.exp(-jnp.abs(sums[(1 + lev) * CHUNK:(2 + lev) * CHUNK]))
            sc = sc + mk_ref[lev] * _dot_nt((q * dl).astype(BF16), (kk * dl).astype(BF16))
        o = o + _dot(sc.astype(BF16), vb)
        orow = pl.ds(pl.multiple_of((pos - n_ctx_chunks) * CHUNK, CHUNK), CHUNK)

        @pl.when(d == 0)
        def _():
            of_ref[orow, :] = o

        @pl.when(d == 1)
        def _():
            ot = of_ref[orow, :] + o
            on = ot * lax.rsqrt(jnp.mean(ot * ot, axis=-1, keepdims=True) + EPS) * ng_ref[...]
            gate = gt_ref[rows, :]
            y_ref[orow, :] = (on * (gate * _sigmoid(gate))).astype(BF16)

    def ctx_step(ci, carry):
        chunk(ci + d * (n_ctx_chunks - 1 - 2 * ci), False)
        return carry

    def lat_step(ci, carry):
        chunk(n_ctx_chunks + ci + d * (n_lat_chunks - 1 - 2 * ci), True)
        return carry

    lax.fori_loop(0, n_ctx_chunks, ctx_step, 0)
    lax.fori_loop(0, n_lat_chunks, lat_step, 0)


def _hgrn_call(p3, lb_logits, norm_g, n_ctx, l):
    b, t, _ = p3.shape
    cm, mk = _hgrn_constants()
    cm = jnp.asarray(cm, BF16)
    mk = jnp.asarray(mk, F32)
    lbl = lb_logits.reshape(2, 2, HEADS, 1, HEAD_DIM)
    col = lambda grp: (lambda i, h, d: (i, 0, grp * HEADS + h))
    tok_spec = lambda im: pl.BlockSpec((None, t, HEAD_DIM), im)
    return pl.pallas_call(
        functools.partial(_hgrn_kernel, n_ctx_chunks=n_ctx // CHUNK, n_lat_chunks=l // CHUNK),
        grid=(b, HEADS, 2),
        in_specs=[pl.BlockSpec((None, 2, None, 1, HEAD_DIM), lambda i, h, d: (d, 0, h, 0, 0)),
                  tok_spec(col(0)),
                  tok_spec(lambda i, h, d: (i, 0, (1 + d) * HEADS + h)),
                  tok_spec(col(3)),
                  tok_spec(col(4)),
                  pl.BlockSpec((None, 9 * CHUNK, CHUNK), lambda i, h, d: (d, 0, 0)),
                  pl.BlockSpec((None, N_LEVELS + 1, CHUNK, CHUNK), lambda i, h, d: (d, 0, 0, 0)),
                  pl.BlockSpec((1, HEAD_DIM), lambda i, h, d: (0, 0))],
        out_specs=pl.BlockSpec((None, l, HEAD_DIM), lambda i, h, d: (i, 0, h)),
        out_shape=jax.ShapeDtypeStruct((b, l, MIX_W), BF16),
        scratch_shapes=[pltpu.VMEM((l, HEAD_DIM), F32), pltpu.VMEM((HEAD_DIM, HEAD_DIM), F32)],
        compiler_params=_params(("parallel", "parallel", "arbitrary")),
        name="hgrn",
    )(lbl, p3, p3, p3, p3, cm, mk, norm_g)


def _rope_tables(l):
    pos = np.arange(l)
    nf = HEAD_DIM // 4
    inv = ROPE_THETA ** (-np.arange(nf, dtype=np.float64) / nf)
    ang_r = (pos // GRID_W)[:, None] * inv[None, :]
    ang_c = (pos % GRID_W)[:, None] * inv[None, :]
    cos = np.concatenate([np.cos(ang_r)] * 2 + [np.cos(ang_c)] * 2, axis=-1)
    sin = np.concatenate([-np.sin(ang_r), np.sin(ang_r), -np.sin(ang_c), np.sin(ang_c)], axis=-1)
    return jnp.asarray(cos, F32), jnp.asarray(sin, F32)


def _na_bias_table(rel_bias):
    cq = np.arange(GRID_W)[:, None]
    ck = np.arange(GRID_W)[None, :]
    cs = np.clip(cq - WIN_C // 2, 0, GRID_W - WIN_C)
    col_in = (ck >= cs) & (ck < cs + WIN_C)
    dc = np.clip(ck - cq, -(WIN_C - 1), WIN_C - 1) + WIN_C - 1
    dr = np.arange(WIN_R)[None, :] - np.arange(WIN_R)[:, None] + WIN_R - 1
    tbl = rel_bias.astype(F32)[:, dr[:, None, :, None], dc[None, :, None, :]]
    tbl = jnp.where(col_in[None, None, :, None, :], tbl, MASKED)
    return tbl.reshape(rel_bias.shape[0], WIN_R, GRID_W, WIN_R * GRID_W)


def _na_kernel(q_ref, k_ref, v_ref, cos_ref, sin_ref, tb_ref, gq_ref, gk_ref, o_ref,
               qs, ks, vs, kcs, vcs, *, n_ctx, grid_rows):
    lane = lax.broadcasted_iota(jnp.int32, (1, HEAD_DIM), 1)
    first_half = (lane % (HEAD_DIM // 2)) < (HEAD_DIM // 4)

    def normed(tv, g):
        return tv * lax.rsqrt(jnp.mean(tv * tv, axis=-1, keepdims=True) + EPS) * g

    def rope(tv, cs, sn):
        partner = jnp.where(first_half, pltpu.roll(tv, 3 * HEAD_DIM // 4, 1), pltpu.roll(tv, HEAD_DIM // 4, 1))
        return tv * cs + partner * sn

    gq = gq_ref[...]
    gk = gk_ref[...]
    kcs[...] = normed(k_ref[0:n_ctx, :], gk).astype(BF16)
    vcs[...] = v_ref[0:n_ctx, :].astype(BF16)

    def prep(i, carry):
        r0 = pl.multiple_of(i * TOK, TOK)
        rows = pl.ds(r0, TOK)
        src = pl.ds(n_ctx + r0, TOK)
        cs, sn = cos_ref[rows, :], sin_ref[rows, :]
        qs[rows, :] = rope(normed(q_ref[src, :], gq), cs, sn).astype(BF16)
        ks[rows, :] = rope(normed(k_ref[src, :], gk), cs, sn).astype(BF16)
        vs[rows, :] = v_ref[src, :].astype(BF16)
        return carry

    lax.fori_loop(0, grid_rows * GRID_W // TOK, prep, 0)
    scale = HEAD_DIM ** -0.5
    band = WIN_R * GRID_W

    def row(r, carry):
        rs = jnp.clip(r - WIN_R // 2, 0, grid_rows - WIN_R)
        qrow = pl.ds(pl.multiple_of(r * GRID_W, GRID_W), GRID_W)
        krow = pl.ds(pl.multiple_of(rs * GRID_W, GRID_W), band)
        qr = qs[qrow, :]
        sb = _dot_nt(qr, ks[krow, :]) * scale + tb_ref[r - rs]
        sc = _dot_nt(qr, kcs[...]) * scale
        m = jnp.maximum(jnp.max(sb, axis=-1, keepdims=True), jnp.max(sc, axis=-1, keepdims=True))
        pb = jnp.exp(sb - m)
        pc = jnp.exp(sc - m)
        den = jnp.sum(pb, axis=-1, keepdims=True) + jnp.sum(pc, axis=-1, keepdims=True)
        o = _dot(pb.astype(BF16), vs[krow, :]) + _dot(pc.astype(BF16), vcs[...])
        o_ref[qrow, :] = (o / den).astype(BF16)
        return carry

    lax.fori_loop(0, grid_rows, row, 0)


def _na_call(p3, rel_bias, gq, gk, n_ctx, l):
    b, t, _ = p3.shape
    cos, sin = _rope_tables(l)
    tb = _na_bias_table(rel_bias)
    col = lambda grp: (lambda i, h: (i, 0, grp * HEADS + h))
    tok_spec = lambda im: pl.BlockSpec((None, t, HEAD_DIM), im)
    full = lambda shape: pl.BlockSpec(shape, lambda i, h: (0,) * len(shape))
    return pl.pallas_call(
        functools.partial(_na_kernel, n_ctx=n_ctx, grid_rows=l // GRID_W),
        grid=(b, HEADS),
        in_specs=[tok_spec(col(5)), tok_spec(col(6)), tok_spec(col(7)),
                  full((l, HEAD_DIM)), full((l, HEAD_DIM)),
                  pl.BlockSpec((None, WIN_R, GRID_W, WIN_R * GRID_W), lambda i, h: (h, 0, 0, 0)),
                  full((1, HEAD_DIM)), full((1, HEAD_DIM))],
        out_specs=pl.BlockSpec((None, l, HEAD_DIM), lambda i, h: (i, 0, h)),
        out_shape=jax.ShapeDtypeStruct((b, l, MIX_W), BF16),
        scratch_shapes=[pltpu.VMEM((l, HEAD_DIM), BF16), pltpu.VMEM((l, HEAD_DIM), BF16),
                        pltpu.VMEM((l, HEAD_DIM), BF16), pltpu.VMEM((n_ctx, HEAD_DIM), BF16),
                        pltpu.VMEM((n_ctx, HEAD_DIM), BF16)],
        compiler_params=_params(("parallel", "parallel")),
        name="na",
    )(p3, p3, p3, cos, sin, tb, gq, gk)


def _merge_kernel(ya_ref, yb_ref, ga_ref, gb_ref, x_ref, wa_ref, wb_ref, wo_ref, mod_ref, n2_ref,
                  xm_ref, h2_ref):
    za = _dot(ya_ref[...], wa_ref[...])
    zb = _dot(yb_ref[...], wb_ref[...])
    z = _sigmoid(ga_ref[...]) * za + _sigmoid(gb_ref[...]) * zb
    xm = x_ref[...] + mod_ref[0:1, :] * _dot(z.astype(BF16), wo_ref[...])
    xm_ref[...] = xm
    y = xm * lax.rsqrt(jnp.mean(xm * xm, axis=-1, keepdims=True) + EPS) * n2_ref[...]
    h2_ref[...] = (y * (1.0 + mod_ref[2:3, :]) + mod_ref[1:2, :]).astype(BF16)


def _merge_call(ya, yb, p3, x, wa, wb, wo, mod3, n2, n_ctx):
    b, l, d = x.shape
    ncb = n_ctx // TOK
    gate_blk = MIX_W * 8 // d
    full = lambda shape: pl.BlockSpec(shape, lambda i, t: (0,) * len(shape))
    return pl.pallas_call(
        _merge_kernel,
        grid=(b, l // TOK),
        in_specs=[pl.BlockSpec((None, TOK, MIX_W), lambda i, t: (i, t, 0)),
                  pl.BlockSpec((None, TOK, MIX_W), lambda i, t: (i, t, 0)),
                  pl.BlockSpec((None, TOK, d), lambda i, t: (i, t + ncb, gate_blk)),
                  pl.BlockSpec((None, TOK, d), lambda i, t: (i, t + ncb, gate_blk + 1)),
                  pl.BlockSpec((None, TOK, d), lambda i, t: (i, t, 0)),
                  full((MIX_W, d)), full((MIX_W, d)), full((d, d)),
                  pl.BlockSpec((None, 3, d), lambda i, t: (i, 0, 0)),
                  full((1, d))],
        out_specs=[pl.BlockSpec((None, TOK, d), lambda i, t: (i, t, 0)),
                   pl.BlockSpec((None, TOK, d), lambda i, t: (i, t, 0))],
        out_shape=[jax.ShapeDtypeStruct((b, l, d), F32), jax.ShapeDtypeStruct((b, l, d), BF16)],
        compiler_params=_params(("parallel", "arbitrary")),
        name="merge",
    )(ya, yb, p3, p3, x, wa, wb, wo, mod3, n2)


def _ffn_kernel(h_ref, hp_ref, hn_ref, w1_ref, w3_ref, cw_ref, cb_ref, w2_ref, xm_ref, g2_ref, o_ref,
                *, tiles_per_seq, halo):
    i = pl.program_id(0)
    k = pl.program_id(1)
    tm = h_ref.shape[0]
    w1 = w1_ref[...]
    t1 = _dot(h_ref[...], w1)
    prev_row = _dot(hp_ref[...], w1)[halo - 1:halo, :]
    next_row = _dot(hn_ref[...], w1)[0:1, :]
    prev_row = jnp.where(i % tiles_per_seq == 0, 0.0, prev_row)
    next_row = jnp.where(i % tiles_per_seq == tiles_per_seq - 1, 0.0, next_row)
    ridx = lax.broadcasted_iota(jnp.int32, (tm, 1), 0)
    up = jnp.where(ridx == 0, prev_row, pltpu.roll(t1, 1, 0))
    dn = jnp.where(ridx == tm - 1, next_row, pltpu.roll(t1, tm - 1, 0))
    u = up * cw_ref[0:1, :] + t1 * cw_ref[1:2, :] + dn * cw_ref[2:3, :] + cb_ref[...]
    a = (u * _sigmoid(u)) * _dot(h_ref[...], w3_ref[...])
    part = _dot(a.astype(BF16), w2_ref[...])

    @pl.when(k == 0)
    def _():
        o_ref[...] = part

    @pl.when(k > 0)
    def _():
        o_ref[...] += part

    @pl.when(k == pl.num_programs(1) - 1)
    def _():
        o_ref[...] = xm_ref[...] + g2_ref[...] * o_ref[...]


def _ffn_call(h2, xm, w1, w3, cw, cb, w2, g2, l):
    m, d = h2.shape
    hid = w1.shape[1]
    tm = 512
    th = min(512, hid)
    halo = 16
    per = tm // halo
    nblk = m // halo
    return pl.pallas_call(
        functools.partial(_ffn_kernel, tiles_per_seq=l // tm, halo=halo),
        grid=(m // tm, hid // th),
        in_specs=[pl.BlockSpec((tm, d), lambda i, k: (i, 0)),
                  pl.BlockSpec((halo, d), lambda i, k: (jnp.maximum(i * per - 1, 0), 0)),
                  pl.BlockSpec((halo, d), lambda i, k: (jnp.minimum((i + 1) * per, nblk - 1), 0)),
                  pl.BlockSpec((d, th), lambda i, k: (0, k)),
                  pl.BlockSpec((d, th), lambda i, k: (0, k)),
                  pl.BlockSpec((3, th), lambda i, k: (0, k)),
                  pl.BlockSpec((1, th), lambda i, k: (0, k)),
                  pl.BlockSpec((th, d), lambda i, k: (k, 0)),
                  pl.BlockSpec((tm, d), lambda i, k: (i, 0)),
                  pl.BlockSpec((None, 1, d), lambda i, k: (i // (l // tm), 0, 0))],
        out_specs=pl.BlockSpec((tm, d), lambda i, k: (i, 0)),
        out_shape=jax.ShapeDtypeStruct((m, d), F32),
        compiler_params=_params(("parallel", "arbitrary")),
        name="ffn",
    )(h2, h2, h2, w1, w3, cw, cb, w2, xm, g2)


def kernel(x, c, ctx, c_ctx, ada_w, ada_b, norm1_g, norm2_g, w_in, hgrn_lb_logits, hgrn_norm_g,
           na_q_norm_g, na_k_norm_g, na_rel_bias, w_branch_a, w_branch_b, w_out,
           ffn_w1, ffn_w3, ffn_conv_w, ffn_conv_b, ffn_w2):
    b, l, d = x.shape
    n_ctx = ctx.shape[1]
    assert ada_w.shape[0] == 1 and b + 1 <= 8 and n_ctx % TOK == 0 and l % (2 * TOK) == 0

    c_all = jnp.zeros((8, d), F32).at[:b].set(c).at[b].set(c_ctx)
    mod = _mod_call(c_all, ada_w[0], ada_b).reshape(8, N_MOD, d)
    sh1, sc1, g1, sh2, sc2, g2 = (mod[:b, n] for n in range(N_MOD))
    ss_ctx = jnp.broadcast_to(mod[b, 0:2][None], (b, 2, d))
    ss1 = jnp.stack([ss_ctx, jnp.stack([sh1, sc1], axis=1)], axis=1)

    h = _prenorm_call(ctx, x, norm1_g, ss1)
    t = n_ctx + l
    p = _inproj_call(h.reshape(b * t, d), w_in[0].astype(BF16))
    p3 = p.reshape(b, t, p.shape[1])

    y_a = _hgrn_call(p3, hgrn_lb_logits, hgrn_norm_g, n_ctx, l)
    y_b = _na_call(p3, na_rel_bias[0], na_q_norm_g, na_k_norm_g, n_ctx, l)

    mod3 = jnp.stack([g1, sh2, sc2], axis=1)
    x_mid, h2 = _merge_call(y_a, y_b, p3, x, w_branch_a[0].astype(BF16), w_branch_b[0].astype(BF16),
                            w_out[0].astype(BF16), mod3, norm2_g, n_ctx)

    out = _ffn_call(h2.reshape(b * l, d), x_mid.reshape(b * l, d), ffn_w1[0].astype(BF16),
                    ffn_w3[0].astype(BF16), ffn_conv_w[0], ffn_conv_b, ffn_w2[0].astype(BF16),
                    g2[:, None, :], l)
    return out.reshape(b, l, d)
```

```python
import functools

import numpy as np
import jax
import jax.numpy as jnp
from jax import lax
from jax.experimental import pallas as pl
from jax.experimental.pallas import tpu as pltpu

GRID_W = 64
HEADS = 8
HEAD_DIM = 128
MIX_W = HEADS * HEAD_DIM
CHUNK = 64
N_LEVELS = 6
WIN_R = 8
WIN_C = 16
ROPE_THETA = 10000.0
N_MOD = 6
EPS = 1e-6
TOK = 256
MASKED = -1e30
V7X_VMEM_LIMIT = 56 * 1024 * 1024

BF16 = jnp.bfloat16
F32 = jnp.float32


def _dot(a, b):
    return jnp.dot(a, b, preferred_element_type=F32)


def _dot_nt(a, b):
    return lax.dot_general(a, b, (((1,), (1,)), ((), ())), preferred_element_type=F32)


def _sigmoid(t):
    return 1.0 / (1.0 + jnp.exp(-t))


def _params(sem):
    return pltpu.CompilerParams(dimension_semantics=sem, vmem_limit_bytes=V7X_VMEM_LIMIT)


def _mod_kernel(c_ref, w_ref, b_ref, o_ref):
    cv = c_ref[...]
    s = (cv * _sigmoid(cv)).astype(BF16)
    o_ref[...] = _dot(s, w_ref[...].astype(BF16)) + b_ref[...]


def _mod_call(c_all, w, b):
    d, n = w.shape
    tn = min(1024, d)
    return pl.pallas_call(
        _mod_kernel,
        grid=(n // tn,),
        in_specs=[pl.BlockSpec((8, d), lambda j: (0, 0)),
                  pl.BlockSpec((d, tn), lambda j: (0, j)),
                  pl.BlockSpec((1, tn), lambda j: (0, j))],
        out_specs=pl.BlockSpec((8, tn), lambda j: (0, j)),
        out_shape=jax.ShapeDtypeStruct((8, n), F32),
        compiler_params=_params(("arbitrary",)),
        name="mod",
    )(c_all, w, b)


def _prenorm_kernel(ctx_ref, x_ref, g_ref, ss_ref, o_ref, *, n_ctx_blk):
    t = pl.program_id(1)
    xv = jnp.where(t < n_ctx_blk, ctx_ref[...], x_ref[...])
    y = xv * lax.rsqrt(jnp.mean(xv * xv, axis=-1, keepdims=True) + EPS) * g_ref[...]
    o_ref[...] = (y * (1.0 + ss_ref[1:2, :]) + ss_ref[0:1, :]).astype(BF16)


def _prenorm_call(ctx, x, g, ss):
    b, l, d = x.shape
    ncb = ctx.shape[1] // TOK
    nt = ncb + l // TOK
    return pl.pallas_call(
        functools.partial(_prenorm_kernel, n_ctx_blk=ncb),
        grid=(b, nt),
        in_specs=[pl.BlockSpec((None, TOK, d), lambda i, t: (i, jnp.minimum(t, ncb - 1), 0)),
                  pl.BlockSpec((None, TOK, d), lambda i, t: (i, jnp.maximum(t - ncb, 0), 0)),
                  pl.BlockSpec((1, d), lambda i, t: (0, 0)),
                  pl.BlockSpec((None, None, 2, d), lambda i, t: (i, (t >= ncb).astype(jnp.int32), 0, 0))],
        out_specs=pl.BlockSpec((None, TOK, d), lambda i, t: (i, t, 0)),
        out_shape=jax.ShapeDtypeStruct((b, nt * TOK, d), BF16),
        compiler_params=_params(("parallel", "arbitrary")),
        name="prenorm",
    )(ctx, x, g, ss)


def _matmul_kernel(h_ref, w_ref, o_ref):
    o_ref[...] = _dot(h_ref[...], w_ref[...])


def _inproj_call(h, w):
    m, d = h.shape
    n = w.shape[1]
    tm = 1024 if m % 1024 == 0 else TOK
    tn = min(1024, d)
    return pl.pallas_call(
        _matmul_kernel,
        grid=(n // tn, m // tm),
        in_specs=[pl.BlockSpec((tm, d), lambda j, i: (i, 0)),
                  pl.BlockSpec((d, tn), lambda j, i: (0, j))],
        out_specs=pl.BlockSpec((tm, tn), lambda j, i: (i, j)),
        out_shape=jax.ShapeDtypeStruct((m, n), F32),
        compiler_params=_params(("parallel", "arbitrary")),
        name="inproj",
    )(h, w)


def _hgrn_constants():
    u = np.arange(CHUNK)
    mats, masks = [], []
    for rev in (False, True):
        pos = CHUNK - 1 - u if rev else u
        incl = (pos[None, :] <= pos[:, None]).astype(np.float32)
        blocks, lvl_masks = [incl], []
        for lev in range(N_LEVELS):
            half = CHUNK >> (lev + 1)
            ref = (pos // (2 * half)) * (2 * half) + half - 1
            upto_ref = (pos[None, :] <= ref[:, None]).astype(np.float32)
            blocks.append(incl - upto_ref)
            same = (pos[:, None] // (2 * half)) == (pos[None, :] // (2 * half))
            late_q = (pos[:, None] % (2 * half)) >= half
            early_k = (pos[None, :] % (2 * half)) < half
            lvl_masks.append((same & late_q & early_k).astype(np.float32))
        ones = np.ones((CHUNK, CHUNK), np.float32)
        blocks += [ones - incl, ones]
        lvl_masks.append(np.eye(CHUNK, dtype=np.float32))
        mats.append(np.concatenate(blocks, axis=0))
        masks.append(np.stack(lvl_masks))
    return np.stack(mats), np.stack(masks)


def _hgrn_kernel(lbl_ref, q_ref, ff_ref, fb_ref, i_ref, gt_ref, cm_ref, mk_ref, ng_ref, y_ref,
                 of_ref, ob_ref, sf_ref, sb_ref, *, n_ctx_chunks, n_lat_chunks):
    f_refs, o_refs, st_refs = (ff_ref, fb_ref), (of_ref, ob_ref), (sf_ref, sb_ref)
    lowers = []
    for dirn in range(2):
        la, lb_ = lbl_ref[dirn, 0], lbl_ref[dirn, 1]
        mx = jnp.maximum(la, lb_)
        ea, eb = jnp.exp(la - mx), jnp.exp(lb_ - mx)
        lowers.append(ea / (ea + eb))
        st_refs[dirn][...] = jnp.zeros(st_refs[dirn].shape, F32)

    def chunk(dirn, pos, with_out):
        rows = pl.ds(pl.multiple_of(pos * CHUNK, CHUNK), CHUNK)
        lower, st_ref = lowers[dirn], st_refs[dirn]
        f = lower + (1.0 - lower) * _sigmoid(f_refs[dirn][rows, :])
        g = jnp.log(f)
        kk = 1.0 - f
        g1 = g.astype(BF16)
        r1 = g - g1.astype(F32)
        g2 = r1.astype(BF16)
        g3 = (r1 - g2.astype(F32)).astype(BF16)
        sums = _dot(cm_ref[dirn], jnp.concatenate([g1, g2, g3], axis=0))
        cum = sums[0:CHUNK]
        rest = sums[7 * CHUNK:8 * CHUNK]
        tot = sums[8 * CHUNK:8 * CHUNK + 1]
        v = i_ref[rows, :]
        st = st_ref[...]
        kdec = (kk * jnp.exp(rest)).astype(BF16)
        st_ref[...] = st * jnp.exp(tot) + _dot(v.T.astype(BF16), kdec)
        if not with_out:
            return
        q = q_ref[rows, :]
        o = _dot_nt((q * jnp.exp(cum)).astype(BF16), st.astype(BF16))
        sc = mk_ref[dirn, N_LEVELS] * _dot_nt(q.astype(BF16), kk.astype(BF16))
        for lev in range(N_LEVELS):
            dl = jnp.exp(-jnp.abs(sums[(1 + lev) * CHUNK:(2 + lev) * CHUNK]))
            sc = sc + mk_ref[dirn, lev] * _dot_nt((q * dl).astype(BF16), (kk * dl).astype(BF16))
        o = o + _dot(sc.astype(BF16), v.astype(BF16))
        orow = pl.ds(pl.multiple_of((pos - n_ctx_chunks) * CHUNK, CHUNK), CHUNK)
        o_refs[dirn][orow, :] = o

    def ctx_step(ci, carry):
        chunk(0, ci, False)
        chunk(1, n_ctx_chunks - 1 - ci, False)
        return carry

    def lat_step(ci, carry):
        chunk(0, n_ctx_chunks + ci, True)
        chunk(1, n_ctx_chunks + n_lat_chunks - 1 - ci, True)
        return carry

    lax.fori_loop(0, n_ctx_chunks, ctx_step, 0)
    lax.fori_loop(0, n_lat_chunks, lat_step, 0, unroll=2)

    def readout(i, carry):
        rows = pl.ds(pl.multiple_of(i * TOK, TOK), TOK)
        ot = of_ref[rows, :] + ob_ref[rows, :]
        on = ot * lax.rsqrt(jnp.mean(ot * ot, axis=-1, keepdims=True) + EPS) * ng_ref[...]
        gate = gt_ref[pl.ds(pl.multiple_of(n_ctx_chunks * CHUNK + i * TOK, CHUNK), TOK), :]
        y_ref[rows, :] = (on * (gate * _sigmoid(gate))).astype(BF16)
        return carry

    lax.fori_loop(0, n_lat_chunks * CHUNK // TOK, readout, 0)


def _hgrn_call(p3, lb_logits, norm_g, n_ctx, l):
    b, t, _ = p3.shape
    cm, mk = _hgrn_constants()
    cm = jnp.asarray(np.concatenate([cm] * 3, axis=-1), BF16)
    mk = jnp.asarray(mk, F32)
    lbl = lb_logits.reshape(2, 2, HEADS, 1, HEAD_DIM)
    tok_spec = lambda grp: pl.BlockSpec((None, t, HEAD_DIM), lambda i, h: (i, 0, grp * HEADS + h))
    full = lambda shape: pl.BlockSpec(shape, lambda i, h: (0,) * len(shape))
    return pl.pallas_call(
        functools.partial(_hgrn_kernel, n_ctx_chunks=n_ctx // CHUNK, n_lat_chunks=l // CHUNK),
        grid=(b, HEADS),
        in_specs=[pl.BlockSpec((2, 2, None, 1, HEAD_DIM), lambda i, h: (0, 0, h, 0, 0)),
                  tok_spec(0), tok_spec(1), tok_spec(2), tok_spec(3), tok_spec(4),
                  full((2, 9 * CHUNK, 3 * CHUNK)),
                  full((2, N_LEVELS + 1, CHUNK, CHUNK)),
                  full((1, HEAD_DIM))],
        out_specs=pl.BlockSpec((None, l, HEAD_DIM), lambda i, h: (i, 0, h)),
        out_shape=jax.ShapeDtypeStruct((b, l, MIX_W), BF16),
        scratch_shapes=[pltpu.VMEM((l, HEAD_DIM), F32), pltpu.VMEM((l, HEAD_DIM), F32),
                        pltpu.VMEM((HEAD_DIM, HEAD_DIM), F32), pltpu.VMEM((HEAD_DIM, HEAD_DIM), F32)],
        compiler_params=_params(("parallel", "parallel")),
        name="hgrn",
    )(lbl, p3, p3, p3, p3, p3, cm, mk, norm_g)


def _rope_tables(l):
    pos = np.arange(l)
    nf = HEAD_DIM // 4
    inv = ROPE_THETA ** (-np.arange(nf, dtype=np.float64) / nf)
    ang_r = (pos // GRID_W)[:, None] * inv[None, :]
    ang_c = (pos % GRID_W)[:, None] * inv[None, :]
    cos = np.concatenate([np.cos(ang_r)] * 2 + [np.cos(ang_c)] * 2, axis=-1)
    sin = np.concatenate([-np.sin(ang_r), np.sin(ang_r), -np.sin(ang_c), np.sin(ang_c)], axis=-1)
    return jnp.asarray(cos, F32), jnp.asarray(sin, F32)


def _na_build_bias(rb_ref, tb_ref):
    lanes = 2 * GRID_W
    cq = lax.broadcasted_iota(jnp.int32, (GRID_W, lanes), 0)
    lane = lax.broadcasted_iota(jnp.int32, (GRID_W, lanes), 1)
    ck = lane % GRID_W
    cs = jnp.clip(cq - WIN_C // 2, 0, GRID_W - WIN_C)
    col_in = (ck >= cs) & (ck < cs + WIN_C)

    def toeplitz(d, lane0):
        row = jnp.broadcast_to(rb_ref[d:d + 1, :], (GRID_W, lanes))
        return pltpu.roll(row, (lane0 - (WIN_C - 1)) % lanes, 1, stride=1, stride_axis=0)

    pair = [jnp.where(col_in, jnp.where(lane < GRID_W, toeplitz(d, 0), toeplitz(d + 1, GRID_W)), MASKED)
            for d in range(2 * WIN_R - 2)]
    for e in range(WIN_R):
        for p in range(WIN_R // 2):
            tb_ref[e, :, p * lanes:(p + 1) * lanes] = pair[2 * p - e + WIN_R - 1]


def _na_kernel(q_ref, k_ref, v_ref, cos_ref, sin_ref, rb_ref, gq_ref, gk_ref, o_ref,
               qs, ks, vs, kcs, vcs, tb_ref, *, n_ctx, grid_rows):
    lane = lax.broadcasted_iota(jnp.int32, (1, HEAD_DIM), 1)
    first_half = (lane % (HEAD_DIM // 2)) < (HEAD_DIM // 4)
    _na_build_bias(rb_ref, tb_ref)

    def normed(tv, g):
        return tv * lax.rsqrt(jnp.mean(tv * tv, axis=-1, keepdims=True) + EPS) * g

    def rope(tv, cs, sn):
        partner = jnp.where(first_half, pltpu.roll(tv, 3 * HEAD_DIM // 4, 1), pltpu.roll(tv, HEAD_DIM // 4, 1))
        return tv * cs + partner * sn

    gq = gq_ref[...]
    gk = gk_ref[...]
    kcs[...] = normed(k_ref[0:n_ctx, :], gk).astype(BF16)
    vcs[...] = v_ref[0:n_ctx, :].astype(BF16)

    def prep(i, carry):
        r0 = pl.multiple_of(i * TOK, TOK)
        rows = pl.ds(r0, TOK)
        src = pl.ds(n_ctx + r0, TOK)
        cs, sn = cos_ref[rows, :], sin_ref[rows, :]
        qs[rows, :] = rope(normed(q_ref[src, :], gq), cs, sn).astype(BF16)
        ks[rows, :] = rope(normed(k_ref[src, :], gk), cs, sn).astype(BF16)
        vs[rows, :] = v_ref[src, :].astype(BF16)
        return carry

    lax.fori_loop(0, grid_rows * GRID_W // TOK, prep, 0)
    scale = HEAD_DIM ** -0.5
    band = WIN_R * GRID_W

    def row(r, carry):
        rs = jnp.clip(r - WIN_R // 2, 0, grid_rows - WIN_R)
        qrow = pl.ds(pl.multiple_of(r * GRID_W, GRID_W), GRID_W)
        krow = pl.ds(pl.multiple_of(rs * GRID_W, GRID_W), band)
        qr = qs[qrow, :]
        sb = _dot_nt(qr, ks[krow, :]) * scale + tb_ref[r - rs]
        sc = _dot_nt(qr, kcs[...]) * scale
        m = jnp.maximum(jnp.max(sb, axis=-1, keepdims=True), jnp.max(sc, axis=-1, keepdims=True))
        pb = jnp.exp(sb - m)
        pc = jnp.exp(sc - m)
        den = jnp.sum(pb, axis=-1, keepdims=True) + jnp.sum(pc, axis=-1, keepdims=True)
        o = _dot(pb.astype(BF16), vs[krow, :]) + _dot(pc.astype(BF16), vcs[...])
        o_ref[qrow, :] = (o / den).astype(BF16)
        return carry

    lax.fori_loop(0, grid_rows, row, 0, unroll=4)


def _na_call(p3, rel_bias, gq, gk, n_ctx, l):
    b, t, _ = p3.shape
    cos, sin = _rope_tables(l)
    nr, nc = rel_bias.shape[1:]
    rb = jnp.zeros((HEADS, 2 * WIN_R, 2 * GRID_W), F32).at[:, :nr, :nc].set(rel_bias)
    col = lambda grp: (lambda i, h: (i, 0, grp * HEADS + h))
    tok_spec = lambda im: pl.BlockSpec((None, t, HEAD_DIM), im)
    full = lambda shape: pl.BlockSpec(shape, lambda i, h: (0,) * len(shape))
    return pl.pallas_call(
        functools.partial(_na_kernel, n_ctx=n_ctx, grid_rows=l // GRID_W),
        grid=(b, HEADS),
        in_specs=[tok_spec(col(5)), tok_spec(col(6)), tok_spec(col(7)),
                  full((l, HEAD_DIM)), full((l, HEAD_DIM)),
                  pl.BlockSpec((None, 2 * WIN_R, 2 * GRID_W), lambda i, h: (h, 0, 0)),
                  full((1, HEAD_DIM)), full((1, HEAD_DIM))],
        out_specs=pl.BlockSpec((None, l, HEAD_DIM), lambda i, h: (i, 0, h)),
        out_shape=jax.ShapeDtypeStruct((b, l, MIX_W), BF16),
        scratch_shapes=[pltpu.VMEM((l, HEAD_DIM), BF16), pltpu.VMEM((l, HEAD_DIM), BF16),
                        pltpu.VMEM((l, HEAD_DIM), BF16), pltpu.VMEM((n_ctx, HEAD_DIM), BF16),
                        pltpu.VMEM((n_ctx, HEAD_DIM), BF16),
                        pltpu.VMEM((WIN_R, GRID_W, WIN_R * GRID_W), F32)],
        compiler_params=_params(("parallel", "parallel")),
        name="na",
    )(p3, p3, p3, cos, sin, rb, gq, gk)


def _merge_kernel(ya_ref, yb_ref, ga_ref, gb_ref, x_ref, wa_ref, wb_ref, wo_ref, mod_ref, n2_ref,
                  xm_ref, h2_ref):
    za = _dot(ya_ref[...], wa_ref[...])
    zb = _dot(yb_ref[...], wb_ref[...])
    z = _sigmoid(ga_ref[...]) * za + _sigmoid(gb_ref[...]) * zb
    xm = x_ref[...] + mod_ref[0:1, :] * _dot(z.astype(BF16), wo_ref[...])
    xm_ref[...] = xm
    y = xm * lax.rsqrt(jnp.mean(xm * xm, axis=-1, keepdims=True) + EPS) * n2_ref[...]
    h2_ref[...] = (y * (1.0 + mod_ref[2:3, :]) + mod_ref[1:2, :]).astype(BF16)


def _merge_call(ya, yb, p3, x, wa, wb, wo, mod3, n2, n_ctx):
    b, l, d = x.shape
    ncb = n_ctx // TOK
    gate_blk = MIX_W * 8 // d
    full = lambda shape: pl.BlockSpec(shape, lambda i, t: (0,) * len(shape))
    return pl.pallas_call(
        _merge_kernel,
        grid=(b, l // TOK),
        in_specs=[pl.BlockSpec((None, TOK, MIX_W), lambda i, t: (i, t, 0)),
                  pl.BlockSpec((None, TOK, MIX_W), lambda i, t: (i, t, 0)),
                  pl.BlockSpec((None, TOK, d), lambda i, t: (i, t + ncb, gate_blk)),
                  pl.BlockSpec((None, TOK, d), lambda i, t: (i, t + ncb, gate_blk + 1)),
                  pl.BlockSpec((None, TOK, d), lambda i, t: (i, t, 0)),
                  full((MIX_W, d)), full((MIX_W, d)), full((d, d)),
                  pl.BlockSpec((None, 3, d), lambda i, t: (i, 0, 0)),
                  full((1, d))],
        out_specs=[pl.BlockSpec((None, TOK, d), lambda i, t: (i, t, 0)),
                   pl.BlockSpec((None, TOK, d), lambda i, t: (i, t, 0))],
        out_shape=[jax.ShapeDtypeStruct((b, l, d), F32), jax.ShapeDtypeStruct((b, l, d), BF16)],
        compiler_params=_params(("parallel", "arbitrary")),
        name="merge",
    )(ya, yb, p3, p3, x, wa, wb, wo, mod3, n2)


def _ffn_kernel(h_ref, hp_ref, hn_ref, w1_ref, w3_ref, cw_ref, cb_ref, w2_ref, xm_ref, g2_ref, o_ref,
                *, tiles_per_seq, halo):
    i = pl.program_id(0)
    k = pl.program_id(1)
    tm = h_ref.shape[0]
    w1 = w1_ref[...]
    t1 = _dot(h_ref[...], w1)
    prev_row = _dot(hp_ref[...], w1)[halo - 1:halo, :]
    next_row = _dot(hn_ref[...], w1)[0:1, :]
    prev_row = jnp.where(i % tiles_per_seq == 0, 0.0, prev_row)
    next_row = jnp.where(i % tiles_per_seq == tiles_per_seq - 1, 0.0, next_row)
    ridx = lax.broadcasted_iota(jnp.int32, (tm, 1), 0)
    up = jnp.where(ridx == 0, prev_row, pltpu.roll(t1, 1, 0))
    dn = jnp.where(ridx == tm - 1, next_row, pltpu.roll(t1, tm - 1, 0))
    u = up * cw_ref[0:1, :] + t1 * cw_ref[1:2, :] + dn * cw_ref[2:3, :] + cb_ref[...]
    a = (u * _sigmoid(u)) * _dot(h_ref[...], w3_ref[...])
    part = _dot(a.astype(BF16), w2_ref[...])

    @pl.when(k == 0)
    def _():
        o_ref[...] = part

    @pl.when(k > 0)
    def _():
        o_ref[...] += part

    @pl.when(k == pl.num_programs(1) - 1)
    def _():
        o_ref[...] = xm_ref[...] + g2_ref[...] * o_ref[...]


def _ffn_call(h2, xm, w1, w3, cw, cb, w2, g2, l):
    m, d = h2.shape
    hid = w1.shape[1]
    tm = 512
    th = min(512, hid)
    halo = 16
    per = tm // halo
    nblk = m // halo
    return pl.pallas_call(
        functools.partial(_ffn_kernel, tiles_per_seq=l // tm, halo=halo),
        grid=(m // tm, hid // th),
        in_specs=[pl.BlockSpec((tm, d), lambda i, k: (i, 0)),
                  pl.BlockSpec((halo, d), lambda i, k: (jnp.maximum(i * per - 1, 0), 0)),
                  pl.BlockSpec((halo, d), lambda i, k: (jnp.minimum((i + 1) * per, nblk - 1), 0)),
                  pl.BlockSpec((d, th), lambda i, k: (0, k)),
                  pl.BlockSpec((d, th), lambda i, k: (0, k)),
                  pl.BlockSpec((3, th), lambda i, k: (0, k)),
                  pl.BlockSpec((1, th), lambda i, k: (0, k)),
                  pl.BlockSpec((th, d), lambda i, k: (k, 0)),
                  pl.BlockSpec((tm, d), lambda i, k: (i, 0)),
                  pl.BlockSpec((None, 1, d), lambda i, k: (i // (l // tm), 0, 0))],
        out_specs=pl.BlockSpec((tm, d), lambda i, k: (i, 0)),
        out_shape=jax.ShapeDtypeStruct((m, d), F32),
        compiler_params=_params(("parallel", "arbitrary")),
        name="ffn",
    )(h2, h2, h2, w1, w3, cw, cb, w2, xm, g2)


def kernel(x, c, ctx, c_ctx, ada_w, ada_b, norm1_g, norm2_g, w_in, hgrn_lb_logits, hgrn_norm_g,
           na_q_norm_g, na_k_norm_g, na_rel_bias, w_branch_a, w_branch_b, w_out,
           ffn_w1, ffn_w3, ffn_conv_w, ffn_conv_b, ffn_w2):
    b, l, d = x.shape
    n_ctx = ctx.shape[1]
    assert ada_w.shape[0] == 1 and b + 1 <= 8 and n_ctx % TOK == 0 and l % (2 * TOK) == 0

    c_all = jnp.zeros((8, d), F32).at[:b].set(c).at[b].set(c_ctx)
    mod = _mod_call(c_all, ada_w[0], ada_b).reshape(8, N_MOD, d)
    sh1, sc1, g1, sh2, sc2, g2 = (mod[:b, n] for n in range(N_MOD))
    ss_ctx = jnp.broadcast_to(mod[b, 0:2][None], (b, 2, d))
    ss1 = jnp.stack([ss_ctx, jnp.stack([sh1, sc1], axis=1)], axis=1)

    h = _prenorm_call(ctx, x, norm1_g, ss1)
    t = n_ctx + l
    p = _inproj_call(h.reshape(b * t, d), w_in[0].astype(BF16))
    p3 = p.reshape(b, t, p.shape[1])

    y_a = _hgrn_call(p3, hgrn_lb_logits, hgrn_norm_g, n_ctx, l)
    y_b = _na_call(p3, na_rel_bias[0], na_q_norm_g, na_k_norm_g, n_ctx, l)

    mod3 = jnp.stack([g1, sh2, sc2], axis=1)
    x_mid, h2 = _merge_call(y_a, y_b, p3, x, w_branch_a[0].astype(BF16), w_branch_b[0].astype(BF16),
                            w_out[0].astype(BF16), mod3, norm2_g, n_ctx)

    out = _ffn_call(h2.reshape(b * l, d), x_mid.reshape(b * l, d), ffn_w1[0].astype(BF16),
                    ffn_w3[0].astype(BF16), ffn_conv_w[0], ffn_conv_b, ffn_w2[0].astype(BF16),
                    g2[:, None, :], l)
    return out.reshape(b, l, d)
```

```python
import functools

import numpy as np
import jax
import jax.numpy as jnp
from jax import lax
from jax.experimental import pallas as pl
from jax.experimental.pallas import tpu as pltpu

GRID_W = 64
HEADS = 8
HEAD_DIM = 128
MIX_W = HEADS * HEAD_DIM
CHUNK = 64
N_LEVELS = 6
MXU_LEVEL_HALVES = (4, 2)
PREP_CHUNKS = 4
SCAN_CHUNKS = 4
NA_ROWS = 4
WIN_R = 8
WIN_C = 16
ROPE_THETA = 10000.0
N_MOD = 6
EPS = 1e-6
TOK = 256
MASKED = -1e30
V7X_VMEM_LIMIT = 56 * 1024 * 1024

BF16 = jnp.bfloat16
F32 = jnp.float32


def _dot(a, b):
    return jnp.dot(a, b, preferred_element_type=F32)


def _dot_nt(a, b):
    return lax.dot_general(a, b, (((1,), (1,)), ((), ())), preferred_element_type=F32)


def _sigmoid(t):
    return 1.0 / (1.0 + jnp.exp(-t))


def _params(sem):
    return pltpu.CompilerParams(dimension_semantics=sem, vmem_limit_bytes=V7X_VMEM_LIMIT)


def _mod_kernel(c_ref, w_ref, b_ref, o_ref):
    cv = c_ref[...]
    s = (cv * _sigmoid(cv)).astype(BF16)
    o_ref[...] = _dot(s, w_ref[...].astype(BF16)) + b_ref[...]


def _mod_call(c_all, w, b):
    d, n = w.shape
    tn = min(1024, d)
    return pl.pallas_call(
        _mod_kernel,
        grid=(n // tn,),
        in_specs=[pl.BlockSpec((8, d), lambda j: (0, 0)),
                  pl.BlockSpec((d, tn), lambda j: (0, j)),
                  pl.BlockSpec((1, tn), lambda j: (0, j))],
        out_specs=pl.BlockSpec((8, tn), lambda j: (0, j)),
        out_shape=jax.ShapeDtypeStruct((8, n), F32),
        compiler_params=_params(("arbitrary",)),
        name="mod",
    )(c_all, w, b)


def _prenorm_kernel(ctx_ref, x_ref, g_ref, ss_ref, o_ref, *, n_ctx_blk):
    t = pl.program_id(1)
    xv = jnp.where(t < n_ctx_blk, ctx_ref[...], x_ref[...])
    y = xv * lax.rsqrt(jnp.mean(xv * xv, axis=-1, keepdims=True) + EPS) * g_ref[...]
    o_ref[...] = (y * (1.0 + ss_ref[1:2, :]) + ss_ref[0:1, :]).astype(BF16)


def _prenorm_call(ctx, x, g, ss):
    b, l, d = x.shape
    ncb = ctx.shape[1] // TOK
    nt = ncb + l // TOK
    return pl.pallas_call(
        functools.partial(_prenorm_kernel, n_ctx_blk=ncb),
        grid=(b, nt),
        in_specs=[pl.BlockSpec((None, TOK, d), lambda i, t: (i, jnp.minimum(t, ncb - 1), 0)),
                  pl.BlockSpec((None, TOK, d), lambda i, t: (i, jnp.maximum(t - ncb, 0), 0)),
                  pl.BlockSpec((1, d), lambda i, t: (0, 0)),
                  pl.BlockSpec((None, None, 2, d), lambda i, t: (i, (t >= ncb).astype(jnp.int32), 0, 0))],
        out_specs=pl.BlockSpec((None, TOK, d), lambda i, t: (i, t, 0)),
        out_shape=jax.ShapeDtypeStruct((b, nt * TOK, d), BF16),
        compiler_params=_params(("parallel", "arbitrary")),
        name="prenorm",
    )(ctx, x, g, ss)


def _matmul_kernel(h_ref, w_ref, o_ref):
    o_ref[...] = _dot(h_ref[...], w_ref[...])


def _inproj_call(h, w):
    m, d = h.shape
    n = w.shape[1]
    tm = 1024 if m % 1024 == 0 else TOK
    tn = min(1024, d)
    return pl.pallas_call(
        _matmul_kernel,
        grid=(n // tn, m // tm),
        in_specs=[pl.BlockSpec((tm, d), lambda j, i: (i, 0)),
                  pl.BlockSpec((d, tn), lambda j, i: (0, j))],
        out_specs=pl.BlockSpec((tm, tn), lambda j, i: (i, j)),
        out_shape=jax.ShapeDtypeStruct((m, n), F32),
        compiler_params=_params(("parallel", "arbitrary")),
        name="inproj",
    )(h, w)


def _hgrn_constants():
    u = np.arange(CHUNK)
    mats, masks = [], []
    for rev in (False, True):
        pos = CHUNK - 1 - u if rev else u
        incl = (pos[None, :] <= pos[:, None]).astype(np.float32)
        blocks, lvl_masks = [incl], []
        for lev in range(N_LEVELS):
            half = CHUNK >> (lev + 1)
            ref = (pos // (2 * half)) * (2 * half) + half - 1
            upto_ref = (pos[None, :] <= ref[:, None]).astype(np.float32)
            if half in MXU_LEVEL_HALVES:
                late = (pos % (2 * half)) >= half
                blocks.append(np.where(late[:, None], 1.0, -1.0).astype(np.float32) * (incl - upto_ref))
            same = (pos[:, None] // (2 * half)) == (pos[None, :] // (2 * half))
            late_q = (pos[:, None] % (2 * half)) >= half
            early_k = (pos[None, :] % (2 * half)) < half
            lvl_masks.append((same & late_q & early_k).astype(np.float32))
        lvl_masks.append(np.eye(CHUNK, dtype=np.float32))
        mats.append(np.concatenate(blocks, axis=0))
        masks.append(np.stack(lvl_masks))
    return np.stack(mats), np.stack(masks)


def _hgrn_kernel(lbl_ref, q_ref, ff_ref, fb_ref, i_ref, gt_ref, cm_ref, mk_ref, ng_ref, y_ref,
                 of_ref, ob_ref, sf_ref, sb_ref, kd_ref, et_ref, qd_ref, sc_ref, *, n_ctx_chunks, n_lat_chunks):
    f_refs, o_refs, st_refs = (ff_ref, fb_ref), (of_ref, ob_ref), (sf_ref, sb_ref)
    lowers = []
    for dirn in range(2):
        la, lb_ = lbl_ref[dirn, 0], lbl_ref[dirn, 1]
        mx = jnp.maximum(la, lb_)
        ea, eb = jnp.exp(la - mx), jnp.exp(lb_ - mx)
        lowers.append(ea / (ea + eb))
        st_refs[dirn][...] = jnp.zeros(st_refs[dirn].shape, F32)
    row_odd = (lax.broadcasted_iota(jnp.int32, (CHUNK, 1), 0) % 2) == 1

    def chunk_rows(pos):
        return pl.ds(pl.multiple_of(pos * CHUNK, CHUNK), CHUNK)

    def level_decay(dirn, lev, f, sums):
        half = CHUNK >> (lev + 1)
        cum = sums[0:CHUNK]
        if half == 1:
            return jnp.where(row_odd, 1.0, f) if dirn else jnp.where(row_odd, f, 1.0)
        if half in MXU_LEVEL_HALVES:
            blk = 1 + MXU_LEVEL_HALVES.index(half)
            return jnp.exp(sums[blk * CHUNK:(blk + 1) * CHUNK])
        parts = []
        for p in range(0, CHUNK, 2 * half):
            ref = cum[p + half - 1 + dirn:p + half + dirn]
            lo, hi = cum[p:p + half], cum[p + half:p + 2 * half]
            parts += [lo - ref, ref - hi] if dirn else [ref - lo, hi - ref]
        return jnp.exp(jnp.concatenate(parts, axis=0))

    def prepare(pos0, n, with_out):
        items = [(dirn, pos0 + c) for c in range(n) for dirn in range(2)]
        fs, kks, splits = [], [], []
        for dirn, pos in items:
            lower = lowers[dirn]
            f = lower + (1.0 - lower) * _sigmoid(f_refs[dirn][chunk_rows(pos), :])
            g = jnp.log(f)
            g1 = g.astype(BF16)
            r1 = g - g1.astype(F32)
            g2 = r1.astype(BF16)
            g3 = (r1 - g2.astype(F32)).astype(BF16)
            fs.append(f)
            kks.append(1.0 - f)
            splits.append(jnp.concatenate([g1, g2, g3], axis=0))
        sums = [None] * len(items)
        for dirn in range(2):
            idx = [j for j, it in enumerate(items) if it[0] == dirn]
            wide = _dot(cm_ref[dirn], jnp.concatenate([splits[j] for j in idx], axis=1))
            for c, j in enumerate(idx):
                sums[j] = wide[:, c * HEAD_DIM:(c + 1) * HEAD_DIM]
        for j, (dirn, pos) in enumerate(items):
            cum = sums[j][0:CHUNK]
            last = 0 if dirn else CHUNK - 1
            tot = cum[last:last + 1]
            kd_ref[dirn, chunk_rows(pos), :] = (kks[j] * jnp.exp(tot - cum)).astype(BF16)
            et_ref[dirn, pos] = jnp.broadcast_to(jnp.exp(tot), et_ref.shape[2:])
        if not with_out:
            return
        qs = [q_ref[chunk_rows(pos), :] for _, pos in items]
        for j, (dirn, pos) in enumerate(items):
            qd_ref[dirn, chunk_rows(pos - n_ctx_chunks), :] = (qs[j] * jnp.exp(sums[j][0:CHUNK])).astype(BF16)
        qbs = [qv.astype(BF16) for qv in qs]
        kbs = [kv.astype(BF16) for kv in kks]
        scs = [mk_ref[dirn, N_LEVELS] * _dot_nt(qbs[j], kbs[j]) for j, (dirn, _) in enumerate(items)]
        for lev in range(N_LEVELS):
            for j, (dirn, _) in enumerate(items):
                dl = level_decay(dirn, lev, fs[j], sums[j]).astype(BF16)
                scs[j] = scs[j] + mk_ref[dirn, lev] * _dot_nt(qbs[j] * dl, kbs[j] * dl)
        for j, (dirn, pos) in enumerate(items):
            sc_ref[dirn, chunk_rows(pos - n_ctx_chunks), :] = scs[j].astype(BF16)

    def scan(pos_f, pos_b, n, with_out):
        items = [(dirn, (pos_b - c) if dirn else (pos_f + c)) for c in range(n) for dirn in range(2)]
        vs = [i_ref[chunk_rows(pos), :] for _, pos in items]
        ups = [_dot(vs[j].T.astype(BF16), kd_ref[dirn, chunk_rows(pos), :]) for j, (dirn, pos) in enumerate(items)]
        sts = [st_refs[0][...], st_refs[1][...]]
        before = []
        for j, (dirn, pos) in enumerate(items):
            before.append(sts[dirn])
            sts[dirn] = sts[dirn] * et_ref[dirn, pos][0:1] + ups[j]
        for dirn in range(2):
            st_refs[dirn][...] = sts[dirn]
        if not with_out:
            return
        for j, (dirn, pos) in enumerate(items):
            orow = chunk_rows(pos - n_ctx_chunks)
            o_refs[dirn][orow, :] = (_dot_nt(qd_ref[dirn, orow, :], before[j].astype(BF16))
                                     + _dot(sc_ref[dirn, orow, :], vs[j].astype(BF16)))

    def loop(n, body):
        lax.fori_loop(0, n, lambda ci, c: (body(ci), c)[1], 0)

    last_pos = n_ctx_chunks + n_lat_chunks - 1
    prepare(0, n_ctx_chunks, False)
    loop(n_lat_chunks // PREP_CHUNKS, lambda ci: prepare(n_ctx_chunks + ci * PREP_CHUNKS, PREP_CHUNKS, True))
    scan(0, n_ctx_chunks - 1, n_ctx_chunks, False)
    loop(n_lat_chunks // SCAN_CHUNKS,
         lambda ci: scan(n_ctx_chunks + ci * SCAN_CHUNKS, last_pos - ci * SCAN_CHUNKS, SCAN_CHUNKS, True))

    def readout(i, carry):
        rows = pl.ds(pl.multiple_of(i * TOK, TOK), TOK)
        ot = of_ref[rows, :] + ob_ref[rows, :]
        on = ot * lax.rsqrt(jnp.mean(ot * ot, axis=-1, keepdims=True) + EPS) * ng_ref[...]
        gate = gt_ref[pl.ds(pl.multiple_of(n_ctx_chunks * CHUNK + i * TOK, CHUNK), TOK), :]
        y_ref[rows, :] = (on * (gate * _sigmoid(gate))).astype(BF16)
        return carry

    lax.fori_loop(0, n_lat_chunks * CHUNK // TOK, readout, 0)


def _hgrn_call(p3, lb_logits, norm_g, n_ctx, l):
    b, t, _ = p3.shape
    cm, mk = _hgrn_constants()
    cm = jnp.asarray(np.concatenate([cm] * 3, axis=-1), BF16)
    mk = jnp.asarray(mk, F32)
    lbl = lb_logits.reshape(2, 2, HEADS, 1, HEAD_DIM)
    tok_spec = lambda grp: pl.BlockSpec((None, t, HEAD_DIM), lambda i, h: (i, 0, grp * HEADS + h))
    full = lambda shape: pl.BlockSpec(shape, lambda i, h: (0,) * len(shape))
    return pl.pallas_call(
        functools.partial(_hgrn_kernel, n_ctx_chunks=n_ctx // CHUNK, n_lat_chunks=l // CHUNK),
        grid=(b, HEADS),
        in_specs=[pl.BlockSpec((2, 2, None, 1, HEAD_DIM), lambda i, h: (0, 0, h, 0, 0)),
                  tok_spec(0), tok_spec(1), tok_spec(2), tok_spec(3), tok_spec(4),
                  full((2, 3 * CHUNK, 3 * CHUNK)),
                  full((2, N_LEVELS + 1, CHUNK, CHUNK)),
                  full((1, HEAD_DIM))],
        out_specs=pl.BlockSpec((None, l, HEAD_DIM), lambda i, h: (i, 0, h)),
        out_shape=jax.ShapeDtypeStruct((b, l, MIX_W), BF16),
        scratch_shapes=[pltpu.VMEM((l, HEAD_DIM), F32), pltpu.VMEM((l, HEAD_DIM), F32),
                        pltpu.VMEM((HEAD_DIM, HEAD_DIM), F32), pltpu.VMEM((HEAD_DIM, HEAD_DIM), F32),
                        pltpu.VMEM((2, t, HEAD_DIM), BF16), pltpu.VMEM((2, t // CHUNK, 8, HEAD_DIM), F32),
                        pltpu.VMEM((2, l, HEAD_DIM), BF16), pltpu.VMEM((2, l, CHUNK), BF16)],
        compiler_params=_params(("parallel", "parallel")),
        name="hgrn",
    )(lbl, p3, p3, p3, p3, p3, cm, mk, norm_g)


def _rope_tables(l):
    pos = np.arange(l)
    nf = HEAD_DIM // 4
    inv = ROPE_THETA ** (-np.arange(nf, dtype=np.float64) / nf)
    ang_r = (pos // GRID_W)[:, None] * inv[None, :]
    ang_c = (pos % GRID_W)[:, None] * inv[None, :]
    cos = np.concatenate([np.cos(ang_r)] * 2 + [np.cos(ang_c)] * 2, axis=-1)
    sin = np.concatenate([-np.sin(ang_r), np.sin(ang_r), -np.sin(ang_c), np.sin(ang_c)], axis=-1)
    return jnp.asarray(cos, F32), jnp.asarray(sin, F32)


def _na_build_bias(rb_ref, tb_ref):
    lanes = 2 * GRID_W
    cq = lax.broadcasted_iota(jnp.int32, (GRID_W, lanes), 0)
    lane = lax.broadcasted_iota(jnp.int32, (GRID_W, lanes), 1)
    ck = lane % GRID_W
    cs = jnp.clip(cq - WIN_C // 2, 0, GRID_W - WIN_C)
    col_in = (ck >= cs) & (ck < cs + WIN_C)

    def toeplitz(d, lane0):
        row = jnp.broadcast_to(rb_ref[d:d + 1, :], (GRID_W, lanes))
        return pltpu.roll(row, (lane0 - (WIN_C - 1)) % lanes, 1, stride=1, stride_axis=0)

    pair = [jnp.where(col_in, jnp.where(lane < GRID_W, toeplitz(d, 0), toeplitz(d + 1, GRID_W)), MASKED)
            for d in range(2 * WIN_R - 2)]
    for e in range(WIN_R):
        for p in range(WIN_R // 2):
            tb_ref[e, :, p * lanes:(p + 1) * lanes] = pair[2 * p - e + WIN_R - 1]


def _na_kernel(q_ref, k_ref, v_ref, cos_ref, sin_ref, rb_ref, gq_ref, gk_ref, o_ref,
               qs, ks, vs, kcs, vcs, tb_ref, *, n_ctx, grid_rows):
    lane = lax.broadcasted_iota(jnp.int32, (1, HEAD_DIM), 1)
    first_half = (lane % (HEAD_DIM // 2)) < (HEAD_DIM // 4)
    _na_build_bias(rb_ref, tb_ref)

    def normed(tv, g):
        return tv * lax.rsqrt(jnp.mean(tv * tv, axis=-1, keepdims=True) + EPS) * g

    def rope(tv, cs, sn):
        partner = jnp.where(first_half, pltpu.roll(tv, 3 * HEAD_DIM // 4, 1), pltpu.roll(tv, HEAD_DIM // 4, 1))
        return tv * cs + partner * sn

    gq = gq_ref[...]
    gk = gk_ref[...]
    kcs[...] = normed(k_ref[0:n_ctx, :], gk).astype(BF16)
    vcs[...] = v_ref[0:n_ctx, :].astype(BF16)

    def prep(i, carry):
        r0 = pl.multiple_of(i * TOK, TOK)
        rows = pl.ds(r0, TOK)
        src = pl.ds(n_ctx + r0, TOK)
        cs, sn = cos_ref[rows, :], sin_ref[rows, :]
        qs[rows, :] = rope(normed(q_ref[src, :], gq), cs, sn).astype(BF16)
        ks[rows, :] = rope(normed(k_ref[src, :], gk), cs, sn).astype(BF16)
        vs[rows, :] = v_ref[src, :].astype(BF16)
        return carry

    lax.fori_loop(0, grid_rows * GRID_W // TOK, prep, 0)
    scale = HEAD_DIM ** -0.5
    band = WIN_R * GRID_W

    def rows_step(i, carry):
        rr = [i * NA_ROWS + n for n in range(NA_ROWS)]
        rs = [jnp.clip(r - WIN_R // 2, 0, grid_rows - WIN_R) for r in rr]
        qrow = [pl.ds(pl.multiple_of(r * GRID_W, GRID_W), GRID_W) for r in rr]
        krow = [pl.ds(pl.multiple_of(s * GRID_W, GRID_W), band) for s in rs]
        qr = [qs[qw, :] for qw in qrow]
        kc = kcs[...]
        sb = [_dot_nt(qr[n], ks[krow[n], :]) * scale + tb_ref[rr[n] - rs[n]] for n in range(NA_ROWS)]
        sc = [_dot_nt(qr[n], kc) * scale for n in range(NA_ROWS)]
        m = [jnp.maximum(jnp.max(sb[n], axis=-1, keepdims=True), jnp.max(sc[n], axis=-1, keepdims=True))
             for n in range(NA_ROWS)]
        pb = [jnp.exp(sb[n] - m[n]) for n in range(NA_ROWS)]
        pc = [jnp.exp(sc[n] - m[n]) for n in range(NA_ROWS)]
        den = [jnp.sum(pb[n], axis=-1, keepdims=True) + jnp.sum(pc[n], axis=-1, keepdims=True)
               for n in range(NA_ROWS)]
        vc = vcs[...]
        o = [_dot(pb[n].astype(BF16), vs[krow[n], :]) + _dot(pc[n].astype(BF16), vc) for n in range(NA_ROWS)]
        for n in range(NA_ROWS):
            o_ref[qrow[n], :] = (o[n] / den[n]).astype(BF16)
        return carry

    lax.fori_loop(0, grid_rows // NA_ROWS, rows_step, 0)


def _na_call(p3, rel_bias, gq, gk, n_ctx, l):
    b, t, _ = p3.shape
    cos, sin = _rope_tables(l)
    nr, nc = rel_bias.shape[1:]
    rb = jnp.zeros((HEADS, 2 * WIN_R, 2 * GRID_W), F32).at[:, :nr, :nc].set(rel_bias)
    col = lambda grp: (lambda i, h: (i, 0, grp * HEADS + h))
    tok_spec = lambda im: pl.BlockSpec((None, t, HEAD_DIM), im)
    full = lambda shape: pl.BlockSpec(shape, lambda i, h: (0,) * len(shape))
    return pl.pallas_call(
        functools.partial(_na_kernel, n_ctx=n_ctx, grid_rows=l // GRID_W),
        grid=(b, HEADS),
        in_specs=[tok_spec(col(5)), tok_spec(col(6)), tok_spec(col(7)),
                  full((l, HEAD_DIM)), full((l, HEAD_DIM)),
                  pl.BlockSpec((None, 2 * WIN_R, 2 * GRID_W), lambda i, h: (h, 0, 0)),
                  full((1, HEAD_DIM)), full((1, HEAD_DIM))],
        out_specs=pl.BlockSpec((None, l, HEAD_DIM), lambda i, h: (i, 0, h)),
        out_shape=jax.ShapeDtypeStruct((b, l, MIX_W), BF16),
        scratch_shapes=[pltpu.VMEM((l, HEAD_DIM), BF16), pltpu.VMEM((l, HEAD_DIM), BF16),
                        pltpu.VMEM((l, HEAD_DIM), BF16), pltpu.VMEM((n_ctx, HEAD_DIM), BF16),
                        pltpu.VMEM((n_ctx, HEAD_DIM), BF16),
                        pltpu.VMEM((WIN_R, GRID_W, WIN_R * GRID_W), F32)],
        compiler_params=_params(("parallel", "parallel")),
        name="na",
    )(p3, p3, p3, cos, sin, rb, gq, gk)


def _merge_kernel(ya_ref, yb_ref, ga_ref, gb_ref, x_ref, wa_ref, wb_ref, wo_ref, mod_ref, n2_ref,
                  xm_ref, h2_ref):
    za = _dot(ya_ref[...], wa_ref[...])
    zb = _dot(yb_ref[...], wb_ref[...])
    z = _sigmoid(ga_ref[...]) * za + _sigmoid(gb_ref[...]) * zb
    xm = x_ref[...] + mod_ref[0:1, :] * _dot(z.astype(BF16), wo_ref[...])
    xm_ref[...] = xm
    y = xm * lax.rsqrt(jnp.mean(xm * xm, axis=-1, keepdims=True) + EPS) * n2_ref[...]
    h2_ref[...] = (y * (1.0 + mod_ref[2:3, :]) + mod_ref[1:2, :]).astype(BF16)


def _merge_call(ya, yb, p3, x, wa, wb, wo, mod3, n2, n_ctx):
    b, l, d = x.shape
    ncb = n_ctx // TOK
    gate_blk = MIX_W * 8 // d
    full = lambda shape: pl.BlockSpec(shape, lambda i, t: (0,) * len(shape))
    return pl.pallas_call(
        _merge_kernel,
        grid=(b, l // TOK),
        in_specs=[pl.BlockSpec((None, TOK, MIX_W), lambda i, t: (i, t, 0)),
                  pl.BlockSpec((None, TOK, MIX_W), lambda i, t: (i, t, 0)),
                  pl.BlockSpec((None, TOK, d), lambda i, t: (i, t + ncb, gate_blk)),
                  pl.BlockSpec((None, TOK, d), lambda i, t: (i, t + ncb, gate_blk + 1)),
                  pl.BlockSpec((None, TOK, d), lambda i, t: (i, t, 0)),
                  full((MIX_W, d)), full((MIX_W, d)), full((d, d)),
                  pl.BlockSpec((None, 3, d), lambda i, t: (i, 0, 0)),
                  full((1, d))],
        out_specs=[pl.BlockSpec((None, TOK, d), lambda i, t: (i, t, 0)),
                   pl.BlockSpec((None, TOK, d), lambda i, t: (i, t, 0))],
        out_shape=[jax.ShapeDtypeStruct((b, l, d), F32), jax.ShapeDtypeStruct((b, l, d), BF16)],
        compiler_params=_params(("parallel", "arbitrary")),
        name="merge",
    )(ya, yb, p3, p3, x, wa, wb, wo, mod3, n2)


def _ffn_kernel(h_ref, hp_ref, hn_ref, w1_ref, w3_ref, cw_ref, cb_ref, w2_ref, xm_ref, g2_ref, o_ref,
                *, tiles_per_seq, halo):
    i = pl.program_id(0)
    k = pl.program_id(1)
    tm = h_ref.shape[0]

    @pl.when(k == 0)
    def _():
        o_ref[...] = jnp.zeros(o_ref.shape, F32)

    w1 = w1_ref[...]
    t1 = _dot(h_ref[...], w1)
    prev_row = _dot(hp_ref[...], w1)[halo - 1:halo, :]
    next_row = _dot(hn_ref[...], w1)[0:1, :]
    prev_row = jnp.where(i % tiles_per_seq == 0, 0.0, prev_row)
    next_row = jnp.where(i % tiles_per_seq == tiles_per_seq - 1, 0.0, next_row)
    ridx = lax.broadcasted_iota(jnp.int32, (tm, 1), 0)
    up = jnp.where(ridx == 0, prev_row, pltpu.roll(t1, 1, 0))
    dn = jnp.where(ridx == tm - 1, next_row, pltpu.roll(t1, tm - 1, 0))
    u = up * cw_ref[0:1, :] + t1 * cw_ref[1:2, :] + dn * cw_ref[2:3, :] + cb_ref[...]
    a = (u * _sigmoid(u)) * _dot(h_ref[...], w3_ref[...])
    o_ref[...] += _dot(a.astype(BF16), w2_ref[...])

    @pl.when(k == pl.num_programs(1) - 1)
    def _():
        o_ref[...] = xm_ref[...] + g2_ref[...] * o_ref[...]


def _ffn_call(h2, xm, w1, w3, cw, cb, w2, g2, l):
    m, d = h2.shape
    hid = w1.shape[1]
    tm = 512
    th = min(512, hid)
    halo = 16
    per = tm // halo
    nblk = m // halo
    return pl.pallas_call(
        functools.partial(_ffn_kernel, tiles_per_seq=l // tm, halo=halo),
        grid=(m // tm, hid // th),
        in_specs=[pl.BlockSpec((tm, d), lambda i, k: (i, 0)),
                  pl.BlockSpec((halo, d), lambda i, k: (jnp.maximum(i * per - 1, 0), 0)),
                  pl.BlockSpec((halo, d), lambda i, k: (jnp.minimum((i + 1) * per, nblk - 1), 0)),
                  pl.BlockSpec((d, th), lambda i, k: (0, k)),
                  pl.BlockSpec((d, th), lambda i, k: (0, k)),
                  pl.BlockSpec((3, th), lambda i, k: (0, k)),
                  pl.BlockSpec((1, th), lambda i, k: (0, k)),
                  pl.BlockSpec((th, d), lambda i, k: (k, 0)),
                  pl.BlockSpec((tm, d), lambda i, k: (i, 0)),
                  pl.BlockSpec((None, 1, d), lambda i, k: (i // (l // tm), 0, 0))],
        out_specs=pl.BlockSpec((tm, d), lambda i, k: (i, 0)),
        out_shape=jax.ShapeDtypeStruct((m, d), F32),
        compiler_params=_params(("parallel", "arbitrary")),
        name="ffn",
    )(h2, h2, h2, w1, w3, cw, cb, w2, xm, g2)


def kernel(x, c, ctx, c_ctx, ada_w, ada_b, norm1_g, norm2_g, w_in, hgrn_lb_logits, hgrn_norm_g,
           na_q_norm_g, na_k_norm_g, na_rel_bias, w_branch_a, w_branch_b, w_out,
           ffn_w1, ffn_w3, ffn_conv_w, ffn_conv_b, ffn_w2):
    b, l, d = x.shape
    n_ctx = ctx.shape[1]
    assert ada_w.shape[0] == 1 and b + 1 <= 8 and n_ctx % TOK == 0 and l % (2 * TOK) == 0

    c_all = jnp.zeros((8, d), F32).at[:b].set(c).at[b].set(c_ctx)
    mod = _mod_call(c_all, ada_w[0], ada_b).reshape(8, N_MOD, d)
    sh1, sc1, g1, sh2, sc2, g2 = (mod[:b, n] for n in range(N_MOD))
    ss_ctx = jnp.broadcast_to(mod[b, 0:2][None], (b, 2, d))
    ss1 = jnp.stack([ss_ctx, jnp.stack([sh1, sc1], axis=1)], axis=1)

    h = _prenorm_call(ctx, x, norm1_g, ss1)
    t = n_ctx + l
    p = _inproj_call(h.reshape(b * t, d), w_in[0].astype(BF16))
    p3 = p.reshape(b, t, p.shape[1])

    y_a = _hgrn_call(p3, hgrn_lb_logits, hgrn_norm_g, n_ctx, l)
    y_b = _na_call(p3, na_rel_bias[0], na_q_norm_g, na_k_norm_g, n_ctx, l)

    mod3 = jnp.stack([g1, sh2, sc2], axis=1)
    x_mid, h2 = _merge_call(y_a, y_b, p3, x, w_branch_a[0].astype(BF16), w_branch_b[0].astype(BF16),
                            w_out[0].astype(BF16), mod3, norm2_g, n_ctx)

    out = _ffn_call(h2.reshape(b * l, d), x_mid.reshape(b * l, d), ffn_w1[0].astype(BF16),
                    ffn_w3[0].astype(BF16), ffn_conv_w[0], ffn_conv_b, ffn_w2[0].astype(BF16),
                    g2[:, None, :], l)
    return out.reshape(b, l, d)
```

```python
import functools

import numpy as np
import jax
import jax.numpy as jnp
from jax import lax
from jax.experimental import pallas as pl
from jax.experimental.pallas import tpu as pltpu

GRID_W = 64
HEADS = 8
HEAD_DIM = 128
MIX_W = HEADS * HEAD_DIM
CHUNK = 64
N_LEVELS = 6
MXU_LEVEL_HALVES = (4, 2)
PREP_CHUNKS = 4
SCAN_CHUNKS = 4
NA_ROWS = 8
WIN_R = 8
WIN_C = 16
ROPE_THETA = 10000.0
N_MOD = 6
EPS = 1e-6
TOK = 256
MASKED = -1e30
V7X_VMEM_LIMIT = 56 * 1024 * 1024

BF16 = jnp.bfloat16
F32 = jnp.float32


def _dot(a, b):
    return jnp.dot(a, b, preferred_element_type=F32)


def _dot_nt(a, b):
    return lax.dot_general(a, b, (((1,), (1,)), ((), ())), preferred_element_type=F32)


def _sigmoid(t):
    return 1.0 / (1.0 + jnp.exp(-t))


def _params(sem):
    return pltpu.CompilerParams(dimension_semantics=sem, vmem_limit_bytes=V7X_VMEM_LIMIT)


def _mod_kernel(c_ref, w_ref, b_ref, o_ref):
    cv = c_ref[...]
    s = (cv * _sigmoid(cv)).astype(BF16)
    o_ref[...] = _dot(s, w_ref[...].astype(BF16)) + b_ref[...]


def _mod_call(c_all, w, b):
    d, n = w.shape
    tn = min(1024, d)
    return pl.pallas_call(
        _mod_kernel,
        grid=(n // tn,),
        in_specs=[pl.BlockSpec((8, d), lambda j: (0, 0)),
                  pl.BlockSpec((d, tn), lambda j: (0, j)),
                  pl.BlockSpec((1, tn), lambda j: (0, j))],
        out_specs=pl.BlockSpec((8, tn), lambda j: (0, j)),
        out_shape=jax.ShapeDtypeStruct((8, n), F32),
        compiler_params=_params(("arbitrary",)),
        name="mod",
    )(c_all, w, b)


def _prenorm_kernel(ctx_ref, x_ref, g_ref, ss_ref, o_ref, *, n_ctx_blk):
    t = pl.program_id(1)
    xv = jnp.where(t < n_ctx_blk, ctx_ref[...], x_ref[...])
    y = xv * lax.rsqrt(jnp.mean(xv * xv, axis=-1, keepdims=True) + EPS) * g_ref[...]
    o_ref[...] = (y * (1.0 + ss_ref[1:2, :]) + ss_ref[0:1, :]).astype(BF16)


def _prenorm_call(ctx, x, g, ss):
    b, l, d = x.shape
    ncb = ctx.shape[1] // TOK
    nt = ncb + l // TOK
    return pl.pallas_call(
        functools.partial(_prenorm_kernel, n_ctx_blk=ncb),
        grid=(b, nt),
        in_specs=[pl.BlockSpec((None, TOK, d), lambda i, t: (i, jnp.minimum(t, ncb - 1), 0)),
                  pl.BlockSpec((None, TOK, d), lambda i, t: (i, jnp.maximum(t - ncb, 0), 0)),
                  pl.BlockSpec((1, d), lambda i, t: (0, 0)),
                  pl.BlockSpec((None, None, 2, d), lambda i, t: (i, (t >= ncb).astype(jnp.int32), 0, 0))],
        out_specs=pl.BlockSpec((None, TOK, d), lambda i, t: (i, t, 0)),
        out_shape=jax.ShapeDtypeStruct((b, nt * TOK, d), BF16),
        compiler_params=_params(("parallel", "arbitrary")),
        name="prenorm",
    )(ctx, x, g, ss)


def _matmul_kernel(h_ref, w_ref, o_ref):
    o_ref[...] = _dot(h_ref[...], w_ref[...])


def _inproj_call(h, w):
    m, d = h.shape
    n = w.shape[1]
    tm = 1024 if m % 1024 == 0 else TOK
    tn = min(1024, d)
    return pl.pallas_call(
        _matmul_kernel,
        grid=(n // tn, m // tm),
        in_specs=[pl.BlockSpec((tm, d), lambda j, i: (i, 0)),
                  pl.BlockSpec((d, tn), lambda j, i: (0, j))],
        out_specs=pl.BlockSpec((tm, tn), lambda j, i: (i, j)),
        out_shape=jax.ShapeDtypeStruct((m, n), F32),
        compiler_params=_params(("parallel", "arbitrary")),
        name="inproj",
    )(h, w)


def _hgrn_constants():
    u = np.arange(CHUNK)
    mats, masks = [], []
    for rev in (False, True):
        pos = CHUNK - 1 - u if rev else u
        incl = (pos[None, :] <= pos[:, None]).astype(np.float32)
        blocks, lvl_masks = [incl], []
        for lev in range(N_LEVELS):
            half = CHUNK >> (lev + 1)
            ref = (pos // (2 * half)) * (2 * half) + half - 1
            upto_ref = (pos[None, :] <= ref[:, None]).astype(np.float32)
            if half in MXU_LEVEL_HALVES:
                late = (pos % (2 * half)) >= half
                blocks.append(np.where(late[:, None], 1.0, -1.0).astype(np.float32) * (incl - upto_ref))
            same = (pos[:, None] // (2 * half)) == (pos[None, :] // (2 * half))
            late_q = (pos[:, None] % (2 * half)) >= half
            early_k = (pos[None, :] % (2 * half)) < half
            lvl_masks.append((same & late_q & early_k).astype(np.float32))
        lvl_masks.append(np.eye(CHUNK, dtype=np.float32))
        mats.append(np.concatenate(blocks, axis=0))
        masks.append(np.stack(lvl_masks))
    return np.stack(mats), np.stack(masks)


def _hgrn_kernel(lbl_ref, q_ref, ff_ref, fb_ref, i_ref, gt_ref, cm_ref, mk_ref, ng_ref, y_ref,
                 of_ref, ob_ref, sf_ref, sb_ref, kd_ref, et_ref, qd_ref, sc_ref, *, n_ctx_chunks, n_lat_chunks):
    f_refs, o_refs, st_refs = (ff_ref, fb_ref), (of_ref, ob_ref), (sf_ref, sb_ref)
    lowers = []
    for dirn in range(2):
        la, lb_ = lbl_ref[dirn, 0], lbl_ref[dirn, 1]
        mx = jnp.maximum(la, lb_)
        ea, eb = jnp.exp(la - mx), jnp.exp(lb_ - mx)
        lowers.append(ea / (ea + eb))
        st_refs[dirn][...] = jnp.zeros(st_refs[dirn].shape, F32)
    row_odd = (lax.broadcasted_iota(jnp.int32, (CHUNK, 1), 0) % 2) == 1

    def chunk_rows(pos):
        return pl.ds(pl.multiple_of(pos * CHUNK, CHUNK), CHUNK)

    def level_decay(dirn, lev, f, sums):
        half = CHUNK >> (lev + 1)
        cum = sums[0:CHUNK]
        if half == 1:
            return jnp.where(row_odd, 1.0, f) if dirn else jnp.where(row_odd, f, 1.0)
        if half in MXU_LEVEL_HALVES:
            blk = 1 + MXU_LEVEL_HALVES.index(half)
            return jnp.exp(sums[blk * CHUNK:(blk + 1) * CHUNK])
        parts = []
        for p in range(0, CHUNK, 2 * half):
            ref = cum[p + half - 1 + dirn:p + half + dirn]
            lo, hi = cum[p:p + half], cum[p + half:p + 2 * half]
            parts += [lo - ref, ref - hi] if dirn else [ref - lo, hi - ref]
        return jnp.exp(jnp.concatenate(parts, axis=0))

    def prepare(pos0, n, with_out):
        items = [(dirn, pos0 + c) for c in range(n) for dirn in range(2)]
        fs, kks, splits = [], [], []
        for dirn, pos in items:
            lower = lowers[dirn]
            f = lower + (1.0 - lower) * _sigmoid(f_refs[dirn][chunk_rows(pos), :])
            g = jnp.log(f)
            g1 = g.astype(BF16)
            r1 = g - g1.astype(F32)
            g2 = r1.astype(BF16)
            g3 = (r1 - g2.astype(F32)).astype(BF16)
            fs.append(f)
            kks.append(1.0 - f)
            splits.append(jnp.concatenate([g1, g2, g3], axis=0))
        sums = [None] * len(items)
        for dirn in range(2):
            idx = [j for j, it in enumerate(items) if it[0] == dirn]
            wide = _dot(cm_ref[dirn], jnp.concatenate([splits[j] for j in idx], axis=1))
            for c, j in enumerate(idx):
                sums[j] = wide[:, c * HEAD_DIM:(c + 1) * HEAD_DIM]
        for j, (dirn, pos) in enumerate(items):
            cum = sums[j][0:CHUNK]
            last = 0 if dirn else CHUNK - 1
            tot = cum[last:last + 1]
            kd_ref[dirn, chunk_rows(pos), :] = (kks[j] * jnp.exp(tot - cum)).astype(BF16)
            et_ref[dirn, pos] = jnp.broadcast_to(jnp.exp(tot), et_ref.shape[2:])
        if not with_out:
            return
        qs = [q_ref[chunk_rows(pos), :] for _, pos in items]
        for j, (dirn, pos) in enumerate(items):
            qd_ref[dirn, chunk_rows(pos - n_ctx_chunks), :] = (qs[j] * jnp.exp(sums[j][0:CHUNK])).astype(BF16)
        qbs = [qv.astype(BF16) for qv in qs]
        kbs = [kv.astype(BF16) for kv in kks]
        scs = [mk_ref[dirn, N_LEVELS] * _dot_nt(qbs[j], kbs[j]) for j, (dirn, _) in enumerate(items)]
        for lev in range(N_LEVELS):
            for j, (dirn, _) in enumerate(items):
                dl = level_decay(dirn, lev, fs[j], sums[j]).astype(BF16)
                scs[j] = scs[j] + mk_ref[dirn, lev] * _dot_nt(qbs[j] * dl, kbs[j] * dl)
        for j, (dirn, pos) in enumerate(items):
            sc_ref[dirn, chunk_rows(pos - n_ctx_chunks), :] = scs[j].astype(BF16)

    def scan(pos_f, pos_b, n, with_out):
        items = [(dirn, (pos_b - c) if dirn else (pos_f + c)) for c in range(n) for dirn in range(2)]
        vs = [i_ref[chunk_rows(pos), :] for _, pos in items]
        ups = [_dot(vs[j].T.astype(BF16), kd_ref[dirn, chunk_rows(pos), :]) for j, (dirn, pos) in enumerate(items)]
        sts = [st_refs[0][...], st_refs[1][...]]
        before = []
        for j, (dirn, pos) in enumerate(items):
            before.append(sts[dirn])
            sts[dirn] = sts[dirn] * et_ref[dirn, pos][0:1] + ups[j]
        for dirn in range(2):
            st_refs[dirn][...] = sts[dirn]
        if not with_out:
            return
        for j, (dirn, pos) in enumerate(items):
            orow = chunk_rows(pos - n_ctx_chunks)
            o_refs[dirn][orow, :] = (_dot_nt(qd_ref[dirn, orow, :], before[j].astype(BF16))
                                     + _dot(sc_ref[dirn, orow, :], vs[j].astype(BF16)))

    def loop(n, body):
        lax.fori_loop(0, n, lambda ci, c: (body(ci), c)[1], 0)

    last_pos = n_ctx_chunks + n_lat_chunks - 1
    prepare(0, n_ctx_chunks, False)
    loop(n_lat_chunks // PREP_CHUNKS, lambda ci: prepare(n_ctx_chunks + ci * PREP_CHUNKS, PREP_CHUNKS, True))
    scan(0, n_ctx_chunks - 1, n_ctx_chunks, False)
    loop(n_lat_chunks // SCAN_CHUNKS,
         lambda ci: scan(n_ctx_chunks + ci * SCAN_CHUNKS, last_pos - ci * SCAN_CHUNKS, SCAN_CHUNKS, True))

    def readout(i, carry):
        rows = pl.ds(pl.multiple_of(i * TOK, TOK), TOK)
        ot = of_ref[rows, :] + ob_ref[rows, :]
        on = ot * lax.rsqrt(jnp.mean(ot * ot, axis=-1, keepdims=True) + EPS) * ng_ref[...]
        gate = gt_ref[pl.ds(pl.multiple_of(n_ctx_chunks * CHUNK + i * TOK, CHUNK), TOK), :]
        y_ref[rows, :] = (on * (gate * _sigmoid(gate))).astype(BF16)
        return carry

    lax.fori_loop(0, n_lat_chunks * CHUNK // TOK, readout, 0)


def _hgrn_call(p3, lb_logits, norm_g, n_ctx, l):
    b, t, _ = p3.shape
    cm, mk = _hgrn_constants()
    cm = jnp.asarray(np.concatenate([cm] * 3, axis=-1), BF16)
    mk = jnp.asarray(mk, F32)
    lbl = lb_logits.reshape(2, 2, HEADS, 1, HEAD_DIM)
    tok_spec = lambda grp: pl.BlockSpec((None, t, HEAD_DIM), lambda i, h: (i, 0, grp * HEADS + h))
    full = lambda shape: pl.BlockSpec(shape, lambda i, h: (0,) * len(shape))
    return pl.pallas_call(
        functools.partial(_hgrn_kernel, n_ctx_chunks=n_ctx // CHUNK, n_lat_chunks=l // CHUNK),
        grid=(b, HEADS),
        in_specs=[pl.BlockSpec((2, 2, None, 1, HEAD_DIM), lambda i, h: (0, 0, h, 0, 0)),
                  tok_spec(0), tok_spec(1), tok_spec(2), tok_spec(3), tok_spec(4),
                  full((2, 3 * CHUNK, 3 * CHUNK)),
                  full((2, N_LEVELS + 1, CHUNK, CHUNK)),
                  full((1, HEAD_DIM))],
        out_specs=pl.BlockSpec((None, l, HEAD_DIM), lambda i, h: (i, 0, h)),
        out_shape=jax.ShapeDtypeStruct((b, l, MIX_W), BF16),
        scratch_shapes=[pltpu.VMEM((l, HEAD_DIM), F32), pltpu.VMEM((l, HEAD_DIM), F32),
                        pltpu.VMEM((HEAD_DIM, HEAD_DIM), F32), pltpu.VMEM((HEAD_DIM, HEAD_DIM), F32),
                        pltpu.VMEM((2, t, HEAD_DIM), BF16), pltpu.VMEM((2, t // CHUNK, 8, HEAD_DIM), F32),
                        pltpu.VMEM((2, l, HEAD_DIM), BF16), pltpu.VMEM((2, l, CHUNK), BF16)],
        compiler_params=_params(("parallel", "parallel")),
        name="hgrn",
    )(lbl, p3, p3, p3, p3, p3, cm, mk, norm_g)


def _rope_tables(l):
    pos = np.arange(l)
    nf = HEAD_DIM // 4
    inv = ROPE_THETA ** (-np.arange(nf, dtype=np.float64) / nf)
    ang_r = (pos // GRID_W)[:, None] * inv[None, :]
    ang_c = (pos % GRID_W)[:, None] * inv[None, :]
    cos = np.concatenate([np.cos(ang_r)] * 2 + [np.cos(ang_c)] * 2, axis=-1)
    sin = np.concatenate([-np.sin(ang_r), np.sin(ang_r), -np.sin(ang_c), np.sin(ang_c)], axis=-1)
    return jnp.asarray(cos, F32), jnp.asarray(sin, F32)


def _na_build_bias(rb_ref, tb_ref):
    lanes = 2 * GRID_W
    cq = lax.broadcasted_iota(jnp.int32, (GRID_W, lanes), 0)
    lane = lax.broadcasted_iota(jnp.int32, (GRID_W, lanes), 1)
    ck = lane % GRID_W
    cs = jnp.clip(cq - WIN_C // 2, 0, GRID_W - WIN_C)
    col_in = (ck >= cs) & (ck < cs + WIN_C)

    def toeplitz(d, lane0):
        row = jnp.broadcast_to(rb_ref[d:d + 1, :], (GRID_W, lanes))
        return pltpu.roll(row, (lane0 - (WIN_C - 1)) % lanes, 1, stride=1, stride_axis=0)

    pair = [jnp.where(col_in, jnp.where(lane < GRID_W, toeplitz(d, 0), toeplitz(d + 1, GRID_W)), MASKED)
            for d in range(2 * WIN_R - 2)]
    for e in range(WIN_R):
        for p in range(WIN_R // 2):
            tb_ref[e, :, p * lanes:(p + 1) * lanes] = pair[2 * p - e + WIN_R - 1]


def _na_kernel(q_ref, k_ref, v_ref, cos_ref, sin_ref, rb_ref, gq_ref, gk_ref, o_ref,
               qs, ks, vs, kcs, vcs, tb_ref, *, n_ctx, grid_rows):
    lane = lax.broadcasted_iota(jnp.int32, (1, HEAD_DIM), 1)
    first_half = (lane % (HEAD_DIM // 2)) < (HEAD_DIM // 4)
    _na_build_bias(rb_ref, tb_ref)

    def normed(tv, g):
        return tv * lax.rsqrt(jnp.mean(tv * tv, axis=-1, keepdims=True) + EPS) * g

    def rope(tv, cs, sn):
        partner = jnp.where(first_half, pltpu.roll(tv, 3 * HEAD_DIM // 4, 1), pltpu.roll(tv, HEAD_DIM // 4, 1))
        return tv * cs + partner * sn

    gq = gq_ref[...]
    gk = gk_ref[...]
    kcs[...] = normed(k_ref[0:n_ctx, :], gk).astype(BF16)
    vcs[...] = v_ref[0:n_ctx, :].astype(BF16)

    def prep(i, carry):
        r0 = pl.multiple_of(i * TOK, TOK)
        rows = pl.ds(r0, TOK)
        src = pl.ds(n_ctx + r0, TOK)
        cs, sn = cos_ref[rows, :], sin_ref[rows, :]
        qs[rows, :] = rope(normed(q_ref[src, :], gq), cs, sn).astype(BF16)
        ks[rows, :] = rope(normed(k_ref[src, :], gk), cs, sn).astype(BF16)
        vs[rows, :] = v_ref[src, :].astype(BF16)
        return carry

    lax.fori_loop(0, grid_rows * GRID_W // TOK, prep, 0)
    scale = HEAD_DIM ** -0.5
    band = WIN_R * GRID_W

    def rows_step(i, carry):
        rr = [i * NA_ROWS + n for n in range(NA_ROWS)]
        rs = [jnp.clip(r - WIN_R // 2, 0, grid_rows - WIN_R) for r in rr]
        qrow = [pl.ds(pl.multiple_of(r * GRID_W, GRID_W), GRID_W) for r in rr]
        krow = [pl.ds(pl.multiple_of(s * GRID_W, GRID_W), band) for s in rs]
        qr = [qs[qw, :] for qw in qrow]
        kc = kcs[...]
        sb = [_dot_nt(qr[n], ks[krow[n], :]) * scale + tb_ref[rr[n] - rs[n]] for n in range(NA_ROWS)]
        sc = [_dot_nt(qr[n], kc) * scale for n in range(NA_ROWS)]
        m = [jnp.maximum(jnp.max(sb[n], axis=-1, keepdims=True), jnp.max(sc[n], axis=-1, keepdims=True))
             for n in range(NA_ROWS)]
        pb = [jnp.exp(sb[n] - m[n]) for n in range(NA_ROWS)]
        pc = [jnp.exp(sc[n] - m[n]) for n in range(NA_ROWS)]
        den = [jnp.sum(pb[n], axis=-1, keepdims=True) + jnp.sum(pc[n], axis=-1, keepdims=True)
               for n in range(NA_ROWS)]
        vc = vcs[...]
        o = [_dot(pb[n].astype(BF16), vs[krow[n], :]) + _dot(pc[n].astype(BF16), vc) for n in range(NA_ROWS)]
        for n in range(NA_ROWS):
            o_ref[qrow[n], :] = (o[n] / den[n]).astype(BF16)
        return carry

    lax.fori_loop(0, grid_rows // NA_ROWS, rows_step, 0)


def _na_call(p3, rel_bias, gq, gk, n_ctx, l):
    b, t, _ = p3.shape
    cos, sin = _rope_tables(l)
    nr, nc = rel_bias.shape[1:]
    rb = jnp.zeros((HEADS, 2 * WIN_R, 2 * GRID_W), F32).at[:, :nr, :nc].set(rel_bias)
    col = lambda grp: (lambda i, h: (i, 0, grp * HEADS + h))
    tok_spec = lambda im: pl.BlockSpec((None, t, HEAD_DIM), im)
    full = lambda shape: pl.BlockSpec(shape, lambda i, h: (0,) * len(shape))
    return pl.pallas_call(
        functools.partial(_na_kernel, n_ctx=n_ctx, grid_rows=l // GRID_W),
        grid=(b, HEADS),
        in_specs=[tok_spec(col(5)), tok_spec(col(6)), tok_spec(col(7)),
                  full((l, HEAD_DIM)), full((l, HEAD_DIM)),
                  pl.BlockSpec((None, 2 * WIN_R, 2 * GRID_W), lambda i, h: (h, 0, 0)),
                  full((1, HEAD_DIM)), full((1, HEAD_DIM))],
        out_specs=pl.BlockSpec((None, l, HEAD_DIM), lambda i, h: (i, 0, h)),
        out_shape=jax.ShapeDtypeStruct((b, l, MIX_W), BF16),
        scratch_shapes=[pltpu.VMEM((l, HEAD_DIM), BF16), pltpu.VMEM((l, HEAD_DIM), BF16),
                        pltpu.VMEM((l, HEAD_DIM), BF16), pltpu.VMEM((n_ctx, HEAD_DIM), BF16),
                        pltpu.VMEM((n_ctx, HEAD_DIM), BF16),
                        pltpu.VMEM((WIN_R, GRID_W, WIN_R * GRID_W), F32)],
        compiler_params=_params(("parallel", "parallel")),
        name="na",
    )(p3, p3, p3, cos, sin, rb, gq, gk)


def _merge_kernel(ya_ref, yb_ref, ga_ref, gb_ref, x_ref, wa_ref, wb_ref, wo_ref, mod_ref, n2_ref,
                  xm_ref, h2_ref):
    za = _dot(ya_ref[...], wa_ref[...])
    zb = _dot(yb_ref[...], wb_ref[...])
    z = _sigmoid(ga_ref[...]) * za + _sigmoid(gb_ref[...]) * zb
    xm = x_ref[...] + mod_ref[0:1, :] * _dot(z.astype(BF16), wo_ref[...])
    xm_ref[...] = xm
    y = xm * lax.rsqrt(jnp.mean(xm * xm, axis=-1, keepdims=True) + EPS) * n2_ref[...]
    h2_ref[...] = (y * (1.0 + mod_ref[2:3, :]) + mod_ref[1:2, :]).astype(BF16)


def _merge_call(ya, yb, p3, x, wa, wb, wo, mod3, n2, n_ctx):
    b, l, d = x.shape
    ncb = n_ctx // TOK
    gate_blk = MIX_W * 8 // d
    full = lambda shape: pl.BlockSpec(shape, lambda i, t: (0,) * len(shape))
    return pl.pallas_call(
        _merge_kernel,
        grid=(b, l // TOK),
        in_specs=[pl.BlockSpec((None, TOK, MIX_W), lambda i, t: (i, t, 0)),
                  pl.BlockSpec((None, TOK, MIX_W), lambda i, t: (i, t, 0)),
                  pl.BlockSpec((None, TOK, d), lambda i, t: (i, t + ncb, gate_blk)),
                  pl.BlockSpec((None, TOK, d), lambda i, t: (i, t + ncb, gate_blk + 1)),
                  pl.BlockSpec((None, TOK, d), lambda i, t: (i, t, 0)),
                  full((MIX_W, d)), full((MIX_W, d)), full((d, d)),
                  pl.BlockSpec((None, 3, d), lambda i, t: (i, 0, 0)),
                  full((1, d))],
        out_specs=[pl.BlockSpec((None, TOK, d), lambda i, t: (i, t, 0)),
                   pl.BlockSpec((None, TOK, d), lambda i, t: (i, t, 0))],
        out_shape=[jax.ShapeDtypeStruct((b, l, d), F32), jax.ShapeDtypeStruct((b, l, d), BF16)],
        compiler_params=_params(("parallel", "arbitrary")),
        name="merge",
    )(ya, yb, p3, p3, x, wa, wb, wo, mod3, n2)


def _ffn_kernel(h_ref, hp_ref, hn_ref, w1_ref, w3_ref, cw_ref, cb_ref, w2_ref, xm_ref, g2_ref, o_ref, a_ref,
                *, tiles_per_seq, halo):
    i = pl.program_id(0)
    s = pl.program_id(1)
    n_hid, tm, th = a_ref.shape

    @pl.when(s < n_hid)
    def _():
        w1 = w1_ref[...]
        t1 = _dot(h_ref[...], w1)
        prev_row = _dot(hp_ref[...], w1)[halo - 1:halo, :]
        next_row = _dot(hn_ref[...], w1)[0:1, :]
        prev_row = jnp.where(i % tiles_per_seq == 0, 0.0, prev_row)
        next_row = jnp.where(i % tiles_per_seq == tiles_per_seq - 1, 0.0, next_row)
        ridx = lax.broadcasted_iota(jnp.int32, (tm, 1), 0)
        up = jnp.where(ridx == 0, prev_row, pltpu.roll(t1, 1, 0))
        dn = jnp.where(ridx == tm - 1, next_row, pltpu.roll(t1, tm - 1, 0))
        u = up * cw_ref[0:1, :] + t1 * cw_ref[1:2, :] + dn * cw_ref[2:3, :] + cb_ref[...]
        a_ref[s] = ((u * _sigmoid(u)) * _dot(h_ref[...], w3_ref[...])).astype(BF16)

    @pl.when(s >= n_hid)
    def _():
        acc = _dot(a_ref[0], w2_ref[0:th, :])
        for k in range(1, n_hid):
            acc = acc + _dot(a_ref[k], w2_ref[k * th:(k + 1) * th, :])
        o_ref[...] = xm_ref[...] + g2_ref[...] * acc


def _ffn_call(h2, xm, w1, w3, cw, cb, w2, g2, l):
    m, d = h2.shape
    hid = w1.shape[1]
    tm = 1024
    th = min(512, hid)
    tn = 256
    n_hid = hid // th
    halo = 16
    per = tm // halo
    nblk = m // halo
    hcol = lambda i, s: (0, jnp.minimum(s, n_hid - 1))
    ocol = lambda s: jnp.maximum(s - n_hid, 0)
    return pl.pallas_call(
        functools.partial(_ffn_kernel, tiles_per_seq=l // tm, halo=halo),
        grid=(m // tm, n_hid + d // tn),
        in_specs=[pl.BlockSpec((tm, d), lambda i, s: (i, 0)),
                  pl.BlockSpec((halo, d), lambda i, s: (jnp.maximum(i * per - 1, 0), 0)),
                  pl.BlockSpec((halo, d), lambda i, s: (jnp.minimum((i + 1) * per, nblk - 1), 0)),
                  pl.BlockSpec((d, th), hcol),
                  pl.BlockSpec((d, th), hcol),
                  pl.BlockSpec((3, th), hcol),
                  pl.BlockSpec((1, th), hcol),
                  pl.BlockSpec((hid, tn), lambda i, s: (0, ocol(s))),
                  pl.BlockSpec((tm, tn), lambda i, s: (i, ocol(s))),
                  pl.BlockSpec((None, 1, tn), lambda i, s: (i // (l // tm), 0, ocol(s)))],
        out_specs=pl.BlockSpec((tm, tn), lambda i, s: (i, ocol(s))),
        out_shape=jax.ShapeDtypeStruct((m, d), F32),
        scratch_shapes=[pltpu.VMEM((n_hid, tm, th), BF16)],
        compiler_params=_params(("parallel", "arbitrary")),
        name="ffn",
    )(h2, h2, h2, w1, w3, cw, cb, w2, xm, g2)


def kernel(x, c, ctx, c_ctx, ada_w, ada_b, norm1_g, norm2_g, w_in, hgrn_lb_logits, hgrn_norm_g,
           na_q_norm_g, na_k_norm_g, na_rel_bias, w_branch_a, w_branch_b, w_out,
           ffn_w1, ffn_w3, ffn_conv_w, ffn_conv_b, ffn_w2):
    b, l, d = x.shape
    n_ctx = ctx.shape[1]
    assert ada_w.shape[0] == 1 and b + 1 <= 8 and n_ctx % TOK == 0 and l % (2 * TOK) == 0

    c_all = jnp.zeros((8, d), F32).at[:b].set(c).at[b].set(c_ctx)
    mod = _mod_call(c_all, ada_w[0], ada_b).reshape(8, N_MOD, d)
    sh1, sc1, g1, sh2, sc2, g2 = (mod[:b, n] for n in range(N_MOD))
    ss_ctx = jnp.broadcast_to(mod[b, 0:2][None], (b, 2, d))
    ss1 = jnp.stack([ss_ctx, jnp.stack([sh1, sc1], axis=1)], axis=1)

    h = _prenorm_call(ctx, x, norm1_g, ss1)
    t = n_ctx + l
    p = _inproj_call(h.reshape(b * t, d), w_in[0].astype(BF16))
    p3 = p.reshape(b, t, p.shape[1])

    y_a = _hgrn_call(p3, hgrn_lb_logits, hgrn_norm_g, n_ctx, l)
    y_b = _na_call(p3, na_rel_bias[0], na_q_norm_g, na_k_norm_g, n_ctx, l)

    mod3 = jnp.stack([g1, sh2, sc2], axis=1)
    x_mid, h2 = _merge_call(y_a, y_b, p3, x, w_branch_a[0].astype(BF16), w_branch_b[0].astype(BF16),
                            w_out[0].astype(BF16), mod3, norm2_g, n_ctx)

    out = _ffn_call(h2.reshape(b * l, d), x_mid.reshape(b * l, d), ffn_w1[0].astype(BF16),
                    ffn_w3[0].astype(BF16), ffn_conv_w[0], ffn_conv_b, ffn_w2[0].astype(BF16),
                    g2[:, None, :], l)
    return out.reshape(b, l, d)
```

```python
import functools

import numpy as np
import jax
import jax.numpy as jnp
from jax import lax
from jax.experimental import pallas as pl
from jax.experimental.pallas import tpu as pltpu

GRID_W = 64
HEADS = 8
HEAD_DIM = 128
MIX_W = HEADS * HEAD_DIM
CHUNK = 64
N_LEVELS = 6
MXU_LEVEL_HALVES = (4, 2)
PREP_CHUNKS = 4
SCAN_CHUNKS = 4
NA_ROWS = 8
WIN_R = 8
WIN_C = 16
ROPE_THETA = 10000.0
N_MOD = 6
EPS = 1e-6
TOK = 256
MASKED = -1e30
V7X_VMEM_LIMIT = 56 * 1024 * 1024

BF16 = jnp.bfloat16
F32 = jnp.float32


def _dot(a, b):
    return jnp.dot(a, b, preferred_element_type=F32)


def _dot_nt(a, b):
    return lax.dot_general(a, b, (((1,), (1,)), ((), ())), preferred_element_type=F32)


def _sigmoid(t):
    return 1.0 / (1.0 + jnp.exp(-t))


def _params(sem):
    return pltpu.CompilerParams(dimension_semantics=sem, vmem_limit_bytes=V7X_VMEM_LIMIT)


def _mod_kernel(c_ref, w_ref, b_ref, o_ref):
    cv = c_ref[...]
    s = (cv * _sigmoid(cv)).astype(BF16)
    o_ref[...] = _dot(s, w_ref[...].astype(BF16)) + b_ref[...]


def _mod_call(c_all, w, b):
    d, n = w.shape
    tn = min(1024, d)
    return pl.pallas_call(
        _mod_kernel,
        grid=(n // tn,),
        in_specs=[pl.BlockSpec((8, d), lambda j: (0, 0)),
                  pl.BlockSpec((d, tn), lambda j: (0, j)),
                  pl.BlockSpec((1, tn), lambda j: (0, j))],
        out_specs=pl.BlockSpec((8, tn), lambda j: (0, j)),
        out_shape=jax.ShapeDtypeStruct((8, n), F32),
        compiler_params=_params(("arbitrary",)),
        name="mod",
    )(c_all, w, b)


def _prenorm_kernel(ctx_ref, x_ref, g_ref, ss_ref, o_ref, *, n_ctx_blk):
    t = pl.program_id(1)
    xv = jnp.where(t < n_ctx_blk, ctx_ref[...], x_ref[...])
    y = xv * lax.rsqrt(jnp.mean(xv * xv, axis=-1, keepdims=True) + EPS) * g_ref[...]
    o_ref[...] = (y * (1.0 + ss_ref[1:2, :]) + ss_ref[0:1, :]).astype(BF16)


def _prenorm_call(ctx, x, g, ss):
    b, l, d = x.shape
    ncb = ctx.shape[1] // TOK
    nt = ncb + l // TOK
    return pl.pallas_call(
        functools.partial(_prenorm_kernel, n_ctx_blk=ncb),
        grid=(b, nt),
        in_specs=[pl.BlockSpec((None, TOK, d), lambda i, t: (i, jnp.minimum(t, ncb - 1), 0)),
                  pl.BlockSpec((None, TOK, d), lambda i, t: (i, jnp.maximum(t - ncb, 0), 0)),
                  pl.BlockSpec((1, d), lambda i, t: (0, 0)),
                  pl.BlockSpec((None, None, 2, d), lambda i, t: (i, (t >= ncb).astype(jnp.int32), 0, 0))],
        out_specs=pl.BlockSpec((None, TOK, d), lambda i, t: (i, t, 0)),
        out_shape=jax.ShapeDtypeStruct((b, nt * TOK, d), BF16),
        compiler_params=_params(("parallel", "arbitrary")),
        name="prenorm",
    )(ctx, x, g, ss)


def _inproj_kernel(h_ref, w_ref, o_ref, wb_ref):
    @pl.when(pl.program_id(1) == 0)
    def _():
        wb_ref[...] = w_ref[...].astype(BF16)

    o_ref[...] = _dot(h_ref[...], wb_ref[...])


def _inproj_call(h, w):
    m, d = h.shape
    n = w.shape[1]
    tm = 1024 if m % 1024 == 0 else TOK
    tn = min(1024, d)
    return pl.pallas_call(
        _inproj_kernel,
        grid=(n // tn, m // tm),
        in_specs=[pl.BlockSpec((tm, d), lambda j, i: (i, 0)),
                  pl.BlockSpec((d, tn), lambda j, i: (0, j))],
        out_specs=pl.BlockSpec((tm, tn), lambda j, i: (i, j)),
        out_shape=jax.ShapeDtypeStruct((m, n), F32),
        scratch_shapes=[pltpu.VMEM((d, tn), BF16)],
        compiler_params=_params(("parallel", "arbitrary")),
        name="inproj",
    )(h, w)


def _hgrn_constants():
    u = np.arange(CHUNK)
    mats, masks = [], []
    for rev in (False, True):
        pos = CHUNK - 1 - u if rev else u
        incl = (pos[None, :] <= pos[:, None]).astype(np.float32)
        blocks, lvl_masks = [incl], []
        for lev in range(N_LEVELS):
            half = CHUNK >> (lev + 1)
            ref = (pos // (2 * half)) * (2 * half) + half - 1
            upto_ref = (pos[None, :] <= ref[:, None]).astype(np.float32)
            if half in MXU_LEVEL_HALVES:
                late = (pos % (2 * half)) >= half
                blocks.append(np.where(late[:, None], 1.0, -1.0).astype(np.float32) * (incl - upto_ref))
            same = (pos[:, None] // (2 * half)) == (pos[None, :] // (2 * half))
            late_q = (pos[:, None] % (2 * half)) >= half
            early_k = (pos[None, :] % (2 * half)) < half
            lvl_masks.append((same & late_q & early_k).astype(np.float32))
        lvl_masks.append(np.eye(CHUNK, dtype=np.float32))
        mats.append(np.concatenate(blocks, axis=0))
        masks.append(np.stack(lvl_masks))
    return np.stack(mats), np.stack(masks)


def _hgrn_kernel(lbl_ref, q_ref, ff_ref, fb_ref, i_ref, gt_ref, cm_ref, mk_ref, ng_ref, y_ref,
                 of_ref, ob_ref, sf_ref, sb_ref, kd_ref, et_ref, qd_ref, sc_ref, *, n_ctx_chunks, n_lat_chunks):
    f_refs, o_refs, st_refs = (ff_ref, fb_ref), (of_ref, ob_ref), (sf_ref, sb_ref)
    lowers = []
    for dirn in range(2):
        la, lb_ = lbl_ref[dirn, 0], lbl_ref[dirn, 1]
        mx = jnp.maximum(la, lb_)
        ea, eb = jnp.exp(la - mx), jnp.exp(lb_ - mx)
        lowers.append(ea / (ea + eb))
        st_refs[dirn][...] = jnp.zeros(st_refs[dirn].shape, F32)
    row_odd = (lax.broadcasted_iota(jnp.int32, (CHUNK, 1), 0) % 2) == 1

    def chunk_rows(pos):
        return pl.ds(pl.multiple_of(pos * CHUNK, CHUNK), CHUNK)

    def level_decay(dirn, lev, f, sums):
        half = CHUNK >> (lev + 1)
        cum = sums[0:CHUNK]
        if half == 1:
            return jnp.where(row_odd, 1.0, f) if dirn else jnp.where(row_odd, f, 1.0)
        if half in MXU_LEVEL_HALVES:
            blk = 1 + MXU_LEVEL_HALVES.index(half)
            return jnp.exp(sums[blk * CHUNK:(blk + 1) * CHUNK])
        parts = []
        for p in range(0, CHUNK, 2 * half):
            ref = cum[p + half - 1 + dirn:p + half + dirn]
            lo, hi = cum[p:p + half], cum[p + half:p + 2 * half]
            parts += [lo - ref, ref - hi] if dirn else [ref - lo, hi - ref]
        return jnp.exp(jnp.concatenate(parts, axis=0))

    def prepare(pos0, n, with_out):
        items = [(dirn, pos0 + c) for c in range(n) for dirn in range(2)]
        fs, kks, splits = [], [], []
        for dirn, pos in items:
            lower = lowers[dirn]
            f = lower + (1.0 - lower) * _sigmoid(f_refs[dirn][chunk_rows(pos), :])
            g = jnp.log(f)
            g1 = g.astype(BF16)
            r1 = g - g1.astype(F32)
            g2 = r1.astype(BF16)
            g3 = (r1 - g2.astype(F32)).astype(BF16)
            fs.append(f)
            kks.append(1.0 - f)
            splits.append(jnp.concatenate([g1, g2, g3], axis=0))
        sums = [None] * len(items)
        for dirn in range(2):
            idx = [j for j, it in enumerate(items) if it[0] == dirn]
            wide = _dot(cm_ref[dirn], jnp.concatenate([splits[j] for j in idx], axis=1))
            for c, j in enumerate(idx):
                sums[j] = wide[:, c * HEAD_DIM:(c + 1) * HEAD_DIM]
        for j, (dirn, pos) in enumerate(items):
            cum = sums[j][0:CHUNK]
            last = 0 if dirn else CHUNK - 1
            tot = cum[last:last + 1]
            kd_ref[dirn, chunk_rows(pos), :] = (kks[j] * jnp.exp(tot - cum)).astype(BF16)
            et_ref[dirn, pos] = jnp.broadcast_to(jnp.exp(tot), et_ref.shape[2:])
        if not with_out:
            return
        qs = [q_ref[chunk_rows(pos), :] for _, pos in items]
        for j, (dirn, pos) in enumerate(items):
            qd_ref[dirn, chunk_rows(pos - n_ctx_chunks), :] = (qs[j] * jnp.exp(sums[j][0:CHUNK])).astype(BF16)
        qbs = [qv.astype(BF16) for qv in qs]
        kbs = [kv.astype(BF16) for kv in kks]
        scs = [mk_ref[dirn, N_LEVELS] * _dot_nt(qbs[j], kbs[j]).astype(BF16) for j, (dirn, _) in enumerate(items)]
        for lev in range(N_LEVELS):
            for j, (dirn, _) in enumerate(items):
                dl = level_decay(dirn, lev, fs[j], sums[j]).astype(BF16)
                scs[j] = scs[j] + mk_ref[dirn, lev] * _dot_nt(qbs[j] * dl, kbs[j] * dl).astype(BF16)
        for j, (dirn, pos) in enumerate(items):
            sc_ref[dirn, chunk_rows(pos - n_ctx_chunks), :] = scs[j]

    def scan(pos_f, pos_b, n, with_out):
        items = [(dirn, (pos_b - c) if dirn else (pos_f + c)) for c in range(n) for dirn in range(2)]
        vs = [i_ref[chunk_rows(pos), :] for _, pos in items]
        ups = [_dot(vs[j].T.astype(BF16), kd_ref[dirn, chunk_rows(pos), :]) for j, (dirn, pos) in enumerate(items)]
        sts = [st_refs[0][...], st_refs[1][...]]
        before = []
        for j, (dirn, pos) in enumerate(items):
            before.append(sts[dirn])
            sts[dirn] = sts[dirn] * et_ref[dirn, pos][0:1] + ups[j]
        for dirn in range(2):
            st_refs[dirn][...] = sts[dirn]
        if not with_out:
            return
        for j, (dirn, pos) in enumerate(items):
            orow = chunk_rows(pos - n_ctx_chunks)
            o_refs[dirn][orow, :] = (_dot_nt(qd_ref[dirn, orow, :], before[j].astype(BF16))
                                     + _dot(sc_ref[dirn, orow, :], vs[j].astype(BF16)))

    def loop(n, body):
        lax.fori_loop(0, n, lambda ci, c: (body(ci), c)[1], 0)

    last_pos = n_ctx_chunks + n_lat_chunks - 1
    prepare(0, n_ctx_chunks, False)
    loop(n_lat_chunks // PREP_CHUNKS, lambda ci: prepare(n_ctx_chunks + ci * PREP_CHUNKS, PREP_CHUNKS, True))
    scan(0, n_ctx_chunks - 1, n_ctx_chunks, False)
    loop(n_lat_chunks // SCAN_CHUNKS,
         lambda ci: scan(n_ctx_chunks + ci * SCAN_CHUNKS, last_pos - ci * SCAN_CHUNKS, SCAN_CHUNKS, True))

    def readout(i, carry):
        rows = pl.ds(pl.multiple_of(i * TOK, TOK), TOK)
        ot = of_ref[rows, :] + ob_ref[rows, :]
        on = ot * lax.rsqrt(jnp.mean(ot * ot, axis=-1, keepdims=True) + EPS) * ng_ref[...]
        gate = gt_ref[pl.ds(pl.multiple_of(n_ctx_chunks * CHUNK + i * TOK, CHUNK), TOK), :]
        y_ref[rows, :] = (on * (gate * _sigmoid(gate))).astype(BF16)
        return carry

    lax.fori_loop(0, n_lat_chunks * CHUNK // TOK, readout, 0)


def _hgrn_call(p3, lb_logits, norm_g, n_ctx, l):
    b, t, _ = p3.shape
    cm, mk = _hgrn_constants()
    cm = jnp.asarray(np.concatenate([cm] * 3, axis=-1), BF16)
    mk = jnp.asarray(mk, BF16)
    lbl = lb_logits.reshape(2, 2, HEADS, 1, HEAD_DIM)
    tok_spec = lambda grp: pl.BlockSpec((None, t, HEAD_DIM), lambda i, h: (i, 0, grp * HEADS + h))
    full = lambda shape: pl.BlockSpec(shape, lambda i, h: (0,) * len(shape))
    return pl.pallas_call(
        functools.partial(_hgrn_kernel, n_ctx_chunks=n_ctx // CHUNK, n_lat_chunks=l // CHUNK),
        grid=(b, HEADS),
        in_specs=[pl.BlockSpec((2, 2, None, 1, HEAD_DIM), lambda i, h: (0, 0, h, 0, 0)),
                  tok_spec(0), tok_spec(1), tok_spec(2), tok_spec(3), tok_spec(4),
                  full((2, 3 * CHUNK, 3 * CHUNK)),
                  full((2, N_LEVELS + 1, CHUNK, CHUNK)),
                  full((1, HEAD_DIM))],
        out_specs=pl.BlockSpec((None, l, HEAD_DIM), lambda i, h: (i, 0, h)),
        out_shape=jax.ShapeDtypeStruct((b, l, MIX_W), BF16),
        scratch_shapes=[pltpu.VMEM((l, HEAD_DIM), F32), pltpu.VMEM((l, HEAD_DIM), F32),
                        pltpu.VMEM((HEAD_DIM, HEAD_DIM), F32), pltpu.VMEM((HEAD_DIM, HEAD_DIM), F32),
                        pltpu.VMEM((2, t, HEAD_DIM), BF16), pltpu.VMEM((2, t // CHUNK, 8, HEAD_DIM), F32),
                        pltpu.VMEM((2, l, HEAD_DIM), BF16), pltpu.VMEM((2, l, CHUNK), BF16)],
        compiler_params=_params(("parallel", "parallel")),
        name="hgrn",
    )(lbl, p3, p3, p3, p3, p3, cm, mk, norm_g)


def _rope_tables(l):
    pos = np.arange(l)
    nf = HEAD_DIM // 4
    inv = ROPE_THETA ** (-np.arange(nf, dtype=np.float64) / nf)
    ang_r = (pos // GRID_W)[:, None] * inv[None, :]
    ang_c = (pos % GRID_W)[:, None] * inv[None, :]
    cos = np.concatenate([np.cos(ang_r)] * 2 + [np.cos(ang_c)] * 2, axis=-1)
    sin = np.concatenate([-np.sin(ang_r), np.sin(ang_r), -np.sin(ang_c), np.sin(ang_c)], axis=-1)
    return jnp.asarray(cos, F32), jnp.asarray(sin, F32)


def _na_build_bias(rb_ref, tb_ref):
    lanes = 2 * GRID_W
    cq = lax.broadcasted_iota(jnp.int32, (GRID_W, lanes), 0)
    lane = lax.broadcasted_iota(jnp.int32, (GRID_W, lanes), 1)
    ck = lane % GRID_W
    cs = jnp.clip(cq - WIN_C // 2, 0, GRID_W - WIN_C)
    col_in = (ck >= cs) & (ck < cs + WIN_C)

    def toeplitz(d, lane0):
        row = jnp.broadcast_to(rb_ref[d:d + 1, :], (GRID_W, lanes))
        return pltpu.roll(row, (lane0 - (WIN_C - 1)) % lanes, 1, stride=1, stride_axis=0)

    pair = [jnp.where(col_in, jnp.where(lane < GRID_W, toeplitz(d, 0), toeplitz(d + 1, GRID_W)), MASKED)
            for d in range(2 * WIN_R - 2)]
    for e in range(WIN_R):
        for p in range(WIN_R // 2):
            tb_ref[e, :, p * lanes:(p + 1) * lanes] = pair[2 * p - e + WIN_R - 1]


def _na_kernel(q_ref, k_ref, v_ref, cos_ref, sin_ref, rb_ref, gq_ref, gk_ref, perm_ref, o_ref,
               qs, ks, vs, kcs, vcs, tb_ref, *, n_ctx, grid_rows):
    @pl.when(pl.program_id(1) == 0)
    def _():
        _na_build_bias(rb_ref, tb_ref)

    ones = jnp.ones((2 * HEAD_DIM, HEAD_DIM), BF16)

    def normed(tv, g):
        sq = tv * tv
        hi = sq.astype(BF16)
        lo = (sq - hi.astype(F32)).astype(BF16)
        ssq = _dot(jnp.concatenate([hi, lo], axis=1), ones)
        return tv * lax.rsqrt(ssq * (1.0 / HEAD_DIM) + EPS) * g


    gq = gq_ref[...]
    gk = gk_ref[...]
    kcs[...] = normed(k_ref[0:n_ctx, :], gk).astype(BF16)
    vcs[...] = v_ref[0:n_ctx, :].astype(BF16)

    def prep(i, carry):
        rows = [pl.ds(pl.multiple_of((2 * i + n) * TOK, TOK), TOK) for n in range(2)]
        srcs = [pl.ds(pl.multiple_of(n_ctx + (2 * i + n) * TOK, TOK), TOK) for n in range(2)]
        jobs = [(src_ref, g, dst, n) for n in range(2) for src_ref, g, dst in ((q_ref, gq, qs), (k_ref, gk, ks))]
        nrm = [normed(src_ref[srcs[n], :], g) for src_ref, g, _, n in jobs]
        par = [_dot(tv.astype(BF16), perm_ref[...]) for tv in nrm]
        for (_, _, dst, n), tv, pv in zip(jobs, nrm, par):
            dst[rows[n], :] = (tv * cos_ref[rows[n], :] + pv * sin_ref[rows[n], :]).astype(BF16)
        for n in range(2):
            vs[rows[n], :] = v_ref[srcs[n], :].astype(BF16)
        return carry

    lax.fori_loop(0, grid_rows * GRID_W // (2 * TOK), prep, 0)
    scale = HEAD_DIM ** -0.5
    band = WIN_R * GRID_W

    def rows_step(i, carry):
        rr = [i * NA_ROWS + n for n in range(NA_ROWS)]
        rs = [jnp.clip(r - WIN_R // 2, 0, grid_rows - WIN_R) for r in rr]
        qrow = [pl.ds(pl.multiple_of(r * GRID_W, GRID_W), GRID_W) for r in rr]
        krow = [pl.ds(pl.multiple_of(s * GRID_W, GRID_W), band) for s in rs]
        qr = [qs[qw, :] for qw in qrow]
        kc = kcs[...]
        sb = [_dot_nt(qr[n], ks[krow[n], :]) * scale + tb_ref[rr[n] - rs[n]] for n in range(NA_ROWS)]
        sc = [_dot_nt(qr[n], kc) * scale for n in range(NA_ROWS)]
        m = [jnp.maximum(jnp.max(sb[n], axis=-1, keepdims=True), jnp.max(sc[n], axis=-1, keepdims=True))
             for n in range(NA_ROWS)]
        pb = [jnp.exp(sb[n] - m[n]) for n in range(NA_ROWS)]
        pc = [jnp.exp(sc[n] - m[n]) for n in range(NA_ROWS)]
        den = [jnp.sum(pb[n], axis=-1, keepdims=True) + jnp.sum(pc[n], axis=-1, keepdims=True)
               for n in range(NA_ROWS)]
        vc = vcs[...]
        o = [_dot(pb[n].astype(BF16), vs[krow[n], :]) + _dot(pc[n].astype(BF16), vc) for n in range(NA_ROWS)]
        for n in range(NA_ROWS):
            o_ref[qrow[n], :] = (o[n] / den[n]).astype(BF16)
        return carry

    lax.fori_loop(0, grid_rows // NA_ROWS, rows_step, 0)


def _na_call(p3, rel_bias, gq, gk, n_ctx, l):
    b, t, _ = p3.shape
    cos, sin = _rope_tables(l)
    nr, nc = rel_bias.shape[1:]
    rb = jnp.zeros((HEADS, 2 * WIN_R, 2 * GRID_W), F32).at[:, :nr, :nc].set(rel_bias)
    lanes = np.arange(HEAD_DIM)
    partner = np.where(lanes % (HEAD_DIM // 2) < HEAD_DIM // 4, lanes + HEAD_DIM // 4, lanes - HEAD_DIM // 4)
    perm = jnp.asarray(lanes[:, None] == partner[None, :], BF16)
    col = lambda grp: (lambda h, i: (i, 0, grp * HEADS + h))
    tok_spec = lambda im: pl.BlockSpec((None, t, HEAD_DIM), im)
    full = lambda shape: pl.BlockSpec(shape, lambda h, i: (0,) * len(shape))
    return pl.pallas_call(
        functools.partial(_na_kernel, n_ctx=n_ctx, grid_rows=l // GRID_W),
        grid=(HEADS, b),
        in_specs=[tok_spec(col(5)), tok_spec(col(6)), tok_spec(col(7)),
                  full((l, HEAD_DIM)), full((l, HEAD_DIM)),
                  pl.BlockSpec((None, 2 * WIN_R, 2 * GRID_W), lambda h, i: (h, 0, 0)),
                  full((1, HEAD_DIM)), full((1, HEAD_DIM)), full((HEAD_DIM, HEAD_DIM))],
        out_specs=pl.BlockSpec((None, l, HEAD_DIM), lambda h, i: (i, 0, h)),
        out_shape=jax.ShapeDtypeStruct((b, l, MIX_W), BF16),
        scratch_shapes=[pltpu.VMEM((l, HEAD_DIM), BF16), pltpu.VMEM((l, HEAD_DIM), BF16),
                        pltpu.VMEM((l, HEAD_DIM), BF16), pltpu.VMEM((n_ctx, HEAD_DIM), BF16),
                        pltpu.VMEM((n_ctx, HEAD_DIM), BF16),
                        pltpu.VMEM((WIN_R, GRID_W, WIN_R * GRID_W), F32)],
        compiler_params=_params(("parallel", "arbitrary")),
        name="na",
    )(p3, p3, p3, cos, sin, rb, gq, gk, perm)


def _merge_kernel(ya_ref, yb_ref, ga_ref, gb_ref, x_ref, wa_ref, wb_ref, wo_ref, mod_ref, n2_ref,
                  xm_ref, h2_ref):
    za = _dot(ya_ref[...], wa_ref[...])
    zb = _dot(yb_ref[...], wb_ref[...])
    z = _sigmoid(ga_ref[...]) * za + _sigmoid(gb_ref[...]) * zb
    xm = x_ref[...] + mod_ref[0:1, :] * _dot(z.astype(BF16), wo_ref[...])
    xm_ref[...] = xm
    y = xm * lax.rsqrt(jnp.mean(xm * xm, axis=-1, keepdims=True) + EPS) * n2_ref[...]
    h2_ref[...] = (y * (1.0 + mod_ref[2:3, :]) + mod_ref[1:2, :]).astype(BF16)


def _merge_call(ya, yb, p3, x, wa, wb, wo, mod3, n2, n_ctx):
    b, l, d = x.shape
    ncb = n_ctx // TOK
    gate_blk = MIX_W * 8 // d
    full = lambda shape: pl.BlockSpec(shape, lambda i, t: (0,) * len(shape))
    return pl.pallas_call(
        _merge_kernel,
        grid=(b, l // TOK),
        in_specs=[pl.BlockSpec((None, TOK, MIX_W), lambda i, t: (i, t, 0)),
                  pl.BlockSpec((None, TOK, MIX_W), lambda i, t: (i, t, 0)),
                  pl.BlockSpec((None, TOK, d), lambda i, t: (i, t + ncb, gate_blk)),
                  pl.BlockSpec((None, TOK, d), lambda i, t: (i, t + ncb, gate_blk + 1)),
                  pl.BlockSpec((None, TOK, d), lambda i, t: (i, t, 0)),
                  full((MIX_W, d)), full((MIX_W, d)), full((d, d)),
                  pl.BlockSpec((None, 3, d), lambda i, t: (i, 0, 0)),
                  full((1, d))],
        out_specs=[pl.BlockSpec((None, TOK, d), lambda i, t: (i, t, 0)),
                   pl.BlockSpec((None, TOK, d), lambda i, t: (i, t, 0))],
        out_shape=[jax.ShapeDtypeStruct((b, l, d), F32), jax.ShapeDtypeStruct((b, l, d), BF16)],
        compiler_params=_params(("parallel", "arbitrary")),
        name="merge",
    )(ya, yb, p3, p3, x, wa, wb, wo, mod3, n2)


def _ffn_kernel(h_ref, hp_ref, hn_ref, w1_ref, w3_ref, cw_ref, cb_ref, w2_ref, xm_ref, g2_ref, o_ref, a_ref,
                *, tiles_per_seq, halo):
    i = pl.program_id(0)
    s = pl.program_id(1)
    n_hid, tm, th = a_ref.shape

    @pl.when(s < n_hid)
    def _():
        w1 = w1_ref[...]
        t1 = _dot(h_ref[...], w1)
        prev_row = _dot(hp_ref[...], w1)[halo - 1:halo, :]
        next_row = _dot(hn_ref[...], w1)[0:1, :]
        prev_row = jnp.where(i % tiles_per_seq == 0, 0.0, prev_row)
        next_row = jnp.where(i % tiles_per_seq == tiles_per_seq - 1, 0.0, next_row)
        ridx = lax.broadcasted_iota(jnp.int32, (tm, 1), 0)
        up = jnp.where(ridx == 0, prev_row, pltpu.roll(t1, 1, 0))
        dn = jnp.where(ridx == tm - 1, next_row, pltpu.roll(t1, tm - 1, 0))
        u = up * cw_ref[0:1, :] + t1 * cw_ref[1:2, :] + dn * cw_ref[2:3, :] + cb_ref[...]
        a_ref[s] = ((u * _sigmoid(u)) * _dot(h_ref[...], w3_ref[...])).astype(BF16)

    @pl.when(s >= n_hid)
    def _():
        acc = _dot(a_ref[0], w2_ref[0:th, :])
        for k in range(1, n_hid):
            acc = acc + _dot(a_ref[k], w2_ref[k * th:(k + 1) * th, :])
        o_ref[...] = xm_ref[...] + g2_ref[...] * acc


def _ffn_call(h2, xm, w1, w3, cw, cb, w2, g2, l):
    m, d = h2.shape
    hid = w1.shape[1]
    tm = 1024
    th = min(512, hid)
    tn = 256
    n_hid = hid // th
    halo = 16
    per = tm // halo
    nblk = m // halo
    hcol = lambda i, s: (0, jnp.minimum(s, n_hid - 1))
    ocol = lambda s: jnp.maximum(s - n_hid, 0)
    return pl.pallas_call(
        functools.partial(_ffn_kernel, tiles_per_seq=l // tm, halo=halo),
        grid=(m // tm, n_hid + d // tn),
        in_specs=[pl.BlockSpec((tm, d), lambda i, s: (i, 0)),
                  pl.BlockSpec((halo, d), lambda i, s: (jnp.maximum(i * per - 1, 0), 0)),
                  pl.BlockSpec((halo, d), lambda i, s: (jnp.minimum((i + 1) * per, nblk - 1), 0)),
                  pl.BlockSpec((d, th), hcol),
                  pl.BlockSpec((d, th), hcol),
                  pl.BlockSpec((3, th), hcol),
                  pl.BlockSpec((1, th), hcol),
                  pl.BlockSpec((hid, tn), lambda i, s: (0, ocol(s))),
                  pl.BlockSpec((tm, tn), lambda i, s: (i, ocol(s))),
                  pl.BlockSpec((None, 1, tn), lambda i, s: (i // (l // tm), 0, ocol(s)))],
        out_specs=pl.BlockSpec((tm, tn), lambda i, s: (i, ocol(s))),
        out_shape=jax.ShapeDtypeStruct((m, d), F32),
        scratch_shapes=[pltpu.VMEM((n_hid, tm, th), BF16)],
        compiler_params=_params(("parallel", "arbitrary")),
        name="ffn",
    )(h2, h2, h2, w1, w3, cw, cb, w2, xm, g2)


def kernel(x, c, ctx, c_ctx, ada_w, ada_b, norm1_g, norm2_g, w_in, hgrn_lb_logits, hgrn_norm_g,
           na_q_norm_g, na_k_norm_g, na_rel_bias, w_branch_a, w_branch_b, w_out,
           ffn_w1, ffn_w3, ffn_conv_w, ffn_conv_b, ffn_w2):
    b, l, d = x.shape
    n_ctx = ctx.shape[1]
    assert ada_w.shape[0] == 1 and b + 1 <= 8 and n_ctx % TOK == 0 and l % (2 * TOK) == 0

    c_all = jnp.zeros((8, d), F32).at[:b].set(c).at[b].set(c_ctx)
    mod = _mod_call(c_all, ada_w[0], ada_b).reshape(8, N_MOD, d)
    sh1, sc1, g1, sh2, sc2, g2 = (mod[:b, n] for n in range(N_MOD))
    ss_ctx = jnp.broadcast_to(mod[b, 0:2][None], (b, 2, d))
    ss1 = jnp.stack([ss_ctx, jnp.stack([sh1, sc1], axis=1)], axis=1)

    h = _prenorm_call(ctx, x, norm1_g, ss1)
    t = n_ctx + l
    p = _inproj_call(h.reshape(b * t, d), w_in[0])
    p3 = p.reshape(b, t, p.shape[1])

    y_a = _hgrn_call(p3, hgrn_lb_logits, hgrn_norm_g, n_ctx, l)
    y_b = _na_call(p3, na_rel_bias[0], na_q_norm_g, na_k_norm_g, n_ctx, l)

    mod3 = jnp.stack([g1, sh2, sc2], axis=1)
    x_mid, h2 = _merge_call(y_a, y_b, p3, x, w_branch_a[0].astype(BF16), w_branch_b[0].astype(BF16),
                            w_out[0].astype(BF16), mod3, norm2_g, n_ctx)

    out = _ffn_call(h2.reshape(b * l, d), x_mid.reshape(b * l, d), ffn_w1[0].astype(BF16),
                    ffn_w3[0].astype(BF16), ffn_conv_w[0], ffn_conv_b, ffn_w2[0].astype(BF16),
                    g2[:, None, :], l)
    return out.reshape(b, l, d)
```

```python
import functools

import numpy as np
import jax
import jax.numpy as jnp
from jax import lax
from jax.experimental import pallas as pl
from jax.experimental.pallas import tpu as pltpu

GRID_W = 64
HEADS = 8
HEAD_DIM = 128
MIX_W = HEADS * HEAD_DIM
CHUNK = 64
N_LEVELS = 6
MXU_LEVEL_HALVES = (4, 2)
PREP_CHUNKS = 8
SCAN_CHUNKS = 8
NA_ROWS = 8
WIN_R = 8
WIN_C = 16
ROPE_THETA = 10000.0
N_MOD = 6
EPS = 1e-6
LOG2E = 1.4426950408889634
TOK = 256
MASKED = -1e30
V7X_VMEM_LIMIT = 56 * 1024 * 1024

BF16 = jnp.bfloat16
F32 = jnp.float32


def _dot(a, b):
    return jnp.dot(a, b, preferred_element_type=F32)


def _dot_nt(a, b):
    return lax.dot_general(a, b, (((1,), (1,)), ((), ())), preferred_element_type=F32)


def _sigmoid(t):
    return 1.0 / (1.0 + jnp.exp(-t))


def _params(sem):
    return pltpu.CompilerParams(dimension_semantics=sem, vmem_limit_bytes=V7X_VMEM_LIMIT)


def _mod_kernel(c_ref, w_ref, b_ref, o_ref):
    cv = c_ref[...]
    s = (cv * _sigmoid(cv)).astype(BF16)
    o_ref[...] = _dot(s, w_ref[...].astype(BF16)) + b_ref[...]


def _mod_call(c_all, w, b):
    d, n = w.shape
    tn = min(1024, d)
    return pl.pallas_call(
        _mod_kernel,
        grid=(n // tn,),
        in_specs=[pl.BlockSpec((8, d), lambda j: (0, 0)),
                  pl.BlockSpec((d, tn), lambda j: (0, j)),
                  pl.BlockSpec((1, tn), lambda j: (0, j))],
        out_specs=pl.BlockSpec((8, tn), lambda j: (0, j)),
        out_shape=jax.ShapeDtypeStruct((8, n), F32),
        compiler_params=_params(("arbitrary",)),
        name="mod",
    )(c_all, w, b)


def _prenorm_kernel(ctx_ref, x_ref, g_ref, ss_ref, o_ref, *, n_ctx_blk):
    t = pl.program_id(1)
    xv = jnp.where(t < n_ctx_blk, ctx_ref[...], x_ref[...])
    y = xv * lax.rsqrt(jnp.mean(xv * xv, axis=-1, keepdims=True) + EPS) * g_ref[...]
    o_ref[...] = (y * (1.0 + ss_ref[1:2, :]) + ss_ref[0:1, :]).astype(BF16)


def _prenorm_call(ctx, x, g, ss):
    b, l, d = x.shape
    ncb = ctx.shape[1] // TOK
    nt = ncb + l // TOK
    return pl.pallas_call(
        functools.partial(_prenorm_kernel, n_ctx_blk=ncb),
        grid=(b, nt),
        in_specs=[pl.BlockSpec((None, TOK, d), lambda i, t: (i, jnp.minimum(t, ncb - 1), 0)),
                  pl.BlockSpec((None, TOK, d), lambda i, t: (i, jnp.maximum(t - ncb, 0), 0)),
                  pl.BlockSpec((1, d), lambda i, t: (0, 0)),
                  pl.BlockSpec((None, None, 2, d), lambda i, t: (i, (t >= ncb).astype(jnp.int32), 0, 0))],
        out_specs=pl.BlockSpec((None, TOK, d), lambda i, t: (i, t, 0)),
        out_shape=jax.ShapeDtypeStruct((b, nt * TOK, d), BF16),
        compiler_params=_params(("parallel", "arbitrary")),
        name="prenorm",
    )(ctx, x, g, ss)


def _inproj_kernel(h_ref, w_ref, o_ref, wb_ref):
    @pl.when(pl.program_id(1) == 0)
    def _():
        wb_ref[...] = w_ref[...].astype(BF16)

    o_ref[...] = _dot(h_ref[...], wb_ref[...])


def _inproj_call(h, w):
    m, d = h.shape
    n = w.shape[1]
    tm = 1024 if m % 1024 == 0 else TOK
    tn = min(1024, d)
    return pl.pallas_call(
        _inproj_kernel,
        grid=(n // tn, m // tm),
        in_specs=[pl.BlockSpec((tm, d), lambda j, i: (i, 0)),
                  pl.BlockSpec((d, tn), lambda j, i: (0, j))],
        out_specs=pl.BlockSpec((tm, tn), lambda j, i: (i, j)),
        out_shape=jax.ShapeDtypeStruct((m, n), F32),
        scratch_shapes=[pltpu.VMEM((d, tn), BF16)],
        compiler_params=_params(("parallel", "arbitrary")),
        name="inproj",
    )(h, w)


def _hgrn_constants():
    u = np.arange(CHUNK)
    mats, masks = [], []
    for rev in (False, True):
        pos = CHUNK - 1 - u if rev else u
        incl = (pos[None, :] <= pos[:, None]).astype(np.float32)
        blocks, lvl_masks = [incl], []
        for lev in range(N_LEVELS):
            half = CHUNK >> (lev + 1)
            ref = (pos // (2 * half)) * (2 * half) + half - 1
            upto_ref = (pos[None, :] <= ref[:, None]).astype(np.float32)
            if half in MXU_LEVEL_HALVES:
                late = (pos % (2 * half)) >= half
                blocks.append(np.where(late[:, None], 1.0, -1.0).astype(np.float32) * (incl - upto_ref))
            same = (pos[:, None] // (2 * half)) == (pos[None, :] // (2 * half))
            late_q = (pos[:, None] % (2 * half)) >= half
            early_k = (pos[None, :] % (2 * half)) < half
            lvl_masks.append((same & late_q & early_k).astype(np.float32))
        lvl_masks.append(np.eye(CHUNK, dtype=np.float32))
        mats.append(np.concatenate(blocks, axis=0))
        masks.append(np.stack(lvl_masks))
    return np.stack(mats), np.stack(masks)


def _hgrn_kernel(lbl_ref, q_ref, ff_ref, fb_ref, i_ref, gt_ref, cm_ref, mk_ref, ng_ref, y_ref,
                 of_ref, ob_ref, sf_ref, sb_ref, kd_ref, et_ref, qd_ref, sc_ref, *, n_ctx_chunks, n_lat_chunks):
    f_refs, o_refs, st_refs = (ff_ref, fb_ref), (of_ref, ob_ref), (sf_ref, sb_ref)
    lowers = []
    for dirn in range(2):
        la, lb_ = lbl_ref[dirn, 0], lbl_ref[dirn, 1]
        mx = jnp.maximum(la, lb_)
        ea, eb = jnp.exp(la - mx), jnp.exp(lb_ - mx)
        lowers.append(ea / (ea + eb))
        st_refs[dirn][...] = jnp.zeros(st_refs[dirn].shape, F32)
    row_odd = (lax.broadcasted_iota(jnp.int32, (CHUNK, 1), 0) % 2) == 1

    def chunk_rows(pos):
        return pl.ds(pl.multiple_of(pos * CHUNK, CHUNK), CHUNK)

    def level_decay(dirn, lev, f, sums):
        half = CHUNK >> (lev + 1)
        cum = sums[0:CHUNK]
        if half == 1:
            return jnp.where(row_odd, 1.0, f) if dirn else jnp.where(row_odd, f, 1.0)
        if half in MXU_LEVEL_HALVES:
            blk = 1 + MXU_LEVEL_HALVES.index(half)
            return jnp.exp2(sums[blk * CHUNK:(blk + 1) * CHUNK])
        parts = []
        for p in range(0, CHUNK, 2 * half):
            ref = cum[p + half - 1 + dirn:p + half + dirn]
            lo, hi = cum[p:p + half], cum[p + half:p + 2 * half]
            parts += [lo - ref, ref - hi] if dirn else [ref - lo, hi - ref]
        return jnp.exp2(jnp.concatenate(parts, axis=0))

    def prepare(pos0, n, with_out):
        items = [(dirn, pos0 + c) for c in range(n) for dirn in range(2)]
        fs, kks, splits = [], [], []
        for dirn, pos in items:
            lower = lowers[dirn]
            f = lower + (1.0 - lower) * _sigmoid(f_refs[dirn][chunk_rows(pos), :])
            g = jnp.log(f) * LOG2E
            g1 = g.astype(BF16)
            r1 = g - g1.astype(F32)
            g2 = r1.astype(BF16)
            g3 = (r1 - g2.astype(F32)).astype(BF16)
            fs.append(f)
            kks.append(1.0 - f)
            splits.append(jnp.concatenate([g1, g2, g3], axis=0))
        sums = [None] * len(items)
        for dirn in range(2):
            idx = [j for j, it in enumerate(items) if it[0] == dirn]
            wide = _dot(cm_ref[dirn], jnp.concatenate([splits[j] for j in idx], axis=1))
            for c, j in enumerate(idx):
                sums[j] = wide[:, c * HEAD_DIM:(c + 1) * HEAD_DIM]
        for j, (dirn, pos) in enumerate(items):
            cum = sums[j][0:CHUNK]
            last = 0 if dirn else CHUNK - 1
            tot = cum[last:last + 1]
            kd_ref[dirn, chunk_rows(pos), :] = (kks[j] * jnp.exp2(tot - cum)).astype(BF16)
            et_ref[dirn, pos] = jnp.broadcast_to(jnp.exp2(tot), et_ref.shape[2:])
        if not with_out:
            return
        qs = [q_ref[chunk_rows(pos), :] for _, pos in items]
        for j, (dirn, pos) in enumerate(items):
            qd_ref[dirn, chunk_rows(pos - n_ctx_chunks), :] = (qs[j] * jnp.exp2(sums[j][0:CHUNK])).astype(BF16)
        qbs = [qv.astype(BF16) for qv in qs]
        kbs = [kv.astype(BF16) for kv in kks]
        scs = [mk_ref[dirn, N_LEVELS] * _dot_nt(qbs[j], kbs[j]) for j, (dirn, _) in enumerate(items)]
        for lev in range(N_LEVELS):
            for j, (dirn, _) in enumerate(items):
                dl = level_decay(dirn, lev, fs[j], sums[j]).astype(BF16)
                scs[j] = scs[j] + mk_ref[dirn, lev] * _dot_nt(qbs[j] * dl, kbs[j] * dl)
        for j, (dirn, pos) in enumerate(items):
            sc_ref[dirn, chunk_rows(pos - n_ctx_chunks), :] = scs[j].astype(BF16)

    def scan(pos_f, pos_b, n, with_out):
        items = [(dirn, (pos_b - c) if dirn else (pos_f + c)) for c in range(n) for dirn in range(2)]
        vs = [i_ref[chunk_rows(pos), :] for _, pos in items]
        ups = [_dot(vs[j].T.astype(BF16), kd_ref[dirn, chunk_rows(pos), :]) for j, (dirn, pos) in enumerate(items)]
        sts = [st_refs[0][...], st_refs[1][...]]
        before = []
        for j, (dirn, pos) in enumerate(items):
            before.append(sts[dirn])
            sts[dirn] = sts[dirn] * et_ref[dirn, pos][0:1] + ups[j]
        for dirn in range(2):
            st_refs[dirn][...] = sts[dirn]
        if not with_out:
            return
        for j, (dirn, pos) in enumerate(items):
            orow = chunk_rows(pos - n_ctx_chunks)
            o_refs[dirn][orow, :] = (_dot_nt(qd_ref[dirn, orow, :], before[j].astype(BF16))
                                     + _dot(sc_ref[dirn, orow, :], vs[j].astype(BF16)))

    def loop(n, body):
        lax.fori_loop(0, n, lambda ci, c: (body(ci), c)[1], 0)

    last_pos = n_ctx_chunks + n_lat_chunks - 1
    prepare(0, n_ctx_chunks, False)
    loop(n_lat_chunks // PREP_CHUNKS, lambda ci: prepare(n_ctx_chunks + ci * PREP_CHUNKS, PREP_CHUNKS, True))
    scan(0, n_ctx_chunks - 1, n_ctx_chunks, False)
    loop(n_lat_chunks // SCAN_CHUNKS,
         lambda ci: scan(n_ctx_chunks + ci * SCAN_CHUNKS, last_pos - ci * SCAN_CHUNKS, SCAN_CHUNKS, True))

    def readout(i, carry):
        rows = pl.ds(pl.multiple_of(i * TOK, TOK), TOK)
        ot = of_ref[rows, :] + ob_ref[rows, :]
        on = ot * lax.rsqrt(jnp.mean(ot * ot, axis=-1, keepdims=True) + EPS) * ng_ref[...]
        gate = gt_ref[pl.ds(pl.multiple_of(n_ctx_chunks * CHUNK + i * TOK, CHUNK), TOK), :]
        y_ref[rows, :] = (on * (gate * _sigmoid(gate))).astype(BF16)
        return carry

    lax.fori_loop(0, n_lat_chunks * CHUNK // TOK, readout, 0)


def _hgrn_call(p3, lb_logits, norm_g, n_ctx, l):
    b, t, _ = p3.shape
    cm, mk = _hgrn_constants()
    cm = jnp.asarray(np.concatenate([cm] * 3, axis=-1), BF16)
    mk = jnp.asarray(mk, F32)
    lbl = lb_logits.reshape(2, 2, HEADS, 1, HEAD_DIM)
    tok_spec = lambda grp: pl.BlockSpec((None, t, HEAD_DIM), lambda i, h: (i, 0, grp * HEADS + h))
    full = lambda shape: pl.BlockSpec(shape, lambda i, h: (0,) * len(shape))
    return pl.pallas_call(
        functools.partial(_hgrn_kernel, n_ctx_chunks=n_ctx // CHUNK, n_lat_chunks=l // CHUNK),
        grid=(b, HEADS),
        in_specs=[pl.BlockSpec((2, 2, None, 1, HEAD_DIM), lambda i, h: (0, 0, h, 0, 0)),
                  tok_spec(0), tok_spec(1), tok_spec(2), tok_spec(3), tok_spec(4),
                  full((2, 3 * CHUNK, 3 * CHUNK)),
                  full((2, N_LEVELS + 1, CHUNK, CHUNK)),
                  full((1, HEAD_DIM))],
        out_specs=pl.BlockSpec((None, l, HEAD_DIM), lambda i, h: (i, 0, h)),
        out_shape=jax.ShapeDtypeStruct((b, l, MIX_W), BF16),
        scratch_shapes=[pltpu.VMEM((l, HEAD_DIM), F32), pltpu.VMEM((l, HEAD_DIM), F32),
                        pltpu.VMEM((HEAD_DIM, HEAD_DIM), F32), pltpu.VMEM((HEAD_DIM, HEAD_DIM), F32),
                        pltpu.VMEM((2, t, HEAD_DIM), BF16), pltpu.VMEM((2, t // CHUNK, 8, HEAD_DIM), F32),
                        pltpu.VMEM((2, l, HEAD_DIM), BF16), pltpu.VMEM((2, l, CHUNK), BF16)],
        compiler_params=_params(("parallel", "parallel")),
        name="hgrn",
    )(lbl, p3, p3, p3, p3, p3, cm, mk, norm_g)


def _rope_tables(l):
    pos = np.arange(l)
    nf = HEAD_DIM // 4
    inv = ROPE_THETA ** (-np.arange(nf, dtype=np.float64) / nf)
    ang_r = (pos // GRID_W)[:, None] * inv[None, :]
    ang_c = (pos % GRID_W)[:, None] * inv[None, :]
    cos = np.concatenate([np.cos(ang_r)] * 2 + [np.cos(ang_c)] * 2, axis=-1)
    sin = np.concatenate([-np.sin(ang_r), np.sin(ang_r), -np.sin(ang_c), np.sin(ang_c)], axis=-1)
    return jnp.asarray(cos, F32), jnp.asarray(sin, F32)


def _na_build_bias(rb_ref, tb_ref):
    lanes = 2 * GRID_W
    cq = lax.broadcasted_iota(jnp.int32, (GRID_W, lanes), 0)
    lane = lax.broadcasted_iota(jnp.int32, (GRID_W, lanes), 1)
    ck = lane % GRID_W
    cs = jnp.clip(cq - WIN_C // 2, 0, GRID_W - WIN_C)
    col_in = (ck >= cs) & (ck < cs + WIN_C)

    def toeplitz(d, lane0):
        row = jnp.broadcast_to(rb_ref[d:d + 1, :] * LOG2E, (GRID_W, lanes))
        return pltpu.roll(row, (lane0 - (WIN_C - 1)) % lanes, 1, stride=1, stride_axis=0)

    pair = [jnp.where(col_in, jnp.where(lane < GRID_W, toeplitz(d, 0), toeplitz(d + 1, GRID_W)), MASKED)
            for d in range(2 * WIN_R - 2)]
    for e in range(WIN_R):
        for p in range(WIN_R // 2):
            tb_ref[e, :, p * lanes:(p + 1) * lanes] = pair[2 * p - e + WIN_R - 1]


def _na_kernel(q_ref, k_ref, v_ref, cos_ref, sin_ref, rb_ref, gq_ref, gk_ref, perm_ref, o_ref,
               qs, ks, vs, kcs, vcs, tb_ref, *, n_ctx, grid_rows):
    @pl.when(pl.program_id(1) == 0)
    def _():
        _na_build_bias(rb_ref, tb_ref)

    ones = jnp.ones((2 * HEAD_DIM, HEAD_DIM), BF16)

    def normed(tv, g):
        sq = tv * tv
        hi = sq.astype(BF16)
        lo = (sq - hi.astype(F32)).astype(BF16)
        ssq = _dot(jnp.concatenate([hi, lo], axis=1), ones)
        return tv * lax.rsqrt(ssq * (1.0 / HEAD_DIM) + EPS) * g


    gq = gq_ref[...]
    gk = gk_ref[...]
    kcs[...] = normed(k_ref[0:n_ctx, :], gk).astype(BF16)
    vcs[...] = v_ref[0:n_ctx, :].astype(BF16)

    def prep(i, carry):
        rows = [pl.ds(pl.multiple_of((2 * i + n) * TOK, TOK), TOK) for n in range(2)]
        srcs = [pl.ds(pl.multiple_of(n_ctx + (2 * i + n) * TOK, TOK), TOK) for n in range(2)]
        jobs = [(src_ref, g, dst, n) for n in range(2) for src_ref, g, dst in ((q_ref, gq, qs), (k_ref, gk, ks))]
        nrm = [normed(src_ref[srcs[n], :], g) for src_ref, g, _, n in jobs]
        par = [_dot(tv.astype(BF16), perm_ref[...]) for tv in nrm]
        for (_, _, dst, n), tv, pv in zip(jobs, nrm, par):
            dst[rows[n], :] = (tv * cos_ref[rows[n], :] + pv * sin_ref[rows[n], :]).astype(BF16)
        for n in range(2):
            vs[rows[n], :] = v_ref[srcs[n], :].astype(BF16)
        return carry

    lax.fori_loop(0, grid_rows * GRID_W // (2 * TOK), prep, 0)
    scale = HEAD_DIM ** -0.5 * LOG2E
    band = WIN_R * GRID_W

    def rows_step(i, carry):
        rr = [i * NA_ROWS + n for n in range(NA_ROWS)]
        rs = [jnp.clip(r - WIN_R // 2, 0, grid_rows - WIN_R) for r in rr]
        qrow = [pl.ds(pl.multiple_of(r * GRID_W, GRID_W), GRID_W) for r in rr]
        krow = [pl.ds(pl.multiple_of(s * GRID_W, GRID_W), band) for s in rs]
        qr = [qs[qw, :] for qw in qrow]
        kc = kcs[...]
        sb = [_dot_nt(qr[n], ks[krow[n], :]) * scale + tb_ref[rr[n] - rs[n]] for n in range(NA_ROWS)]
        sc = [_dot_nt(qr[n], kc) * scale for n in range(NA_ROWS)]
        m = [jnp.maximum(jnp.max(sb[n], axis=-1, keepdims=True), jnp.max(sc[n], axis=-1, keepdims=True))
             for n in range(NA_ROWS)]
        pb = [jnp.exp2(sb[n] - m[n]) for n in range(NA_ROWS)]
        pc = [jnp.exp2(sc[n] - m[n]) for n in range(NA_ROWS)]
        den = [jnp.sum(pb[n], axis=-1, keepdims=True) + jnp.sum(pc[n], axis=-1, keepdims=True)
               for n in range(NA_ROWS)]
        vc = vcs[...]
        o = [_dot(pb[n].astype(BF16), vs[krow[n], :]) + _dot(pc[n].astype(BF16), vc) for n in range(NA_ROWS)]
        for n in range(NA_ROWS):
            o_ref[qrow[n], :] = (o[n] / den[n]).astype(BF16)
        return carry

    lax.fori_loop(0, grid_rows // NA_ROWS, rows_step, 0)


def _na_call(p3, rel_bias, gq, gk, n_ctx, l):
    b, t, _ = p3.shape
    cos, sin = _rope_tables(l)
    nr, nc = rel_bias.shape[1:]
    rb = jnp.zeros((HEADS, 2 * WIN_R, 2 * GRID_W), F32).at[:, :nr, :nc].set(rel_bias)
    lanes = np.arange(HEAD_DIM)
    partner = np.where(lanes % (HEAD_DIM // 2) < HEAD_DIM // 4, lanes + HEAD_DIM // 4, lanes - HEAD_DIM // 4)
    perm = jnp.asarray(lanes[:, None] == partner[None, :], BF16)
    col = lambda grp: (lambda h, i: (i, 0, grp * HEADS + h))
    tok_spec = lambda im: pl.BlockSpec((None, t, HEAD_DIM), im)
    full = lambda shape: pl.BlockSpec(shape, lambda h, i: (0,) * len(shape))
    return pl.pallas_call(
        functools.partial(_na_kernel, n_ctx=n_ctx, grid_rows=l // GRID_W),
        grid=(HEADS, b),
        in_specs=[tok_spec(col(5)), tok_spec(col(6)), tok_spec(col(7)),
                  full((l, HEAD_DIM)), full((l, HEAD_DIM)),
                  pl.BlockSpec((None, 2 * WIN_R, 2 * GRID_W), lambda h, i: (h, 0, 0)),
                  full((1, HEAD_DIM)), full((1, HEAD_DIM)), full((HEAD_DIM, HEAD_DIM))],
        out_specs=pl.BlockSpec((None, l, HEAD_DIM), lambda h, i: (i, 0, h)),
        out_shape=jax.ShapeDtypeStruct((b, l, MIX_W), BF16),
        scratch_shapes=[pltpu.VMEM((l, HEAD_DIM), BF16), pltpu.VMEM((l, HEAD_DIM), BF16),
                        pltpu.VMEM((l, HEAD_DIM), BF16), pltpu.VMEM((n_ctx, HEAD_DIM), BF16),
                        pltpu.VMEM((n_ctx, HEAD_DIM), BF16),
                        pltpu.VMEM((WIN_R, GRID_W, WIN_R * GRID_W), F32)],
        compiler_params=_params(("parallel", "arbitrary")),
        name="na",
    )(p3, p3, p3, cos, sin, rb, gq, gk, perm)


def _merge_kernel(ya_ref, yb_ref, ga_ref, gb_ref, x_ref, wa_ref, wb_ref, wo_ref, mod_ref, n2_ref,
                  xm_ref, h2_ref):
    za = _dot(ya_ref[...], wa_ref[...])
    zb = _dot(yb_ref[...], wb_ref[...])
    z = _sigmoid(ga_ref[...]) * za + _sigmoid(gb_ref[...]) * zb
    xm = x_ref[...] + mod_ref[0:1, :] * _dot(z.astype(BF16), wo_ref[...])
    xm_ref[...] = xm
    y = xm * lax.rsqrt(jnp.mean(xm * xm, axis=-1, keepdims=True) + EPS) * n2_ref[...]
    h2_ref[...] = (y * (1.0 + mod_ref[2:3, :]) + mod_ref[1:2, :]).astype(BF16)


def _merge_call(ya, yb, p3, x, wa, wb, wo, mod3, n2, n_ctx):
    b, l, d = x.shape
    ncb = n_ctx // TOK
    gate_blk = MIX_W * 8 // d
    full = lambda shape: pl.BlockSpec(shape, lambda i, t: (0,) * len(shape))
    return pl.pallas_call(
        _merge_kernel,
        grid=(b, l // TOK),
        in_specs=[pl.BlockSpec((None, TOK, MIX_W), lambda i, t: (i, t, 0)),
                  pl.BlockSpec((None, TOK, MIX_W), lambda i, t: (i, t, 0)),
                  pl.BlockSpec((None, TOK, d), lambda i, t: (i, t + ncb, gate_blk)),
                  pl.BlockSpec((None, TOK, d), lambda i, t: (i, t + ncb, gate_blk + 1)),
                  pl.BlockSpec((None, TOK, d), lambda i, t: (i, t, 0)),
                  full((MIX_W, d)), full((MIX_W, d)), full((d, d)),
                  pl.BlockSpec((None, 3, d), lambda i, t: (i, 0, 0)),
                  full((1, d))],
        out_specs=[pl.BlockSpec((None, TOK, d), lambda i, t: (i, t, 0)),
                   pl.BlockSpec((None, TOK, d), lambda i, t: (i, t, 0))],
        out_shape=[jax.ShapeDtypeStruct((b, l, d), F32), jax.ShapeDtypeStruct((b, l, d), BF16)],
        compiler_params=_params(("parallel", "arbitrary")),
        name="merge",
    )(ya, yb, p3, p3, x, wa, wb, wo, mod3, n2)


def _ffn_kernel(h_ref, hp_ref, hn_ref, w1_ref, w3_ref, cw_ref, cb_ref, w2_ref, xm_ref, g2_ref, o_ref, a_ref,
                *, tiles_per_seq, halo):
    i = pl.program_id(0)
    s = pl.program_id(1)
    n_hid, tm, th = a_ref.shape

    @pl.when(s < n_hid)
    def _():
        w1 = w1_ref[...]
        t1 = _dot(h_ref[...], w1)
        prev_row = _dot(hp_ref[...], w1)[halo - 1:halo, :]
        next_row = _dot(hn_ref[...], w1)[0:1, :]
        prev_row = jnp.where(i % tiles_per_seq == 0, 0.0, prev_row)
        next_row = jnp.where(i % tiles_per_seq == tiles_per_seq - 1, 0.0, next_row)
        ridx = lax.broadcasted_iota(jnp.int32, (tm, 1), 0)
        up = jnp.where(ridx == 0, prev_row, pltpu.roll(t1, 1, 0))
        dn = jnp.where(ridx == tm - 1, next_row, pltpu.roll(t1, tm - 1, 0))
        u = up * cw_ref[0:1, :] + t1 * cw_ref[1:2, :] + dn * cw_ref[2:3, :] + cb_ref[...]
        a_ref[s] = ((u * _sigmoid(u)) * _dot(h_ref[...], w3_ref[...])).astype(BF16)

    @pl.when(s >= n_hid)
    def _():
        acc = _dot(a_ref[0], w2_ref[0:th, :])
        for k in range(1, n_hid):
            acc = acc + _dot(a_ref[k], w2_ref[k * th:(k + 1) * th, :])
        o_ref[...] = xm_ref[...] + g2_ref[...] * acc


def _ffn_call(h2, xm, w1, w3, cw, cb, w2, g2, l):
    m, d = h2.shape
    hid = w1.shape[1]
    tm = 1024
    th = min(512, hid)
    tn = 256
    n_hid = hid // th
    halo = 16
    per = tm // halo
    nblk = m // halo
    hcol = lambda i, s: (0, jnp.minimum(s, n_hid - 1))
    ocol = lambda s: jnp.maximum(s - n_hid, 0)
    return pl.pallas_call(
        functools.partial(_ffn_kernel, tiles_per_seq=l // tm, halo=halo),
        grid=(m // tm, n_hid + d // tn),
        in_specs=[pl.BlockSpec((tm, d), lambda i, s: (i, 0)),
                  pl.BlockSpec((halo, d), lambda i, s: (jnp.maximum(i * per - 1, 0), 0)),
                  pl.BlockSpec((halo, d), lambda i, s: (jnp.minimum((i + 1) * per, nblk - 1), 0)),
                  pl.BlockSpec((d, th), hcol),
                  pl.BlockSpec((d, th), hcol),
                  pl.BlockSpec((3, th), hcol),
                  pl.BlockSpec((1, th), hcol),
                  pl.BlockSpec((hid, tn), lambda i, s: (0, ocol(s))),
                  pl.BlockSpec((tm, tn), lambda i, s: (i, ocol(s))),
                  pl.BlockSpec((None, 1, tn), lambda i, s: (i // (l // tm), 0, ocol(s)))],
        out_specs=pl.BlockSpec((tm, tn), lambda i, s: (i, ocol(s))),
        out_shape=jax.ShapeDtypeStruct((m, d), F32),
        scratch_shapes=[pltpu.VMEM((n_hid, tm, th), BF16)],
        compiler_params=_params(("parallel", "arbitrary")),
        name="ffn",
    )(h2, h2, h2, w1, w3, cw, cb, w2, xm, g2)


def kernel(x, c, ctx, c_ctx, ada_w, ada_b, norm1_g, norm2_g, w_in, hgrn_lb_logits, hgrn_norm_g,
           na_q_norm_g, na_k_norm_g, na_rel_bias, w_branch_a, w_branch_b, w_out,
           ffn_w1, ffn_w3, ffn_conv_w, ffn_conv_b, ffn_w2):
    b, l, d = x.shape
    n_ctx = ctx.shape[1]
    assert ada_w.shape[0] == 1 and b + 1 <= 8 and n_ctx % TOK == 0 and l % (2 * TOK) == 0

    c_all = jnp.zeros((8, d), F32).at[:b].set(c).at[b].set(c_ctx)
    mod = _mod_call(c_all, ada_w[0], ada_b).reshape(8, N_MOD, d)
    sh1, sc1, g1, sh2, sc2, g2 = (mod[:b, n] for n in range(N_MOD))
    ss_ctx = jnp.broadcast_to(mod[b, 0:2][None], (b, 2, d))
    ss1 = jnp.stack([ss_ctx, jnp.stack([sh1, sc1], axis=1)], axis=1)

    h = _prenorm_call(ctx, x, norm1_g, ss1)
    t = n_ctx + l
    p = _inproj_call(h.reshape(b * t, d), w_in[0])
    p3 = p.reshape(b, t, p.shape[1])

    y_a = _hgrn_call(p3, hgrn_lb_logits, hgrn_norm_g, n_ctx, l)
    y_b = _na_call(p3, na_rel_bias[0], na_q_norm_g, na_k_norm_g, n_ctx, l)

    mod3 = jnp.stack([g1, sh2, sc2], axis=1)
    x_mid, h2 = _merge_call(y_a, y_b, p3, x, w_branch_a[0].astype(BF16), w_branch_b[0].astype(BF16),
                            w_out[0].astype(BF16), mod3, norm2_g, n_ctx)

    out = _ffn_call(h2.reshape(b * l, d), x_mid.reshape(b * l, d), ffn_w1[0].astype(BF16),
                    ffn_w3[0].astype(BF16), ffn_conv_w[0], ffn_conv_b, ffn_w2[0].astype(BF16),
                    g2[:, None, :], l)
    return out.reshape(b, l, d)
```

```python
import functools

import numpy as np
import jax
import jax.numpy as jnp
from jax import lax
from jax.experimental import pallas as pl
from jax.experimental.pallas import tpu as pltpu

GRID_W = 64
HEADS = 8
HEAD_DIM = 128
MIX_W = HEADS * HEAD_DIM
CHUNK = 64
N_LEVELS = 6
MXU_LEVEL_HALVES = (4, 2)
PREP_CHUNKS = 8
SCAN_CHUNKS = 8
NA_ROWS = 8
WIN_R = 8
WIN_C = 16
ROPE_THETA = 10000.0
N_MOD = 6
EPS = 1e-6
LOG2E = 1.4426950408889634
TOK = 256
MASKED = -1e30
V7X_VMEM_LIMIT = 60 * 1024 * 1024

BF16 = jnp.bfloat16
F32 = jnp.float32


def _dot(a, b):
    return jnp.dot(a, b, preferred_element_type=F32)


def _dot_nt(a, b):
    return lax.dot_general(a, b, (((1,), (1,)), ((), ())), preferred_element_type=F32)


def _sigmoid(t):
    return 1.0 / (1.0 + jnp.exp(-t))


def _params(sem):
    return pltpu.CompilerParams(dimension_semantics=sem, vmem_limit_bytes=V7X_VMEM_LIMIT)


def _mod_kernel(c_ref, w_ref, b_ref, o_ref):
    cv = c_ref[...]
    s = (cv * _sigmoid(cv)).astype(BF16)
    o_ref[...] = _dot(s, w_ref[...].astype(BF16)) + b_ref[...]


def _mod_call(c_all, w, b):
    d, n = w.shape
    tn = min(1024, d)
    return pl.pallas_call(
        _mod_kernel,
        grid=(n // tn,),
        in_specs=[pl.BlockSpec((8, d), lambda j: (0, 0)),
                  pl.BlockSpec((d, tn), lambda j: (0, j)),
                  pl.BlockSpec((1, tn), lambda j: (0, j))],
        out_specs=pl.BlockSpec((8, tn), lambda j: (0, j)),
        out_shape=jax.ShapeDtypeStruct((8, n), F32),
        compiler_params=_params(("arbitrary",)),
        name="mod",
    )(c_all, w, b)


def _prenorm_kernel(x_ref, ctx_ref, g_ref, ss_ref, o_ref, *, n_lat_blk):
    t = pl.program_id(1)
    xv = jnp.where(t < n_lat_blk, x_ref[...], ctx_ref[...])
    y = xv * lax.rsqrt(jnp.mean(xv * xv, axis=-1, keepdims=True) + EPS) * g_ref[...]
    o_ref[...] = (y * (1.0 + ss_ref[1:2, :]) + ss_ref[0:1, :]).astype(BF16)


def _prenorm_call(x, ctx, g, ss):
    b, l, d = x.shape
    nlb = l // TOK
    nt = nlb + ctx.shape[1] // TOK
    return pl.pallas_call(
        functools.partial(_prenorm_kernel, n_lat_blk=nlb),
        grid=(b, nt),
        in_specs=[pl.BlockSpec((None, TOK, d), lambda i, t: (i, jnp.minimum(t, nlb - 1), 0)),
                  pl.BlockSpec((None, TOK, d), lambda i, t: (i, jnp.maximum(t - nlb, 0), 0)),
                  pl.BlockSpec((1, d), lambda i, t: (0, 0)),
                  pl.BlockSpec((None, None, 2, d), lambda i, t: (i, (t >= nlb).astype(jnp.int32), 0, 0))],
        out_specs=pl.BlockSpec((None, TOK, d), lambda i, t: (i, t, 0)),
        out_shape=jax.ShapeDtypeStruct((b, nt * TOK, d), BF16),
        compiler_params=_params(("parallel", "arbitrary")),
        name="prenorm",
    )(x, ctx, g, ss)


def _inproj_kernel(h_ref, w_ref, o_ref, wb_ref):
    @pl.when(pl.program_id(1) == 0)
    def _():
        wb_ref[...] = w_ref[...].astype(BF16)

    o_ref[...] = _dot(h_ref[...], wb_ref[...]).astype(o_ref.dtype)


def _inproj_call(h, w, col0, n, out_dtype, name):
    m, d = h.shape
    tm = 1024 if m % 1024 == 0 else TOK
    tn = min(1024, d)
    j0 = col0 // tn
    return pl.pallas_call(
        _inproj_kernel,
        grid=(n // tn, m // tm),
        in_specs=[pl.BlockSpec((tm, d), lambda j, i: (i, 0)),
                  pl.BlockSpec((d, tn), lambda j, i: (0, j + j0))],
        out_specs=pl.BlockSpec((tm, tn), lambda j, i: (i, j)),
        out_shape=jax.ShapeDtypeStruct((m, n), out_dtype),
        scratch_shapes=[pltpu.VMEM((d, tn), BF16)],
        compiler_params=_params(("parallel", "arbitrary")),
        name=name,
    )(h, w)


def _hgrn_constants():
    u = np.arange(CHUNK)
    mats, masks = [], []
    for rev in (False, True):
        pos = CHUNK - 1 - u if rev else u
        incl = (pos[None, :] <= pos[:, None]).astype(np.float32)
        blocks, lvl_masks = [incl], []
        for lev in range(N_LEVELS):
            half = CHUNK >> (lev + 1)
            ref = (pos // (2 * half)) * (2 * half) + half - 1
            upto_ref = (pos[None, :] <= ref[:, None]).astype(np.float32)
            if half in MXU_LEVEL_HALVES:
                late = (pos % (2 * half)) >= half
                blocks.append(np.where(late[:, None], 1.0, -1.0).astype(np.float32) * (incl - upto_ref))
            same = (pos[:, None] // (2 * half)) == (pos[None, :] // (2 * half))
            late_q = (pos[:, None] % (2 * half)) >= half
            early_k = (pos[None, :] % (2 * half)) < half
            lvl_masks.append((same & late_q & early_k).astype(np.float32))
        lvl_masks.append(np.eye(CHUNK, dtype=np.float32))
        mats.append(np.concatenate(blocks, axis=0))
        masks.append(np.stack(lvl_masks))
    return np.stack(mats), np.stack(masks)


def _hgrn_kernel(lbl_ref, q_ref, ff_ref, fb_ref, i_ref, gt_ref, cm_ref, mk_ref, ng_ref, y_ref,
                 of_ref, ob_ref, sf_ref, sb_ref, kd_ref, et_ref, qd_ref, sc_ref, *, n_ctx_chunks, n_lat_chunks):
    f_refs, o_refs, st_refs = (ff_ref, fb_ref), (of_ref, ob_ref), (sf_ref, sb_ref)
    lowers = []
    for dirn in range(2):
        la, lb_ = lbl_ref[dirn, 0], lbl_ref[dirn, 1]
        mx = jnp.maximum(la, lb_)
        ea, eb = jnp.exp(la - mx), jnp.exp(lb_ - mx)
        lowers.append(ea / (ea + eb))
        st_refs[dirn][...] = jnp.zeros(st_refs[dirn].shape, F32)
    row_odd = (lax.broadcasted_iota(jnp.int32, (CHUNK, 1), 0) % 2) == 1

    def chunk_rows(pos):
        return pl.ds(pl.multiple_of(pos * CHUNK, CHUNK), CHUNK)

    def level_decay(dirn, lev, f, sums):
        half = CHUNK >> (lev + 1)
        cum = sums[0:CHUNK]
        if half == 1:
            return jnp.where(row_odd, 1.0, f) if dirn else jnp.where(row_odd, f, 1.0)
        if half in MXU_LEVEL_HALVES:
            blk = 1 + MXU_LEVEL_HALVES.index(half)
            return jnp.exp2(sums[blk * CHUNK:(blk + 1) * CHUNK])
        parts = []
        for p in range(0, CHUNK, 2 * half):
            ref = cum[p + half - 1 + dirn:p + half + dirn]
            lo, hi = cum[p:p + half], cum[p + half:p + 2 * half]
            parts += [lo - ref, ref - hi] if dirn else [ref - lo, hi - ref]
        return jnp.exp2(jnp.concatenate(parts, axis=0))

    def prepare(pos0, n, with_out):
        items = [(dirn, pos0 + c) for c in range(n) for dirn in range(2)]
        fs, kks, splits = [], [], []
        for dirn, pos in items:
            lower = lowers[dirn]
            f = lower + (1.0 - lower) * _sigmoid(f_refs[dirn][chunk_rows(pos), :])
            g = jnp.log(f) * LOG2E
            g1 = g.astype(BF16)
            r1 = g - g1.astype(F32)
            g2 = r1.astype(BF16)
            g3 = (r1 - g2.astype(F32)).astype(BF16)
            fs.append(f)
            kks.append(1.0 - f)
            splits.append(jnp.concatenate([g1, g2, g3], axis=0))
        sums = [None] * len(items)
        for dirn in range(2):
            idx = [j for j, it in enumerate(items) if it[0] == dirn]
            wide = _dot(cm_ref[dirn], jnp.concatenate([splits[j] for j in idx], axis=1))
            for c, j in enumerate(idx):
                sums[j] = wide[:, c * HEAD_DIM:(c + 1) * HEAD_DIM]
        for j, (dirn, pos) in enumerate(items):
            cum = sums[j][0:CHUNK]
            last = 0 if dirn else CHUNK - 1
            tot = cum[last:last + 1]
            kd_ref[dirn, chunk_rows(pos), :] = (kks[j] * jnp.exp2(tot - cum)).astype(BF16)
            et_ref[dirn, pos] = jnp.broadcast_to(jnp.exp2(tot), et_ref.shape[2:])
        if not with_out:
            return
        qs = [q_ref[chunk_rows(pos), :] for _, pos in items]
        for j, (dirn, pos) in enumerate(items):
            qd_ref[dirn, chunk_rows(pos), :] = (qs[j] * jnp.exp2(sums[j][0:CHUNK])).astype(BF16)
        qbs = [qv.astype(BF16) for qv in qs]
        kbs = [kv.astype(BF16) for kv in kks]
        scs = [mk_ref[dirn, N_LEVELS] * _dot_nt(qbs[j], kbs[j]) for j, (dirn, _) in enumerate(items)]
        for lev in range(N_LEVELS):
            for j, (dirn, _) in enumerate(items):
                dl = level_decay(dirn, lev, fs[j], sums[j]).astype(BF16)
                scs[j] = scs[j] + mk_ref[dirn, lev] * _dot_nt(qbs[j] * dl, kbs[j] * dl)
        for j, (dirn, pos) in enumerate(items):
            sc_ref[dirn, chunk_rows(pos), :] = scs[j].astype(BF16)

    def scan(pos_f, pos_b, n, with_out):
        items = [(dirn, (pos_b - c) if dirn else (pos_f + c)) for c in range(n) for dirn in range(2)]
        vs = [i_ref[chunk_rows(pos), :] for _, pos in items]
        ups = [_dot(vs[j].T.astype(BF16), kd_ref[dirn, chunk_rows(pos), :]) for j, (dirn, pos) in enumerate(items)]
        sts = [st_refs[0][...], st_refs[1][...]]
        before = []
        for j, (dirn, pos) in enumerate(items):
            before.append(sts[dirn])
            sts[dirn] = sts[dirn] * et_ref[dirn, pos][0:1] + ups[j]
        for dirn in range(2):
            st_refs[dirn][...] = sts[dirn]
        if not with_out:
            return
        for j, (dirn, pos) in enumerate(items):
            orow = chunk_rows(pos)
            o_refs[dirn][orow, :] = (_dot_nt(qd_ref[dirn, orow, :], before[j].astype(BF16))
                                     + _dot(sc_ref[dirn, orow, :], vs[j].astype(BF16)))

    def loop(n, body):
        lax.fori_loop(0, n, lambda ci, c: (body(ci), c)[1], 0)

    ctx0 = n_lat_chunks
    prepare(ctx0, n_ctx_chunks, False)
    loop(n_lat_chunks // PREP_CHUNKS, lambda ci: prepare(ci * PREP_CHUNKS, PREP_CHUNKS, True))
    scan(ctx0, ctx0 + n_ctx_chunks - 1, n_ctx_chunks, False)
    loop(n_lat_chunks // SCAN_CHUNKS,
         lambda ci: scan(ci * SCAN_CHUNKS, n_lat_chunks - 1 - ci * SCAN_CHUNKS, SCAN_CHUNKS, True))

    def readout(i, carry):
        rows = pl.ds(pl.multiple_of(i * TOK, TOK), TOK)
        ot = of_ref[rows, :] + ob_ref[rows, :]
        on = ot * lax.rsqrt(jnp.mean(ot * ot, axis=-1, keepdims=True) + EPS) * ng_ref[...]
        gate = gt_ref[rows, :]
        y_ref[rows, :] = (on * (gate * _sigmoid(gate))).astype(BF16)
        return carry

    lax.fori_loop(0, n_lat_chunks * CHUNK // TOK, readout, 0)


def _hgrn_call(p3, lb_logits, norm_g, n_ctx, l):
    b, t, _ = p3.shape
    cm, mk = _hgrn_constants()
    cm = jnp.asarray(np.concatenate([cm] * 3, axis=-1), BF16)
    mk = jnp.asarray(mk, F32)
    lbl = lb_logits.reshape(2, 2, HEADS, 1, HEAD_DIM)
    tok_spec = lambda grp: pl.BlockSpec((None, t, HEAD_DIM), lambda i, h: (i, 0, grp * HEADS + h))
    full = lambda shape: pl.BlockSpec(shape, lambda i, h: (0,) * len(shape))
    return pl.pallas_call(
        functools.partial(_hgrn_kernel, n_ctx_chunks=n_ctx // CHUNK, n_lat_chunks=l // CHUNK),
        grid=(b, HEADS),
        in_specs=[pl.BlockSpec((2, 2, None, 1, HEAD_DIM), lambda i, h: (0, 0, h, 0, 0)),
                  tok_spec(0), tok_spec(1), tok_spec(2), tok_spec(3), tok_spec(4),
                  full((2, 3 * CHUNK, 3 * CHUNK)),
                  full((2, N_LEVELS + 1, CHUNK, CHUNK)),
                  full((1, HEAD_DIM))],
        out_specs=pl.BlockSpec((None, l, HEAD_DIM), lambda i, h: (i, 0, h)),
        out_shape=jax.ShapeDtypeStruct((b, l, MIX_W), BF16),
        scratch_shapes=[pltpu.VMEM((l, HEAD_DIM), F32), pltpu.VMEM((l, HEAD_DIM), F32),
                        pltpu.VMEM((HEAD_DIM, HEAD_DIM), F32), pltpu.VMEM((HEAD_DIM, HEAD_DIM), F32),
                        pltpu.VMEM((2, t, HEAD_DIM), BF16), pltpu.VMEM((2, t // CHUNK, 8, HEAD_DIM), F32),
                        pltpu.VMEM((2, l, HEAD_DIM), BF16), pltpu.VMEM((2, l, CHUNK), BF16)],
        compiler_params=_params(("parallel", "parallel")),
        name="hgrn",
    )(lbl, p3, p3, p3, p3, p3, cm, mk, norm_g)


def _rope_tables(l):
    pos = np.arange(l)
    nf = HEAD_DIM // 4
    inv = ROPE_THETA ** (-np.arange(nf, dtype=np.float64) / nf)
    ang_r = (pos // GRID_W)[:, None] * inv[None, :]
    ang_c = (pos % GRID_W)[:, None] * inv[None, :]
    cos = np.concatenate([np.cos(ang_r)] * 2 + [np.cos(ang_c)] * 2, axis=-1)
    sin = np.concatenate([-np.sin(ang_r), np.sin(ang_r), -np.sin(ang_c), np.sin(ang_c)], axis=-1)
    return jnp.asarray(cos, F32), jnp.asarray(sin, F32)


def _na_build_bias(rb_ref, tb_ref):
    lanes = 2 * GRID_W
    cq = lax.broadcasted_iota(jnp.int32, (GRID_W, lanes), 0)
    lane = lax.broadcasted_iota(jnp.int32, (GRID_W, lanes), 1)
    ck = lane % GRID_W
    cs = jnp.clip(cq - WIN_C // 2, 0, GRID_W - WIN_C)
    col_in = (ck >= cs) & (ck < cs + WIN_C)

    def toeplitz(d, lane0):
        row = jnp.broadcast_to(rb_ref[d:d + 1, :] * LOG2E, (GRID_W, lanes))
        return pltpu.roll(row, (lane0 - (WIN_C - 1)) % lanes, 1, stride=1, stride_axis=0)

    pair = [jnp.where(col_in, jnp.where(lane < GRID_W, toeplitz(d, 0), toeplitz(d + 1, GRID_W)), MASKED)
            for d in range(2 * WIN_R - 2)]
    for e in range(WIN_R):
        for p in range(WIN_R // 2):
            tb_ref[e, :, p * lanes:(p + 1) * lanes] = pair[2 * p - e + WIN_R - 1]


def _na_kernel(q_ref, k_ref, v_ref, cos_ref, sin_ref, rb_ref, gq_ref, gk_ref, perm_ref, o_ref,
               qs, ks, vs, kcs, vcs, tb_ref, *, n_ctx, grid_rows):
    @pl.when(pl.program_id(1) == 0)
    def _():
        _na_build_bias(rb_ref, tb_ref)

    ones = jnp.ones((2 * HEAD_DIM, HEAD_DIM), BF16)

    def normed(tv, g):
        sq = tv * tv
        hi = sq.astype(BF16)
        lo = (sq - hi.astype(F32)).astype(BF16)
        ssq = _dot(jnp.concatenate([hi, lo], axis=1), ones)
        return tv * lax.rsqrt(ssq * (1.0 / HEAD_DIM) + EPS) * g


    gq = gq_ref[...]
    gk = gk_ref[...]
    n_lat = grid_rows * GRID_W
    kcs[...] = normed(k_ref[n_lat:n_lat + n_ctx, :], gk).astype(BF16)
    vcs[...] = v_ref[n_lat:n_lat + n_ctx, :].astype(BF16)

    def prep(i, carry):
        rows = [pl.ds(pl.multiple_of((2 * i + n) * TOK, TOK), TOK) for n in range(2)]
        srcs = rows
        jobs = [(src_ref, g, dst, n) for n in range(2) for src_ref, g, dst in ((q_ref, gq, qs), (k_ref, gk, ks))]
        nrm = [normed(src_ref[srcs[n], :], g) for src_ref, g, _, n in jobs]
        par = [_dot(tv.astype(BF16), perm_ref[...]) for tv in nrm]
        for (_, _, dst, n), tv, pv in zip(jobs, nrm, par):
            dst[rows[n], :] = (tv * cos_ref[rows[n], :] + pv * sin_ref[rows[n], :]).astype(BF16)
        for n in range(2):
            vs[rows[n], :] = v_ref[srcs[n], :].astype(BF16)
        return carry

    lax.fori_loop(0, grid_rows * GRID_W // (2 * TOK), prep, 0)
    scale = HEAD_DIM ** -0.5 * LOG2E
    band = WIN_R * GRID_W

    def rows_step(i, carry):
        rr = [i * NA_ROWS + n for n in range(NA_ROWS)]
        rs = [jnp.clip(r - WIN_R // 2, 0, grid_rows - WIN_R) for r in rr]
        qrow = [pl.ds(pl.multiple_of(r * GRID_W, GRID_W), GRID_W) for r in rr]
        krow = [pl.ds(pl.multiple_of(s * GRID_W, GRID_W), band) for s in rs]
        qr = [qs[qw, :] for qw in qrow]
        kc = kcs[...]
        sb = [_dot_nt(qr[n], ks[krow[n], :]) * scale + tb_ref[rr[n] - rs[n]] for n in range(NA_ROWS)]
        sc = [_dot_nt(qr[n], kc) * scale for n in range(NA_ROWS)]
        m = [jnp.maximum(jnp.max(sb[n], axis=-1, keepdims=True), jnp.max(sc[n], axis=-1, keepdims=True))
             for n in range(NA_ROWS)]
        pb = [jnp.exp2(sb[n] - m[n]) for n in range(NA_ROWS)]
        pc = [jnp.exp2(sc[n] - m[n]) for n in range(NA_ROWS)]
        den = [jnp.sum(pb[n], axis=-1, keepdims=True) + jnp.sum(pc[n], axis=-1, keepdims=True)
               for n in range(NA_ROWS)]
        vc = vcs[...]
        o = [_dot(pb[n].astype(BF16), vs[krow[n], :]) + _dot(pc[n].astype(BF16), vc) for n in range(NA_ROWS)]
        for n in range(NA_ROWS):
            o_ref[qrow[n], :] = (o[n] / den[n]).astype(BF16)
        return carry

    lax.fori_loop(0, grid_rows // NA_ROWS, rows_step, 0)


def _na_call(p3, rel_bias, gq, gk, n_ctx, l):
    b, t, _ = p3.shape
    cos, sin = _rope_tables(l)
    nr, nc = rel_bias.shape[1:]
    rb = jnp.zeros((HEADS, 2 * WIN_R, 2 * GRID_W), F32).at[:, :nr, :nc].set(rel_bias)
    lanes = np.arange(HEAD_DIM)
    partner = np.where(lanes % (HEAD_DIM // 2) < HEAD_DIM // 4, lanes + HEAD_DIM // 4, lanes - HEAD_DIM // 4)
    perm = jnp.asarray(lanes[:, None] == partner[None, :], BF16)
    col = lambda grp: (lambda h, i: (i, 0, grp * HEADS + h))
    tok_spec = lambda im: pl.BlockSpec((None, t, HEAD_DIM), im)
    full = lambda shape: pl.BlockSpec(shape, lambda h, i: (0,) * len(shape))
    return pl.pallas_call(
        functools.partial(_na_kernel, n_ctx=n_ctx, grid_rows=l // GRID_W),
        grid=(HEADS, b),
        in_specs=[tok_spec(col(5)), tok_spec(col(6)), tok_spec(col(7)),
                  full((l, HEAD_DIM)), full((l, HEAD_DIM)),
                  pl.BlockSpec((None, 2 * WIN_R, 2 * GRID_W), lambda h, i: (h, 0, 0)),
                  full((1, HEAD_DIM)), full((1, HEAD_DIM)), full((HEAD_DIM, HEAD_DIM))],
        out_specs=pl.BlockSpec((None, l, HEAD_DIM), lambda h, i: (i, 0, h)),
        out_shape=jax.ShapeDtypeStruct((b, l, MIX_W), BF16),
        scratch_shapes=[pltpu.VMEM((l, HEAD_DIM), BF16), pltpu.VMEM((l, HEAD_DIM), BF16),
                        pltpu.VMEM((l, HEAD_DIM), BF16), pltpu.VMEM((n_ctx, HEAD_DIM), BF16),
                        pltpu.VMEM((n_ctx, HEAD_DIM), BF16),
                        pltpu.VMEM((WIN_R, GRID_W, WIN_R * GRID_W), F32)],
        compiler_params=_params(("parallel", "arbitrary")),
        name="na",
    )(p3, p3, p3, cos, sin, rb, gq, gk, perm)


def _merge_kernel(ya_ref, yb_ref, ga_ref, gb_ref, x_ref, wa_ref, wb_ref, wo_ref, mod_ref, n2_ref,
                  xm_ref, h2_ref):
    tm = x_ref.shape[0]
    halves = [pl.ds(n * (tm // 2), tm // 2) for n in range(2)]
    za = [_dot(ya_ref[r, :], wa_ref[...]) for r in halves]
    zb = [_dot(yb_ref[r, :], wb_ref[...]) for r in halves]
    z = [(_sigmoid(ga_ref[r, :].astype(F32)) * za[n] + _sigmoid(gb_ref[r, :].astype(F32)) * zb[n]).astype(BF16)
         for n, r in enumerate(halves)]
    br = [_dot(zn, wo_ref[...]) for zn in z]
    for n, r in enumerate(halves):
        xm = x_ref[r, :] + mod_ref[0:1, :] * br[n]
        xm_ref[r, :] = xm
        y = xm * lax.rsqrt(jnp.mean(xm * xm, axis=-1, keepdims=True) + EPS) * n2_ref[...]
        h2_ref[r, :] = (y * (1.0 + mod_ref[2:3, :]) + mod_ref[1:2, :]).astype(BF16)


def _merge_call(ya, yb, pg3, x, wa, wb, wo, mod3, n2):
    b, l, d = x.shape
    tm = 2 * TOK
    const = lambda shape: pl.BlockSpec(shape, lambda i, t: (0,) * len(shape), pipeline_mode=pl.Buffered(1))
    return pl.pallas_call(
        _merge_kernel,
        grid=(b, l // tm),
        in_specs=[pl.BlockSpec((None, tm, MIX_W), lambda i, t: (i, t, 0)),
                  pl.BlockSpec((None, tm, MIX_W), lambda i, t: (i, t, 0)),
                  pl.BlockSpec((None, tm, d), lambda i, t: (i, t, 0)),
                  pl.BlockSpec((None, tm, d), lambda i, t: (i, t, 1)),
                  pl.BlockSpec((None, tm, d), lambda i, t: (i, t, 0)),
                  const((MIX_W, d)), const((MIX_W, d)), const((d, d)),
                  pl.BlockSpec((None, 3, d), lambda i, t: (i, 0, 0)),
                  pl.BlockSpec((1, d), lambda i, t: (0, 0))],
        out_specs=[pl.BlockSpec((None, tm, d), lambda i, t: (i, t, 0)),
                   pl.BlockSpec((None, tm, d), lambda i, t: (i, t, 0))],
        out_shape=[jax.ShapeDtypeStruct((b, l, d), F32), jax.ShapeDtypeStruct((b, l, d), BF16)],
        compiler_params=_params(("parallel", "arbitrary")),
        name="merge",
    )(ya, yb, pg3, pg3, x, wa, wb, wo, mod3, n2)


def _ffn_kernel(h_ref, hp_ref, hn_ref, w1_ref, w3_ref, cw_ref, cb_ref, w2_ref, xm_ref, g2_ref, o_ref, a_ref,
                hc_ref, *, tiles_per_seq, halo):
    i = pl.program_id(0)
    s = pl.program_id(1)
    n_hid, tm, th = a_ref.shape

    @pl.when(s == 0)
    def _():
        hc_ref[0:tm, :] = h_ref[...]
        hc_ref[tm:tm + halo, :] = hp_ref[...]
        hc_ref[tm + halo:tm + 2 * halo, :] = hn_ref[...]

    @pl.when(s < n_hid)
    def _():
        t1_all = _dot(hc_ref[...], w1_ref[...])
        t1 = t1_all[0:tm]
        prev_row = t1_all[tm + halo - 1:tm + halo, :]
        next_row = t1_all[tm + halo:tm + halo + 1, :]
        prev_row = jnp.where(i % tiles_per_seq == 0, 0.0, prev_row)
        next_row = jnp.where(i % tiles_per_seq == tiles_per_seq - 1, 0.0, next_row)
        ridx = lax.broadcasted_iota(jnp.int32, (tm, 1), 0)
        up = jnp.where(ridx == 0, prev_row, pltpu.roll(t1, 1, 0))
        dn = jnp.where(ridx == tm - 1, next_row, pltpu.roll(t1, tm - 1, 0))
        u = up * cw_ref[0:1, :] + t1 * cw_ref[1:2, :] + dn * cw_ref[2:3, :] + cb_ref[...]
        a_ref[s] = ((u * _sigmoid(u)) * _dot(hc_ref[0:tm, :], w3_ref[...])).astype(BF16)

    @pl.when(s >= n_hid)
    def _():
        acc = _dot(a_ref[0], w2_ref[0:th, :])
        for k in range(1, n_hid):
            acc = acc + _dot(a_ref[k], w2_ref[k * th:(k + 1) * th, :])
        o_ref[...] = xm_ref[...] + g2_ref[...] * acc


def _ffn_call(h2, xm, w1, w3, cw, cb, w2, g2, l):
    m, d = h2.shape
    hid = w1.shape[1]
    tm = 1024
    th = min(512, hid)
    tn = 256
    n_hid = hid // th
    halo = 16
    per = tm // halo
    nblk = m // halo
    hcol = lambda i, s: (0, jnp.minimum(s, n_hid - 1))
    ocol = lambda s: jnp.maximum(s - n_hid, 0)
    return pl.pallas_call(
        functools.partial(_ffn_kernel, tiles_per_seq=l // tm, halo=halo),
        grid=(m // tm, n_hid + d // tn),
        in_specs=[pl.BlockSpec((tm, d), lambda i, s: (i, 0)),
                  pl.BlockSpec((halo, d), lambda i, s: (jnp.maximum(i * per - 1, 0), 0)),
                  pl.BlockSpec((halo, d), lambda i, s: (jnp.minimum((i + 1) * per, nblk - 1), 0)),
                  pl.BlockSpec((d, th), hcol),
                  pl.BlockSpec((d, th), hcol),
                  pl.BlockSpec((3, th), hcol),
                  pl.BlockSpec((1, th), hcol),
                  pl.BlockSpec((hid, tn), lambda i, s: (0, ocol(s))),
                  pl.BlockSpec((tm, tn), lambda i, s: (i, ocol(s))),
                  pl.BlockSpec((None, 1, tn), lambda i, s: (i // (l // tm), 0, ocol(s)))],
        out_specs=pl.BlockSpec((tm, tn), lambda i, s: (i, ocol(s))),
        out_shape=jax.ShapeDtypeStruct((m, d), F32),
        scratch_shapes=[pltpu.VMEM((n_hid, tm, th), BF16), pltpu.VMEM((tm + 2 * halo, d), BF16)],
        compiler_params=_params(("parallel", "arbitrary")),
        name="ffn",
    )(h2, h2, h2, w1, w3, cw, cb, w2, xm, g2)


def kernel(x, c, ctx, c_ctx, ada_w, ada_b, norm1_g, norm2_g, w_in, hgrn_lb_logits, hgrn_norm_g,
           na_q_norm_g, na_k_norm_g, na_rel_bias, w_branch_a, w_branch_b, w_out,
           ffn_w1, ffn_w3, ffn_conv_w, ffn_conv_b, ffn_w2):
    b, l, d = x.shape
    n_ctx = ctx.shape[1]
    assert ada_w.shape[0] == 1 and b + 1 <= 8 and n_ctx % TOK == 0 and l % (2 * TOK) == 0

    c_all = jnp.zeros((8, d), F32).at[:b].set(c).at[b].set(c_ctx)
    mod = _mod_call(c_all, ada_w[0], ada_b).reshape(8, N_MOD, d)
    sh1, sc1, g1, sh2, sc2, g2 = (mod[:b, n] for n in range(N_MOD))
    ss_ctx = jnp.broadcast_to(mod[b, 0:2][None], (b, 2, d))
    ss1 = jnp.stack([jnp.stack([sh1, sc1], axis=1), ss_ctx], axis=1)

    h = _prenorm_call(x, ctx, norm1_g, ss1)
    t = l + n_ctx
    h = h.reshape(b * t, d)
    n_mix = 8 * MIX_W
    p3 = _inproj_call(h, w_in[0], 0, n_mix, F32, "inproj").reshape(b, t, n_mix)
    pg3 = _inproj_call(h, w_in[0], n_mix, 2 * d, BF16, "inproj_gates").reshape(b, t, 2 * d)

    y_a = _hgrn_call(p3, hgrn_lb_logits, hgrn_norm_g, n_ctx, l)
    y_b = _na_call(p3, na_rel_bias[0], na_q_norm_g, na_k_norm_g, n_ctx, l)

    mod3 = jnp.stack([g1, sh2, sc2], axis=1)
    x_mid, h2 = _merge_call(y_a, y_b, pg3, x, w_branch_a[0].astype(BF16), w_branch_b[0].astype(BF16),
                            w_out[0].astype(BF16), mod3, norm2_g)

    out = _ffn_call(h2.reshape(b * l, d), x_mid.reshape(b * l, d), ffn_w1[0].astype(BF16),
                    ffn_w3[0].astype(BF16), ffn_conv_w[0], ffn_conv_b, ffn_w2[0].astype(BF16),
                    g2[:, None, :], l)
    return out.reshape(b, l, d)
```

```python
import functools

import numpy as np
import jax
import jax.numpy as jnp
from jax import lax
from jax.experimental import pallas as pl
from jax.experimental.pallas import tpu as pltpu

GRID_W = 64
HEADS = 8
HEAD_DIM = 128
MIX_W = HEADS * HEAD_DIM
CHUNK = 64
N_LEVELS = 6
MXU_LEVEL_HALVES = (4, 2)
PREP_CHUNKS = 8
SCAN_CHUNKS = 8
HGRN_HEADS_PER_STEP = 2
NA_HEADS_PER_STEP = 2
NA_ROWS = 8
WIN_R = 8
WIN_C = 16
ROPE_THETA = 10000.0
N_MOD = 6
EPS = 1e-6
LOG2E = 1.4426950408889634
TOK = 256
MASKED = -1e30
V7X_VMEM_LIMIT = 60 * 1024 * 1024

BF16 = jnp.bfloat16
F32 = jnp.float32


def _dot(a, b):
    return jnp.dot(a, b, preferred_element_type=F32)


def _dot_nt(a, b):
    return lax.dot_general(a, b, (((1,), (1,)), ((), ())), preferred_element_type=F32)


def _sigmoid(t):
    return 1.0 / (1.0 + jnp.exp(-t))


def _params(sem):
    return pltpu.CompilerParams(dimension_semantics=sem, vmem_limit_bytes=V7X_VMEM_LIMIT)


def _mod_kernel(c_ref, w_ref, b_ref, o_ref):
    cv = c_ref[...]
    s = (cv * _sigmoid(cv)).astype(BF16)
    o_ref[...] = _dot(s, w_ref[...].astype(BF16)) + b_ref[...]


def _mod_call(c_all, w, b):
    d, n = w.shape
    tn = min(1024, d)
    return pl.pallas_call(
        _mod_kernel,
        grid=(n // tn,),
        in_specs=[pl.BlockSpec((8, d), lambda j: (0, 0)),
                  pl.BlockSpec((d, tn), lambda j: (0, j)),
                  pl.BlockSpec((1, tn), lambda j: (0, j))],
        out_specs=pl.BlockSpec((8, tn), lambda j: (0, j)),
        out_shape=jax.ShapeDtypeStruct((8, n), F32),
        compiler_params=_params(("arbitrary",)),
        name="mod",
    )(c_all, w, b)


def _prenorm_kernel(x_ref, ctx_ref, g_ref, ss_ref, o_ref, *, n_lat_blk):
    t = pl.program_id(1)
    xv = jnp.where(t < n_lat_blk, x_ref[...], ctx_ref[...])
    y = xv * lax.rsqrt(jnp.mean(xv * xv, axis=-1, keepdims=True) + EPS) * g_ref[...]
    o_ref[...] = (y * (1.0 + ss_ref[1:2, :]) + ss_ref[0:1, :]).astype(BF16)


def _prenorm_call(x, ctx, g, ss):
    b, l, d = x.shape
    nlb = l // TOK
    nt = nlb + ctx.shape[1] // TOK
    return pl.pallas_call(
        functools.partial(_prenorm_kernel, n_lat_blk=nlb),
        grid=(b, nt),
        in_specs=[pl.BlockSpec((None, TOK, d), lambda i, t: (i, jnp.minimum(t, nlb - 1), 0)),
                  pl.BlockSpec((None, TOK, d), lambda i, t: (i, jnp.maximum(t - nlb, 0), 0)),
                  pl.BlockSpec((1, d), lambda i, t: (0, 0)),
                  pl.BlockSpec((None, None, 2, d), lambda i, t: (i, (t >= nlb).astype(jnp.int32), 0, 0))],
        out_specs=pl.BlockSpec((None, TOK, d), lambda i, t: (i, t, 0)),
        out_shape=jax.ShapeDtypeStruct((b, nt * TOK, d), BF16),
        compiler_params=_params(("parallel", "arbitrary")),
        name="prenorm",
    )(x, ctx, g, ss)


def _inproj_kernel(h_ref, w_ref, o_ref, wb_ref):
    @pl.when(pl.program_id(1) == 0)
    def _():
        wb_ref[...] = w_ref[...].astype(BF16)

    o_ref[...] = _dot(h_ref[...], wb_ref[...]).astype(o_ref.dtype)


def _inproj_call(h, w, col0, n, out_dtype, name):
    m, d = h.shape
    tm = 1024 if m % 1024 == 0 else TOK
    tn = min(1024, d)
    j0 = col0 // tn
    return pl.pallas_call(
        _inproj_kernel,
        grid=(n // tn, m // tm),
        in_specs=[pl.BlockSpec((tm, d), lambda j, i: (i, 0)),
                  pl.BlockSpec((d, tn), lambda j, i: (0, j + j0))],
        out_specs=pl.BlockSpec((tm, tn), lambda j, i: (i, j)),
        out_shape=jax.ShapeDtypeStruct((m, n), out_dtype),
        scratch_shapes=[pltpu.VMEM((d, tn), BF16)],
        compiler_params=_params(("parallel", "arbitrary")),
        name=name,
    )(h, w)


def _hgrn_constants():
    u = np.arange(CHUNK)
    mats, masks = [], []
    for rev in (False, True):
        pos = CHUNK - 1 - u if rev else u
        incl = (pos[None, :] <= pos[:, None]).astype(np.float32)
        blocks, lvl_masks = [incl], []
        for lev in range(N_LEVELS):
            half = CHUNK >> (lev + 1)
            ref = (pos // (2 * half)) * (2 * half) + half - 1
            upto_ref = (pos[None, :] <= ref[:, None]).astype(np.float32)
            if half in MXU_LEVEL_HALVES:
                late = (pos % (2 * half)) >= half
                blocks.append(np.where(late[:, None], 1.0, -1.0).astype(np.float32) * (incl - upto_ref))
            same = (pos[:, None] // (2 * half)) == (pos[None, :] // (2 * half))
            late_q = (pos[:, None] % (2 * half)) >= half
            early_k = (pos[None, :] % (2 * half)) < half
            lvl_masks.append((same & late_q & early_k).astype(np.float32))
        lvl_masks.append(np.eye(CHUNK, dtype=np.float32))
        mats.append(np.concatenate(blocks, axis=0))
        masks.append(np.stack(lvl_masks))
    return np.stack(mats), np.stack(masks)


def _hgrn_kernel(lbl_ref, q_ref, ff_ref, fb_ref, i_ref, gt_ref, cm_ref, mk_ref, ng_ref, y_ref, *scratch, **chunks):
    for hh in range(HGRN_HEADS_PER_STEP):
        lanes = pl.ds(hh * HEAD_DIM, HEAD_DIM)
        head = [r.at[:, lanes] for r in (q_ref, ff_ref, fb_ref, i_ref, gt_ref)]
        _hgrn_head(lbl_ref.at[:, :, hh], *head, cm_ref, mk_ref, ng_ref, y_ref.at[:, lanes], *scratch, **chunks)


def _hgrn_head(lbl_ref, q_ref, ff_ref, fb_ref, i_ref, gt_ref, cm_ref, mk_ref, ng_ref, y_ref,
               of_ref, ob_ref, sf_ref, sb_ref, kd_ref, et_ref, qd_ref, sc_ref, *, n_ctx_chunks, n_lat_chunks):
    f_refs, o_refs, st_refs = (ff_ref, fb_ref), (of_ref, ob_ref), (sf_ref, sb_ref)
    lowers = []
    for dirn in range(2):
        la, lb_ = lbl_ref[dirn, 0], lbl_ref[dirn, 1]
        mx = jnp.maximum(la, lb_)
        ea, eb = jnp.exp(la - mx), jnp.exp(lb_ - mx)
        lowers.append(ea / (ea + eb))
        st_refs[dirn][...] = jnp.zeros(st_refs[dirn].shape, F32)
    row_odd = (lax.broadcasted_iota(jnp.int32, (CHUNK, 1), 0) % 2) == 1

    def chunk_rows(pos):
        return pl.ds(pl.multiple_of(pos * CHUNK, CHUNK), CHUNK)

    def level_decay(dirn, lev, f, sums):
        half = CHUNK >> (lev + 1)
        cum = sums[0:CHUNK]
        if half == 1:
            return jnp.where(row_odd, 1.0, f) if dirn else jnp.where(row_odd, f, 1.0)
        if half in MXU_LEVEL_HALVES:
            blk = 1 + MXU_LEVEL_HALVES.index(half)
            return jnp.exp2(sums[blk * CHUNK:(blk + 1) * CHUNK])
        parts = []
        for p in range(0, CHUNK, 2 * half):
            ref = cum[p + half - 1 + dirn:p + half + dirn]
            lo, hi = cum[p:p + half], cum[p + half:p + 2 * half]
            parts += [lo - ref, ref - hi] if dirn else [ref - lo, hi - ref]
        return jnp.exp2(jnp.concatenate(parts, axis=0))

    def prepare(pos0, n, with_out):
        items = [(dirn, pos0 + c) for c in range(n) for dirn in range(2)]
        fs, kks, splits = [], [], []
        for dirn, pos in items:
            lower = lowers[dirn]
            f = lower + (1.0 - lower) * _sigmoid(f_refs[dirn][chunk_rows(pos), :])
            g = jnp.log(f) * LOG2E
            g1 = g.astype(BF16)
            r1 = g - g1.astype(F32)
            g2 = r1.astype(BF16)
            g3 = (r1 - g2.astype(F32)).astype(BF16)
            fs.append(f)
            kks.append(1.0 - f)
            splits.append(jnp.concatenate([g1, g2, g3], axis=0))
        sums = [None] * len(items)
        for dirn in range(2):
            idx = [j for j, it in enumerate(items) if it[0] == dirn]
            wide = _dot(cm_ref[dirn], jnp.concatenate([splits[j] for j in idx], axis=1))
            for c, j in enumerate(idx):
                sums[j] = wide[:, c * HEAD_DIM:(c + 1) * HEAD_DIM]
        for j, (dirn, pos) in enumerate(items):
            cum = sums[j][0:CHUNK]
            last = 0 if dirn else CHUNK - 1
            tot = cum[last:last + 1]
            kd_ref[dirn, chunk_rows(pos), :] = (kks[j] * jnp.exp2(tot - cum)).astype(BF16)
            et_ref[dirn, pos] = jnp.broadcast_to(jnp.exp2(tot), et_ref.shape[2:])
        if not with_out:
            return
        qs = [q_ref[chunk_rows(pos), :] for _, pos in items]
        for j, (dirn, pos) in enumerate(items):
            qd_ref[dirn, chunk_rows(pos), :] = (qs[j] * jnp.exp2(sums[j][0:CHUNK])).astype(BF16)
        qbs = [qv.astype(BF16) for qv in qs]
        kbs = [kv.astype(BF16) for kv in kks]
        scs = [mk_ref[dirn, N_LEVELS] * _dot_nt(qbs[j], kbs[j]) for j, (dirn, _) in enumerate(items)]
        for lev in range(N_LEVELS):
            for j, (dirn, _) in enumerate(items):
                dl = level_decay(dirn, lev, fs[j], sums[j]).astype(BF16)
                scs[j] = scs[j] + mk_ref[dirn, lev] * _dot_nt(qbs[j] * dl, kbs[j] * dl)
        for j, (dirn, pos) in enumerate(items):
            sc_ref[dirn, chunk_rows(pos), :] = scs[j].astype(BF16)

    def scan(pos_f, pos_b, n, with_out):
        items = [(dirn, (pos_b - c) if dirn else (pos_f + c)) for c in range(n) for dirn in range(2)]
        vs = [i_ref[chunk_rows(pos), :] for _, pos in items]
        ups = [_dot(vs[j].T.astype(BF16), kd_ref[dirn, chunk_rows(pos), :]) for j, (dirn, pos) in enumerate(items)]
        sts = [st_refs[0][...], st_refs[1][...]]
        before = []
        for j, (dirn, pos) in enumerate(items):
            before.append(sts[dirn])
            sts[dirn] = sts[dirn] * et_ref[dirn, pos][0:1] + ups[j]
        for dirn in range(2):
            st_refs[dirn][...] = sts[dirn]
        if not with_out:
            return
        for j, (dirn, pos) in enumerate(items):
            orow = chunk_rows(pos)
            o_refs[dirn][orow, :] = (_dot_nt(qd_ref[dirn, orow, :], before[j].astype(BF16))
                                     + _dot(sc_ref[dirn, orow, :], vs[j].astype(BF16)))

    def loop(n, body):
        lax.fori_loop(0, n, lambda ci, c: (body(ci), c)[1], 0)

    ctx0 = n_lat_chunks
    prepare(ctx0, n_ctx_chunks, False)
    loop(n_lat_chunks // PREP_CHUNKS, lambda ci: prepare(ci * PREP_CHUNKS, PREP_CHUNKS, True))
    scan(ctx0, ctx0 + n_ctx_chunks - 1, n_ctx_chunks, False)
    loop(n_lat_chunks // SCAN_CHUNKS,
         lambda ci: scan(ci * SCAN_CHUNKS, n_lat_chunks - 1 - ci * SCAN_CHUNKS, SCAN_CHUNKS, True))

    def readout(i, carry):
        rows = pl.ds(pl.multiple_of(i * TOK, TOK), TOK)
        ot = of_ref[rows, :] + ob_ref[rows, :]
        on = ot * lax.rsqrt(jnp.mean(ot * ot, axis=-1, keepdims=True) + EPS) * ng_ref[...]
        gate = gt_ref[rows, :]
        y_ref[rows, :] = (on * (gate * _sigmoid(gate))).astype(BF16)
        return carry

    lax.fori_loop(0, n_lat_chunks * CHUNK // TOK, readout, 0)


def _hgrn_call(p3, lb_logits, norm_g, n_ctx, l):
    b, t, _ = p3.shape
    cm, mk = _hgrn_constants()
    cm = jnp.asarray(np.concatenate([cm] * 3, axis=-1), BF16)
    mk = jnp.asarray(mk, F32)
    lbl = lb_logits.reshape(2, 2, HEADS, 1, HEAD_DIM)
    hps = HGRN_HEADS_PER_STEP
    steps = HEADS // hps
    tok_spec = lambda grp: pl.BlockSpec((None, t, hps * HEAD_DIM), lambda i, h: (i, 0, grp * steps + h))
    full = lambda shape: pl.BlockSpec(shape, lambda i, h: (0,) * len(shape))
    return pl.pallas_call(
        functools.partial(_hgrn_kernel, n_ctx_chunks=n_ctx // CHUNK, n_lat_chunks=l // CHUNK),
        grid=(b, steps),
        in_specs=[pl.BlockSpec((2, 2, hps, 1, HEAD_DIM), lambda i, h: (0, 0, h, 0, 0)),
                  tok_spec(0), tok_spec(1), tok_spec(2), tok_spec(3), tok_spec(4),
                  full((2, 3 * CHUNK, 3 * CHUNK)),
                  full((2, N_LEVELS + 1, CHUNK, CHUNK)),
                  full((1, HEAD_DIM))],
        out_specs=pl.BlockSpec((None, l, hps * HEAD_DIM), lambda i, h: (i, 0, h)),
        out_shape=jax.ShapeDtypeStruct((b, l, MIX_W), BF16),
        scratch_shapes=[pltpu.VMEM((l, HEAD_DIM), F32), pltpu.VMEM((l, HEAD_DIM), F32),
                        pltpu.VMEM((HEAD_DIM, HEAD_DIM), F32), pltpu.VMEM((HEAD_DIM, HEAD_DIM), F32),
                        pltpu.VMEM((2, t, HEAD_DIM), BF16), pltpu.VMEM((2, t // CHUNK, 8, HEAD_DIM), F32),
                        pltpu.VMEM((2, l, HEAD_DIM), BF16), pltpu.VMEM((2, l, CHUNK), BF16)],
        compiler_params=_params(("parallel", "parallel")),
        name="hgrn",
    )(lbl, p3, p3, p3, p3, p3, cm, mk, norm_g)


def _rope_tables(l):
    pos = np.arange(l)
    nf = HEAD_DIM // 4
    inv = ROPE_THETA ** (-np.arange(nf, dtype=np.float64) / nf)
    ang_r = (pos // GRID_W)[:, None] * inv[None, :]
    ang_c = (pos % GRID_W)[:, None] * inv[None, :]
    cos = np.concatenate([np.cos(ang_r)] * 2 + [np.cos(ang_c)] * 2, axis=-1)
    sin = np.concatenate([-np.sin(ang_r), np.sin(ang_r), -np.sin(ang_c), np.sin(ang_c)], axis=-1)
    return jnp.asarray(cos, F32), jnp.asarray(sin, F32)


def _na_build_bias(rb_ref, tb_ref):
    lanes = 2 * GRID_W
    cq = lax.broadcasted_iota(jnp.int32, (GRID_W, lanes), 0)
    lane = lax.broadcasted_iota(jnp.int32, (GRID_W, lanes), 1)
    ck = lane % GRID_W
    cs = jnp.clip(cq - WIN_C // 2, 0, GRID_W - WIN_C)
    col_in = (ck >= cs) & (ck < cs + WIN_C)

    def toeplitz(d, lane0):
        row = jnp.broadcast_to(rb_ref[d:d + 1, :] * LOG2E, (GRID_W, lanes))
        return pltpu.roll(row, (lane0 - (WIN_C - 1)) % lanes, 1, stride=1, stride_axis=0)

    pair = [jnp.where(col_in, jnp.where(lane < GRID_W, toeplitz(d, 0), toeplitz(d + 1, GRID_W)), MASKED)
            for d in range(2 * WIN_R - 2)]
    for e in range(WIN_R):
        for p in range(WIN_R // 2):
            tb_ref[e, :, p * lanes:(p + 1) * lanes] = pair[2 * p - e + WIN_R - 1]


def _na_kernel(q_ref, k_ref, v_ref, cos_ref, sin_ref, rb_ref, gq_ref, gk_ref, perm_ref, o_ref,
               qs, ks, vs, kcs, vcs, tb_ref, **static):
    for hh in range(NA_HEADS_PER_STEP):
        lanes = pl.ds(hh * HEAD_DIM, HEAD_DIM)
        head = [r.at[:, lanes] for r in (q_ref, k_ref, v_ref)]
        _na_head(*head, cos_ref, sin_ref, rb_ref.at[hh], gq_ref, gk_ref, perm_ref, o_ref.at[:, lanes],
                 qs, ks, vs, kcs, vcs, tb_ref.at[hh], **static)


def _na_head(q_ref, k_ref, v_ref, cos_ref, sin_ref, rb_ref, gq_ref, gk_ref, perm_ref, o_ref,
             qs, ks, vs, kcs, vcs, tb_ref, *, n_ctx, grid_rows):
    @pl.when(pl.program_id(1) == 0)
    def _():
        _na_build_bias(rb_ref, tb_ref)

    ones = jnp.ones((2 * HEAD_DIM, HEAD_DIM), BF16)

    def normed(tv, g):
        sq = tv * tv
        hi = sq.astype(BF16)
        lo = (sq - hi.astype(F32)).astype(BF16)
        ssq = _dot(jnp.concatenate([hi, lo], axis=1), ones)
        return tv * lax.rsqrt(ssq * (1.0 / HEAD_DIM) + EPS) * g


    gq = gq_ref[...]
    gk = gk_ref[...]
    n_lat = grid_rows * GRID_W
    kcs[...] = normed(k_ref[n_lat:n_lat + n_ctx, :], gk).astype(BF16)
    vcs[...] = v_ref[n_lat:n_lat + n_ctx, :].astype(BF16)

    def prep(i, carry):
        rows = [pl.ds(pl.multiple_of((2 * i + n) * TOK, TOK), TOK) for n in range(2)]
        srcs = rows
        jobs = [(src_ref, g, dst, n) for n in range(2) for src_ref, g, dst in ((q_ref, gq, qs), (k_ref, gk, ks))]
        nrm = [normed(src_ref[srcs[n], :], g) for src_ref, g, _, n in jobs]
        par = [_dot(tv.astype(BF16), perm_ref[...]) for tv in nrm]
        for (_, _, dst, n), tv, pv in zip(jobs, nrm, par):
            dst[rows[n], :] = (tv * cos_ref[rows[n], :] + pv * sin_ref[rows[n], :]).astype(BF16)
        for n in range(2):
            vs[rows[n], :] = v_ref[srcs[n], :].astype(BF16)
        return carry

    lax.fori_loop(0, grid_rows * GRID_W // (2 * TOK), prep, 0)
    scale = HEAD_DIM ** -0.5 * LOG2E
    band = WIN_R * GRID_W

    def rows_step(i, carry):
        rr = [i * NA_ROWS + n for n in range(NA_ROWS)]
        rs = [jnp.clip(r - WIN_R // 2, 0, grid_rows - WIN_R) for r in rr]
        qrow = [pl.ds(pl.multiple_of(r * GRID_W, GRID_W), GRID_W) for r in rr]
        krow = [pl.ds(pl.multiple_of(s * GRID_W, GRID_W), band) for s in rs]
        qr = [qs[qw, :] for qw in qrow]
        kc = kcs[...]
        sb = [_dot_nt(qr[n], ks[krow[n], :]) * scale + tb_ref[rr[n] - rs[n]] for n in range(NA_ROWS)]
        sc = [_dot_nt(qr[n], kc) * scale for n in range(NA_ROWS)]
        m = [jnp.maximum(jnp.max(sb[n], axis=-1, keepdims=True), jnp.max(sc[n], axis=-1, keepdims=True))
             for n in range(NA_ROWS)]
        pb = [jnp.exp2(sb[n] - m[n]) for n in range(NA_ROWS)]
        pc = [jnp.exp2(sc[n] - m[n]) for n in range(NA_ROWS)]
        den = [jnp.sum(pb[n], axis=-1, keepdims=True) + jnp.sum(pc[n], axis=-1, keepdims=True)
               for n in range(NA_ROWS)]
        vc = vcs[...]
        o = [_dot(pb[n].astype(BF16), vs[krow[n], :]) + _dot(pc[n].astype(BF16), vc) for n in range(NA_ROWS)]
        for n in range(NA_ROWS):
            o_ref[qrow[n], :] = (o[n] / den[n]).astype(BF16)
        return carry

    lax.fori_loop(0, grid_rows // NA_ROWS, rows_step, 0)


def _na_call(p3, rel_bias, gq, gk, n_ctx, l):
    b, t, _ = p3.shape
    cos, sin = _rope_tables(l)
    nr, nc = rel_bias.shape[1:]
    rb = jnp.zeros((HEADS, 2 * WIN_R, 2 * GRID_W), F32).at[:, :nr, :nc].set(rel_bias)
    lanes = np.arange(HEAD_DIM)
    partner = np.where(lanes % (HEAD_DIM // 2) < HEAD_DIM // 4, lanes + HEAD_DIM // 4, lanes - HEAD_DIM // 4)
    perm = jnp.asarray(lanes[:, None] == partner[None, :], BF16)
    hps = NA_HEADS_PER_STEP
    steps = HEADS // hps
    col = lambda grp: (lambda h, i: (i, 0, grp * steps + h))
    tok_spec = lambda im: pl.BlockSpec((None, t, hps * HEAD_DIM), im)
    full = lambda shape: pl.BlockSpec(shape, lambda h, i: (0,) * len(shape))
    return pl.pallas_call(
        functools.partial(_na_kernel, n_ctx=n_ctx, grid_rows=l // GRID_W),
        grid=(steps, b),
        in_specs=[tok_spec(col(5)), tok_spec(col(6)), tok_spec(col(7)),
                  full((l, HEAD_DIM)), full((l, HEAD_DIM)),
                  pl.BlockSpec((hps, 2 * WIN_R, 2 * GRID_W), lambda h, i: (h, 0, 0)),
                  full((1, HEAD_DIM)), full((1, HEAD_DIM)), full((HEAD_DIM, HEAD_DIM))],
        out_specs=pl.BlockSpec((None, l, hps * HEAD_DIM), lambda h, i: (i, 0, h)),
        out_shape=jax.ShapeDtypeStruct((b, l, MIX_W), BF16),
        scratch_shapes=[pltpu.VMEM((l, HEAD_DIM), BF16), pltpu.VMEM((l, HEAD_DIM), BF16),
                        pltpu.VMEM((l, HEAD_DIM), BF16), pltpu.VMEM((n_ctx, HEAD_DIM), BF16),
                        pltpu.VMEM((n_ctx, HEAD_DIM), BF16),
                        pltpu.VMEM((hps, WIN_R, GRID_W, WIN_R * GRID_W), F32)],
        compiler_params=_params(("parallel", "arbitrary")),
        name="na",
    )(p3, p3, p3, cos, sin, rb, gq, gk, perm)


def _merge_kernel(ya_ref, yb_ref, ga_ref, gb_ref, x_ref, wa_ref, wb_ref, wo_ref, mod_ref, n2_ref,
                  xm_ref, h2_ref):
    tm = x_ref.shape[0]
    halves = [pl.ds(n * (tm // 2), tm // 2) for n in range(2)]
    za = [_dot(ya_ref[r, :], wa_ref[...]) for r in halves]
    zb = [_dot(yb_ref[r, :], wb_ref[...]) for r in halves]
    z = [(_sigmoid(ga_ref[r, :].astype(F32)) * za[n] + _sigmoid(gb_ref[r, :].astype(F32)) * zb[n]).astype(BF16)
         for n, r in enumerate(halves)]
    br = [_dot(zn, wo_ref[...]) for zn in z]
    for n, r in enumerate(halves):
        xm = x_ref[r, :] + mod_ref[0:1, :] * br[n]
        xm_ref[r, :] = xm
        y = xm * lax.rsqrt(jnp.mean(xm * xm, axis=-1, keepdims=True) + EPS) * n2_ref[...]
        h2_ref[r, :] = (y * (1.0 + mod_ref[2:3, :]) + mod_ref[1:2, :]).astype(BF16)


def _merge_call(ya, yb, pg3, x, wa, wb, wo, mod3, n2):
    b, l, d = x.shape
    tm = 2 * TOK
    const = lambda shape: pl.BlockSpec(shape, lambda i, t: (0,) * len(shape), pipeline_mode=pl.Buffered(1))
    return pl.pallas_call(
        _merge_kernel,
        grid=(b, l // tm),
        in_specs=[pl.BlockSpec((None, tm, MIX_W), lambda i, t: (i, t, 0)),
                  pl.BlockSpec((None, tm, MIX_W), lambda i, t: (i, t, 0)),
                  pl.BlockSpec((None, tm, d), lambda i, t: (i, t, 0)),
                  pl.BlockSpec((None, tm, d), lambda i, t: (i, t, 1)),
                  pl.BlockSpec((None, tm, d), lambda i, t: (i, t, 0)),
                  const((MIX_W, d)), const((MIX_W, d)), const((d, d)),
                  pl.BlockSpec((None, 3, d), lambda i, t: (i, 0, 0)),
                  pl.BlockSpec((1, d), lambda i, t: (0, 0))],
        out_specs=[pl.BlockSpec((None, tm, d), lambda i, t: (i, t, 0)),
                   pl.BlockSpec((None, tm, d), lambda i, t: (i, t, 0))],
        out_shape=[jax.ShapeDtypeStruct((b, l, d), F32), jax.ShapeDtypeStruct((b, l, d), BF16)],
        compiler_params=_params(("parallel", "arbitrary")),
        name="merge",
    )(ya, yb, pg3, pg3, x, wa, wb, wo, mod3, n2)


def _ffn_kernel(h_ref, hp_ref, hn_ref, w1_ref, w3_ref, cw_ref, cb_ref, w2_ref, xm_ref, g2_ref, o_ref, a_ref,
                hc_ref, *, tiles_per_seq, halo):
    i = pl.program_id(0)
    s = pl.program_id(1)
    n_hid, tm, th = a_ref.shape

    @pl.when(s == 0)
    def _():
        hc_ref[0:tm, :] = h_ref[...]
        hc_ref[tm:tm + halo, :] = hp_ref[...]
        hc_ref[tm + halo:tm + 2 * halo, :] = hn_ref[...]

    @pl.when(s < n_hid)
    def _():
        t1_all = _dot(hc_ref[...], w1_ref[...])
        t1 = t1_all[0:tm]
        prev_row = t1_all[tm + halo - 1:tm + halo, :]
        next_row = t1_all[tm + halo:tm + halo + 1, :]
        prev_row = jnp.where(i % tiles_per_seq == 0, 0.0, prev_row)
        next_row = jnp.where(i % tiles_per_seq == tiles_per_seq - 1, 0.0, next_row)
        ridx = lax.broadcasted_iota(jnp.int32, (tm, 1), 0)
        up = jnp.where(ridx == 0, prev_row, pltpu.roll(t1, 1, 0))
        dn = jnp.where(ridx == tm - 1, next_row, pltpu.roll(t1, tm - 1, 0))
        u = up * cw_ref[0:1, :] + t1 * cw_ref[1:2, :] + dn * cw_ref[2:3, :] + cb_ref[...]
        a_ref[s] = ((u * _sigmoid(u)) * _dot(hc_ref[0:tm, :], w3_ref[...])).astype(BF16)

    @pl.when(s >= n_hid)
    def _():
        acc = _dot(a_ref[0], w2_ref[0:th, :])
        for k in range(1, n_hid):
            acc = acc + _dot(a_ref[k], w2_ref[k * th:(k + 1) * th, :])
        o_ref[...] = xm_ref[...] + g2_ref[...] * acc


def _ffn_call(h2, xm, w1, w3, cw, cb, w2, g2, l):
    m, d = h2.shape
    hid = w1.shape[1]
    tm = 1024
    th = min(512, hid)
    tn = 256
    n_hid = hid // th
    halo = 16
    per = tm // halo
    nblk = m // halo
    hcol = lambda i, s: (0, jnp.minimum(s, n_hid - 1))
    ocol = lambda s: jnp.maximum(s - n_hid, 0)
    return pl.pallas_call(
        functools.partial(_ffn_kernel, tiles_per_seq=l // tm, halo=halo),
        grid=(m // tm, n_hid + d // tn),
        in_specs=[pl.BlockSpec((tm, d), lambda i, s: (i, 0)),
                  pl.BlockSpec((halo, d), lambda i, s: (jnp.maximum(i * per - 1, 0), 0)),
                  pl.BlockSpec((halo, d), lambda i, s: (jnp.minimum((i + 1) * per, nblk - 1), 0)),
                  pl.BlockSpec((d, th), hcol),
                  pl.BlockSpec((d, th), hcol),
                  pl.BlockSpec((3, th), hcol),
                  pl.BlockSpec((1, th), hcol),
                  pl.BlockSpec((hid, tn), lambda i, s: (0, ocol(s))),
                  pl.BlockSpec((tm, tn), lambda i, s: (i, ocol(s))),
                  pl.BlockSpec((None, 1, tn), lambda i, s: (i // (l // tm), 0, ocol(s)))],
        out_specs=pl.BlockSpec((tm, tn), lambda i, s: (i, ocol(s))),
        out_shape=jax.ShapeDtypeStruct((m, d), F32),
        scratch_shapes=[pltpu.VMEM((n_hid, tm, th), BF16), pltpu.VMEM((tm + 2 * halo, d), BF16)],
        compiler_params=_params(("parallel", "arbitrary")),
        name="ffn",
    )(h2, h2, h2, w1, w3, cw, cb, w2, xm, g2)


def kernel(x, c, ctx, c_ctx, ada_w, ada_b, norm1_g, norm2_g, w_in, hgrn_lb_logits, hgrn_norm_g,
           na_q_norm_g, na_k_norm_g, na_rel_bias, w_branch_a, w_branch_b, w_out,
           ffn_w1, ffn_w3, ffn_conv_w, ffn_conv_b, ffn_w2):
    b, l, d = x.shape
    n_ctx = ctx.shape[1]
    assert ada_w.shape[0] == 1 and b + 1 <= 8 and n_ctx % TOK == 0 and l % (2 * TOK) == 0

    c_all = jnp.zeros((8, d), F32).at[:b].set(c).at[b].set(c_ctx)
    mod = _mod_call(c_all, ada_w[0], ada_b).reshape(8, N_MOD, d)
    sh1, sc1, g1, sh2, sc2, g2 = (mod[:b, n] for n in range(N_MOD))
    ss_ctx = jnp.broadcast_to(mod[b, 0:2][None], (b, 2, d))
    ss1 = jnp.stack([jnp.stack([sh1, sc1], axis=1), ss_ctx], axis=1)

    h = _prenorm_call(x, ctx, norm1_g, ss1)
    t = l + n_ctx
    h = h.reshape(b * t, d)
    n_mix = 8 * MIX_W
    p3 = _inproj_call(h, w_in[0], 0, n_mix, F32, "inproj").reshape(b, t, n_mix)
    pg3 = _inproj_call(h, w_in[0], n_mix, 2 * d, BF16, "inproj_gates").reshape(b, t, 2 * d)

    y_a = _hgrn_call(p3, hgrn_lb_logits, hgrn_norm_g, n_ctx, l)
    y_b = _na_call(p3, na_rel_bias[0], na_q_norm_g, na_k_norm_g, n_ctx, l)

    mod3 = jnp.stack([g1, sh2, sc2], axis=1)
    x_mid, h2 = _merge_call(y_a, y_b, pg3, x, w_branch_a[0].astype(BF16), w_branch_b[0].astype(BF16),
                            w_out[0].astype(BF16), mod3, norm2_g)

    out = _ffn_call(h2.reshape(b * l, d), x_mid.reshape(b * l, d), ffn_w1[0].astype(BF16),
                    ffn_w3[0].astype(BF16), ffn_conv_w[0], ffn_conv_b, ffn_w2[0].astype(BF16),
                    g2[:, None, :], l)
    return out.reshape(b, l, d)
```

```python
import functools

import numpy as np
import jax
import jax.numpy as jnp
from jax import lax
from jax.experimental import pallas as pl
from jax.experimental.pallas import tpu as pltpu

GRID_W = 64
HEADS = 8
HEAD_DIM = 128
MIX_W = HEADS * HEAD_DIM
CHUNK = 64
N_LEVELS = 6
MXU_LEVEL_HALVES = (4, 2)
PREP_CHUNKS = 8
SCAN_CHUNKS = 8
HGRN_HEADS_PER_STEP = 2
NA_HEADS_PER_STEP = 2
NA_ROWS = 8
WIN_R = 8
WIN_C = 16
ROPE_THETA = 10000.0
N_MOD = 6
EPS = 1e-6
LOG2E = 1.4426950408889634
TOK = 256
MASKED = -1e30
V7X_VMEM_LIMIT = 60 * 1024 * 1024

BF16 = jnp.bfloat16
F32 = jnp.float32


def _dot(a, b):
    return jnp.dot(a, b, preferred_element_type=F32)


def _dot_nt(a, b):
    return lax.dot_general(a, b, (((1,), (1,)), ((), ())), preferred_element_type=F32)


def _sigmoid(t):
    return 1.0 / (1.0 + jnp.exp(-t))


def _params(sem):
    return pltpu.CompilerParams(dimension_semantics=sem, vmem_limit_bytes=V7X_VMEM_LIMIT)


def _mod_kernel(c_ref, w_ref, b_ref, o_ref):
    cv = c_ref[...]
    s = (cv * _sigmoid(cv)).astype(BF16)
    o_ref[...] = _dot(s, w_ref[...].astype(BF16)) + b_ref[...]


def _mod_call(c_all, w, b):
    d, n = w.shape
    tn = min(1024, d)
    return pl.pallas_call(
        _mod_kernel,
        grid=(n // tn,),
        in_specs=[pl.BlockSpec((8, d), lambda j: (0, 0)),
                  pl.BlockSpec((d, tn), lambda j: (0, j)),
                  pl.BlockSpec((1, tn), lambda j: (0, j))],
        out_specs=pl.BlockSpec((8, tn), lambda j: (0, j)),
        out_shape=jax.ShapeDtypeStruct((8, n), F32),
        compiler_params=_params(("arbitrary",)),
        name="mod",
    )(c_all, w, b)


def _prenorm_kernel(x_ref, ctx_ref, g_ref, ss_ref, o_ref, *, n_lat_blk):
    t = pl.program_id(1)
    xv = jnp.where(t < n_lat_blk, x_ref[...], ctx_ref[...])
    y = xv * lax.rsqrt(jnp.mean(xv * xv, axis=-1, keepdims=True) + EPS) * g_ref[...]
    o_ref[...] = (y * (1.0 + ss_ref[1:2, :]) + ss_ref[0:1, :]).astype(BF16)


def _prenorm_call(x, ctx, g, ss):
    b, l, d = x.shape
    nlb = l // TOK
    nt = nlb + ctx.shape[1] // TOK
    return pl.pallas_call(
        functools.partial(_prenorm_kernel, n_lat_blk=nlb),
        grid=(b, nt),
        in_specs=[pl.BlockSpec((None, TOK, d), lambda i, t: (i, jnp.minimum(t, nlb - 1), 0)),
                  pl.BlockSpec((None, TOK, d), lambda i, t: (i, jnp.maximum(t - nlb, 0), 0)),
                  pl.BlockSpec((1, d), lambda i, t: (0, 0)),
                  pl.BlockSpec((None, None, 2, d), lambda i, t: (i, (t >= nlb).astype(jnp.int32), 0, 0))],
        out_specs=pl.BlockSpec((None, TOK, d), lambda i, t: (i, t, 0)),
        out_shape=jax.ShapeDtypeStruct((b, nt * TOK, d), BF16),
        compiler_params=_params(("parallel", "arbitrary")),
        name="prenorm",
    )(x, ctx, g, ss)


def _inproj_kernel(*refs, n_cast):
    h_ref, w_ref = refs[:2]
    cast_in = refs[2:2 + n_cast]
    o_ref = refs[2 + n_cast]
    cast_out = refs[3 + n_cast:3 + 2 * n_cast]
    wb_ref = refs[-1]

    @pl.when(pl.program_id(1) == 0)
    def _():
        wb_ref[...] = w_ref[...].astype(BF16)

    o_ref[...] = _dot(h_ref[...], wb_ref[...]).astype(o_ref.dtype)
    for src, dst in zip(cast_in, cast_out):
        dst[...] = src[...].astype(BF16)


def _slab_rows(rows, steps):
    return next(r for r in range(32, rows + 1, 32) if rows % r == 0 and rows // r <= steps)


def _inproj_call(h, w, col0, n, out_dtype, name, to_cast=()):
    m, d = h.shape
    tm = 1024 if m % 1024 == 0 else TOK
    tn = min(1024, d)
    j0 = col0 // tn
    ni = m // tm
    steps = (n // tn) * ni
    slab_specs = []
    for a in to_cast:
        r = _slab_rows(a.shape[0], steps)
        last = a.shape[0] // r - 1
        slab_specs.append(pl.BlockSpec((r, a.shape[1]), lambda j, i, last=last: (jnp.minimum(j * ni + i, last), 0)))
    outs = pl.pallas_call(
        functools.partial(_inproj_kernel, n_cast=len(to_cast)),
        grid=(n // tn, ni),
        in_specs=[pl.BlockSpec((tm, d), lambda j, i: (i, 0)),
                  pl.BlockSpec((d, tn), lambda j, i: (0, j + j0))] + slab_specs,
        out_specs=[pl.BlockSpec((tm, tn), lambda j, i: (i, j))] + slab_specs,
        out_shape=[jax.ShapeDtypeStruct((m, n), out_dtype)] + [jax.ShapeDtypeStruct(a.shape, BF16) for a in to_cast],
        scratch_shapes=[pltpu.VMEM((d, tn), BF16)],
        compiler_params=_params(("arbitrary", "arbitrary")),
        name=name,
    )(h, w, *to_cast)
    return outs[0], outs[1:]


def _hgrn_constants():
    u = np.arange(CHUNK)
    mats, masks = [], []
    for rev in (False, True):
        pos = CHUNK - 1 - u if rev else u
        incl = (pos[None, :] <= pos[:, None]).astype(np.float32)
        blocks, lvl_masks = [incl], []
        for lev in range(N_LEVELS):
            half = CHUNK >> (lev + 1)
            ref = (pos // (2 * half)) * (2 * half) + half - 1
            upto_ref = (pos[None, :] <= ref[:, None]).astype(np.float32)
            if half in MXU_LEVEL_HALVES:
                late = (pos % (2 * half)) >= half
                blocks.append(np.where(late[:, None], 1.0, -1.0).astype(np.float32) * (incl - upto_ref))
            same = (pos[:, None] // (2 * half)) == (pos[None, :] // (2 * half))
            late_q = (pos[:, None] % (2 * half)) >= half
            early_k = (pos[None, :] % (2 * half)) < half
            lvl_masks.append((same & late_q & early_k).astype(np.float32))
        lvl_masks.append(np.eye(CHUNK, dtype=np.float32))
        mats.append(np.concatenate(blocks, axis=0))
        masks.append(np.stack(lvl_masks))
    return np.stack(mats), np.stack(masks)


def _hgrn_kernel(lbl_ref, q_ref, ff_ref, fb_ref, i_ref, gt_ref, cm_ref, mk_ref, ng_ref, y_ref, *scratch, **chunks):
    for hh in range(HGRN_HEADS_PER_STEP):
        lanes = pl.ds(hh * HEAD_DIM, HEAD_DIM)
        head = [r.at[:, lanes] for r in (q_ref, ff_ref, fb_ref, i_ref, gt_ref)]
        _hgrn_head(lbl_ref.at[:, :, hh], *head, cm_ref, mk_ref, ng_ref, y_ref.at[:, lanes], *scratch, **chunks)


def _hgrn_head(lbl_ref, q_ref, ff_ref, fb_ref, i_ref, gt_ref, cm_ref, mk_ref, ng_ref, y_ref,
               of_ref, ob_ref, sf_ref, sb_ref, kd_ref, et_ref, qd_ref, sc_ref, *, n_ctx_chunks, n_lat_chunks):
    f_refs, o_refs, st_refs = (ff_ref, fb_ref), (of_ref, ob_ref), (sf_ref, sb_ref)
    lowers = []
    for dirn in range(2):
        la, lb_ = lbl_ref[dirn, 0], lbl_ref[dirn, 1]
        mx = jnp.maximum(la, lb_)
        ea, eb = jnp.exp(la - mx), jnp.exp(lb_ - mx)
        lowers.append(ea / (ea + eb))
        st_refs[dirn][...] = jnp.zeros(st_refs[dirn].shape, F32)
    row_odd = (lax.broadcasted_iota(jnp.int32, (CHUNK, 1), 0) % 2) == 1

    def chunk_rows(pos):
        return pl.ds(pl.multiple_of(pos * CHUNK, CHUNK), CHUNK)

    def level_decay(dirn, lev, f, sums):
        half = CHUNK >> (lev + 1)
        cum = sums[0:CHUNK]
        if half == 1:
            return jnp.where(row_odd, 1.0, f) if dirn else jnp.where(row_odd, f, 1.0)
        if half in MXU_LEVEL_HALVES:
            blk = 1 + MXU_LEVEL_HALVES.index(half)
            return jnp.exp2(sums[blk * CHUNK:(blk + 1) * CHUNK])
        parts = []
        for p in range(0, CHUNK, 2 * half):
            ref = cum[p + half - 1 + dirn:p + half + dirn]
            lo, hi = cum[p:p + half], cum[p + half:p + 2 * half]
            parts += [lo - ref, ref - hi] if dirn else [ref - lo, hi - ref]
        return jnp.exp2(jnp.concatenate(parts, axis=0))

    def prepare(pos0, n, with_out):
        items = [(dirn, pos0 + c) for c in range(n) for dirn in range(2)]
        fs, kks, splits = [], [], []
        for dirn, pos in items:
            lower = lowers[dirn]
            f = lower + (1.0 - lower) * _sigmoid(f_refs[dirn][chunk_rows(pos), :])
            g = jnp.log(f) * LOG2E
            g1 = g.astype(BF16)
            r1 = g - g1.astype(F32)
            g2 = r1.astype(BF16)
            g3 = (r1 - g2.astype(F32)).astype(BF16)
            fs.append(f)
            kks.append(1.0 - f)
            splits.append(jnp.concatenate([g1, g2, g3], axis=0))
        sums = [None] * len(items)
        for dirn in range(2):
            idx = [j for j, it in enumerate(items) if it[0] == dirn]
            wide = _dot(cm_ref[dirn], jnp.concatenate([splits[j] for j in idx], axis=1))
            for c, j in enumerate(idx):
                sums[j] = wide[:, c * HEAD_DIM:(c + 1) * HEAD_DIM]
        for j, (dirn, pos) in enumerate(items):
            cum = sums[j][0:CHUNK]
            last = 0 if dirn else CHUNK - 1
            tot = cum[last:last + 1]
            kd_ref[dirn, chunk_rows(pos), :] = (kks[j] * jnp.exp2(tot - cum)).astype(BF16)
            et_ref[dirn, pos] = jnp.broadcast_to(jnp.exp2(tot), et_ref.shape[2:])
        if not with_out:
            return
        qs = [q_ref[chunk_rows(pos), :] for _, pos in items]
        for j, (dirn, pos) in enumerate(items):
            qd_ref[dirn, chunk_rows(pos), :] = (qs[j] * jnp.exp2(sums[j][0:CHUNK])).astype(BF16)
        qbs = [qv.astype(BF16) for qv in qs]
        kbs = [kv.astype(BF16) for kv in kks]
        scs = [mk_ref[dirn, N_LEVELS] * _dot_nt(qbs[j], kbs[j]) for j, (dirn, _) in enumerate(items)]
        for lev in range(N_LEVELS):
            for j, (dirn, _) in enumerate(items):
                dl = level_decay(dirn, lev, fs[j], sums[j]).astype(BF16)
                scs[j] = scs[j] + mk_ref[dirn, lev] * _dot_nt(qbs[j] * dl, kbs[j] * dl)
        for j, (dirn, pos) in enumerate(items):
            sc_ref[dirn, chunk_rows(pos), :] = scs[j].astype(BF16)

    def scan(pos_f, pos_b, n, with_out):
        items = [(dirn, (pos_b - c) if dirn else (pos_f + c)) for c in range(n) for dirn in range(2)]
        vs = [i_ref[chunk_rows(pos), :] for _, pos in items]
        ups = [_dot(vs[j].T.astype(BF16), kd_ref[dirn, chunk_rows(pos), :]) for j, (dirn, pos) in enumerate(items)]
        sts = [st_refs[0][...], st_refs[1][...]]
        before = []
        for j, (dirn, pos) in enumerate(items):
            before.append(sts[dirn])
            sts[dirn] = sts[dirn] * et_ref[dirn, pos][0:1] + ups[j]
        for dirn in range(2):
            st_refs[dirn][...] = sts[dirn]
        if not with_out:
            return
        for j, (dirn, pos) in enumerate(items):
            orow = chunk_rows(pos)
            o_refs[dirn][orow, :] = (_dot_nt(qd_ref[dirn, orow, :], before[j].astype(BF16))
                                     + _dot(sc_ref[dirn, orow, :], vs[j].astype(BF16)))

    def loop(n, body):
        lax.fori_loop(0, n, lambda ci, c: (body(ci), c)[1], 0)

    ctx0 = n_lat_chunks
    prepare(ctx0, n_ctx_chunks, False)
    loop(n_lat_chunks // PREP_CHUNKS, lambda ci: prepare(ci * PREP_CHUNKS, PREP_CHUNKS, True))
    scan(ctx0, ctx0 + n_ctx_chunks - 1, n_ctx_chunks, False)
    loop(n_lat_chunks // SCAN_CHUNKS,
         lambda ci: scan(ci * SCAN_CHUNKS, n_lat_chunks - 1 - ci * SCAN_CHUNKS, SCAN_CHUNKS, True))

    def readout(i, carry):
        rows = pl.ds(pl.multiple_of(i * TOK, TOK), TOK)
        ot = of_ref[rows, :] + ob_ref[rows, :]
        on = ot * lax.rsqrt(jnp.mean(ot * ot, axis=-1, keepdims=True) + EPS) * ng_ref[...]
        gate = gt_ref[rows, :]
        y_ref[rows, :] = (on * (gate * _sigmoid(gate))).astype(BF16)
        return carry

    lax.fori_loop(0, n_lat_chunks * CHUNK // TOK, readout, 0)


def _hgrn_call(p3, lb_logits, norm_g, n_ctx, l):
    b, t, _ = p3.shape
    cm, mk = _hgrn_constants()
    cm = jnp.asarray(np.concatenate([cm] * 3, axis=-1), BF16)
    mk = jnp.asarray(mk, F32)
    lbl = lb_logits.reshape(2, 2, HEADS, 1, HEAD_DIM)
    hps = HGRN_HEADS_PER_STEP
    steps = HEADS // hps
    tok_spec = lambda grp: pl.BlockSpec((None, t, hps * HEAD_DIM), lambda i, h: (i, 0, grp * steps + h))
    full = lambda shape: pl.BlockSpec(shape, lambda i, h: (0,) * len(shape))
    return pl.pallas_call(
        functools.partial(_hgrn_kernel, n_ctx_chunks=n_ctx // CHUNK, n_lat_chunks=l // CHUNK),
        grid=(b, steps),
        in_specs=[pl.BlockSpec((2, 2, hps, 1, HEAD_DIM), lambda i, h: (0, 0, h, 0, 0)),
                  tok_spec(0), tok_spec(1), tok_spec(2), tok_spec(3), tok_spec(4),
                  full((2, 3 * CHUNK, 3 * CHUNK)),
                  full((2, N_LEVELS + 1, CHUNK, CHUNK)),
                  full((1, HEAD_DIM))],
        out_specs=pl.BlockSpec((None, l, hps * HEAD_DIM), lambda i, h: (i, 0, h)),
        out_shape=jax.ShapeDtypeStruct((b, l, MIX_W), BF16),
        scratch_shapes=[pltpu.VMEM((l, HEAD_DIM), F32), pltpu.VMEM((l, HEAD_DIM), F32),
                        pltpu.VMEM((HEAD_DIM, HEAD_DIM), F32), pltpu.VMEM((HEAD_DIM, HEAD_DIM), F32),
                        pltpu.VMEM((2, t, HEAD_DIM), BF16), pltpu.VMEM((2, t // CHUNK, 8, HEAD_DIM), F32),
                        pltpu.VMEM((2, l, HEAD_DIM), BF16), pltpu.VMEM((2, l, CHUNK), BF16)],
        compiler_params=_params(("parallel", "parallel")),
        name="hgrn",
    )(lbl, p3, p3, p3, p3, p3, cm, mk, norm_g)


def _rope_tables(l):
    pos = np.arange(l)
    nf = HEAD_DIM // 4
    inv = ROPE_THETA ** (-np.arange(nf, dtype=np.float64) / nf)
    ang_r = (pos // GRID_W)[:, None] * inv[None, :]
    ang_c = (pos % GRID_W)[:, None] * inv[None, :]
    cos = np.concatenate([np.cos(ang_r)] * 2 + [np.cos(ang_c)] * 2, axis=-1)
    sin = np.concatenate([-np.sin(ang_r), np.sin(ang_r), -np.sin(ang_c), np.sin(ang_c)], axis=-1)
    return jnp.asarray(cos, F32), jnp.asarray(sin, F32)


def _na_build_bias(rb_ref, tb_ref):
    lanes = 2 * GRID_W
    cq = lax.broadcasted_iota(jnp.int32, (GRID_W, lanes), 0)
    lane = lax.broadcasted_iota(jnp.int32, (GRID_W, lanes), 1)
    ck = lane % GRID_W
    cs = jnp.clip(cq - WIN_C // 2, 0, GRID_W - WIN_C)
    col_in = (ck >= cs) & (ck < cs + WIN_C)

    def toeplitz(d, lane0):
        row = jnp.broadcast_to(rb_ref[d:d + 1, :] * LOG2E, (GRID_W, lanes))
        return pltpu.roll(row, (lane0 - (WIN_C - 1)) % lanes, 1, stride=1, stride_axis=0)

    pair = [jnp.where(col_in, jnp.where(lane < GRID_W, toeplitz(d, 0), toeplitz(d + 1, GRID_W)), MASKED)
            for d in range(2 * WIN_R - 2)]
    for e in range(WIN_R):
        for p in range(WIN_R // 2):
            tb_ref[e, :, p * lanes:(p + 1) * lanes] = pair[2 * p - e + WIN_R - 1]


def _na_kernel(q_ref, k_ref, v_ref, cos_ref, sin_ref, rb_ref, gq_ref, gk_ref, perm_ref, o_ref,
               qs, ks, vs, kcs, vcs, tb_ref, **static):
    for hh in range(NA_HEADS_PER_STEP):
        lanes = pl.ds(hh * HEAD_DIM, HEAD_DIM)
        head = [r.at[:, lanes] for r in (q_ref, k_ref, v_ref)]
        _na_head(*head, cos_ref, sin_ref, rb_ref.at[hh], gq_ref, gk_ref, perm_ref, o_ref.at[:, lanes],
                 qs, ks, vs, kcs, vcs, tb_ref.at[hh], **static)


def _na_head(q_ref, k_ref, v_ref, cos_ref, sin_ref, rb_ref, gq_ref, gk_ref, perm_ref, o_ref,
             qs, ks, vs, kcs, vcs, tb_ref, *, n_ctx, grid_rows):
    @pl.when(pl.program_id(1) == 0)
    def _():
        _na_build_bias(rb_ref, tb_ref)

    ones = jnp.ones((2 * HEAD_DIM, HEAD_DIM), BF16)

    def normed(tv, g):
        sq = tv * tv
        hi = sq.astype(BF16)
        lo = (sq - hi.astype(F32)).astype(BF16)
        ssq = _dot(jnp.concatenate([hi, lo], axis=1), ones)
        return tv * lax.rsqrt(ssq * (1.0 / HEAD_DIM) + EPS) * g


    gq = gq_ref[...]
    gk = gk_ref[...]
    n_lat = grid_rows * GRID_W
    kcs[...] = normed(k_ref[n_lat:n_lat + n_ctx, :], gk).astype(BF16)
    vcs[...] = v_ref[n_lat:n_lat + n_ctx, :].astype(BF16)

    def prep(i, carry):
        rows = [pl.ds(pl.multiple_of((2 * i + n) * TOK, TOK), TOK) for n in range(2)]
        srcs = rows
        jobs = [(src_ref, g, dst, n) for n in range(2) for src_ref, g, dst in ((q_ref, gq, qs), (k_ref, gk, ks))]
        nrm = [normed(src_ref[srcs[n], :], g) for src_ref, g, _, n in jobs]
        par = [_dot(tv.astype(BF16), perm_ref[...]) for tv in nrm]
        for (_, _, dst, n), tv, pv in zip(jobs, nrm, par):
            dst[rows[n], :] = (tv * cos_ref[rows[n], :] + pv * sin_ref[rows[n], :]).astype(BF16)
        for n in range(2):
            vs[rows[n], :] = v_ref[srcs[n], :].astype(BF16)
        return carry

    lax.fori_loop(0, grid_rows * GRID_W // (2 * TOK), prep, 0)
    scale = HEAD_DIM ** -0.5 * LOG2E
    band = WIN_R * GRID_W

    def rows_step(i, carry):
        rr = [i * NA_ROWS + n for n in range(NA_ROWS)]
        rs = [jnp.clip(r - WIN_R // 2, 0, grid_rows - WIN_R) for r in rr]
        qrow = [pl.ds(pl.multiple_of(r * GRID_W, GRID_W), GRID_W) for r in rr]
        krow = [pl.ds(pl.multiple_of(s * GRID_W, GRID_W), band) for s in rs]
        qr = [qs[qw, :] for qw in qrow]
        kc = kcs[...]
        sb = [_dot_nt(qr[n], ks[krow[n], :]) * scale + tb_ref[rr[n] - rs[n]] for n in range(NA_ROWS)]
        sc = [_dot_nt(qr[n], kc) * scale for n in range(NA_ROWS)]
        m = [jnp.maximum(jnp.max(sb[n], axis=-1, keepdims=True), jnp.max(sc[n], axis=-1, keepdims=True))
             for n in range(NA_ROWS)]
        pb = [jnp.exp2(sb[n] - m[n]) for n in range(NA_ROWS)]
        pc = [jnp.exp2(sc[n] - m[n]) for n in range(NA_ROWS)]
        den = [jnp.sum(pb[n], axis=-1, keepdims=True) + jnp.sum(pc[n], axis=-1, keepdims=True)
               for n in range(NA_ROWS)]
        vc = vcs[...]
        o = [_dot(pb[n].astype(BF16), vs[krow[n], :]) + _dot(pc[n].astype(BF16), vc) for n in range(NA_ROWS)]
        for n in range(NA_ROWS):
            o_ref[qrow[n], :] = (o[n] / den[n]).astype(BF16)
        return carry

    lax.fori_loop(0, grid_rows // NA_ROWS, rows_step, 0)


def _na_call(p3, rel_bias, gq, gk, n_ctx, l):
    b, t, _ = p3.shape
    cos, sin = _rope_tables(l)
    nr, nc = rel_bias.shape[1:]
    rb = jnp.zeros((HEADS, 2 * WIN_R, 2 * GRID_W), F32).at[:, :nr, :nc].set(rel_bias)
    lanes = np.arange(HEAD_DIM)
    partner = np.where(lanes % (HEAD_DIM // 2) < HEAD_DIM // 4, lanes + HEAD_DIM // 4, lanes - HEAD_DIM // 4)
    perm = jnp.asarray(lanes[:, None] == partner[None, :], BF16)
    hps = NA_HEADS_PER_STEP
    steps = HEADS // hps
    col = lambda grp: (lambda h, i: (i, 0, grp * steps + h))
    tok_spec = lambda im: pl.BlockSpec((None, t, hps * HEAD_DIM), im)
    full = lambda shape: pl.BlockSpec(shape, lambda h, i: (0,) * len(shape))
    return pl.pallas_call(
        functools.partial(_na_kernel, n_ctx=n_ctx, grid_rows=l // GRID_W),
        grid=(steps, b),
        in_specs=[tok_spec(col(5)), tok_spec(col(6)), tok_spec(col(7)),
                  full((l, HEAD_DIM)), full((l, HEAD_DIM)),
                  pl.BlockSpec((hps, 2 * WIN_R, 2 * GRID_W), lambda h, i: (h, 0, 0)),
                  full((1, HEAD_DIM)), full((1, HEAD_DIM)), full((HEAD_DIM, HEAD_DIM))],
        out_specs=pl.BlockSpec((None, l, hps * HEAD_DIM), lambda h, i: (i, 0, h)),
        out_shape=jax.ShapeDtypeStruct((b, l, MIX_W), BF16),
        scratch_shapes=[pltpu.VMEM((l, HEAD_DIM), BF16), pltpu.VMEM((l, HEAD_DIM), BF16),
                        pltpu.VMEM((l, HEAD_DIM), BF16), pltpu.VMEM((n_ctx, HEAD_DIM), BF16),
                        pltpu.VMEM((n_ctx, HEAD_DIM), BF16),
                        pltpu.VMEM((hps, WIN_R, GRID_W, WIN_R * GRID_W), F32)],
        compiler_params=_params(("parallel", "arbitrary")),
        name="na",
    )(p3, p3, p3, cos, sin, rb, gq, gk, perm)


def _merge_kernel(ya_ref, yb_ref, ga_ref, gb_ref, x_ref, wa_ref, wb_ref, wo_ref, mod_ref, n2_ref,
                  xm_ref, h2_ref):
    tm = x_ref.shape[0]
    halves = [pl.ds(n * (tm // 2), tm // 2) for n in range(2)]
    za = [_dot(ya_ref[r, :], wa_ref[...]) for r in halves]
    zb = [_dot(yb_ref[r, :], wb_ref[...]) for r in halves]
    z = [(_sigmoid(ga_ref[r, :].astype(F32)) * za[n] + _sigmoid(gb_ref[r, :].astype(F32)) * zb[n]).astype(BF16)
         for n, r in enumerate(halves)]
    br = [_dot(zn, wo_ref[...]) for zn in z]
    for n, r in enumerate(halves):
        xm = x_ref[r, :] + mod_ref[0:1, :] * br[n]
        xm_ref[r, :] = xm
        y = xm * lax.rsqrt(jnp.mean(xm * xm, axis=-1, keepdims=True) + EPS) * n2_ref[...]
        h2_ref[r, :] = (y * (1.0 + mod_ref[2:3, :]) + mod_ref[1:2, :]).astype(BF16)


def _merge_call(ya, yb, pg3, x, wa, wb, wo, mod3, n2):
    b, l, d = x.shape
    tm = 2 * TOK
    const = lambda shape: pl.BlockSpec(shape, lambda i, t: (0,) * len(shape), pipeline_mode=pl.Buffered(1))
    return pl.pallas_call(
        _merge_kernel,
        grid=(b, l // tm),
        in_specs=[pl.BlockSpec((None, tm, MIX_W), lambda i, t: (i, t, 0)),
                  pl.BlockSpec((None, tm, MIX_W), lambda i, t: (i, t, 0)),
                  pl.BlockSpec((None, tm, d), lambda i, t: (i, t, 0)),
                  pl.BlockSpec((None, tm, d), lambda i, t: (i, t, 1)),
                  pl.BlockSpec((None, tm, d), lambda i, t: (i, t, 0)),
                  const((MIX_W, d)), const((MIX_W, d)), const((d, d)),
                  pl.BlockSpec((None, 3, d), lambda i, t: (i, 0, 0)),
                  pl.BlockSpec((1, d), lambda i, t: (0, 0))],
        out_specs=[pl.BlockSpec((None, tm, d), lambda i, t: (i, t, 0)),
                   pl.BlockSpec((None, tm, d), lambda i, t: (i, t, 0))],
        out_shape=[jax.ShapeDtypeStruct((b, l, d), F32), jax.ShapeDtypeStruct((b, l, d), BF16)],
        compiler_params=_params(("parallel", "arbitrary")),
        name="merge",
    )(ya, yb, pg3, pg3, x, wa, wb, wo, mod3, n2)


def _ffn_kernel(h_ref, hp_ref, hn_ref, w1_ref, w3_ref, cw_ref, cb_ref, w2_ref, xm_ref, g2_ref, o_ref, a_ref,
                hc_ref, *, tiles_per_seq, halo):
    i = pl.program_id(0)
    s = pl.program_id(1)
    n_hid, tm, th = a_ref.shape

    @pl.when(s == 0)
    def _():
        hc_ref[0:tm, :] = h_ref[...]
        hc_ref[tm:tm + halo, :] = hp_ref[...]
        hc_ref[tm + halo:tm + 2 * halo, :] = hn_ref[...]

    @pl.when(s < n_hid)
    def _():
        t1_all = _dot(hc_ref[...], w1_ref[...])
        t1 = t1_all[0:tm]
        prev_row = t1_all[tm + halo - 1:tm + halo, :]
        next_row = t1_all[tm + halo:tm + halo + 1, :]
        prev_row = jnp.where(i % tiles_per_seq == 0, 0.0, prev_row)
        next_row = jnp.where(i % tiles_per_seq == tiles_per_seq - 1, 0.0, next_row)
        ridx = lax.broadcasted_iota(jnp.int32, (tm, 1), 0)
        up = jnp.where(ridx == 0, prev_row, pltpu.roll(t1, 1, 0))
        dn = jnp.where(ridx == tm - 1, next_row, pltpu.roll(t1, tm - 1, 0))
        u = up * cw_ref[0:1, :] + t1 * cw_ref[1:2, :] + dn * cw_ref[2:3, :] + cb_ref[...]
        a_ref[s] = ((u * _sigmoid(u)) * _dot(hc_ref[0:tm, :], w3_ref[...])).astype(BF16)

    @pl.when(s >= n_hid)
    def _():
        acc = _dot(a_ref[0], w2_ref[0:th, :])
        for k in range(1, n_hid):
            acc = acc + _dot(a_ref[k], w2_ref[k * th:(k + 1) * th, :])
        o_ref[...] = xm_ref[...] + g2_ref[...] * acc


def _ffn_call(h2, xm, w1, w3, cw, cb, w2, g2, l):
    m, d = h2.shape
    hid = w1.shape[1]
    tm = 1024
    th = min(512, hid)
    tn = 256
    n_hid = hid // th
    halo = 16
    per = tm // halo
    nblk = m // halo
    hcol = lambda i, s: (0, jnp.minimum(s, n_hid - 1))
    ocol = lambda s: jnp.maximum(s - n_hid, 0)
    return pl.pallas_call(
        functools.partial(_ffn_kernel, tiles_per_seq=l // tm, halo=halo),
        grid=(m // tm, n_hid + d // tn),
        in_specs=[pl.BlockSpec((tm, d), lambda i, s: (i, 0)),
                  pl.BlockSpec((halo, d), lambda i, s: (jnp.maximum(i * per - 1, 0), 0)),
                  pl.BlockSpec((halo, d), lambda i, s: (jnp.minimum((i + 1) * per, nblk - 1), 0)),
                  pl.BlockSpec((d, th), hcol),
                  pl.BlockSpec((d, th), hcol),
                  pl.BlockSpec((3, th), hcol),
                  pl.BlockSpec((1, th), hcol),
                  pl.BlockSpec((hid, tn), lambda i, s: (0, ocol(s))),
                  pl.BlockSpec((tm, tn), lambda i, s: (i, ocol(s))),
                  pl.BlockSpec((None, 1, tn), lambda i, s: (i // (l // tm), 0, ocol(s)))],
        out_specs=pl.BlockSpec((tm, tn), lambda i, s: (i, ocol(s))),
        out_shape=jax.ShapeDtypeStruct((m, d), F32),
        scratch_shapes=[pltpu.VMEM((n_hid, tm, th), BF16), pltpu.VMEM((tm + 2 * halo, d), BF16)],
        compiler_params=_params(("parallel", "arbitrary")),
        name="ffn",
    )(h2, h2, h2, w1, w3, cw, cb, w2, xm, g2)


def kernel(x, c, ctx, c_ctx, ada_w, ada_b, norm1_g, norm2_g, w_in, hgrn_lb_logits, hgrn_norm_g,
           na_q_norm_g, na_k_norm_g, na_rel_bias, w_branch_a, w_branch_b, w_out,
           ffn_w1, ffn_w3, ffn_conv_w, ffn_conv_b, ffn_w2):
    b, l, d = x.shape
    n_ctx = ctx.shape[1]
    assert ada_w.shape[0] == 1 and b + 1 <= 8 and n_ctx % TOK == 0 and l % (2 * TOK) == 0

    c_all = jnp.zeros((8, d), F32).at[:b].set(c).at[b].set(c_ctx)
    mod = _mod_call(c_all, ada_w[0], ada_b).reshape(8, N_MOD, d)
    sh1, sc1, g1, sh2, sc2, g2 = (mod[:b, n] for n in range(N_MOD))
    ss_ctx = jnp.broadcast_to(mod[b, 0:2][None], (b, 2, d))
    ss1 = jnp.stack([jnp.stack([sh1, sc1], axis=1), ss_ctx], axis=1)

    h = _prenorm_call(x, ctx, norm1_g, ss1)
    t = l + n_ctx
    h = h.reshape(b * t, d)
    n_mix = 8 * MIX_W
    p, (w1, w3, w2) = _inproj_call(h, w_in[0], 0, n_mix, F32, "inproj", (ffn_w1[0], ffn_w3[0], ffn_w2[0]))
    pg, (wa, wb, wo) = _inproj_call(h, w_in[0], n_mix, 2 * d, BF16, "inproj_gates",
                                    (w_branch_a[0], w_branch_b[0], w_out[0]))
    p3 = p.reshape(b, t, n_mix)
    pg3 = pg.reshape(b, t, 2 * d)

    y_a = _hgrn_call(p3, hgrn_lb_logits, hgrn_norm_g, n_ctx, l)
    y_b = _na_call(p3, na_rel_bias[0], na_q_norm_g, na_k_norm_g, n_ctx, l)

    mod3 = jnp.stack([g1, sh2, sc2], axis=1)
    x_mid, h2 = _merge_call(y_a, y_b, pg3, x, wa, wb, wo, mod3, norm2_g)

    out = _ffn_call(h2.reshape(b * l, d), x_mid.reshape(b * l, d), w1, w3, ffn_conv_w[0], ffn_conv_b, w2,
                    g2[:, None, :], l)
    return out.reshape(b, l, d)
```

```python
import functools

import numpy as np
import jax
import jax.numpy as jnp
from jax import lax
from jax.experimental import pallas as pl
from jax.experimental.pallas import tpu as pltpu

GRID_W = 64
HEADS = 8
HEAD_DIM = 128
MIX_W = HEADS * HEAD_DIM
CHUNK = 64
N_LEVELS = 6
MXU_LEVEL_HALVES = (4, 2)
PREP_CHUNKS = 8
SCAN_CHUNKS = 8
HGRN_HEADS_PER_STEP = 2
NA_HEADS_PER_STEP = 2
NA_ROWS = 8
WIN_R = 8
WIN_C = 16
ROPE_THETA = 10000.0
N_MOD = 6
EPS = 1e-6
LOG2E = 1.4426950408889634
TOK = 256
MASKED = -1e30
V7X_VMEM_LIMIT = 60 * 1024 * 1024

BF16 = jnp.bfloat16
F32 = jnp.float32


def _dot(a, b):
    return jnp.dot(a, b, preferred_element_type=F32)


def _dot_nt(a, b):
    return lax.dot_general(a, b, (((1,), (1,)), ((), ())), preferred_element_type=F32)


def _sigmoid(t):
    return 1.0 / (1.0 + jnp.exp(-t))


def _params(sem):
    return pltpu.CompilerParams(dimension_semantics=sem, vmem_limit_bytes=V7X_VMEM_LIMIT)


def _mod_kernel(c_ref, w_ref, b_ref, o_ref):
    cv = c_ref[...]
    s = (cv * _sigmoid(cv)).astype(BF16)
    o_ref[...] = _dot(s, w_ref[...].astype(BF16)) + b_ref[...]


def _mod_call(c_all, w, b):
    d, n = w.shape
    tn = min(1024, d)
    return pl.pallas_call(
        _mod_kernel,
        grid=(n // tn,),
        in_specs=[pl.BlockSpec((8, d), lambda j: (0, 0)),
                  pl.BlockSpec((d, tn), lambda j: (0, j)),
                  pl.BlockSpec((1, tn), lambda j: (0, j))],
        out_specs=pl.BlockSpec((8, tn), lambda j: (0, j)),
        out_shape=jax.ShapeDtypeStruct((8, n), F32),
        compiler_params=_params(("arbitrary",)),
        name="mod",
    )(c_all, w, b)


def _prenorm_kernel(x_ref, ctx_ref, g_ref, ss_ref, o_ref, *, n_lat_blk):
    t = pl.program_id(1)
    xv = jnp.where(t < n_lat_blk, x_ref[...], ctx_ref[...])
    y = xv * lax.rsqrt(jnp.mean(xv * xv, axis=-1, keepdims=True) + EPS) * g_ref[...]
    o_ref[...] = (y * (1.0 + ss_ref[1:2, :]) + ss_ref[0:1, :]).astype(BF16)


def _prenorm_call(x, ctx, g, ss):
    b, l, d = x.shape
    nlb = l // TOK
    nt = nlb + ctx.shape[1] // TOK
    return pl.pallas_call(
        functools.partial(_prenorm_kernel, n_lat_blk=nlb),
        grid=(b, nt),
        in_specs=[pl.BlockSpec((None, TOK, d), lambda i, t: (i, jnp.minimum(t, nlb - 1), 0)),
                  pl.BlockSpec((None, TOK, d), lambda i, t: (i, jnp.maximum(t - nlb, 0), 0)),
                  pl.BlockSpec((1, d), lambda i, t: (0, 0)),
                  pl.BlockSpec((None, None, 2, d), lambda i, t: (i, (t >= nlb).astype(jnp.int32), 0, 0))],
        out_specs=pl.BlockSpec((None, TOK, d), lambda i, t: (i, t, 0)),
        out_shape=jax.ShapeDtypeStruct((b, nt * TOK, d), BF16),
        compiler_params=_params(("parallel", "arbitrary")),
        name="prenorm",
    )(x, ctx, g, ss)


def _inproj_kernel(*refs, n_cast):
    h_ref, w_ref = refs[:2]
    cast_in = refs[2:2 + n_cast]
    o_ref = refs[2 + n_cast]
    cast_out = refs[3 + n_cast:3 + 2 * n_cast]
    wb_ref = refs[-1]

    @pl.when(pl.program_id(1) == 0)
    def _():
        wb_ref[...] = w_ref[...].astype(BF16)

    o_ref[...] = _dot(h_ref[...], wb_ref[...]).astype(o_ref.dtype)
    for src, dst in zip(cast_in, cast_out):
        ct = dst.shape[-1]
        for c in range(dst.shape[0]):
            dst[c] = src[:, c * ct:(c + 1) * ct].astype(BF16)


def _slab_rows(rows, steps):
    return next(r for r in range(32, rows + 1, 32) if rows % r == 0 and rows // r <= steps)


def _inproj_call(h, w, col0, n, out_dtype, name, to_cast=()):
    m, d = h.shape
    tm = 1024 if m % 1024 == 0 else TOK
    tn = min(1024, d)
    j0 = col0 // tn
    ni = m // tm
    steps = (n // tn) * ni
    in_slabs, out_slabs, out_shapes = [], [], []
    for a, ct in to_cast:
        rows, cols = a.shape
        r = _slab_rows(rows, steps)
        slab = lambda j, i, last=rows // r - 1: jnp.minimum(j * ni + i, last)
        in_slabs.append(pl.BlockSpec((r, cols), lambda j, i, slab=slab: (slab(j, i), 0)))
        out_slabs.append(pl.BlockSpec((cols // ct, r, ct), lambda j, i, slab=slab: (0, slab(j, i), 0)))
        out_shapes.append(jax.ShapeDtypeStruct((cols // ct, rows, ct), BF16))
    outs = pl.pallas_call(
        functools.partial(_inproj_kernel, n_cast=len(to_cast)),
        grid=(n // tn, ni),
        in_specs=[pl.BlockSpec((tm, d), lambda j, i: (i, 0)),
                  pl.BlockSpec((d, tn), lambda j, i: (0, j + j0))] + in_slabs,
        out_specs=[pl.BlockSpec((tm, tn), lambda j, i: (i, j))] + out_slabs,
        out_shape=[jax.ShapeDtypeStruct((m, n), out_dtype)] + out_shapes,
        scratch_shapes=[pltpu.VMEM((d, tn), BF16)],
        compiler_params=_params(("arbitrary", "arbitrary")),
        name=name,
    )(h, w, *[a for a, _ in to_cast])
    return outs[0], outs[1:]


def _hgrn_constants():
    u = np.arange(CHUNK)
    mats, masks = [], []
    for rev in (False, True):
        pos = CHUNK - 1 - u if rev else u
        incl = (pos[None, :] <= pos[:, None]).astype(np.float32)
        blocks, lvl_masks = [incl], []
        for lev in range(N_LEVELS):
            half = CHUNK >> (lev + 1)
            ref = (pos // (2 * half)) * (2 * half) + half - 1
            upto_ref = (pos[None, :] <= ref[:, None]).astype(np.float32)
            if half in MXU_LEVEL_HALVES:
                late = (pos % (2 * half)) >= half
                blocks.append(np.where(late[:, None], 1.0, -1.0).astype(np.float32) * (incl - upto_ref))
            same = (pos[:, None] // (2 * half)) == (pos[None, :] // (2 * half))
            late_q = (pos[:, None] % (2 * half)) >= half
            early_k = (pos[None, :] % (2 * half)) < half
            lvl_masks.append((same & late_q & early_k).astype(np.float32))
        lvl_masks.append(np.eye(CHUNK, dtype=np.float32))
        mats.append(np.concatenate(blocks, axis=0))
        masks.append(np.stack(lvl_masks))
    return np.stack(mats), np.stack(masks)


def _hgrn_kernel(lbl_ref, q_ref, ff_ref, fb_ref, i_ref, gt_ref, cm_ref, mk_ref, ng_ref, y_ref, *scratch, **chunks):
    for hh in range(HGRN_HEADS_PER_STEP):
        lanes = pl.ds(hh * HEAD_DIM, HEAD_DIM)
        head = [r.at[:, lanes] for r in (q_ref, ff_ref, fb_ref, i_ref, gt_ref)]
        _hgrn_head(lbl_ref.at[:, :, hh], *head, cm_ref, mk_ref, ng_ref, y_ref.at[:, lanes], *scratch, **chunks)


def _hgrn_head(lbl_ref, q_ref, ff_ref, fb_ref, i_ref, gt_ref, cm_ref, mk_ref, ng_ref, y_ref,
               of_ref, ob_ref, sf_ref, sb_ref, kd_ref, et_ref, qd_ref, sc_ref, *, n_ctx_chunks, n_lat_chunks):
    f_refs, o_refs, st_refs = (ff_ref, fb_ref), (of_ref, ob_ref), (sf_ref, sb_ref)
    lowers = []
    for dirn in range(2):
        la, lb_ = lbl_ref[dirn, 0], lbl_ref[dirn, 1]
        mx = jnp.maximum(la, lb_)
        ea, eb = jnp.exp(la - mx), jnp.exp(lb_ - mx)
        lowers.append(ea / (ea + eb))
        st_refs[dirn][...] = jnp.zeros(st_refs[dirn].shape, F32)
    row_odd = (lax.broadcasted_iota(jnp.int32, (CHUNK, 1), 0) % 2) == 1

    def chunk_rows(pos):
        return pl.ds(pl.multiple_of(pos * CHUNK, CHUNK), CHUNK)

    def level_decay(dirn, lev, f, sums):
        half = CHUNK >> (lev + 1)
        cum = sums[0:CHUNK]
        if half == 1:
            return jnp.where(row_odd, 1.0, f) if dirn else jnp.where(row_odd, f, 1.0)
        if half in MXU_LEVEL_HALVES:
            blk = 1 + MXU_LEVEL_HALVES.index(half)
            return jnp.exp2(sums[blk * CHUNK:(blk + 1) * CHUNK])
        parts = []
        for p in range(0, CHUNK, 2 * half):
            ref = cum[p + half - 1 + dirn:p + half + dirn]
            lo, hi = cum[p:p + half], cum[p + half:p + 2 * half]
            parts += [lo - ref, ref - hi] if dirn else [ref - lo, hi - ref]
        return jnp.exp2(jnp.concatenate(parts, axis=0))

    def prepare(pos0, n, with_out):
        items = [(dirn, pos0 + c) for c in range(n) for dirn in range(2)]
        fs, kks, splits = [], [], []
        for dirn, pos in items:
            lower = lowers[dirn]
            f = lower + (1.0 - lower) * _sigmoid(f_refs[dirn][chunk_rows(pos), :])
            g = jnp.log(f) * LOG2E
            g1 = g.astype(BF16)
            r1 = g - g1.astype(F32)
            g2 = r1.astype(BF16)
            g3 = (r1 - g2.astype(F32)).astype(BF16)
            fs.append(f)
            kks.append(1.0 - f)
            splits.append(jnp.concatenate([g1, g2, g3], axis=0))
        sums = [None] * len(items)
        for dirn in range(2):
            idx = [j for j, it in enumerate(items) if it[0] == dirn]
            wide = _dot(cm_ref[dirn], jnp.concatenate([splits[j] for j in idx], axis=1))
            for c, j in enumerate(idx):
                sums[j] = wide[:, c * HEAD_DIM:(c + 1) * HEAD_DIM]
        for j, (dirn, pos) in enumerate(items):
            cum = sums[j][0:CHUNK]
            last = 0 if dirn else CHUNK - 1
            tot = cum[last:last + 1]
            kd_ref[dirn, chunk_rows(pos), :] = (kks[j] * jnp.exp2(tot - cum)).astype(BF16)
            et_ref[dirn, pos] = jnp.broadcast_to(jnp.exp2(tot), et_ref.shape[2:])
        if not with_out:
            return
        qs = [q_ref[chunk_rows(pos), :] for _, pos in items]
        for j, (dirn, pos) in enumerate(items):
            qd_ref[dirn, chunk_rows(pos), :] = (qs[j] * jnp.exp2(sums[j][0:CHUNK])).astype(BF16)
        qbs = [qv.astype(BF16) for qv in qs]
        kbs = [kv.astype(BF16) for kv in kks]
        scs = [mk_ref[dirn, N_LEVELS] * _dot_nt(qbs[j], kbs[j]) for j, (dirn, _) in enumerate(items)]
        for lev in range(N_LEVELS):
            for j, (dirn, _) in enumerate(items):
                dl = level_decay(dirn, lev, fs[j], sums[j]).astype(BF16)
                scs[j] = scs[j] + mk_ref[dirn, lev] * _dot_nt(qbs[j] * dl, kbs[j] * dl)
        for j, (dirn, pos) in enumerate(items):
            sc_ref[dirn, chunk_rows(pos), :] = scs[j].astype(BF16)

    def scan(pos_f, pos_b, n, with_out):
        items = [(dirn, (pos_b - c) if dirn else (pos_f + c)) for c in range(n) for dirn in range(2)]
        vs = [i_ref[chunk_rows(pos), :] for _, pos in items]
        ups = [_dot(vs[j].T.astype(BF16), kd_ref[dirn, chunk_rows(pos), :]) for j, (dirn, pos) in enumerate(items)]
        sts = [st_refs[0][...], st_refs[1][...]]
        before = []
        for j, (dirn, pos) in enumerate(items):
            before.append(sts[dirn])
            sts[dirn] = sts[dirn] * et_ref[dirn, pos][0:1] + ups[j]
        for dirn in range(2):
            st_refs[dirn][...] = sts[dirn]
        if not with_out:
            return
        for j, (dirn, pos) in enumerate(items):
            orow = chunk_rows(pos)
            o_refs[dirn][orow, :] = (_dot_nt(qd_ref[dirn, orow, :], before[j].astype(BF16))
                                     + _dot(sc_ref[dirn, orow, :], vs[j].astype(BF16)))

    def loop(n, body):
        lax.fori_loop(0, n, lambda ci, c: (body(ci), c)[1], 0)

    ctx0 = n_lat_chunks
    prepare(ctx0, n_ctx_chunks, False)
    loop(n_lat_chunks // PREP_CHUNKS, lambda ci: prepare(ci * PREP_CHUNKS, PREP_CHUNKS, True))
    scan(ctx0, ctx0 + n_ctx_chunks - 1, n_ctx_chunks, False)
    loop(n_lat_chunks // SCAN_CHUNKS,
         lambda ci: scan(ci * SCAN_CHUNKS, n_lat_chunks - 1 - ci * SCAN_CHUNKS, SCAN_CHUNKS, True))

    def readout(i, carry):
        rows = pl.ds(pl.multiple_of(i * TOK, TOK), TOK)
        ot = of_ref[rows, :] + ob_ref[rows, :]
        on = ot * lax.rsqrt(jnp.mean(ot * ot, axis=-1, keepdims=True) + EPS) * ng_ref[...]
        gate = gt_ref[rows, :]
        y_ref[rows, :] = (on * (gate * _sigmoid(gate))).astype(BF16)
        return carry

    lax.fori_loop(0, n_lat_chunks * CHUNK // TOK, readout, 0)


def _hgrn_call(p3, lb_logits, norm_g, n_ctx, l):
    b, t, _ = p3.shape
    cm, mk = _hgrn_constants()
    cm = jnp.asarray(np.concatenate([cm] * 3, axis=-1), BF16)
    mk = jnp.asarray(mk, F32)
    lbl = lb_logits.reshape(2, 2, HEADS, 1, HEAD_DIM)
    hps = HGRN_HEADS_PER_STEP
    steps = HEADS // hps
    tok_spec = lambda grp: pl.BlockSpec((None, t, hps * HEAD_DIM), lambda i, h: (i, 0, grp * steps + h))
    full = lambda shape: pl.BlockSpec(shape, lambda i, h: (0,) * len(shape))
    return pl.pallas_call(
        functools.partial(_hgrn_kernel, n_ctx_chunks=n_ctx // CHUNK, n_lat_chunks=l // CHUNK),
        grid=(b, steps),
        in_specs=[pl.BlockSpec((2, 2, hps, 1, HEAD_DIM), lambda i, h: (0, 0, h, 0, 0)),
                  tok_spec(0), tok_spec(1), tok_spec(2), tok_spec(3), tok_spec(4),
                  full((2, 3 * CHUNK, 3 * CHUNK)),
                  full((2, N_LEVELS + 1, CHUNK, CHUNK)),
                  full((1, HEAD_DIM))],
        out_specs=pl.BlockSpec((None, l, hps * HEAD_DIM), lambda i, h: (i, 0, h)),
        out_shape=jax.ShapeDtypeStruct((b, l, MIX_W), BF16),
        scratch_shapes=[pltpu.VMEM((l, HEAD_DIM), F32), pltpu.VMEM((l, HEAD_DIM), F32),
                        pltpu.VMEM((HEAD_DIM, HEAD_DIM), F32), pltpu.VMEM((HEAD_DIM, HEAD_DIM), F32),
                        pltpu.VMEM((2, t, HEAD_DIM), BF16), pltpu.VMEM((2, t // CHUNK, 8, HEAD_DIM), F32),
                        pltpu.VMEM((2, l, HEAD_DIM), BF16), pltpu.VMEM((2, l, CHUNK), BF16)],
        compiler_params=_params(("parallel", "parallel")),
        name="hgrn",
    )(lbl, p3, p3, p3, p3, p3, cm, mk, norm_g)


def _rope_tables(l):
    pos = np.arange(l)
    nf = HEAD_DIM // 4
    inv = ROPE_THETA ** (-np.arange(nf, dtype=np.float64) / nf)
    ang_r = (pos // GRID_W)[:, None] * inv[None, :]
    ang_c = (pos % GRID_W)[:, None] * inv[None, :]
    cos = np.concatenate([np.cos(ang_r)] * 2 + [np.cos(ang_c)] * 2, axis=-1)
    sin = np.concatenate([-np.sin(ang_r), np.sin(ang_r), -np.sin(ang_c), np.sin(ang_c)], axis=-1)
    return jnp.asarray(cos, F32), jnp.asarray(sin, F32)


def _na_build_bias(rb_ref, tb_ref):
    lanes = 2 * GRID_W
    cq = lax.broadcasted_iota(jnp.int32, (GRID_W, lanes), 0)
    lane = lax.broadcasted_iota(jnp.int32, (GRID_W, lanes), 1)
    ck = lane % GRID_W
    cs = jnp.clip(cq - WIN_C // 2, 0, GRID_W - WIN_C)
    col_in = (ck >= cs) & (ck < cs + WIN_C)

    def toeplitz(d, lane0):
        row = jnp.broadcast_to(rb_ref[d:d + 1, :] * LOG2E, (GRID_W, lanes))
        return pltpu.roll(row, (lane0 - (WIN_C - 1)) % lanes, 1, stride=1, stride_axis=0)

    pair = [jnp.where(col_in, jnp.where(lane < GRID_W, toeplitz(d, 0), toeplitz(d + 1, GRID_W)), MASKED)
            for d in range(2 * WIN_R - 2)]
    for e in range(WIN_R):
        for p in range(WIN_R // 2):
            tb_ref[e, :, p * lanes:(p + 1) * lanes] = pair[2 * p - e + WIN_R - 1]


def _na_kernel(q_ref, k_ref, v_ref, cos_ref, sin_ref, rb_ref, gq_ref, gk_ref, perm_ref, o_ref,
               qs, ks, vs, kcs, vcs, tb_ref, **static):
    for hh in range(NA_HEADS_PER_STEP):
        lanes = pl.ds(hh * HEAD_DIM, HEAD_DIM)
        head = [r.at[:, lanes] for r in (q_ref, k_ref, v_ref)]
        _na_head(*head, cos_ref, sin_ref, rb_ref.at[hh], gq_ref, gk_ref, perm_ref, o_ref.at[:, lanes],
                 qs, ks, vs, kcs, vcs, tb_ref.at[hh], **static)


def _na_head(q_ref, k_ref, v_ref, cos_ref, sin_ref, rb_ref, gq_ref, gk_ref, perm_ref, o_ref,
             qs, ks, vs, kcs, vcs, tb_ref, *, n_ctx, grid_rows):
    @pl.when(pl.program_id(1) == 0)
    def _():
        _na_build_bias(rb_ref, tb_ref)

    ones = jnp.ones((2 * HEAD_DIM, HEAD_DIM), BF16)

    def normed(tv, g):
        sq = tv * tv
        hi = sq.astype(BF16)
        lo = (sq - hi.astype(F32)).astype(BF16)
        ssq = _dot(jnp.concatenate([hi, lo], axis=1), ones)
        return tv * lax.rsqrt(ssq * (1.0 / HEAD_DIM) + EPS) * g


    gq = gq_ref[...]
    gk = gk_ref[...]
    n_lat = grid_rows * GRID_W
    kcs[...] = normed(k_ref[n_lat:n_lat + n_ctx, :], gk).astype(BF16)
    vcs[...] = v_ref[n_lat:n_lat + n_ctx, :].astype(BF16)

    def prep(i, carry):
        rows = [pl.ds(pl.multiple_of((2 * i + n) * TOK, TOK), TOK) for n in range(2)]
        srcs = rows
        jobs = [(src_ref, g, dst, n) for n in range(2) for src_ref, g, dst in ((q_ref, gq, qs), (k_ref, gk, ks))]
        nrm = [normed(src_ref[srcs[n], :], g) for src_ref, g, _, n in jobs]
        par = [_dot(tv.astype(BF16), perm_ref[...]) for tv in nrm]
        for (_, _, dst, n), tv, pv in zip(jobs, nrm, par):
            dst[rows[n], :] = (tv * cos_ref[rows[n], :] + pv * sin_ref[rows[n], :]).astype(BF16)
        for n in range(2):
            vs[rows[n], :] = v_ref[srcs[n], :].astype(BF16)
        return carry

    lax.fori_loop(0, grid_rows * GRID_W // (2 * TOK), prep, 0)
    scale = HEAD_DIM ** -0.5 * LOG2E
    band = WIN_R * GRID_W

    def rows_step(i, carry):
        rr = [i * NA_ROWS + n for n in range(NA_ROWS)]
        rs = [jnp.clip(r - WIN_R // 2, 0, grid_rows - WIN_R) for r in rr]
        qrow = [pl.ds(pl.multiple_of(r * GRID_W, GRID_W), GRID_W) for r in rr]
        krow = [pl.ds(pl.multiple_of(s * GRID_W, GRID_W), band) for s in rs]
        qr = [qs[qw, :] for qw in qrow]
        kc = kcs[...]
        sb = [_dot_nt(qr[n], ks[krow[n], :]) * scale + tb_ref[rr[n] - rs[n]] for n in range(NA_ROWS)]
        sc = [_dot_nt(qr[n], kc) * scale for n in range(NA_ROWS)]
        m = [jnp.maximum(jnp.max(sb[n], axis=-1, keepdims=True), jnp.max(sc[n], axis=-1, keepdims=True))
             for n in range(NA_ROWS)]
        pb = [jnp.exp2(sb[n] - m[n]) for n in range(NA_ROWS)]
        pc = [jnp.exp2(sc[n] - m[n]) for n in range(NA_ROWS)]
        den = [jnp.sum(pb[n], axis=-1, keepdims=True) + jnp.sum(pc[n], axis=-1, keepdims=True)
               for n in range(NA_ROWS)]
        vc = vcs[...]
        o = [_dot(pb[n].astype(BF16), vs[krow[n], :]) + _dot(pc[n].astype(BF16), vc) for n in range(NA_ROWS)]
        for n in range(NA_ROWS):
            o_ref[qrow[n], :] = (o[n] / den[n]).astype(BF16)
        return carry

    lax.fori_loop(0, grid_rows // NA_ROWS, rows_step, 0)


def _na_call(p3, rel_bias, gq, gk, n_ctx, l):
    b, t, _ = p3.shape
    cos, sin = _rope_tables(l)
    nr, nc = rel_bias.shape[1:]
    rb = jnp.zeros((HEADS, 2 * WIN_R, 2 * GRID_W), F32).at[:, :nr, :nc].set(rel_bias)
    lanes = np.arange(HEAD_DIM)
    partner = np.where(lanes % (HEAD_DIM // 2) < HEAD_DIM // 4, lanes + HEAD_DIM // 4, lanes - HEAD_DIM // 4)
    perm = jnp.asarray(lanes[:, None] == partner[None, :], BF16)
    hps = NA_HEADS_PER_STEP
    steps = HEADS // hps
    col = lambda grp: (lambda h, i: (i, 0, grp * steps + h))
    tok_spec = lambda im: pl.BlockSpec((None, t, hps * HEAD_DIM), im)
    full = lambda shape: pl.BlockSpec(shape, lambda h, i: (0,) * len(shape))
    return pl.pallas_call(
        functools.partial(_na_kernel, n_ctx=n_ctx, grid_rows=l // GRID_W),
        grid=(steps, b),
        in_specs=[tok_spec(col(5)), tok_spec(col(6)), tok_spec(col(7)),
                  full((l, HEAD_DIM)), full((l, HEAD_DIM)),
                  pl.BlockSpec((hps, 2 * WIN_R, 2 * GRID_W), lambda h, i: (h, 0, 0)),
                  full((1, HEAD_DIM)), full((1, HEAD_DIM)), full((HEAD_DIM, HEAD_DIM))],
        out_specs=pl.BlockSpec((None, l, hps * HEAD_DIM), lambda h, i: (i, 0, h)),
        out_shape=jax.ShapeDtypeStruct((b, l, MIX_W), BF16),
        scratch_shapes=[pltpu.VMEM((l, HEAD_DIM), BF16), pltpu.VMEM((l, HEAD_DIM), BF16),
                        pltpu.VMEM((l, HEAD_DIM), BF16), pltpu.VMEM((n_ctx, HEAD_DIM), BF16),
                        pltpu.VMEM((n_ctx, HEAD_DIM), BF16),
                        pltpu.VMEM((hps, WIN_R, GRID_W, WIN_R * GRID_W), F32)],
        compiler_params=_params(("parallel", "arbitrary")),
        name="na",
    )(p3, p3, p3, cos, sin, rb, gq, gk, perm)


def _merge_kernel(ya_ref, yb_ref, ga_ref, gb_ref, x_ref, wa_ref, wb_ref, wo_ref, mod_ref, n2_ref,
                  xm_ref, h2_ref):
    tm = x_ref.shape[0]
    halves = [pl.ds(n * (tm // 2), tm // 2) for n in range(2)]
    za = [_dot(ya_ref[r, :], wa_ref[...]) for r in halves]
    zb = [_dot(yb_ref[r, :], wb_ref[...]) for r in halves]
    z = [(_sigmoid(ga_ref[r, :].astype(F32)) * za[n] + _sigmoid(gb_ref[r, :].astype(F32)) * zb[n]).astype(BF16)
         for n, r in enumerate(halves)]
    br = [_dot(zn, wo_ref[...]) for zn in z]
    for n, r in enumerate(halves):
        xm = x_ref[r, :] + mod_ref[0:1, :] * br[n]
        xm_ref[r, :] = xm
        y = xm * lax.rsqrt(jnp.mean(xm * xm, axis=-1, keepdims=True) + EPS) * n2_ref[...]
        h2_ref[r, :] = (y * (1.0 + mod_ref[2:3, :]) + mod_ref[1:2, :]).astype(BF16)


def _merge_call(ya, yb, pg3, x, wa, wb, wo, mod3, n2):
    b, l, d = x.shape
    tm = 2 * TOK
    const = lambda shape: pl.BlockSpec(shape, lambda i, t: (0,) * len(shape), pipeline_mode=pl.Buffered(1))
    return pl.pallas_call(
        _merge_kernel,
        grid=(b, l // tm),
        in_specs=[pl.BlockSpec((None, tm, MIX_W), lambda i, t: (i, t, 0)),
                  pl.BlockSpec((None, tm, MIX_W), lambda i, t: (i, t, 0)),
                  pl.BlockSpec((None, tm, d), lambda i, t: (i, t, 0)),
                  pl.BlockSpec((None, tm, d), lambda i, t: (i, t, 1)),
                  pl.BlockSpec((None, tm, d), lambda i, t: (i, t, 0)),
                  const((MIX_W, d)), const((MIX_W, d)), const((d, d)),
                  pl.BlockSpec((None, 3, d), lambda i, t: (i, 0, 0)),
                  pl.BlockSpec((1, d), lambda i, t: (0, 0))],
        out_specs=[pl.BlockSpec((None, tm, d), lambda i, t: (i, t, 0)),
                   pl.BlockSpec((None, tm, d), lambda i, t: (i, t, 0))],
        out_shape=[jax.ShapeDtypeStruct((b, l, d), F32), jax.ShapeDtypeStruct((b, l, d), BF16)],
        compiler_params=_params(("parallel", "arbitrary")),
        name="merge",
    )(ya, yb, pg3, pg3, x, wa, wb, wo, mod3, n2)


def _ffn_kernel(h_ref, hp_ref, hn_ref, w1_ref, w3_ref, cw_ref, cb_ref, w2_ref, xm_ref, g2_ref, o_ref, a_ref,
                hc_ref, *, tiles_per_seq, halo):
    i = pl.program_id(0)
    s = pl.program_id(1)
    n_hid, tm, th = a_ref.shape

    @pl.when(s == 0)
    def _():
        hc_ref[0:tm, :] = h_ref[...]
        hc_ref[tm:tm + halo, :] = hp_ref[...]
        hc_ref[tm + halo:tm + 2 * halo, :] = hn_ref[...]

    @pl.when(s < n_hid)
    def _():
        t1_all = _dot(hc_ref[...], w1_ref[...])
        t1 = t1_all[0:tm]
        prev_row = t1_all[tm + halo - 1:tm + halo, :]
        next_row = t1_all[tm + halo:tm + halo + 1, :]
        prev_row = jnp.where(i % tiles_per_seq == 0, 0.0, prev_row)
        next_row = jnp.where(i % tiles_per_seq == tiles_per_seq - 1, 0.0, next_row)
        ridx = lax.broadcasted_iota(jnp.int32, (tm, 1), 0)
        up = jnp.where(ridx == 0, prev_row, pltpu.roll(t1, 1, 0))
        dn = jnp.where(ridx == tm - 1, next_row, pltpu.roll(t1, tm - 1, 0))
        u = up * cw_ref[0:1, :] + t1 * cw_ref[1:2, :] + dn * cw_ref[2:3, :] + cb_ref[...]
        a_ref[s] = ((u * _sigmoid(u)) * _dot(hc_ref[0:tm, :], w3_ref[...])).astype(BF16)

    @pl.when(s >= n_hid)
    def _():
        acc = _dot(a_ref[0], w2_ref[0:th, :])
        for k in range(1, n_hid):
            acc = acc + _dot(a_ref[k], w2_ref[k * th:(k + 1) * th, :])
        o_ref[...] = xm_ref[...] + g2_ref[...] * acc


FFN_TOKEN_TILE = 1024
FFN_OUT_TILE = 256


def _ffn_hidden_tile(hid):
    return min(512, hid)


def _ffn_call(h2, xm, w1, w3, cw, cb, w2, g2, l):
    m, d = h2.shape
    n_hid, _, th = w1.shape
    hid = n_hid * th
    tm = FFN_TOKEN_TILE
    tn = w2.shape[-1]
    halo = 16
    per = tm // halo
    nblk = m // halo
    hcol = lambda i, s: (0, jnp.minimum(s, n_hid - 1))
    htile = lambda i, s: (jnp.minimum(s, n_hid - 1), 0, 0)
    ocol = lambda s: jnp.maximum(s - n_hid, 0)
    return pl.pallas_call(
        functools.partial(_ffn_kernel, tiles_per_seq=l // tm, halo=halo),
        grid=(m // tm, n_hid + d // tn),
        in_specs=[pl.BlockSpec((tm, d), lambda i, s: (i, 0)),
                  pl.BlockSpec((halo, d), lambda i, s: (jnp.maximum(i * per - 1, 0), 0)),
                  pl.BlockSpec((halo, d), lambda i, s: (jnp.minimum((i + 1) * per, nblk - 1), 0)),
                  pl.BlockSpec((None, d, th), htile),
                  pl.BlockSpec((None, d, th), htile),
                  pl.BlockSpec((3, th), hcol),
                  pl.BlockSpec((1, th), hcol),
                  pl.BlockSpec((None, hid, tn), lambda i, s: (ocol(s), 0, 0)),
                  pl.BlockSpec((tm, tn), lambda i, s: (i, ocol(s))),
                  pl.BlockSpec((None, 1, tn), lambda i, s: (i // (l // tm), 0, ocol(s)))],
        out_specs=pl.BlockSpec((tm, tn), lambda i, s: (i, ocol(s))),
        out_shape=jax.ShapeDtypeStruct((m, d), F32),
        scratch_shapes=[pltpu.VMEM((n_hid, tm, th), BF16), pltpu.VMEM((tm + 2 * halo, d), BF16)],
        compiler_params=_params(("parallel", "arbitrary")),
        name="ffn",
    )(h2, h2, h2, w1, w3, cw, cb, w2, xm, g2)


def kernel(x, c, ctx, c_ctx, ada_w, ada_b, norm1_g, norm2_g, w_in, hgrn_lb_logits, hgrn_norm_g,
           na_q_norm_g, na_k_norm_g, na_rel_bias, w_branch_a, w_branch_b, w_out,
           ffn_w1, ffn_w3, ffn_conv_w, ffn_conv_b, ffn_w2):
    b, l, d = x.shape
    n_ctx = ctx.shape[1]
    assert ada_w.shape[0] == 1 and b + 1 <= 8 and n_ctx % TOK == 0 and l % (2 * TOK) == 0

    c_all = jnp.zeros((8, d), F32).at[:b].set(c).at[b].set(c_ctx)
    mod = _mod_call(c_all, ada_w[0], ada_b).reshape(8, N_MOD, d)
    sh1, sc1, g1, sh2, sc2, g2 = (mod[:b, n] for n in range(N_MOD))
    ss_ctx = jnp.broadcast_to(mod[b, 0:2][None], (b, 2, d))
    ss1 = jnp.stack([jnp.stack([sh1, sc1], axis=1), ss_ctx], axis=1)

    h = _prenorm_call(x, ctx, norm1_g, ss1)
    t = l + n_ctx
    h = h.reshape(b * t, d)
    n_mix = 8 * MIX_W
    th = _ffn_hidden_tile(ffn_w1.shape[-1])
    p, (w1, w3, w2) = _inproj_call(h, w_in[0], 0, n_mix, F32, "inproj",
                                   ((ffn_w1[0], th), (ffn_w3[0], th), (ffn_w2[0], FFN_OUT_TILE)))
    pg, (wa, wb, wo) = _inproj_call(h, w_in[0], n_mix, 2 * d, BF16, "inproj_gates",
                                    ((w_branch_a[0], d), (w_branch_b[0], d), (w_out[0], d)))
    wa, wb, wo = (wt.reshape(wt.shape[1:]) for wt in (wa, wb, wo))
    p3 = p.reshape(b, t, n_mix)
    pg3 = pg.reshape(b, t, 2 * d)

    y_a = _hgrn_call(p3, hgrn_lb_logits, hgrn_norm_g, n_ctx, l)
    y_b = _na_call(p3, na_rel_bias[0], na_q_norm_g, na_k_norm_g, n_ctx, l)

    mod3 = jnp.stack([g1, sh2, sc2], axis=1)
    x_mid, h2 = _merge_call(y_a, y_b, pg3, x, wa, wb, wo, mod3, norm2_g)

    out = _ffn_call(h2.reshape(b * l, d), x_mid.reshape(b * l, d), w1, w3, ffn_conv_w[0], ffn_conv_b, w2,
                    g2[:, None, :], l)
    return out.reshape(b, l, d)
```

```python
import functools

import numpy as np
import jax
import jax.numpy as jnp
from jax import lax
from jax.experimental import pallas as pl
from jax.experimental.pallas import tpu as pltpu

GRID_W = 64
HEADS = 8
HEAD_DIM = 128
MIX_W = HEADS * HEAD_DIM
CHUNK = 64
N_LEVELS = 6
MXU_LEVEL_HALVES = (4, 2)
PREP_CHUNKS = 8
SCAN_CHUNKS = 8
HGRN_HEADS_PER_STEP = 2
NA_HEADS_PER_STEP = 2
NA_ROWS = 8
WIN_R = 8
WIN_C = 16
ROPE_THETA = 10000.0
N_MOD = 6
EPS = 1e-6
LOG2E = 1.4426950408889634
TOK = 256
MASKED = -1e30
V7X_VMEM_LIMIT = 60 * 1024 * 1024

BF16 = jnp.bfloat16
F32 = jnp.float32


def _dot(a, b):
    return jnp.dot(a, b, preferred_element_type=F32)


def _dot_nt(a, b):
    return lax.dot_general(a, b, (((1,), (1,)), ((), ())), preferred_element_type=F32)


def _sigmoid(t):
    return 1.0 / (1.0 + jnp.exp(-t))


def _params(sem):
    return pltpu.CompilerParams(dimension_semantics=sem, vmem_limit_bytes=V7X_VMEM_LIMIT)


def _mod_kernel(c_ref, w_ref, b_ref, o_ref):
    cv = c_ref[...]
    s = (cv * _sigmoid(cv)).astype(BF16)
    o_ref[...] = _dot(s, w_ref[...].astype(BF16)) + b_ref[...]


def _mod_call(c_all, w, b):
    d, n = w.shape
    tn = min(1024, d)
    return pl.pallas_call(
        _mod_kernel,
        grid=(n // tn,),
        in_specs=[pl.BlockSpec((8, d), lambda j: (0, 0)),
                  pl.BlockSpec((d, tn), lambda j: (0, j)),
                  pl.BlockSpec((1, tn), lambda j: (0, j))],
        out_specs=pl.BlockSpec((8, tn), lambda j: (0, j)),
        out_shape=jax.ShapeDtypeStruct((8, n), F32),
        compiler_params=_params(("arbitrary",)),
        name="mod",
    )(c_all, w, b)


def _prenorm_kernel(x_ref, ctx_ref, g_ref, ss_ref, o_ref, *, n_lat_blk):
    t = pl.program_id(1)
    xv = jnp.where(t < n_lat_blk, x_ref[...], ctx_ref[...])
    y = xv * lax.rsqrt(jnp.mean(xv * xv, axis=-1, keepdims=True) + EPS) * g_ref[...]
    o_ref[...] = (y * (1.0 + ss_ref[1:2, :]) + ss_ref[0:1, :]).astype(BF16)


def _prenorm_call(x, ctx, g, ss):
    b, l, d = x.shape
    nlb = l // TOK
    nt = nlb + ctx.shape[1] // TOK
    return pl.pallas_call(
        functools.partial(_prenorm_kernel, n_lat_blk=nlb),
        grid=(b, nt),
        in_specs=[pl.BlockSpec((None, TOK, d), lambda i, t: (i, jnp.minimum(t, nlb - 1), 0)),
                  pl.BlockSpec((None, TOK, d), lambda i, t: (i, jnp.maximum(t - nlb, 0), 0)),
                  pl.BlockSpec((1, d), lambda i, t: (0, 0)),
                  pl.BlockSpec((None, None, 2, d), lambda i, t: (i, (t >= nlb).astype(jnp.int32), 0, 0))],
        out_specs=pl.BlockSpec((None, TOK, d), lambda i, t: (i, t, 0)),
        out_shape=jax.ShapeDtypeStruct((b, nt * TOK, d), BF16),
        compiler_params=_params(("parallel", "arbitrary")),
        name="prenorm",
    )(x, ctx, g, ss)


def _inproj_kernel(*refs, n_cast):
    h_ref, w_ref = refs[:2]
    cast_in = refs[2:2 + n_cast]
    o_ref = refs[2 + n_cast]
    cast_out = refs[3 + n_cast:3 + 2 * n_cast]
    wb_ref = refs[-1]

    @pl.when(pl.program_id(1) == 0)
    def _():
        wb_ref[...] = w_ref[...].astype(BF16)

    o_ref[...] = _dot(h_ref[...], wb_ref[...]).astype(o_ref.dtype)
    for src, dst in zip(cast_in, cast_out):
        ct = dst.shape[-1]
        for c in range(dst.shape[0]):
            dst[c] = src[:, c * ct:(c + 1) * ct].astype(BF16)


def _slab_rows(rows, steps):
    return next(r for r in range(32, rows + 1, 32) if rows % r == 0 and rows // r <= steps)


def _inproj_call(h, w, col0, n, out_dtype, name, to_cast=()):
    m, d = h.shape
    tm = 1024 if m % 1024 == 0 else TOK
    tn = min(1024, d)
    j0 = col0 // tn
    ni = m // tm
    steps = (n // tn) * ni
    in_slabs, out_slabs, out_shapes = [], [], []
    for a, ct in to_cast:
        rows, cols = a.shape
        r = _slab_rows(rows, steps)
        slab = lambda j, i, last=rows // r - 1: jnp.minimum(j * ni + i, last)
        in_slabs.append(pl.BlockSpec((r, cols), lambda j, i, slab=slab: (slab(j, i), 0)))
        out_slabs.append(pl.BlockSpec((cols // ct, r, ct), lambda j, i, slab=slab: (0, slab(j, i), 0)))
        out_shapes.append(jax.ShapeDtypeStruct((cols // ct, rows, ct), BF16))
    outs = pl.pallas_call(
        functools.partial(_inproj_kernel, n_cast=len(to_cast)),
        grid=(n // tn, ni),
        in_specs=[pl.BlockSpec((tm, d), lambda j, i: (i, 0)),
                  pl.BlockSpec((d, tn), lambda j, i: (0, j + j0))] + in_slabs,
        out_specs=[pl.BlockSpec((tm, tn), lambda j, i: (i, j))] + out_slabs,
        out_shape=[jax.ShapeDtypeStruct((m, n), out_dtype)] + out_shapes,
        scratch_shapes=[pltpu.VMEM((d, tn), BF16)],
        compiler_params=_params(("arbitrary", "arbitrary")),
        name=name,
    )(h, w, *[a for a, _ in to_cast])
    return outs[0], outs[1:]


def _hgrn_constants():
    u = np.arange(CHUNK)
    mats, masks = [], []
    for rev in (False, True):
        pos = CHUNK - 1 - u if rev else u
        incl = (pos[None, :] <= pos[:, None]).astype(np.float32)
        blocks, lvl_masks = [incl], []
        for lev in range(N_LEVELS):
            half = CHUNK >> (lev + 1)
            ref = (pos // (2 * half)) * (2 * half) + half - 1
            upto_ref = (pos[None, :] <= ref[:, None]).astype(np.float32)
            if half in MXU_LEVEL_HALVES:
                late = (pos % (2 * half)) >= half
                blocks.append(np.where(late[:, None], 1.0, -1.0).astype(np.float32) * (incl - upto_ref))
            same = (pos[:, None] // (2 * half)) == (pos[None, :] // (2 * half))
            late_q = (pos[:, None] % (2 * half)) >= half
            early_k = (pos[None, :] % (2 * half)) < half
            lvl_masks.append((same & late_q & early_k).astype(np.float32))
        lvl_masks.append(np.eye(CHUNK, dtype=np.float32))
        mats.append(np.concatenate(blocks, axis=0))
        masks.append(np.stack(lvl_masks))
    return np.stack(mats), np.stack(masks)


def _hgrn_kernel(lbl_ref, q_ref, ff_ref, fb_ref, i_ref, gt_ref, cm_ref, mk_ref, ng_ref, y_ref, *scratch, **chunks):
    for hh in range(HGRN_HEADS_PER_STEP):
        lanes = pl.ds(hh * HEAD_DIM, HEAD_DIM)
        head = [r.at[:, lanes] for r in (q_ref, ff_ref, fb_ref, i_ref, gt_ref)]
        _hgrn_head(lbl_ref.at[:, :, hh], *head, cm_ref, mk_ref, ng_ref, y_ref.at[:, lanes], *scratch, **chunks)


def _hgrn_head(lbl_ref, q_ref, ff_ref, fb_ref, i_ref, gt_ref, cm_ref, mk_ref, ng_ref, y_ref,
               of_ref, ob_ref, sf_ref, sb_ref, kd_ref, et_ref, qd_ref, sc_ref, *, n_ctx_chunks, n_lat_chunks):
    f_refs, o_refs, st_refs = (ff_ref, fb_ref), (of_ref, ob_ref), (sf_ref, sb_ref)
    lowers = []
    for dirn in range(2):
        la, lb_ = lbl_ref[dirn, 0], lbl_ref[dirn, 1]
        mx = jnp.maximum(la, lb_)
        ea, eb = jnp.exp(la - mx), jnp.exp(lb_ - mx)
        lowers.append(ea / (ea + eb))
        st_refs[dirn][...] = jnp.zeros(st_refs[dirn].shape, F32)
    row_odd = (lax.broadcasted_iota(jnp.int32, (CHUNK, 1), 0) % 2) == 1

    def chunk_rows(pos):
        return pl.ds(pl.multiple_of(pos * CHUNK, CHUNK), CHUNK)

    def level_decay(dirn, lev, f, sums):
        half = CHUNK >> (lev + 1)
        cum = sums[0:CHUNK]
        if half == 1:
            return jnp.where(row_odd, 1.0, f) if dirn else jnp.where(row_odd, f, 1.0)
        if half in MXU_LEVEL_HALVES:
            blk = 1 + MXU_LEVEL_HALVES.index(half)
            return jnp.exp2(sums[blk * CHUNK:(blk + 1) * CHUNK])
        parts = []
        for p in range(0, CHUNK, 2 * half):
            ref = cum[p + half - 1 + dirn:p + half + dirn]
            lo, hi = cum[p:p + half], cum[p + half:p + 2 * half]
            parts += [lo - ref, ref - hi] if dirn else [ref - lo, hi - ref]
        return jnp.exp2(jnp.concatenate(parts, axis=0))

    def prepare(pos0, n, with_out):
        items = [(dirn, pos0 + c) for c in range(n) for dirn in range(2)]
        fs, kks, splits = [], [], []
        for dirn, pos in items:
            lower = lowers[dirn]
            f = lower + (1.0 - lower) * _sigmoid(f_refs[dirn][chunk_rows(pos), :])
            g = jnp.log(f) * LOG2E
            g1 = g.astype(BF16)
            r1 = g - g1.astype(F32)
            g2 = r1.astype(BF16)
            g3 = (r1 - g2.astype(F32)).astype(BF16)
            fs.append(f)
            kks.append(1.0 - f)
            splits.append(jnp.concatenate([g1, g2, g3], axis=0))
        sums = [None] * len(items)
        for dirn in range(2):
            idx = [j for j, it in enumerate(items) if it[0] == dirn]
            wide = _dot(cm_ref[dirn], jnp.concatenate([splits[j] for j in idx], axis=1))
            for c, j in enumerate(idx):
                sums[j] = wide[:, c * HEAD_DIM:(c + 1) * HEAD_DIM]
        for j, (dirn, pos) in enumerate(items):
            cum = sums[j][0:CHUNK]
            last = 0 if dirn else CHUNK - 1
            tot = cum[last:last + 1]
            kd_ref[dirn, chunk_rows(pos), :] = (kks[j] * jnp.exp2(tot - cum)).astype(BF16)
            et_ref[dirn, pos] = jnp.broadcast_to(jnp.exp2(tot), et_ref.shape[2:])
        if not with_out:
            return
        qs = [q_ref[chunk_rows(pos), :] for _, pos in items]
        for j, (dirn, pos) in enumerate(items):
            qd_ref[dirn, chunk_rows(pos), :] = (qs[j] * jnp.exp2(sums[j][0:CHUNK])).astype(BF16)
        qbs = [qv.astype(BF16) for qv in qs]
        kbs = [kv.astype(BF16) for kv in kks]
        scs = [mk_ref[dirn, N_LEVELS] * _dot_nt(qbs[j], kbs[j]) for j, (dirn, _) in enumerate(items)]
        for lev in range(N_LEVELS):
            for j, (dirn, _) in enumerate(items):
                dl = level_decay(dirn, lev, fs[j], sums[j]).astype(BF16)
                scs[j] = scs[j] + mk_ref[dirn, lev] * _dot_nt(qbs[j] * dl, kbs[j] * dl)
        for j, (dirn, pos) in enumerate(items):
            sc_ref[dirn, chunk_rows(pos), :] = scs[j].astype(BF16)

    def scan(pos_f, pos_b, n, with_out):
        items = [(dirn, (pos_b - c) if dirn else (pos_f + c)) for c in range(n) for dirn in range(2)]
        vs = [i_ref[chunk_rows(pos), :] for _, pos in items]
        ups = [_dot(vs[j].T.astype(BF16), kd_ref[dirn, chunk_rows(pos), :]) for j, (dirn, pos) in enumerate(items)]
        sts = [st_refs[0][...], st_refs[1][...]]
        before = []
        for j, (dirn, pos) in enumerate(items):
            before.append(sts[dirn])
            sts[dirn] = sts[dirn] * et_ref[dirn, pos][0:1] + ups[j]
        for dirn in range(2):
            st_refs[dirn][...] = sts[dirn]
        if not with_out:
            return
        for j, (dirn, pos) in enumerate(items):
            orow = chunk_rows(pos)
            o_refs[dirn][orow, :] = (_dot_nt(qd_ref[dirn, orow, :], before[j].astype(BF16))
                                     + _dot(sc_ref[dirn, orow, :], vs[j].astype(BF16)))

    def loop(n, body):
        lax.fori_loop(0, n, lambda ci, c: (body(ci), c)[1], 0)

    ctx0 = n_lat_chunks
    prepare(ctx0, n_ctx_chunks, False)
    loop(n_lat_chunks // PREP_CHUNKS, lambda ci: prepare(ci * PREP_CHUNKS, PREP_CHUNKS, True))
    scan(ctx0, ctx0 + n_ctx_chunks - 1, n_ctx_chunks, False)
    loop(n_lat_chunks // SCAN_CHUNKS,
         lambda ci: scan(ci * SCAN_CHUNKS, n_lat_chunks - 1 - ci * SCAN_CHUNKS, SCAN_CHUNKS, True))

    def readout(i, carry):
        rows = pl.ds(pl.multiple_of(i * TOK, TOK), TOK)
        ot = of_ref[rows, :] + ob_ref[rows, :]
        on = ot * lax.rsqrt(jnp.mean(ot * ot, axis=-1, keepdims=True) + EPS) * ng_ref[...]
        gate = gt_ref[rows, :]
        y_ref[rows, :] = (on * (gate * _sigmoid(gate))).astype(BF16)
        return carry

    lax.fori_loop(0, n_lat_chunks * CHUNK // TOK, readout, 0)


def _hgrn_call(p3, lb_logits, norm_g, n_ctx, l):
    b, t, _ = p3.shape
    cm, mk = _hgrn_constants()
    cm = jnp.asarray(np.concatenate([cm] * 3, axis=-1), BF16)
    mk = jnp.asarray(mk, F32)
    lbl = lb_logits.reshape(2, 2, HEADS, 1, HEAD_DIM)
    hps = HGRN_HEADS_PER_STEP
    steps = HEADS // hps
    tok_spec = lambda grp: pl.BlockSpec((None, t, hps * HEAD_DIM), lambda i, h: (i, 0, grp * steps + h))
    full = lambda shape: pl.BlockSpec(shape, lambda i, h: (0,) * len(shape))
    return pl.pallas_call(
        functools.partial(_hgrn_kernel, n_ctx_chunks=n_ctx // CHUNK, n_lat_chunks=l // CHUNK),
        grid=(b, steps),
        in_specs=[pl.BlockSpec((2, 2, hps, 1, HEAD_DIM), lambda i, h: (0, 0, h, 0, 0)),
                  tok_spec(0), tok_spec(1), tok_spec(2), tok_spec(3), tok_spec(4),
                  full((2, 3 * CHUNK, 3 * CHUNK)),
                  full((2, N_LEVELS + 1, CHUNK, CHUNK)),
                  full((1, HEAD_DIM))],
        out_specs=pl.BlockSpec((None, l, hps * HEAD_DIM), lambda i, h: (i, 0, h)),
        out_shape=jax.ShapeDtypeStruct((b, l, MIX_W), BF16),
        scratch_shapes=[pltpu.VMEM((l, HEAD_DIM), F32), pltpu.VMEM((l, HEAD_DIM), F32),
                        pltpu.VMEM((HEAD_DIM, HEAD_DIM), F32), pltpu.VMEM((HEAD_DIM, HEAD_DIM), F32),
                        pltpu.VMEM((2, t, HEAD_DIM), BF16), pltpu.VMEM((2, t // CHUNK, 8, HEAD_DIM), F32),
                        pltpu.VMEM((2, l, HEAD_DIM), BF16), pltpu.VMEM((2, l, CHUNK), BF16)],
        compiler_params=_params(("parallel", "parallel")),
        name="hgrn",
    )(lbl, p3, p3, p3, p3, p3, cm, mk, norm_g)


def _rope_tables(l):
    pos = np.arange(l)
    nf = HEAD_DIM // 4
    inv = ROPE_THETA ** (-np.arange(nf, dtype=np.float64) / nf)
    ang_r = (pos // GRID_W)[:, None] * inv[None, :]
    ang_c = (pos % GRID_W)[:, None] * inv[None, :]
    cos = np.concatenate([np.cos(ang_r)] * 2 + [np.cos(ang_c)] * 2, axis=-1)
    sin = np.concatenate([-np.sin(ang_r), np.sin(ang_r), -np.sin(ang_c), np.sin(ang_c)], axis=-1)
    return jnp.asarray(cos, F32), jnp.asarray(sin, F32)


def _na_build_bias(rb_ref, tb_ref):
    lanes = 2 * GRID_W
    cq = lax.broadcasted_iota(jnp.int32, (GRID_W, lanes), 0)
    lane = lax.broadcasted_iota(jnp.int32, (GRID_W, lanes), 1)
    ck = lane % GRID_W
    cs = jnp.clip(cq - WIN_C // 2, 0, GRID_W - WIN_C)
    col_in = (ck >= cs) & (ck < cs + WIN_C)

    def toeplitz(d, lane0):
        row = jnp.broadcast_to(rb_ref[d:d + 1, :] * LOG2E, (GRID_W, lanes))
        return pltpu.roll(row, (lane0 - (WIN_C - 1)) % lanes, 1, stride=1, stride_axis=0)

    pair = [jnp.where(col_in, jnp.where(lane < GRID_W, toeplitz(d, 0), toeplitz(d + 1, GRID_W)), MASKED)
            for d in range(2 * WIN_R - 2)]
    for e in range(WIN_R):
        for p in range(WIN_R // 2):
            tb_ref[e, :, p * lanes:(p + 1) * lanes] = pair[2 * p - e + WIN_R - 1]


def _na_kernel(q_ref, k_ref, v_ref, cos_ref, sin_ref, rb_ref, gq_ref, gk_ref, perm_ref, o_ref,
               qs, ks, vs, kcs, vcs, tb_ref, **static):
    for hh in range(NA_HEADS_PER_STEP):
        lanes = pl.ds(hh * HEAD_DIM, HEAD_DIM)
        head = [r.at[:, lanes] for r in (q_ref, k_ref, v_ref)]
        _na_head(*head, cos_ref, sin_ref, rb_ref.at[hh], gq_ref, gk_ref, perm_ref, o_ref.at[:, lanes],
                 qs, ks, vs, kcs, vcs, tb_ref.at[hh], **static)


def _na_head(q_ref, k_ref, v_ref, cos_ref, sin_ref, rb_ref, gq_ref, gk_ref, perm_ref, o_ref,
             qs, ks, vs, kcs, vcs, tb_ref, *, n_ctx, grid_rows):
    @pl.when(pl.program_id(1) == 0)
    def _():
        _na_build_bias(rb_ref, tb_ref)

    ones = jnp.ones((2 * HEAD_DIM, HEAD_DIM), BF16)

    def normed(tv, g):
        sq = tv * tv
        hi = sq.astype(BF16)
        lo = (sq - hi.astype(F32)).astype(BF16)
        ssq = _dot(jnp.concatenate([hi, lo], axis=1), ones)
        return tv * lax.rsqrt(ssq * (1.0 / HEAD_DIM) + EPS) * g


    gq = gq_ref[...]
    gk = gk_ref[...]
    n_lat = grid_rows * GRID_W
    kcs[...] = normed(k_ref[n_lat:n_lat + n_ctx, :], gk).astype(BF16)
    vcs[...] = v_ref[n_lat:n_lat + n_ctx, :].astype(BF16)

    def prep(i, carry):
        rows = [pl.ds(pl.multiple_of((2 * i + n) * TOK, TOK), TOK) for n in range(2)]
        srcs = rows
        jobs = [(src_ref, g, dst, n) for n in range(2) for src_ref, g, dst in ((q_ref, gq, qs), (k_ref, gk, ks))]
        nrm = [normed(src_ref[srcs[n], :], g) for src_ref, g, _, n in jobs]
        par = [_dot(tv.astype(BF16), perm_ref[...]) for tv in nrm]
        for (_, _, dst, n), tv, pv in zip(jobs, nrm, par):
            dst[rows[n], :] = (tv * cos_ref[rows[n], :] + pv * sin_ref[rows[n], :]).astype(BF16)
        for n in range(2):
            vs[rows[n], :] = v_ref[srcs[n], :].astype(BF16)
        return carry

    lax.fori_loop(0, grid_rows * GRID_W // (2 * TOK), prep, 0)
    scale = HEAD_DIM ** -0.5 * LOG2E
    band = WIN_R * GRID_W

    def rows_step(i, carry):
        rr = [i * NA_ROWS + n for n in range(NA_ROWS)]
        rs = [jnp.clip(r - WIN_R // 2, 0, grid_rows - WIN_R) for r in rr]
        qrow = [pl.ds(pl.multiple_of(r * GRID_W, GRID_W), GRID_W) for r in rr]
        krow = [pl.ds(pl.multiple_of(s * GRID_W, GRID_W), band) for s in rs]
        qr = [qs[qw, :] for qw in qrow]
        kc = kcs[...]
        sb = [_dot_nt(qr[n], ks[krow[n], :]) * scale + tb_ref[rr[n] - rs[n]] for n in range(NA_ROWS)]
        sc = [_dot_nt(qr[n], kc) * scale for n in range(NA_ROWS)]
        m = [jnp.maximum(jnp.max(sb[n], axis=-1, keepdims=True), jnp.max(sc[n], axis=-1, keepdims=True))
             for n in range(NA_ROWS)]
        pb = [jnp.exp2(sb[n] - m[n]) for n in range(NA_ROWS)]
        pc = [jnp.exp2(sc[n] - m[n]) for n in range(NA_ROWS)]
        den = [jnp.sum(pb[n], axis=-1, keepdims=True) + jnp.sum(pc[n], axis=-1, keepdims=True)
               for n in range(NA_ROWS)]
        vc = vcs[...]
        o = [_dot(pb[n].astype(BF16), vs[krow[n], :]) + _dot(pc[n].astype(BF16), vc) for n in range(NA_ROWS)]
        for n in range(NA_ROWS):
            o_ref[qrow[n], :] = (o[n] / den[n]).astype(BF16)
        return carry

    lax.fori_loop(0, grid_rows // NA_ROWS, rows_step, 0)


def _na_call(p3, rel_bias, gq, gk, n_ctx, l):
    b, t, _ = p3.shape
    cos, sin = _rope_tables(l)
    nr, nc = rel_bias.shape[1:]
    rb = jnp.zeros((HEADS, 2 * WIN_R, 2 * GRID_W), F32).at[:, :nr, :nc].set(rel_bias)
    lanes = np.arange(HEAD_DIM)
    partner = np.where(lanes % (HEAD_DIM // 2) < HEAD_DIM // 4, lanes + HEAD_DIM // 4, lanes - HEAD_DIM // 4)
    perm = jnp.asarray(lanes[:, None] == partner[None, :], BF16)
    hps = NA_HEADS_PER_STEP
    steps = HEADS // hps
    col = lambda grp: (lambda h, i: (i, 0, grp * steps + h))
    tok_spec = lambda im: pl.BlockSpec((None, t, hps * HEAD_DIM), im)
    full = lambda shape: pl.BlockSpec(shape, lambda h, i: (0,) * len(shape))
    return pl.pallas_call(
        functools.partial(_na_kernel, n_ctx=n_ctx, grid_rows=l // GRID_W),
        grid=(steps, b),
        in_specs=[tok_spec(col(5)), tok_spec(col(6)), tok_spec(col(7)),
                  full((l, HEAD_DIM)), full((l, HEAD_DIM)),
                  pl.BlockSpec((hps, 2 * WIN_R, 2 * GRID_W), lambda h, i: (h, 0, 0)),
                  full((1, HEAD_DIM)), full((1, HEAD_DIM)), full((HEAD_DIM, HEAD_DIM))],
        out_specs=pl.BlockSpec((None, l, hps * HEAD_DIM), lambda h, i: (i, 0, h)),
        out_shape=jax.ShapeDtypeStruct((b, l, MIX_W), BF16),
        scratch_shapes=[pltpu.VMEM((l, HEAD_DIM), BF16), pltpu.VMEM((l, HEAD_DIM), BF16),
                        pltpu.VMEM((l, HEAD_DIM), BF16), pltpu.VMEM((n_ctx, HEAD_DIM), BF16),
                        pltpu.VMEM((n_ctx, HEAD_DIM), BF16),
                        pltpu.VMEM((hps, WIN_R, GRID_W, WIN_R * GRID_W), F32)],
        compiler_params=_params(("parallel", "arbitrary")),
        name="na",
    )(p3, p3, p3, cos, sin, rb, gq, gk, perm)


def _merge_kernel(ya_ref, yb_ref, ga_ref, gb_ref, x_ref, wa_ref, wb_ref, wo_ref, mod_ref, n2_ref,
                  xm_ref, h2_ref):
    tm = x_ref.shape[0]
    halves = [pl.ds(n * (tm // 2), tm // 2) for n in range(2)]
    za = [_dot(ya_ref[r, :], wa_ref[...]) for r in halves]
    zb = [_dot(yb_ref[r, :], wb_ref[...]) for r in halves]
    z = [(_sigmoid(ga_ref[r, :].astype(F32)) * za[n] + _sigmoid(gb_ref[r, :].astype(F32)) * zb[n]).astype(BF16)
         for n, r in enumerate(halves)]
    br = [_dot(zn, wo_ref[...]) for zn in z]
    for n, r in enumerate(halves):
        xm = x_ref[r, :] + mod_ref[0:1, :] * br[n]
        xm_ref[r, :] = xm
        y = xm * lax.rsqrt(jnp.mean(xm * xm, axis=-1, keepdims=True) + EPS) * n2_ref[...]
        h2_ref[r, :] = (y * (1.0 + mod_ref[2:3, :]) + mod_ref[1:2, :]).astype(BF16)


def _merge_call(ya, yb, pg3, x, wa, wb, wo, mod3, n2):
    b, l, d = x.shape
    tm = 2 * TOK
    const = lambda shape: pl.BlockSpec(shape, lambda i, t: (0,) * len(shape), pipeline_mode=pl.Buffered(1))
    return pl.pallas_call(
        _merge_kernel,
        grid=(b, l // tm),
        in_specs=[pl.BlockSpec((None, tm, MIX_W), lambda i, t: (i, t, 0)),
                  pl.BlockSpec((None, tm, MIX_W), lambda i, t: (i, t, 0)),
                  pl.BlockSpec((None, tm, d), lambda i, t: (i, t, 0)),
                  pl.BlockSpec((None, tm, d), lambda i, t: (i, t, 1)),
                  pl.BlockSpec((None, tm, d), lambda i, t: (i, t, 0)),
                  const((MIX_W, d)), const((MIX_W, d)), const((d, d)),
                  pl.BlockSpec((None, 3, d), lambda i, t: (i, 0, 0)),
                  pl.BlockSpec((1, d), lambda i, t: (0, 0))],
        out_specs=[pl.BlockSpec((None, tm, d), lambda i, t: (i, t, 0)),
                   pl.BlockSpec((None, tm, d), lambda i, t: (i, t, 0))],
        out_shape=[jax.ShapeDtypeStruct((b, l, d), F32), jax.ShapeDtypeStruct((b, l, d), BF16)],
        compiler_params=_params(("parallel", "arbitrary")),
        name="merge",
    )(ya, yb, pg3, pg3, x, wa, wb, wo, mod3, n2)


def _ffn_kernel(h_ref, hp_ref, hn_ref, w1_ref, w3_ref, cw_ref, cb_ref, w2_ref, xm_ref, g2_ref, o_ref, a_ref,
                hc_ref, *, tiles_per_seq, halo):
    i = pl.program_id(0)
    s = pl.program_id(1)
    n_hid, tm, th = a_ref.shape

    @pl.when(s == 0)
    def _():
        hc_ref[0:tm, :] = h_ref[...]
        hc_ref[tm:tm + halo, :] = hp_ref[...]
        hc_ref[tm + halo:tm + 2 * halo, :] = hn_ref[...]

    @pl.when(s < n_hid)
    def _():
        t1_all = _dot(hc_ref[...], w1_ref[...])
        t1 = t1_all[0:tm]
        prev_row = t1_all[tm + halo - 1:tm + halo, :]
        next_row = t1_all[tm + halo:tm + halo + 1, :]
        prev_row = jnp.where(i % tiles_per_seq == 0, 0.0, prev_row)
        next_row = jnp.where(i % tiles_per_seq == tiles_per_seq - 1, 0.0, next_row)
        ridx = lax.broadcasted_iota(jnp.int32, (tm, 1), 0)
        up = jnp.where(ridx == 0, prev_row, pltpu.roll(t1, 1, 0))
        dn = jnp.where(ridx == tm - 1, next_row, pltpu.roll(t1, tm - 1, 0))
        u = up * cw_ref[0:1, :] + t1 * cw_ref[1:2, :] + dn * cw_ref[2:3, :] + cb_ref[...]
        a_ref[s] = ((u * _sigmoid(u)) * _dot(hc_ref[0:tm, :], w3_ref[...])).astype(BF16)

    @pl.when(s >= n_hid)
    def _():
        acc = _dot(a_ref[0], w2_ref[0:th, :])
        for k in range(1, n_hid):
            acc = acc + _dot(a_ref[k], w2_ref[k * th:(k + 1) * th, :])
        o_ref[...] = xm_ref[...] + g2_ref[...] * acc


FFN_TOKEN_TILE = 1024
FFN_OUT_TILE = 512


def _ffn_hidden_tile(hid):
    return min(512, hid)


def _ffn_call(h2, xm, w1, w3, cw, cb, w2, g2, l):
    m, d = h2.shape
    n_hid, _, th = w1.shape
    hid = n_hid * th
    tm = FFN_TOKEN_TILE
    tn = w2.shape[-1]
    halo = 16
    per = tm // halo
    nblk = m // halo
    hcol = lambda i, s: (0, jnp.minimum(s, n_hid - 1))
    htile = lambda i, s: (jnp.minimum(s, n_hid - 1), 0, 0)
    ocol = lambda s: jnp.maximum(s - n_hid, 0)
    return pl.pallas_call(
        functools.partial(_ffn_kernel, tiles_per_seq=l // tm, halo=halo),
        grid=(m // tm, n_hid + d // tn),
        in_specs=[pl.BlockSpec((tm, d), lambda i, s: (i, 0)),
                  pl.BlockSpec((halo, d), lambda i, s: (jnp.maximum(i * per - 1, 0), 0)),
                  pl.BlockSpec((halo, d), lambda i, s: (jnp.minimum((i + 1) * per, nblk - 1), 0)),
                  pl.BlockSpec((None, d, th), htile),
                  pl.BlockSpec((None, d, th), htile),
                  pl.BlockSpec((3, th), hcol),
                  pl.BlockSpec((1, th), hcol),
                  pl.BlockSpec((None, hid, tn), lambda i, s: (ocol(s), 0, 0)),
                  pl.BlockSpec((tm, tn), lambda i, s: (i, ocol(s))),
                  pl.BlockSpec((None, 1, tn), lambda i, s: (i // (l // tm), 0, ocol(s)))],
        out_specs=pl.BlockSpec((tm, tn), lambda i, s: (i, ocol(s))),
        out_shape=jax.ShapeDtypeStruct((m, d), F32),
        scratch_shapes=[pltpu.VMEM((n_hid, tm, th), BF16), pltpu.VMEM((tm + 2 * halo, d), BF16)],
        compiler_params=_params(("parallel", "arbitrary")),
        name="ffn",
    )(h2, h2, h2, w1, w3, cw, cb, w2, xm, g2)


def kernel(x, c, ctx, c_ctx, ada_w, ada_b, norm1_g, norm2_g, w_in, hgrn_lb_logits, hgrn_norm_g,
           na_q_norm_g, na_k_norm_g, na_rel_bias, w_branch_a, w_branch_b, w_out,
           ffn_w1, ffn_w3, ffn_conv_w, ffn_conv_b, ffn_w2):
    b, l, d = x.shape
    n_ctx = ctx.shape[1]
    assert ada_w.shape[0] == 1 and b + 1 <= 8 and n_ctx % TOK == 0 and l % (2 * TOK) == 0

    c_all = jnp.zeros((8, d), F32).at[:b].set(c).at[b].set(c_ctx)
    mod = _mod_call(c_all, ada_w[0], ada_b).reshape(8, N_MOD, d)
    sh1, sc1, g1, sh2, sc2, g2 = (mod[:b, n] for n in range(N_MOD))
    ss_ctx = jnp.broadcast_to(mod[b, 0:2][None], (b, 2, d))
    ss1 = jnp.stack([jnp.stack([sh1, sc1], axis=1), ss_ctx], axis=1)

    h = _prenorm_call(x, ctx, norm1_g, ss1)
    t = l + n_ctx
    h = h.reshape(b * t, d)
    n_mix = 8 * MIX_W
    th = _ffn_hidden_tile(ffn_w1.shape[-1])
    p, (w1, w3, w2) = _inproj_call(h, w_in[0], 0, n_mix, F32, "inproj",
                                   ((ffn_w1[0], th), (ffn_w3[0], th), (ffn_w2[0], FFN_OUT_TILE)))
    pg, (wa, wb, wo) = _inproj_call(h, w_in[0], n_mix, 2 * d, BF16, "inproj_gates",
                                    ((w_branch_a[0], d), (w_branch_b[0], d), (w_out[0], d)))
    wa, wb, wo = (wt.reshape(wt.shape[1:]) for wt in (wa, wb, wo))
    p3 = p.reshape(b, t, n_mix)
    pg3 = pg.reshape(b, t, 2 * d)

    y_a = _hgrn_call(p3, hgrn_lb_logits, hgrn_norm_g, n_ctx, l)
    y_b = _na_call(p3, na_rel_bias[0], na_q_norm_g, na_k_norm_g, n_ctx, l)

    mod3 = jnp.stack([g1, sh2, sc2], axis=1)
    x_mid, h2 = _merge_call(y_a, y_b, pg3, x, wa, wb, wo, mod3, norm2_g)

    out = _ffn_call(h2.reshape(b * l, d), x_mid.reshape(b * l, d), w1, w3, ffn_conv_w[0], ffn_conv_b, w2,
                    g2[:, None, :], l)
    return out.reshape(b, l, d)
```

```python
import functools

import numpy as np
import jax
import jax.numpy as jnp
from jax import lax
from jax.experimental import pallas as pl
from jax.experimental.pallas import tpu as pltpu

GRID_W = 64
HEADS = 8
HEAD_DIM = 128
MIX_W = HEADS * HEAD_DIM
CHUNK = 64
N_LEVELS = 6
MXU_LEVEL_HALVES = (4, 2)
PREP_CHUNKS = 8
SCAN_CHUNKS = 8
HGRN_HEADS_PER_STEP = 2
NA_HEADS_PER_STEP = 2
NA_ROWS = 8
WIN_R = 8
WIN_C = 16
ROPE_THETA = 10000.0
N_MOD = 6
EPS = 1e-6
LOG2E = 1.4426950408889634
TOK = 256
MASKED = -1e30
V7X_VMEM_LIMIT = 60 * 1024 * 1024

BF16 = jnp.bfloat16
F32 = jnp.float32


def _dot(a, b):
    return jnp.dot(a, b, preferred_element_type=F32)


def _dot_nt(a, b):
    return lax.dot_general(a, b, (((1,), (1,)), ((), ())), preferred_element_type=F32)


def _sigmoid(t):
    return 1.0 / (1.0 + jnp.exp(-t))


def _params(sem):
    return pltpu.CompilerParams(dimension_semantics=sem, vmem_limit_bytes=V7X_VMEM_LIMIT)


def _mod_kernel(c_ref, w_ref, b_ref, o_ref):
    cv = c_ref[...]
    s = (cv * _sigmoid(cv)).astype(BF16)
    o_ref[...] = _dot(s, w_ref[...].astype(BF16)) + b_ref[...]


def _mod_call(c_all, w, b):
    d, n = w.shape
    tn = min(1024, d)
    return pl.pallas_call(
        _mod_kernel,
        grid=(n // tn,),
        in_specs=[pl.BlockSpec((8, d), lambda j: (0, 0)),
                  pl.BlockSpec((d, tn), lambda j: (0, j)),
                  pl.BlockSpec((1, tn), lambda j: (0, j))],
        out_specs=pl.BlockSpec((8, tn), lambda j: (0, j)),
        out_shape=jax.ShapeDtypeStruct((8, n), F32),
        compiler_params=_params(("arbitrary",)),
        name="mod",
    )(c_all, w, b)


def _prenorm_kernel(x_ref, ctx_ref, g_ref, ss_ref, o_ref, *, n_lat_blk):
    t = pl.program_id(1)
    xv = jnp.where(t < n_lat_blk, x_ref[...], ctx_ref[...])
    y = xv * lax.rsqrt(jnp.mean(xv * xv, axis=-1, keepdims=True) + EPS) * g_ref[...]
    o_ref[...] = (y * (1.0 + ss_ref[1:2, :]) + ss_ref[0:1, :]).astype(BF16)


def _prenorm_call(x, ctx, g, ss):
    b, l, d = x.shape
    nlb = l // TOK
    nt = nlb + ctx.shape[1] // TOK
    return pl.pallas_call(
        functools.partial(_prenorm_kernel, n_lat_blk=nlb),
        grid=(b, nt),
        in_specs=[pl.BlockSpec((None, TOK, d), lambda i, t: (i, jnp.minimum(t, nlb - 1), 0)),
                  pl.BlockSpec((None, TOK, d), lambda i, t: (i, jnp.maximum(t - nlb, 0), 0)),
                  pl.BlockSpec((1, d), lambda i, t: (0, 0)),
                  pl.BlockSpec((None, None, 2, d), lambda i, t: (i, (t >= nlb).astype(jnp.int32), 0, 0))],
        out_specs=pl.BlockSpec((None, TOK, d), lambda i, t: (i, t, 0)),
        out_shape=jax.ShapeDtypeStruct((b, nt * TOK, d), BF16),
        compiler_params=_params(("parallel", "arbitrary")),
        name="prenorm",
    )(x, ctx, g, ss)


def _inproj_kernel(*refs, groups):
    n_src = sum(groups)
    h_ref, w_ref = refs[:2]
    cast_in = refs[2:2 + n_src]
    o_ref = refs[2 + n_src]
    cast_out = refs[3 + n_src:3 + n_src + len(groups)]
    wb_ref = refs[-1]

    @pl.when(pl.program_id(1) == 0)
    def _():
        wb_ref[...] = w_ref[...].astype(BF16)

    o_ref[...] = _dot(h_ref[...], wb_ref[...]).astype(o_ref.dtype)
    first = 0
    for n, dst in zip(groups, cast_out):
        ct = dst.shape[-1] // n
        for c in range(dst.shape[0]):
            for k, src in enumerate(cast_in[first:first + n]):
                dst[c, :, k * ct:(k + 1) * ct] = src[:, c * ct:(c + 1) * ct].astype(BF16)
        first += n


def _slab_rows(rows, steps):
    return next(r for r in range(32, rows + 1, 32) if rows % r == 0 and rows // r <= steps)


def _inproj_call(h, w, col0, n, out_dtype, name, to_cast=()):
    m, d = h.shape
    tm = 1024 if m % 1024 == 0 else TOK
    tn = min(1024, d)
    j0 = col0 // tn
    ni = m // tm
    steps = (n // tn) * ni
    in_slabs, out_slabs, out_shapes = [], [], []
    for group, ct in to_cast:
        rows, cols = group[0].shape
        r = _slab_rows(rows, steps)
        slab = lambda j, i, last=rows // r - 1: jnp.minimum(j * ni + i, last)
        in_slabs += [pl.BlockSpec((r, cols), lambda j, i, slab=slab: (slab(j, i), 0))] * len(group)
        out_slabs.append(pl.BlockSpec((cols // ct, r, ct * len(group)), lambda j, i, slab=slab: (0, slab(j, i), 0)))
        out_shapes.append(jax.ShapeDtypeStruct((cols // ct, rows, ct * len(group)), BF16))
    outs = pl.pallas_call(
        functools.partial(_inproj_kernel, groups=tuple(len(g) for g, _ in to_cast)),
        grid=(n // tn, ni),
        in_specs=[pl.BlockSpec((tm, d), lambda j, i: (i, 0)),
                  pl.BlockSpec((d, tn), lambda j, i: (0, j + j0))] + in_slabs,
        out_specs=[pl.BlockSpec((tm, tn), lambda j, i: (i, j))] + out_slabs,
        out_shape=[jax.ShapeDtypeStruct((m, n), out_dtype)] + out_shapes,
        scratch_shapes=[pltpu.VMEM((d, tn), BF16)],
        compiler_params=_params(("arbitrary", "arbitrary")),
        name=name,
    )(h, w, *[a for g, _ in to_cast for a in g])
    return outs[0], outs[1:]


def _hgrn_constants():
    u = np.arange(CHUNK)
    mats, masks = [], []
    for rev in (False, True):
        pos = CHUNK - 1 - u if rev else u
        incl = (pos[None, :] <= pos[:, None]).astype(np.float32)
        blocks, lvl_masks = [incl], []
        for lev in range(N_LEVELS):
            half = CHUNK >> (lev + 1)
            ref = (pos // (2 * half)) * (2 * half) + half - 1
            upto_ref = (pos[None, :] <= ref[:, None]).astype(np.float32)
            if half in MXU_LEVEL_HALVES:
                late = (pos % (2 * half)) >= half
                blocks.append(np.where(late[:, None], 1.0, -1.0).astype(np.float32) * (incl - upto_ref))
            same = (pos[:, None] // (2 * half)) == (pos[None, :] // (2 * half))
            late_q = (pos[:, None] % (2 * half)) >= half
            early_k = (pos[None, :] % (2 * half)) < half
            lvl_masks.append((same & late_q & early_k).astype(np.float32))
        lvl_masks.append(np.eye(CHUNK, dtype=np.float32))
        mats.append(np.concatenate(blocks, axis=0))
        masks.append(np.stack(lvl_masks))
    return np.stack(mats), np.stack(masks)


def _hgrn_kernel(lbl_ref, q_ref, ff_ref, fb_ref, i_ref, gt_ref, cm_ref, mk_ref, ng_ref, y_ref, *scratch, **chunks):
    for hh in range(HGRN_HEADS_PER_STEP):
        lanes = pl.ds(hh * HEAD_DIM, HEAD_DIM)
        head = [r.at[:, lanes] for r in (q_ref, ff_ref, fb_ref, i_ref, gt_ref)]
        _hgrn_head(lbl_ref.at[:, :, hh], *head, cm_ref, mk_ref, ng_ref, y_ref.at[:, lanes], *scratch, **chunks)


def _hgrn_head(lbl_ref, q_ref, ff_ref, fb_ref, i_ref, gt_ref, cm_ref, mk_ref, ng_ref, y_ref,
               of_ref, ob_ref, sf_ref, sb_ref, kd_ref, et_ref, qd_ref, sc_ref, *, n_ctx_chunks, n_lat_chunks):
    f_refs, o_refs, st_refs = (ff_ref, fb_ref), (of_ref, ob_ref), (sf_ref, sb_ref)
    lowers = []
    for dirn in range(2):
        la, lb_ = lbl_ref[dirn, 0], lbl_ref[dirn, 1]
        mx = jnp.maximum(la, lb_)
        ea, eb = jnp.exp(la - mx), jnp.exp(lb_ - mx)
        lowers.append(ea / (ea + eb))
        st_refs[dirn][...] = jnp.zeros(st_refs[dirn].shape, F32)
    row_odd = (lax.broadcasted_iota(jnp.int32, (CHUNK, 1), 0) % 2) == 1

    def chunk_rows(pos):
        return pl.ds(pl.multiple_of(pos * CHUNK, CHUNK), CHUNK)

    def level_decay(dirn, lev, f, sums):
        half = CHUNK >> (lev + 1)
        cum = sums[0:CHUNK]
        if half == 1:
            return jnp.where(row_odd, 1.0, f) if dirn else jnp.where(row_odd, f, 1.0)
        if half in MXU_LEVEL_HALVES:
            blk = 1 + MXU_LEVEL_HALVES.index(half)
            return jnp.exp2(sums[blk * CHUNK:(blk + 1) * CHUNK])
        parts = []
        for p in range(0, CHUNK, 2 * half):
            ref = cum[p + half - 1 + dirn:p + half + dirn]
            lo, hi = cum[p:p + half], cum[p + half:p + 2 * half]
            parts += [lo - ref, ref - hi] if dirn else [ref - lo, hi - ref]
        return jnp.exp2(jnp.concatenate(parts, axis=0))

    def prepare(pos0, n, with_out):
        items = [(dirn, pos0 + c) for c in range(n) for dirn in range(2)]
        fs, kks, splits = [], [], []
        for dirn, pos in items:
            lower = lowers[dirn]
            f = lower + (1.0 - lower) * _sigmoid(f_refs[dirn][chunk_rows(pos), :])
            g = jnp.log(f) * LOG2E
            g1 = g.astype(BF16)
            r1 = g - g1.astype(F32)
            g2 = r1.astype(BF16)
            g3 = (r1 - g2.astype(F32)).astype(BF16)
            fs.append(f)
            kks.append(1.0 - f)
            splits.append(jnp.concatenate([g1, g2, g3], axis=0))
        sums = [None] * len(items)
        for dirn in range(2):
            idx = [j for j, it in enumerate(items) if it[0] == dirn]
            wide = _dot(cm_ref[dirn], jnp.concatenate([splits[j] for j in idx], axis=1))
            for c, j in enumerate(idx):
                sums[j] = wide[:, c * HEAD_DIM:(c + 1) * HEAD_DIM]
        for j, (dirn, pos) in enumerate(items):
            cum = sums[j][0:CHUNK]
            last = 0 if dirn else CHUNK - 1
            tot = cum[last:last + 1]
            kd_ref[dirn, chunk_rows(pos), :] = (kks[j] * jnp.exp2(tot - cum)).astype(BF16)
            et_ref[dirn, pos] = jnp.broadcast_to(jnp.exp2(tot), et_ref.shape[2:])
        if not with_out:
            return
        qs = [q_ref[chunk_rows(pos), :] for _, pos in items]
        for j, (dirn, pos) in enumerate(items):
            qd_ref[dirn, chunk_rows(pos), :] = (qs[j] * jnp.exp2(sums[j][0:CHUNK])).astype(BF16)
        qbs = [qv.astype(BF16) for qv in qs]
        kbs = [kv.astype(BF16) for kv in kks]
        scs = [mk_ref[dirn, N_LEVELS] * _dot_nt(qbs[j], kbs[j]) for j, (dirn, _) in enumerate(items)]
        for lev in range(N_LEVELS):
            for j, (dirn, _) in enumerate(items):
                dl = level_decay(dirn, lev, fs[j], sums[j]).astype(BF16)
                scs[j] = scs[j] + mk_ref[dirn, lev] * _dot_nt(qbs[j] * dl, kbs[j] * dl)
        for j, (dirn, pos) in enumerate(items):
            sc_ref[dirn, chunk_rows(pos), :] = scs[j].astype(BF16)

    def scan(pos_f, pos_b, n, with_out):
        items = [(dirn, (pos_b - c) if dirn else (pos_f + c)) for c in range(n) for dirn in range(2)]
        vs = [i_ref[chunk_rows(pos), :] for _, pos in items]
        ups = [_dot(vs[j].T.astype(BF16), kd_ref[dirn, chunk_rows(pos), :]) for j, (dirn, pos) in enumerate(items)]
        sts = [st_refs[0][...], st_refs[1][...]]
        before = []
        for j, (dirn, pos) in enumerate(items):
            before.append(sts[dirn])
            sts[dirn] = sts[dirn] * et_ref[dirn, pos][0:1] + ups[j]
        for dirn in range(2):
            st_refs[dirn][...] = sts[dirn]
        if not with_out:
            return
        for j, (dirn, pos) in enumerate(items):
            orow = chunk_rows(pos)
            o_refs[dirn][orow, :] = (_dot_nt(qd_ref[dirn, orow, :], before[j].astype(BF16))
                                     + _dot(sc_ref[dirn, orow, :], vs[j].astype(BF16)))

    def loop(n, body):
        lax.fori_loop(0, n, lambda ci, c: (body(ci), c)[1], 0)

    ctx0 = n_lat_chunks
    prepare(ctx0, n_ctx_chunks, False)
    loop(n_lat_chunks // PREP_CHUNKS, lambda ci: prepare(ci * PREP_CHUNKS, PREP_CHUNKS, True))
    scan(ctx0, ctx0 + n_ctx_chunks - 1, n_ctx_chunks, False)
    loop(n_lat_chunks // SCAN_CHUNKS,
         lambda ci: scan(ci * SCAN_CHUNKS, n_lat_chunks - 1 - ci * SCAN_CHUNKS, SCAN_CHUNKS, True))

    def readout(i, carry):
        rows = pl.ds(pl.multiple_of(i * TOK, TOK), TOK)
        ot = of_ref[rows, :] + ob_ref[rows, :]
        on = ot * lax.rsqrt(jnp.mean(ot * ot, axis=-1, keepdims=True) + EPS) * ng_ref[...]
        gate = gt_ref[rows, :]
        y_ref[rows, :] = (on * (gate * _sigmoid(gate))).astype(BF16)
        return carry

    lax.fori_loop(0, n_lat_chunks * CHUNK // TOK, readout, 0)


def _hgrn_call(p3, lb_logits, norm_g, n_ctx, l):
    b, t, _ = p3.shape
    cm, mk = _hgrn_constants()
    cm = jnp.asarray(np.concatenate([cm] * 3, axis=-1), BF16)
    mk = jnp.asarray(mk, F32)
    lbl = lb_logits.reshape(2, 2, HEADS, 1, HEAD_DIM)
    hps = HGRN_HEADS_PER_STEP
    steps = HEADS // hps
    tok_spec = lambda grp: pl.BlockSpec((None, t, hps * HEAD_DIM), lambda i, h: (i, 0, grp * steps + h))
    full = lambda shape: pl.BlockSpec(shape, lambda i, h: (0,) * len(shape))
    return pl.pallas_call(
        functools.partial(_hgrn_kernel, n_ctx_chunks=n_ctx // CHUNK, n_lat_chunks=l // CHUNK),
        grid=(b, steps),
        in_specs=[pl.BlockSpec((2, 2, hps, 1, HEAD_DIM), lambda i, h: (0, 0, h, 0, 0)),
                  tok_spec(0), tok_spec(1), tok_spec(2), tok_spec(3), tok_spec(4),
                  full((2, 3 * CHUNK, 3 * CHUNK)),
                  full((2, N_LEVELS + 1, CHUNK, CHUNK)),
                  full((1, HEAD_DIM))],
        out_specs=pl.BlockSpec((None, l, hps * HEAD_DIM), lambda i, h: (i, 0, h)),
        out_shape=jax.ShapeDtypeStruct((b, l, MIX_W), BF16),
        scratch_shapes=[pltpu.VMEM((l, HEAD_DIM), F32), pltpu.VMEM((l, HEAD_DIM), F32),
                        pltpu.VMEM((HEAD_DIM, HEAD_DIM), F32), pltpu.VMEM((HEAD_DIM, HEAD_DIM), F32),
                        pltpu.VMEM((2, t, HEAD_DIM), BF16), pltpu.VMEM((2, t // CHUNK, 8, HEAD_DIM), F32),
                        pltpu.VMEM((2, l, HEAD_DIM), BF16), pltpu.VMEM((2, l, CHUNK), BF16)],
        compiler_params=_params(("parallel", "parallel")),
        name="hgrn",
    )(lbl, p3, p3, p3, p3, p3, cm, mk, norm_g)


def _rope_tables(l):
    pos = np.arange(l)
    nf = HEAD_DIM // 4
    inv = ROPE_THETA ** (-np.arange(nf, dtype=np.float64) / nf)
    ang_r = (pos // GRID_W)[:, None] * inv[None, :]
    ang_c = (pos % GRID_W)[:, None] * inv[None, :]
    cos = np.concatenate([np.cos(ang_r)] * 2 + [np.cos(ang_c)] * 2, axis=-1)
    sin = np.concatenate([-np.sin(ang_r), np.sin(ang_r), -np.sin(ang_c), np.sin(ang_c)], axis=-1)
    return jnp.asarray(cos, F32), jnp.asarray(sin, F32)


def _na_build_bias(rb_ref, tb_ref):
    lanes = 2 * GRID_W
    cq = lax.broadcasted_iota(jnp.int32, (GRID_W, lanes), 0)
    lane = lax.broadcasted_iota(jnp.int32, (GRID_W, lanes), 1)
    ck = lane % GRID_W
    cs = jnp.clip(cq - WIN_C // 2, 0, GRID_W - WIN_C)
    col_in = (ck >= cs) & (ck < cs + WIN_C)

    def toeplitz(d, lane0):
        row = jnp.broadcast_to(rb_ref[d:d + 1, :] * LOG2E, (GRID_W, lanes))
        return pltpu.roll(row, (lane0 - (WIN_C - 1)) % lanes, 1, stride=1, stride_axis=0)

    pair = [jnp.where(col_in, jnp.where(lane < GRID_W, toeplitz(d, 0), toeplitz(d + 1, GRID_W)), MASKED)
            for d in range(2 * WIN_R - 2)]
    for e in range(WIN_R):
        for p in range(WIN_R // 2):
            tb_ref[e, :, p * lanes:(p + 1) * lanes] = pair[2 * p - e + WIN_R - 1]


def _na_kernel(q_ref, k_ref, v_ref, cos_ref, sin_ref, rb_ref, gq_ref, gk_ref, perm_ref, o_ref,
               qs, ks, vs, kcs, vcs, tb_ref, **static):
    for hh in range(NA_HEADS_PER_STEP):
        lanes = pl.ds(hh * HEAD_DIM, HEAD_DIM)
        head = [r.at[:, lanes] for r in (q_ref, k_ref, v_ref)]
        _na_head(*head, cos_ref, sin_ref, rb_ref.at[hh], gq_ref, gk_ref, perm_ref, o_ref.at[:, lanes],
                 qs, ks, vs, kcs, vcs, tb_ref.at[hh], **static)


def _na_head(q_ref, k_ref, v_ref, cos_ref, sin_ref, rb_ref, gq_ref, gk_ref, perm_ref, o_ref,
             qs, ks, vs, kcs, vcs, tb_ref, *, n_ctx, grid_rows):
    @pl.when(pl.program_id(1) == 0)
    def _():
        _na_build_bias(rb_ref, tb_ref)

    ones = jnp.ones((2 * HEAD_DIM, HEAD_DIM), BF16)

    def normed(tv, g):
        sq = tv * tv
        hi = sq.astype(BF16)
        lo = (sq - hi.astype(F32)).astype(BF16)
        ssq = _dot(jnp.concatenate([hi, lo], axis=1), ones)
        return tv * lax.rsqrt(ssq * (1.0 / HEAD_DIM) + EPS) * g


    gq = gq_ref[...]
    gk = gk_ref[...]
    n_lat = grid_rows * GRID_W
    kcs[...] = normed(k_ref[n_lat:n_lat + n_ctx, :], gk).astype(BF16)
    vcs[...] = v_ref[n_lat:n_lat + n_ctx, :].astype(BF16)

    def prep(i, carry):
        rows = [pl.ds(pl.multiple_of((2 * i + n) * TOK, TOK), TOK) for n in range(2)]
        srcs = rows
        jobs = [(src_ref, g, dst, n) for n in range(2) for src_ref, g, dst in ((q_ref, gq, qs), (k_ref, gk, ks))]
        nrm = [normed(src_ref[srcs[n], :], g) for src_ref, g, _, n in jobs]
        par = [_dot(tv.astype(BF16), perm_ref[...]) for tv in nrm]
        for (_, _, dst, n), tv, pv in zip(jobs, nrm, par):
            dst[rows[n], :] = (tv * cos_ref[rows[n], :] + pv * sin_ref[rows[n], :]).astype(BF16)
        for n in range(2):
            vs[rows[n], :] = v_ref[srcs[n], :].astype(BF16)
        return carry

    lax.fori_loop(0, grid_rows * GRID_W // (2 * TOK), prep, 0)
    scale = HEAD_DIM ** -0.5 * LOG2E
    band = WIN_R * GRID_W

    def rows_step(i, carry):
        rr = [i * NA_ROWS + n for n in range(NA_ROWS)]
        rs = [jnp.clip(r - WIN_R // 2, 0, grid_rows - WIN_R) for r in rr]
        qrow = [pl.ds(pl.multiple_of(r * GRID_W, GRID_W), GRID_W) for r in rr]
        krow = [pl.ds(pl.multiple_of(s * GRID_W, GRID_W), band) for s in rs]
        qr = [qs[qw, :] for qw in qrow]
        kc = kcs[...]
        sb = [_dot_nt(qr[n], ks[krow[n], :]) * scale + tb_ref[rr[n] - rs[n]] for n in range(NA_ROWS)]
        sc = [_dot_nt(qr[n], kc) * scale for n in range(NA_ROWS)]
        m = [jnp.maximum(jnp.max(sb[n], axis=-1, keepdims=True), jnp.max(sc[n], axis=-1, keepdims=True))
             for n in range(NA_ROWS)]
        pb = [jnp.exp2(sb[n] - m[n]) for n in range(NA_ROWS)]
        pc = [jnp.exp2(sc[n] - m[n]) for n in range(NA_ROWS)]
        den = [jnp.sum(pb[n], axis=-1, keepdims=True) + jnp.sum(pc[n], axis=-1, keepdims=True)
               for n in range(NA_ROWS)]
        vc = vcs[...]
        o = [_dot(pb[n].astype(BF16), vs[krow[n], :]) + _dot(pc[n].astype(BF16), vc) for n in range(NA_ROWS)]
        for n in range(NA_ROWS):
            o_ref[qrow[n], :] = (o[n] / den[n]).astype(BF16)
        return carry

    lax.fori_loop(0, grid_rows // NA_ROWS, rows_step, 0)


def _na_call(p3, rel_bias, gq, gk, n_ctx, l):
    b, t, _ = p3.shape
    cos, sin = _rope_tables(l)
    nr, nc = rel_bias.shape[1:]
    rb = jnp.zeros((HEADS, 2 * WIN_R, 2 * GRID_W), F32).at[:, :nr, :nc].set(rel_bias)
    lanes = np.arange(HEAD_DIM)
    partner = np.where(lanes % (HEAD_DIM // 2) < HEAD_DIM // 4, lanes + HEAD_DIM // 4, lanes - HEAD_DIM // 4)
    perm = jnp.asarray(lanes[:, None] == partner[None, :], BF16)
    hps = NA_HEADS_PER_STEP
    steps = HEADS // hps
    col = lambda grp: (lambda h, i: (i, 0, grp * steps + h))
    tok_spec = lambda im: pl.BlockSpec((None, t, hps * HEAD_DIM), im)
    full = lambda shape: pl.BlockSpec(shape, lambda h, i: (0,) * len(shape))
    return pl.pallas_call(
        functools.partial(_na_kernel, n_ctx=n_ctx, grid_rows=l // GRID_W),
        grid=(steps, b),
        in_specs=[tok_spec(col(5)), tok_spec(col(6)), tok_spec(col(7)),
                  full((l, HEAD_DIM)), full((l, HEAD_DIM)),
                  pl.BlockSpec((hps, 2 * WIN_R, 2 * GRID_W), lambda h, i: (h, 0, 0)),
                  full((1, HEAD_DIM)), full((1, HEAD_DIM)), full((HEAD_DIM, HEAD_DIM))],
        out_specs=pl.BlockSpec((None, l, hps * HEAD_DIM), lambda h, i: (i, 0, h)),
        out_shape=jax.ShapeDtypeStruct((b, l, MIX_W), BF16),
        scratch_shapes=[pltpu.VMEM((l, HEAD_DIM), BF16), pltpu.VMEM((l, HEAD_DIM), BF16),
                        pltpu.VMEM((l, HEAD_DIM), BF16), pltpu.VMEM((n_ctx, HEAD_DIM), BF16),
                        pltpu.VMEM((n_ctx, HEAD_DIM), BF16),
                        pltpu.VMEM((hps, WIN_R, GRID_W, WIN_R * GRID_W), F32)],
        compiler_params=_params(("parallel", "arbitrary")),
        name="na",
    )(p3, p3, p3, cos, sin, rb, gq, gk, perm)


def _merge_kernel(ya_ref, yb_ref, ga_ref, gb_ref, x_ref, wa_ref, wb_ref, wo_ref, mod_ref, n2_ref,
                  xm_ref, h2_ref):
    tm = x_ref.shape[0]
    halves = [pl.ds(n * (tm // 2), tm // 2) for n in range(2)]
    za = [_dot(ya_ref[r, :], wa_ref[...]) for r in halves]
    zb = [_dot(yb_ref[r, :], wb_ref[...]) for r in halves]
    z = [(_sigmoid(ga_ref[r, :].astype(F32)) * za[n] + _sigmoid(gb_ref[r, :].astype(F32)) * zb[n]).astype(BF16)
         for n, r in enumerate(halves)]
    br = [_dot(zn, wo_ref[...]) for zn in z]
    for n, r in enumerate(halves):
        xm = x_ref[r, :] + mod_ref[0:1, :] * br[n]
        xm_ref[r, :] = xm
        y = xm * lax.rsqrt(jnp.mean(xm * xm, axis=-1, keepdims=True) + EPS) * n2_ref[...]
        h2_ref[r, :] = (y * (1.0 + mod_ref[2:3, :]) + mod_ref[1:2, :]).astype(BF16)


def _merge_call(ya, yb, pg3, x, wa, wb, wo, mod3, n2):
    b, l, d = x.shape
    tm = 2 * TOK
    const = lambda shape: pl.BlockSpec(shape, lambda i, t: (0,) * len(shape), pipeline_mode=pl.Buffered(1))
    return pl.pallas_call(
        _merge_kernel,
        grid=(b, l // tm),
        in_specs=[pl.BlockSpec((None, tm, MIX_W), lambda i, t: (i, t, 0)),
                  pl.BlockSpec((None, tm, MIX_W), lambda i, t: (i, t, 0)),
                  pl.BlockSpec((None, tm, d), lambda i, t: (i, t, 0)),
                  pl.BlockSpec((None, tm, d), lambda i, t: (i, t, 1)),
                  pl.BlockSpec((None, tm, d), lambda i, t: (i, t, 0)),
                  const((MIX_W, d)), const((MIX_W, d)), const((d, d)),
                  pl.BlockSpec((None, 3, d), lambda i, t: (i, 0, 0)),
                  pl.BlockSpec((1, d), lambda i, t: (0, 0))],
        out_specs=[pl.BlockSpec((None, tm, d), lambda i, t: (i, t, 0)),
                   pl.BlockSpec((None, tm, d), lambda i, t: (i, t, 0))],
        out_shape=[jax.ShapeDtypeStruct((b, l, d), F32), jax.ShapeDtypeStruct((b, l, d), BF16)],
        compiler_params=_params(("parallel", "arbitrary")),
        name="merge",
    )(ya, yb, pg3, pg3, x, wa, wb, wo, mod3, n2)


def _ffn_kernel(h_ref, hp_ref, hn_ref, w13_ref, cw_ref, cb_ref, w2_ref, xm_ref, g2_ref, o_ref, a_ref,
                hc_ref, *, tiles_per_seq, halo):
    i = pl.program_id(0)
    s = pl.program_id(1)
    n_hid, tm, th = a_ref.shape

    @pl.when(s == 0)
    def _():
        hc_ref[0:tm, :] = h_ref[...]
        hc_ref[tm:tm + halo, :] = hp_ref[...]
        hc_ref[tm + halo:tm + 2 * halo, :] = hn_ref[...]

    @pl.when(s < n_hid)
    def _():
        t13 = _dot(hc_ref[...], w13_ref[...])
        t1 = t13[0:tm, 0:th]
        prev_row = t13[tm + halo - 1:tm + halo, 0:th]
        next_row = t13[tm + halo:tm + halo + 1, 0:th]
        prev_row = jnp.where(i % tiles_per_seq == 0, 0.0, prev_row)
        next_row = jnp.where(i % tiles_per_seq == tiles_per_seq - 1, 0.0, next_row)
        ridx = lax.broadcasted_iota(jnp.int32, (tm, 1), 0)
        up = jnp.where(ridx == 0, prev_row, pltpu.roll(t1, 1, 0))
        dn = jnp.where(ridx == tm - 1, next_row, pltpu.roll(t1, tm - 1, 0))
        u = up * cw_ref[0:1, :] + t1 * cw_ref[1:2, :] + dn * cw_ref[2:3, :] + cb_ref[...]
        a_ref[s] = ((u * _sigmoid(u)) * t13[0:tm, th:2 * th]).astype(BF16)

    @pl.when(s >= n_hid)
    def _():
        acc = _dot(a_ref[0], w2_ref[0:th, :])
        for k in range(1, n_hid):
            acc = acc + _dot(a_ref[k], w2_ref[k * th:(k + 1) * th, :])
        o_ref[...] = xm_ref[...] + g2_ref[...] * acc


FFN_TOKEN_TILE = 1024
FFN_OUT_TILE = 512


def _ffn_hidden_tile(hid):
    return min(512, hid)


def _ffn_call(h2, xm, w13, cw, cb, w2, g2, l):
    m, d = h2.shape
    n_hid, _, th2 = w13.shape
    th = th2 // 2
    hid = n_hid * th
    tm = FFN_TOKEN_TILE
    tn = w2.shape[-1]
    halo = 16
    per = tm // halo
    nblk = m // halo
    hcol = lambda i, s: (0, jnp.minimum(s, n_hid - 1))
    htile = lambda i, s: (jnp.minimum(s, n_hid - 1), 0, 0)
    ocol = lambda s: jnp.maximum(s - n_hid, 0)
    return pl.pallas_call(
        functools.partial(_ffn_kernel, tiles_per_seq=l // tm, halo=halo),
        grid=(m // tm, n_hid + d // tn),
        in_specs=[pl.BlockSpec((tm, d), lambda i, s: (i, 0)),
                  pl.BlockSpec((halo, d), lambda i, s: (jnp.maximum(i * per - 1, 0), 0)),
                  pl.BlockSpec((halo, d), lambda i, s: (jnp.minimum((i + 1) * per, nblk - 1), 0)),
                  pl.BlockSpec((None, d, 2 * th), htile),
                  pl.BlockSpec((3, th), hcol),
                  pl.BlockSpec((1, th), hcol),
                  pl.BlockSpec((None, hid, tn), lambda i, s: (ocol(s), 0, 0)),
                  pl.BlockSpec((tm, tn), lambda i, s: (i, ocol(s))),
                  pl.BlockSpec((None, 1, tn), lambda i, s: (i // (l // tm), 0, ocol(s)))],
        out_specs=pl.BlockSpec((tm, tn), lambda i, s: (i, ocol(s))),
        out_shape=jax.ShapeDtypeStruct((m, d), F32),
        scratch_shapes=[pltpu.VMEM((n_hid, tm, th), BF16), pltpu.VMEM((tm + 2 * halo, d), BF16)],
        compiler_params=_params(("parallel", "arbitrary")),
        name="ffn",
    )(h2, h2, h2, w13, cw, cb, w2, xm, g2)


def kernel(x, c, ctx, c_ctx, ada_w, ada_b, norm1_g, norm2_g, w_in, hgrn_lb_logits, hgrn_norm_g,
           na_q_norm_g, na_k_norm_g, na_rel_bias, w_branch_a, w_branch_b, w_out,
           ffn_w1, ffn_w3, ffn_conv_w, ffn_conv_b, ffn_w2):
    b, l, d = x.shape
    n_ctx = ctx.shape[1]
    assert ada_w.shape[0] == 1 and b + 1 <= 8 and n_ctx % TOK == 0 and l % (2 * TOK) == 0

    c_all = jnp.zeros((8, d), F32).at[:b].set(c).at[b].set(c_ctx)
    mod = _mod_call(c_all, ada_w[0], ada_b).reshape(8, N_MOD, d)
    sh1, sc1, g1, sh2, sc2, g2 = (mod[:b, n] for n in range(N_MOD))
    ss_ctx = jnp.broadcast_to(mod[b, 0:2][None], (b, 2, d))
    ss1 = jnp.stack([jnp.stack([sh1, sc1], axis=1), ss_ctx], axis=1)

    h = _prenorm_call(x, ctx, norm1_g, ss1)
    t = l + n_ctx
    h = h.reshape(b * t, d)
    n_mix = 8 * MIX_W
    th = _ffn_hidden_tile(ffn_w1.shape[-1])
    p, (w13, w2) = _inproj_call(h, w_in[0], 0, n_mix, F32, "inproj",
                                (((ffn_w1[0], ffn_w3[0]), th), ((ffn_w2[0],), FFN_OUT_TILE)))
    pg, (wa, wb, wo) = _inproj_call(h, w_in[0], n_mix, 2 * d, BF16, "inproj_gates",
                                    (((w_branch_a[0],), d), ((w_branch_b[0],), d), ((w_out[0],), d)))
    wa, wb, wo = (wt.reshape(wt.shape[1:]) for wt in (wa, wb, wo))
    p3 = p.reshape(b, t, n_mix)
    pg3 = pg.reshape(b, t, 2 * d)

    y_a = _hgrn_call(p3, hgrn_lb_logits, hgrn_norm_g, n_ctx, l)
    y_b = _na_call(p3, na_rel_bias[0], na_q_norm_g, na_k_norm_g, n_ctx, l)

    mod3 = jnp.stack([g1, sh2, sc2], axis=1)
    x_mid, h2 = _merge_call(y_a, y_b, pg3, x, wa, wb, wo, mod3, norm2_g)

    out = _ffn_call(h2.reshape(b * l, d), x_mid.reshape(b * l, d), w13, ffn_conv_w[0], ffn_conv_b, w2,
                    g2[:, None, :], l)
    return out.reshape(b, l, d)
```

```python
import functools

import numpy as np
import jax
import jax.numpy as jnp
from jax import lax
from jax.experimental import pallas as pl
from jax.experimental.pallas import tpu as pltpu

GRID_W = 64
HEADS = 8
HEAD_DIM = 128
MIX_W = HEADS * HEAD_DIM
CHUNK = 64
N_LEVELS = 6
MXU_LEVEL_HALVES = (4, 2)
PREP_CHUNKS = 8
SCAN_CHUNKS = 16
HGRN_HEADS_PER_STEP = 2
NA_HEADS_PER_STEP = 2
NA_ROWS = 16
WIN_R = 8
WIN_C = 16
ROPE_THETA = 10000.0
N_MOD = 6
EPS = 1e-6
LOG2E = 1.4426950408889634
TOK = 256
MASKED = -1e30
V7X_VMEM_LIMIT = 60 * 1024 * 1024

BF16 = jnp.bfloat16
F32 = jnp.float32


def _dot(a, b):
    return jnp.dot(a, b, preferred_element_type=F32)


def _dot_nt(a, b):
    return lax.dot_general(a, b, (((1,), (1,)), ((), ())), preferred_element_type=F32)


def _sigmoid(t):
    return 1.0 / (1.0 + jnp.exp(-t))


def _params(sem):
    return pltpu.CompilerParams(dimension_semantics=sem, vmem_limit_bytes=V7X_VMEM_LIMIT)


def _mod_kernel(c_ref, w_ref, b_ref, o_ref):
    cv = c_ref[...]
    s = (cv * _sigmoid(cv)).astype(BF16)
    o_ref[...] = _dot(s, w_ref[...].astype(BF16)) + b_ref[...]


def _mod_call(c_all, w, b):
    d, n = w.shape
    tn = min(1024, d)
    return pl.pallas_call(
        _mod_kernel,
        grid=(n // tn,),
        in_specs=[pl.BlockSpec((8, d), lambda j: (0, 0)),
                  pl.BlockSpec((d, tn), lambda j: (0, j)),
                  pl.BlockSpec((1, tn), lambda j: (0, j))],
        out_specs=pl.BlockSpec((8, tn), lambda j: (0, j)),
        out_shape=jax.ShapeDtypeStruct((8, n), F32),
        compiler_params=_params(("arbitrary",)),
        name="mod",
    )(c_all, w, b)


def _prenorm_kernel(x_ref, ctx_ref, g_ref, ss_ref, o_ref, *, n_lat_blk):
    t = pl.program_id(1)
    xv = jnp.where(t < n_lat_blk, x_ref[...], ctx_ref[...])
    y = xv * lax.rsqrt(jnp.mean(xv * xv, axis=-1, keepdims=True) + EPS) * g_ref[...]
    o_ref[...] = (y * (1.0 + ss_ref[1:2, :]) + ss_ref[0:1, :]).astype(BF16)


def _prenorm_call(x, ctx, g, ss):
    b, l, d = x.shape
    nlb = l // TOK
    nt = nlb + ctx.shape[1] // TOK
    return pl.pallas_call(
        functools.partial(_prenorm_kernel, n_lat_blk=nlb),
        grid=(b, nt),
        in_specs=[pl.BlockSpec((None, TOK, d), lambda i, t: (i, jnp.minimum(t, nlb - 1), 0)),
                  pl.BlockSpec((None, TOK, d), lambda i, t: (i, jnp.maximum(t - nlb, 0), 0)),
                  pl.BlockSpec((1, d), lambda i, t: (0, 0)),
                  pl.BlockSpec((None, None, 2, d), lambda i, t: (i, (t >= nlb).astype(jnp.int32), 0, 0))],
        out_specs=pl.BlockSpec((None, TOK, d), lambda i, t: (i, t, 0)),
        out_shape=jax.ShapeDtypeStruct((b, nt * TOK, d), BF16),
        compiler_params=_params(("parallel", "arbitrary")),
        name="prenorm",
    )(x, ctx, g, ss)


def _inproj_kernel(*refs, n_cast):
    h_ref, w_ref = refs[:2]
    cast_in = refs[2:2 + n_cast]
    o_ref = refs[2 + n_cast]
    cast_out = refs[3 + n_cast:3 + 2 * n_cast]
    wb_ref = refs[-1]

    @pl.when(pl.program_id(1) == 0)
    def _():
        wb_ref[...] = w_ref[...].astype(BF16)

    o_ref[...] = _dot(h_ref[...], wb_ref[...]).astype(o_ref.dtype)
    for src, dst in zip(cast_in, cast_out):
        ct = dst.shape[-1]
        for c in range(dst.shape[0]):
            dst[c] = src[:, c * ct:(c + 1) * ct].astype(BF16)


def _slab_rows(rows, steps):
    return next(r for r in range(32, rows + 1, 32) if rows % r == 0 and rows // r <= steps)


def _inproj_call(h, w, col0, n, out_dtype, name, to_cast=()):
    m, d = h.shape
    tm = 1024 if m % 1024 == 0 else TOK
    tn = min(1024, d)
    j0 = col0 // tn
    ni = m // tm
    steps = (n // tn) * ni
    in_slabs, out_slabs, out_shapes = [], [], []
    for a, ct in to_cast:
        rows, cols = a.shape
        r = _slab_rows(rows, steps)
        slab = lambda j, i, last=rows // r - 1: jnp.minimum(j * ni + i, last)
        in_slabs.append(pl.BlockSpec((r, cols), lambda j, i, slab=slab: (slab(j, i), 0)))
        out_slabs.append(pl.BlockSpec((cols // ct, r, ct), lambda j, i, slab=slab: (0, slab(j, i), 0)))
        out_shapes.append(jax.ShapeDtypeStruct((cols // ct, rows, ct), BF16))
    outs = pl.pallas_call(
        functools.partial(_inproj_kernel, n_cast=len(to_cast)),
        grid=(n // tn, ni),
        in_specs=[pl.BlockSpec((tm, d), lambda j, i: (i, 0)),
                  pl.BlockSpec((d, tn), lambda j, i: (0, j + j0))] + in_slabs,
        out_specs=[pl.BlockSpec((tm, tn), lambda j, i: (i, j))] + out_slabs,
        out_shape=[jax.ShapeDtypeStruct((m, n), out_dtype)] + out_shapes,
        scratch_shapes=[pltpu.VMEM((d, tn), BF16)],
        compiler_params=_params(("arbitrary", "arbitrary")),
        name=name,
    )(h, w, *[a for a, _ in to_cast])
    return outs[0], outs[1:]


def _hgrn_constants():
    u = np.arange(CHUNK)
    mats, masks = [], []
    for rev in (False, True):
        pos = CHUNK - 1 - u if rev else u
        incl = (pos[None, :] <= pos[:, None]).astype(np.float32)
        blocks, lvl_masks = [incl], []
        for lev in range(N_LEVELS):
            half = CHUNK >> (lev + 1)
            ref = (pos // (2 * half)) * (2 * half) + half - 1
            upto_ref = (pos[None, :] <= ref[:, None]).astype(np.float32)
            if half in MXU_LEVEL_HALVES:
                late = (pos % (2 * half)) >= half
                blocks.append(np.where(late[:, None], 1.0, -1.0).astype(np.float32) * (incl - upto_ref))
            same = (pos[:, None] // (2 * half)) == (pos[None, :] // (2 * half))
            late_q = (pos[:, None] % (2 * half)) >= half
            early_k = (pos[None, :] % (2 * half)) < half
            lvl_masks.append((same & late_q & early_k).astype(np.float32))
        lvl_masks.append(np.eye(CHUNK, dtype=np.float32))
        mats.append(np.concatenate(blocks, axis=0))
        masks.append(np.stack(lvl_masks))
    return np.stack(mats), np.stack(masks)


def _hgrn_kernel(lbl_ref, q_ref, ff_ref, fb_ref, i_ref, gt_ref, cm_ref, mk_ref, ng_ref, y_ref, *scratch, **chunks):
    for hh in range(HGRN_HEADS_PER_STEP):
        lanes = pl.ds(hh * HEAD_DIM, HEAD_DIM)
        head = [r.at[:, lanes] for r in (q_ref, ff_ref, fb_ref, i_ref, gt_ref)]
        _hgrn_head(lbl_ref.at[:, :, hh], *head, cm_ref, mk_ref, ng_ref, y_ref.at[:, lanes], *scratch, **chunks)


def _hgrn_head(lbl_ref, q_ref, ff_ref, fb_ref, i_ref, gt_ref, cm_ref, mk_ref, ng_ref, y_ref,
               of_ref, ob_ref, sf_ref, sb_ref, kd_ref, et_ref, qd_ref, sc_ref, *, n_ctx_chunks, n_lat_chunks):
    f_refs, o_refs, st_refs = (ff_ref, fb_ref), (of_ref, ob_ref), (sf_ref, sb_ref)
    lowers = []
    for dirn in range(2):
        la, lb_ = lbl_ref[dirn, 0], lbl_ref[dirn, 1]
        mx = jnp.maximum(la, lb_)
        ea, eb = jnp.exp(la - mx), jnp.exp(lb_ - mx)
        lowers.append(ea / (ea + eb))
        st_refs[dirn][...] = jnp.zeros(st_refs[dirn].shape, F32)
    row_odd = (lax.broadcasted_iota(jnp.int32, (CHUNK, 1), 0) % 2) == 1

    def chunk_rows(pos):
        return pl.ds(pl.multiple_of(pos * CHUNK, CHUNK), CHUNK)

    def level_decay(dirn, lev, f, sums):
        half = CHUNK >> (lev + 1)
        cum = sums[0:CHUNK]
        if half == 1:
            return jnp.where(row_odd, 1.0, f) if dirn else jnp.where(row_odd, f, 1.0)
        if half in MXU_LEVEL_HALVES:
            blk = 1 + MXU_LEVEL_HALVES.index(half)
            return jnp.exp2(sums[blk * CHUNK:(blk + 1) * CHUNK])
        parts = []
        for p in range(0, CHUNK, 2 * half):
            ref = cum[p + half - 1 + dirn:p + half + dirn]
            lo, hi = cum[p:p + half], cum[p + half:p + 2 * half]
            parts += [lo - ref, ref - hi] if dirn else [ref - lo, hi - ref]
        return jnp.exp2(jnp.concatenate(parts, axis=0))

    def prepare(pos0, n, with_out):
        items = [(dirn, pos0 + c) for c in range(n) for dirn in range(2)]
        fs, kks, splits = [], [], []
        for dirn, pos in items:
            lower = lowers[dirn]
            f = lower + (1.0 - lower) * _sigmoid(f_refs[dirn][chunk_rows(pos), :])
            g = jnp.log(f) * LOG2E
            g1 = g.astype(BF16)
            r1 = g - g1.astype(F32)
            g2 = r1.astype(BF16)
            g3 = (r1 - g2.astype(F32)).astype(BF16)
            fs.append(f)
            kks.append(1.0 - f)
            splits.append(jnp.concatenate([g1, g2, g3], axis=0))
        sums = [None] * len(items)
        for dirn in range(2):
            idx = [j for j, it in enumerate(items) if it[0] == dirn]
            wide = _dot(cm_ref[dirn], jnp.concatenate([splits[j] for j in idx], axis=1))
            for c, j in enumerate(idx):
                sums[j] = wide[:, c * HEAD_DIM:(c + 1) * HEAD_DIM]
        for j, (dirn, pos) in enumerate(items):
            cum = sums[j][0:CHUNK]
            last = 0 if dirn else CHUNK - 1
            tot = cum[last:last + 1]
            kd_ref[dirn, chunk_rows(pos), :] = (kks[j] * jnp.exp2(tot - cum)).astype(BF16)
            et_ref[dirn, pos] = jnp.broadcast_to(jnp.exp2(tot), et_ref.shape[2:])
        if not with_out:
            return
        qs = [q_ref[chunk_rows(pos), :] for _, pos in items]
        for j, (dirn, pos) in enumerate(items):
            qd_ref[dirn, chunk_rows(pos), :] = (qs[j] * jnp.exp2(sums[j][0:CHUNK])).astype(BF16)
        qbs = [qv.astype(BF16) for qv in qs]
        kbs = [kv.astype(BF16) for kv in kks]
        scs = [mk_ref[dirn, N_LEVELS] * _dot_nt(qbs[j], kbs[j]) for j, (dirn, _) in enumerate(items)]
        for lev in range(N_LEVELS):
            for j, (dirn, _) in enumerate(items):
                dl = level_decay(dirn, lev, fs[j], sums[j]).astype(BF16)
                scs[j] = scs[j] + mk_ref[dirn, lev] * _dot_nt(qbs[j] * dl, kbs[j] * dl)
        for j, (dirn, pos) in enumerate(items):
            sc_ref[dirn, chunk_rows(pos), :] = scs[j].astype(BF16)

    def scan(pos_f, pos_b, n, with_out):
        items = [(dirn, (pos_b - c) if dirn else (pos_f + c)) for c in range(n) for dirn in range(2)]
        vs = [i_ref[chunk_rows(pos), :] for _, pos in items]
        ups = [_dot(vs[j].T.astype(BF16), kd_ref[dirn, chunk_rows(pos), :]) for j, (dirn, pos) in enumerate(items)]
        sts = [st_refs[0][...], st_refs[1][...]]
        before = []
        for j, (dirn, pos) in enumerate(items):
            before.append(sts[dirn])
            sts[dirn] = sts[dirn] * et_ref[dirn, pos][0:1] + ups[j]
        for dirn in range(2):
            st_refs[dirn][...] = sts[dirn]
        if not with_out:
            return
        for j, (dirn, pos) in enumerate(items):
            orow = chunk_rows(pos)
            o_refs[dirn][orow, :] = (_dot_nt(qd_ref[dirn, orow, :], before[j].astype(BF16))
                                     + _dot(sc_ref[dirn, orow, :], vs[j].astype(BF16)))

    def loop(n, body):
        lax.fori_loop(0, n, lambda ci, c: (body(ci), c)[1], 0)

    ctx0 = n_lat_chunks
    prepare(ctx0, n_ctx_chunks, False)
    loop(n_lat_chunks // PREP_CHUNKS, lambda ci: prepare(ci * PREP_CHUNKS, PREP_CHUNKS, True))
    scan(ctx0, ctx0 + n_ctx_chunks - 1, n_ctx_chunks, False)
    loop(n_lat_chunks // SCAN_CHUNKS,
         lambda ci: scan(ci * SCAN_CHUNKS, n_lat_chunks - 1 - ci * SCAN_CHUNKS, SCAN_CHUNKS, True))

    def readout(i, carry):
        rows = pl.ds(pl.multiple_of(i * TOK, TOK), TOK)
        ot = of_ref[rows, :] + ob_ref[rows, :]
        on = ot * lax.rsqrt(jnp.mean(ot * ot, axis=-1, keepdims=True) + EPS) * ng_ref[...]
        gate = gt_ref[rows, :]
        y_ref[rows, :] = (on * (gate * _sigmoid(gate))).astype(BF16)
        return carry

    lax.fori_loop(0, n_lat_chunks * CHUNK // TOK, readout, 0, unroll=2)


def _hgrn_call(p3, lb_logits, norm_g, n_ctx, l):
    b, t, _ = p3.shape
    cm, mk = _hgrn_constants()
    cm = jnp.asarray(np.concatenate([cm] * 3, axis=-1), BF16)
    mk = jnp.asarray(mk, F32)
    lbl = lb_logits.reshape(2, 2, HEADS, 1, HEAD_DIM)
    hps = HGRN_HEADS_PER_STEP
    steps = HEADS // hps
    tok_spec = lambda grp: pl.BlockSpec((None, t, hps * HEAD_DIM), lambda i, h: (i, 0, grp * steps + h))
    full = lambda shape: pl.BlockSpec(shape, lambda i, h: (0,) * len(shape))
    return pl.pallas_call(
        functools.partial(_hgrn_kernel, n_ctx_chunks=n_ctx // CHUNK, n_lat_chunks=l // CHUNK),
        grid=(b, steps),
        in_specs=[pl.BlockSpec((2, 2, hps, 1, HEAD_DIM), lambda i, h: (0, 0, h, 0, 0)),
                  tok_spec(0), tok_spec(1), tok_spec(2), tok_spec(3), tok_spec(4),
                  full((2, 3 * CHUNK, 3 * CHUNK)),
                  full((2, N_LEVELS + 1, CHUNK, CHUNK)),
                  full((1, HEAD_DIM))],
        out_specs=pl.BlockSpec((None, l, hps * HEAD_DIM), lambda i, h: (i, 0, h)),
        out_shape=jax.ShapeDtypeStruct((b, l, MIX_W), BF16),
        scratch_shapes=[pltpu.VMEM((l, HEAD_DIM), F32), pltpu.VMEM((l, HEAD_DIM), F32),
                        pltpu.VMEM((HEAD_DIM, HEAD_DIM), F32), pltpu.VMEM((HEAD_DIM, HEAD_DIM), F32),
                        pltpu.VMEM((2, t, HEAD_DIM), BF16), pltpu.VMEM((2, t // CHUNK, 8, HEAD_DIM), F32),
                        pltpu.VMEM((2, l, HEAD_DIM), BF16), pltpu.VMEM((2, l, CHUNK), BF16)],
        compiler_params=_params(("parallel", "parallel")),
        name="hgrn",
    )(lbl, p3, p3, p3, p3, p3, cm, mk, norm_g)


def _rope_tables(l):
    pos = np.arange(l)
    nf = HEAD_DIM // 4
    inv = ROPE_THETA ** (-np.arange(nf, dtype=np.float64) / nf)
    ang_r = (pos // GRID_W)[:, None] * inv[None, :]
    ang_c = (pos % GRID_W)[:, None] * inv[None, :]
    cos = np.concatenate([np.cos(ang_r)] * 2 + [np.cos(ang_c)] * 2, axis=-1)
    sin = np.concatenate([-np.sin(ang_r), np.sin(ang_r), -np.sin(ang_c), np.sin(ang_c)], axis=-1)
    return jnp.asarray(cos, F32), jnp.asarray(sin, F32)


def _na_build_bias(rb_ref, tb_ref):
    lanes = 2 * GRID_W
    cq = lax.broadcasted_iota(jnp.int32, (GRID_W, lanes), 0)
    lane = lax.broadcasted_iota(jnp.int32, (GRID_W, lanes), 1)
    ck = lane % GRID_W
    cs = jnp.clip(cq - WIN_C // 2, 0, GRID_W - WIN_C)
    col_in = (ck >= cs) & (ck < cs + WIN_C)

    def toeplitz(d, lane0):
        row = jnp.broadcast_to(rb_ref[d:d + 1, :] * LOG2E, (GRID_W, lanes))
        return pltpu.roll(row, (lane0 - (WIN_C - 1)) % lanes, 1, stride=1, stride_axis=0)

    pair = [jnp.where(col_in, jnp.where(lane < GRID_W, toeplitz(d, 0), toeplitz(d + 1, GRID_W)), MASKED)
            for d in range(2 * WIN_R - 2)]
    for e in range(WIN_R):
        for p in range(WIN_R // 2):
            tb_ref[e, :, p * lanes:(p + 1) * lanes] = pair[2 * p - e + WIN_R - 1]


def _na_kernel(q_ref, k_ref, v_ref, cos_ref, sin_ref, rb_ref, gq_ref, gk_ref, perm_ref, o_ref,
               qs, ks, vs, kcs, vcs, tb_ref, **static):
    for hh in range(NA_HEADS_PER_STEP):
        lanes = pl.ds(hh * HEAD_DIM, HEAD_DIM)
        head = [r.at[:, lanes] for r in (q_ref, k_ref, v_ref)]
        _na_head(*head, cos_ref, sin_ref, rb_ref.at[hh], gq_ref, gk_ref, perm_ref, o_ref.at[:, lanes],
                 qs, ks, vs, kcs, vcs, tb_ref.at[hh], **static)


def _na_head(q_ref, k_ref, v_ref, cos_ref, sin_ref, rb_ref, gq_ref, gk_ref, perm_ref, o_ref,
             qs, ks, vs, kcs, vcs, tb_ref, *, n_ctx, grid_rows):
    @pl.when(pl.program_id(1) == 0)
    def _():
        _na_build_bias(rb_ref, tb_ref)

    ones = jnp.ones((2 * HEAD_DIM, HEAD_DIM), BF16)

    def normed(tv, g):
        sq = tv * tv
        hi = sq.astype(BF16)
        lo = (sq - hi.astype(F32)).astype(BF16)
        ssq = _dot(jnp.concatenate([hi, lo], axis=1), ones)
        return tv * lax.rsqrt(ssq * (1.0 / HEAD_DIM) + EPS) * g


    gq = gq_ref[...]
    gk = gk_ref[...]
    n_lat = grid_rows * GRID_W
    kcs[...] = normed(k_ref[n_lat:n_lat + n_ctx, :], gk).astype(BF16)
    vcs[...] = v_ref[n_lat:n_lat + n_ctx, :].astype(BF16)

    def prep(i, carry):
        rows = [pl.ds(pl.multiple_of((2 * i + n) * TOK, TOK), TOK) for n in range(2)]
        srcs = rows
        jobs = [(src_ref, g, dst, n) for n in range(2) for src_ref, g, dst in ((q_ref, gq, qs), (k_ref, gk, ks))]
        nrm = [normed(src_ref[srcs[n], :], g) for src_ref, g, _, n in jobs]
        par = [_dot(tv.astype(BF16), perm_ref[...]) for tv in nrm]
        for (_, _, dst, n), tv, pv in zip(jobs, nrm, par):
            dst[rows[n], :] = (tv * cos_ref[rows[n], :] + pv * sin_ref[rows[n], :]).astype(BF16)
        for n in range(2):
            vs[rows[n], :] = v_ref[srcs[n], :].astype(BF16)
        return carry

    lax.fori_loop(0, grid_rows * GRID_W // (2 * TOK), prep, 0, unroll=2)
    scale = HEAD_DIM ** -0.5 * LOG2E
    band = WIN_R * GRID_W

    def rows_step(i, carry):
        rr = [i * NA_ROWS + n for n in range(NA_ROWS)]
        rs = [jnp.clip(r - WIN_R // 2, 0, grid_rows - WIN_R) for r in rr]
        qrow = [pl.ds(pl.multiple_of(r * GRID_W, GRID_W), GRID_W) for r in rr]
        krow = [pl.ds(pl.multiple_of(s * GRID_W, GRID_W), band) for s in rs]
        qr = [qs[qw, :] for qw in qrow]
        kc = kcs[...]
        sb = [_dot_nt(qr[n], ks[krow[n], :]) * scale + tb_ref[rr[n] - rs[n]] for n in range(NA_ROWS)]
        sc = [_dot_nt(qr[n], kc) * scale for n in range(NA_ROWS)]
        m = [jnp.maximum(jnp.max(sb[n], axis=-1, keepdims=True), jnp.max(sc[n], axis=-1, keepdims=True))
             for n in range(NA_ROWS)]
        pb = [jnp.exp2(sb[n] - m[n]) for n in range(NA_ROWS)]
        pc = [jnp.exp2(sc[n] - m[n]) for n in range(NA_ROWS)]
        den = [jnp.sum(pb[n], axis=-1, keepdims=True) + jnp.sum(pc[n], axis=-1, keepdims=True)
               for n in range(NA_ROWS)]
        vc = vcs[...]
        o = [_dot(pb[n].astype(BF16), vs[krow[n], :]) + _dot(pc[n].astype(BF16), vc) for n in range(NA_ROWS)]
        for n in range(NA_ROWS):
            o_ref[qrow[n], :] = (o[n] / den[n]).astype(BF16)
        return carry

    lax.fori_loop(0, grid_rows // NA_ROWS, rows_step, 0)


def _na_call(p3, rel_bias, gq, gk, n_ctx, l):
    b, t, _ = p3.shape
    cos, sin = _rope_tables(l)
    nr, nc = rel_bias.shape[1:]
    rb = jnp.zeros((HEADS, 2 * WIN_R, 2 * GRID_W), F32).at[:, :nr, :nc].set(rel_bias)
    lanes = np.arange(HEAD_DIM)
    partner = np.where(lanes % (HEAD_DIM // 2) < HEAD_DIM // 4, lanes + HEAD_DIM // 4, lanes - HEAD_DIM // 4)
    perm = jnp.asarray(lanes[:, None] == partner[None, :], BF16)
    hps = NA_HEADS_PER_STEP
    steps = HEADS // hps
    col = lambda grp: (lambda h, i: (i, 0, grp * steps + h))
    tok_spec = lambda im: pl.BlockSpec((None, t, hps * HEAD_DIM), im)
    full = lambda shape: pl.BlockSpec(shape, lambda h, i: (0,) * len(shape))
    return pl.pallas_call(
        functools.partial(_na_kernel, n_ctx=n_ctx, grid_rows=l // GRID_W),
        grid=(steps, b),
        in_specs=[tok_spec(col(5)), tok_spec(col(6)), tok_spec(col(7)),
                  full((l, HEAD_DIM)), full((l, HEAD_DIM)),
                  pl.BlockSpec((hps, 2 * WIN_R, 2 * GRID_W), lambda h, i: (h, 0, 0)),
                  full((1, HEAD_DIM)), full((1, HEAD_DIM)), full((HEAD_DIM, HEAD_DIM))],
        out_specs=pl.BlockSpec((None, l, hps * HEAD_DIM), lambda h, i: (i, 0, h)),
        out_shape=jax.ShapeDtypeStruct((b, l, MIX_W), BF16),
        scratch_shapes=[pltpu.VMEM((l, HEAD_DIM), BF16), pltpu.VMEM((l, HEAD_DIM), BF16),
                        pltpu.VMEM((l, HEAD_DIM), BF16), pltpu.VMEM((n_ctx, HEAD_DIM), BF16),
                        pltpu.VMEM((n_ctx, HEAD_DIM), BF16),
                        pltpu.VMEM((hps, WIN_R, GRID_W, WIN_R * GRID_W), F32)],
        compiler_params=_params(("parallel", "arbitrary")),
        name="na",
    )(p3, p3, p3, cos, sin, rb, gq, gk, perm)


def _merge_kernel(ya_ref, yb_ref, ga_ref, gb_ref, x_ref, wa_ref, wb_ref, wo_ref, mod_ref, n2_ref,
                  xm_ref, h2_ref):
    tm = x_ref.shape[0]
    halves = [pl.ds(n * (tm // 2), tm // 2) for n in range(2)]
    za = [_dot(ya_ref[r, :], wa_ref[...]) for r in halves]
    zb = [_dot(yb_ref[r, :], wb_ref[...]) for r in halves]
    z = [(_sigmoid(ga_ref[r, :].astype(F32)) * za[n] + _sigmoid(gb_ref[r, :].astype(F32)) * zb[n]).astype(BF16)
         for n, r in enumerate(halves)]
    br = [_dot(zn, wo_ref[...]) for zn in z]
    for n, r in enumerate(halves):
        xm = x_ref[r, :] + mod_ref[0:1, :] * br[n]
        xm_ref[r, :] = xm
        y = xm * lax.rsqrt(jnp.mean(xm * xm, axis=-1, keepdims=True) + EPS) * n2_ref[...]
        h2_ref[r, :] = (y * (1.0 + mod_ref[2:3, :]) + mod_ref[1:2, :]).astype(BF16)


def _merge_call(ya, yb, pg3, x, wa, wb, wo, mod3, n2):
    b, l, d = x.shape
    tm = 2 * TOK
    const = lambda shape: pl.BlockSpec(shape, lambda i, t: (0,) * len(shape), pipeline_mode=pl.Buffered(1))
    return pl.pallas_call(
        _merge_kernel,
        grid=(b, l // tm),
        in_specs=[pl.BlockSpec((None, tm, MIX_W), lambda i, t: (i, t, 0)),
                  pl.BlockSpec((None, tm, MIX_W), lambda i, t: (i, t, 0)),
                  pl.BlockSpec((None, tm, d), lambda i, t: (i, t, 0)),
                  pl.BlockSpec((None, tm, d), lambda i, t: (i, t, 1)),
                  pl.BlockSpec((None, tm, d), lambda i, t: (i, t, 0)),
                  const((MIX_W, d)), const((MIX_W, d)), const((d, d)),
                  pl.BlockSpec((None, 3, d), lambda i, t: (i, 0, 0)),
                  pl.BlockSpec((1, d), lambda i, t: (0, 0))],
        out_specs=[pl.BlockSpec((None, tm, d), lambda i, t: (i, t, 0)),
                   pl.BlockSpec((None, tm, d), lambda i, t: (i, t, 0))],
        out_shape=[jax.ShapeDtypeStruct((b, l, d), F32), jax.ShapeDtypeStruct((b, l, d), BF16)],
        compiler_params=_params(("parallel", "arbitrary")),
        name="merge",
    )(ya, yb, pg3, pg3, x, wa, wb, wo, mod3, n2)


def _ffn_kernel(h_ref, hp_ref, hn_ref, w1_ref, w3_ref, cw_ref, cb_ref, w2_ref, xm_ref, g2_ref, o_ref, a_ref,
                hc_ref, *, tiles_per_seq, halo):
    i = pl.program_id(0)
    s = pl.program_id(1)
    n_hid, tm, th = a_ref.shape

    @pl.when(s == 0)
    def _():
        hc_ref[0:tm, :] = h_ref[...]
        hc_ref[tm:tm + halo, :] = hp_ref[...]
        hc_ref[tm + halo:tm + 2 * halo, :] = hn_ref[...]

    @pl.when(s < n_hid)
    def _():
        t1_all = _dot(hc_ref[...], w1_ref[...])
        t1 = t1_all[0:tm]
        prev_row = t1_all[tm + halo - 1:tm + halo, :]
        next_row = t1_all[tm + halo:tm + halo + 1, :]
        prev_row = jnp.where(i % tiles_per_seq == 0, 0.0, prev_row)
        next_row = jnp.where(i % tiles_per_seq == tiles_per_seq - 1, 0.0, next_row)
        ridx = lax.broadcasted_iota(jnp.int32, (tm, 1), 0)
        up = jnp.where(ridx == 0, prev_row, pltpu.roll(t1, 1, 0))
        dn = jnp.where(ridx == tm - 1, next_row, pltpu.roll(t1, tm - 1, 0))
        u = up * cw_ref[0:1, :] + t1 * cw_ref[1:2, :] + dn * cw_ref[2:3, :] + cb_ref[...]
        a_ref[s] = ((u * _sigmoid(u)) * _dot(hc_ref[0:tm, :], w3_ref[...])).astype(BF16)

    @pl.when(s >= n_hid)
    def _():
        acc = _dot(a_ref[0], w2_ref[0:th, :])
        for k in range(1, n_hid):
            acc = acc + _dot(a_ref[k], w2_ref[k * th:(k + 1) * th, :])
        o_ref[...] = xm_ref[...] + g2_ref[...] * acc


FFN_TOKEN_TILE = 1024
FFN_OUT_TILE = 512


def _ffn_hidden_tile(hid):
    return min(512, hid)


def _ffn_call(h2, xm, w1, w3, cw, cb, w2, g2, l):
    m, d = h2.shape
    n_hid, _, th = w1.shape
    hid = n_hid * th
    tm = FFN_TOKEN_TILE
    tn = w2.shape[-1]
    halo = 16
    per = tm // halo
    nblk = m // halo
    hcol = lambda i, s: (0, jnp.minimum(s, n_hid - 1))
    htile = lambda i, s: (jnp.minimum(s, n_hid - 1), 0, 0)
    ocol = lambda s: jnp.maximum(s - n_hid, 0)
    return pl.pallas_call(
        functools.partial(_ffn_kernel, tiles_per_seq=l // tm, halo=halo),
        grid=(m // tm, n_hid + d // tn),
        in_specs=[pl.BlockSpec((tm, d), lambda i, s: (i, 0)),
                  pl.BlockSpec((halo, d), lambda i, s: (jnp.maximum(i * per - 1, 0), 0)),
                  pl.BlockSpec((halo, d), lambda i, s: (jnp.minimum((i + 1) * per, nblk - 1), 0)),
                  pl.BlockSpec((None, d, th), htile),
                  pl.BlockSpec((None, d, th), htile),
                  pl.BlockSpec((3, th), hcol),
                  pl.BlockSpec((1, th), hcol),
                  pl.BlockSpec((None, hid, tn), lambda i, s: (ocol(s), 0, 0)),
                  pl.BlockSpec((tm, tn), lambda i, s: (i, ocol(s))),
                  pl.BlockSpec((None, 1, tn), lambda i, s: (i // (l // tm), 0, ocol(s)))],
        out_specs=pl.BlockSpec((tm, tn), lambda i, s: (i, ocol(s))),
        out_shape=jax.ShapeDtypeStruct((m, d), F32),
        scratch_shapes=[pltpu.VMEM((n_hid, tm, th), BF16), pltpu.VMEM((tm + 2 * halo, d), BF16)],
        compiler_params=_params(("parallel", "arbitrary")),
        name="ffn",
    )(h2, h2, h2, w1, w3, cw, cb, w2, xm, g2)


def kernel(x, c, ctx, c_ctx, ada_w, ada_b, norm1_g, norm2_g, w_in, hgrn_lb_logits, hgrn_norm_g,
           na_q_norm_g, na_k_norm_g, na_rel_bias, w_branch_a, w_branch_b, w_out,
           ffn_w1, ffn_w3, ffn_conv_w, ffn_conv_b, ffn_w2):
    b, l, d = x.shape
    n_ctx = ctx.shape[1]
    assert ada_w.shape[0] == 1 and b + 1 <= 8 and n_ctx % TOK == 0 and l % (2 * TOK) == 0

    c_all = jnp.zeros((8, d), F32).at[:b].set(c).at[b].set(c_ctx)
    mod = _mod_call(c_all, ada_w[0], ada_b).reshape(8, N_MOD, d)
    sh1, sc1, g1, sh2, sc2, g2 = (mod[:b, n] for n in range(N_MOD))
    ss_ctx = jnp.broadcast_to(mod[b, 0:2][None], (b, 2, d))
    ss1 = jnp.stack([jnp.stack([sh1, sc1], axis=1), ss_ctx], axis=1)

    h = _prenorm_call(x, ctx, norm1_g, ss1)
    t = l + n_ctx
    h = h.reshape(b * t, d)
    n_mix = 8 * MIX_W
    th = _ffn_hidden_tile(ffn_w1.shape[-1])
    p, (w1, w3, w2) = _inproj_call(h, w_in[0], 0, n_mix, F32, "inproj",
                                   ((ffn_w1[0], th), (ffn_w3[0], th), (ffn_w2[0], FFN_OUT_TILE)))
    pg, (wa, wb, wo) = _inproj_call(h, w_in[0], n_mix, 2 * d, BF16, "inproj_gates",
                                    ((w_branch_a[0], d), (w_branch_b[0], d), (w_out[0], d)))
    wa, wb, wo = (wt.reshape(wt.shape[1:]) for wt in (wa, wb, wo))
    p3 = p.reshape(b, t, n_mix)
    pg3 = pg.reshape(b, t, 2 * d)

    y_a = _hgrn_call(p3, hgrn_lb_logits, hgrn_norm_g, n_ctx, l)
    y_b = _na_call(p3, na_rel_bias[0], na_q_norm_g, na_k_norm_g, n_ctx, l)

    mod3 = jnp.stack([g1, sh2, sc2], axis=1)
    x_mid, h2 = _merge_call(y_a, y_b, pg3, x, wa, wb, wo, mod3, norm2_g)

    out = _ffn_call(h2.reshape(b * l, d), x_mid.reshape(b * l, d), w1, w3, ffn_conv_w[0], ffn_conv_b, w2,
                    g2[:, None, :], l)
    return out.reshape(b, l, d)
```

```python
import functools

import numpy as np
import jax
import jax.numpy as jnp
from jax import lax
from jax.experimental import pallas as pl
from jax.experimental.pallas import tpu as pltpu

GRID_W = 64
HEADS = 8
HEAD_DIM = 128
MIX_W = HEADS * HEAD_DIM
CHUNK = 64
N_LEVELS = 6
MXU_LEVEL_HALVES = (4, 2)
PREP_CHUNKS = 16
SCAN_CHUNKS = 16
HGRN_HEADS_PER_STEP = 2
NA_HEADS_PER_STEP = 2
NA_ROWS = 16
WIN_R = 8
WIN_C = 16
ROPE_THETA = 10000.0
N_MOD = 6
EPS = 1e-6
LOG2E = 1.4426950408889634
TOK = 256
MOD_COL_TILE = 1024
INPROJ_ROW_TILE = 1024
INPROJ_COL_TILE = 1024
MERGE_ROW_TILE = 2 * TOK
FFN_TOKEN_TILE = 1024
FFN_HIDDEN_TILE = 512
FFN_OUT_TILE = 512
MASKED = -1e30
V7X_VMEM_LIMIT = 60 * 1024 * 1024

BF16 = jnp.bfloat16
F32 = jnp.float32


def _dot(a, b):
    return jnp.dot(a, b, preferred_element_type=F32)


def _dot_nt(a, b):
    return lax.dot_general(a, b, (((1,), (1,)), ((), ())), preferred_element_type=F32)


def _sigmoid(t):
    return 1.0 / (1.0 + jnp.exp(-t))


def _params(sem):
    return pltpu.CompilerParams(dimension_semantics=sem, vmem_limit_bytes=V7X_VMEM_LIMIT)


def _mod_kernel(c_ref, w_ref, b_ref, o_ref):
    cv = c_ref[...]
    s = (cv * _sigmoid(cv)).astype(BF16)
    o_ref[...] = _dot(s, w_ref[...].astype(BF16)) + b_ref[...]


def _mod_call(c_all, w, b):
    d, n = w.shape
    tn = min(MOD_COL_TILE, d)
    return pl.pallas_call(
        _mod_kernel,
        grid=(n // tn,),
        in_specs=[pl.BlockSpec((8, d), lambda j: (0, 0)),
                  pl.BlockSpec((d, tn), lambda j: (0, j)),
                  pl.BlockSpec((1, tn), lambda j: (0, j))],
        out_specs=pl.BlockSpec((8, tn), lambda j: (0, j)),
        out_shape=jax.ShapeDtypeStruct((8, n), F32),
        compiler_params=_params(("arbitrary",)),
        name="mod",
    )(c_all, w, b)


def _prenorm_kernel(x_ref, ctx_ref, g_ref, ss_ref, o_ref, *, n_lat_blk):
    t = pl.program_id(1)
    xv = jnp.where(t < n_lat_blk, x_ref[...], ctx_ref[...])
    y = xv * lax.rsqrt(jnp.mean(xv * xv, axis=-1, keepdims=True) + EPS) * g_ref[...]
    o_ref[...] = (y * (1.0 + ss_ref[1:2, :]) + ss_ref[0:1, :]).astype(BF16)


def _prenorm_call(x, ctx, g, ss):
    b, l, d = x.shape
    nlb = l // TOK
    nt = nlb + ctx.shape[1] // TOK
    return pl.pallas_call(
        functools.partial(_prenorm_kernel, n_lat_blk=nlb),
        grid=(b, nt),
        in_specs=[pl.BlockSpec((None, TOK, d), lambda i, t: (i, jnp.minimum(t, nlb - 1), 0)),
                  pl.BlockSpec((None, TOK, d), lambda i, t: (i, jnp.maximum(t - nlb, 0), 0)),
                  pl.BlockSpec((1, d), lambda i, t: (0, 0)),
                  pl.BlockSpec((None, None, 2, d), lambda i, t: (i, (t >= nlb).astype(jnp.int32), 0, 0))],
        out_specs=pl.BlockSpec((None, TOK, d), lambda i, t: (i, t, 0)),
        out_shape=jax.ShapeDtypeStruct((b, nt * TOK, d), BF16),
        compiler_params=_params(("parallel", "arbitrary")),
        name="prenorm",
    )(x, ctx, g, ss)


def _inproj_kernel(*refs, n_cast):
    h_ref, w_ref = refs[:2]
    cast_in = refs[2:2 + n_cast]
    o_ref = refs[2 + n_cast]
    cast_out = refs[3 + n_cast:3 + 2 * n_cast]
    wb_ref = refs[-1]

    @pl.when(pl.program_id(1) == 0)
    def _():
        wb_ref[...] = w_ref[...].astype(BF16)

    o_ref[...] = _dot(h_ref[...], wb_ref[...]).astype(o_ref.dtype)
    for src, dst in zip(cast_in, cast_out):
        ct = dst.shape[-1]
        for c in range(dst.shape[0]):
            dst[c] = src[:, c * ct:(c + 1) * ct].astype(BF16)


def _slab_rows(rows, steps):
    return next(r for r in range(32, rows + 1, 32) if rows % r == 0 and rows // r <= steps)


def _inproj_call(h, w, col0, n, out_dtype, name, to_cast=()):
    m, d = h.shape
    tm = INPROJ_ROW_TILE if m % INPROJ_ROW_TILE == 0 else TOK
    tn = min(INPROJ_COL_TILE, d)
    j0 = col0 // tn
    ni = m // tm
    steps = (n // tn) * ni
    in_slabs, out_slabs, out_shapes = [], [], []
    for a, ct in to_cast:
        rows, cols = a.shape
        r = _slab_rows(rows, steps)
        slab = lambda j, i, last=rows // r - 1: jnp.minimum(j * ni + i, last)
        in_slabs.append(pl.BlockSpec((r, cols), lambda j, i, slab=slab: (slab(j, i), 0)))
        out_slabs.append(pl.BlockSpec((cols // ct, r, ct), lambda j, i, slab=slab: (0, slab(j, i), 0)))
        out_shapes.append(jax.ShapeDtypeStruct((cols // ct, rows, ct), BF16))
    outs = pl.pallas_call(
        functools.partial(_inproj_kernel, n_cast=len(to_cast)),
        grid=(n // tn, ni),
        in_specs=[pl.BlockSpec((tm, d), lambda j, i: (i, 0)),
                  pl.BlockSpec((d, tn), lambda j, i: (0, j + j0))] + in_slabs,
        out_specs=[pl.BlockSpec((tm, tn), lambda j, i: (i, j))] + out_slabs,
        out_shape=[jax.ShapeDtypeStruct((m, n), out_dtype)] + out_shapes,
        scratch_shapes=[pltpu.VMEM((d, tn), BF16)],
        compiler_params=_params(("arbitrary", "arbitrary")),
        name=name,
    )(h, w, *[a for a, _ in to_cast])
    return outs[0], outs[1:]


def _hgrn_constants():
    u = np.arange(CHUNK)
    mats, masks = [], []
    for rev in (False, True):
        pos = CHUNK - 1 - u if rev else u
        incl = (pos[None, :] <= pos[:, None]).astype(np.float32)
        blocks, lvl_masks = [incl], []
        for lev in range(N_LEVELS):
            half = CHUNK >> (lev + 1)
            ref = (pos // (2 * half)) * (2 * half) + half - 1
            upto_ref = (pos[None, :] <= ref[:, None]).astype(np.float32)
            if half in MXU_LEVEL_HALVES:
                late = (pos % (2 * half)) >= half
                blocks.append(np.where(late[:, None], 1.0, -1.0).astype(np.float32) * (incl - upto_ref))
            same = (pos[:, None] // (2 * half)) == (pos[None, :] // (2 * half))
            late_q = (pos[:, None] % (2 * half)) >= half
            early_k = (pos[None, :] % (2 * half)) < half
            lvl_masks.append((same & late_q & early_k).astype(np.float32))
        lvl_masks.append(np.eye(CHUNK, dtype=np.float32))
        mats.append(np.concatenate(blocks, axis=0))
        masks.append(np.stack(lvl_masks))
    return np.stack(mats), np.stack(masks)


def _hgrn_kernel(lbl_ref, q_ref, ff_ref, fb_ref, i_ref, gt_ref, cm_ref, mk_ref, ng_ref, y_ref, *scratch, **chunks):
    for hh in range(HGRN_HEADS_PER_STEP):
        lanes = pl.ds(hh * HEAD_DIM, HEAD_DIM)
        head = [r.at[:, lanes] for r in (q_ref, ff_ref, fb_ref, i_ref, gt_ref)]
        _hgrn_head(lbl_ref.at[:, :, hh], *head, cm_ref, mk_ref, ng_ref, y_ref.at[:, lanes], *scratch, **chunks)


def _hgrn_head(lbl_ref, q_ref, ff_ref, fb_ref, i_ref, gt_ref, cm_ref, mk_ref, ng_ref, y_ref,
               of_ref, ob_ref, sf_ref, sb_ref, kd_ref, et_ref, qd_ref, sc_ref, *, n_ctx_chunks, n_lat_chunks):
    f_refs, o_refs, st_refs = (ff_ref, fb_ref), (of_ref, ob_ref), (sf_ref, sb_ref)
    lowers = []
    for dirn in range(2):
        la, lb_ = lbl_ref[dirn, 0], lbl_ref[dirn, 1]
        mx = jnp.maximum(la, lb_)
        ea, eb = jnp.exp(la - mx), jnp.exp(lb_ - mx)
        lowers.append(ea / (ea + eb))
        st_refs[dirn][...] = jnp.zeros(st_refs[dirn].shape, F32)
    row_odd = (lax.broadcasted_iota(jnp.int32, (CHUNK, 1), 0) % 2) == 1

    def chunk_rows(pos):
        return pl.ds(pl.multiple_of(pos * CHUNK, CHUNK), CHUNK)

    def level_decay(dirn, lev, f, sums):
        half = CHUNK >> (lev + 1)
        cum = sums[0:CHUNK]
        if half == 1:
            return jnp.where(row_odd, 1.0, f) if dirn else jnp.where(row_odd, f, 1.0)
        if half in MXU_LEVEL_HALVES:
            blk = 1 + MXU_LEVEL_HALVES.index(half)
            return jnp.exp2(sums[blk * CHUNK:(blk + 1) * CHUNK])
        parts = []
        for p in range(0, CHUNK, 2 * half):
            ref = cum[p + half - 1 + dirn:p + half + dirn]
            lo, hi = cum[p:p + half], cum[p + half:p + 2 * half]
            parts += [lo - ref, ref - hi] if dirn else [ref - lo, hi - ref]
        return jnp.exp2(jnp.concatenate(parts, axis=0))

    def prepare(pos0, n, with_out):
        items = [(dirn, pos0 + c) for c in range(n) for dirn in range(2)]
        fs, kks, splits = [], [], []
        for dirn, pos in items:
            lower = lowers[dirn]
            f = lower + (1.0 - lower) * _sigmoid(f_refs[dirn][chunk_rows(pos), :])
            g = jnp.log(f) * LOG2E
            g1 = g.astype(BF16)
            r1 = g - g1.astype(F32)
            g2 = r1.astype(BF16)
            g3 = (r1 - g2.astype(F32)).astype(BF16)
            fs.append(f)
            kks.append(1.0 - f)
            splits.append(jnp.concatenate([g1, g2, g3], axis=0))
        sums = [None] * len(items)
        for dirn in range(2):
            idx = [j for j, it in enumerate(items) if it[0] == dirn]
            wide = _dot(cm_ref[dirn], jnp.concatenate([splits[j] for j in idx], axis=1))
            for c, j in enumerate(idx):
                sums[j] = wide[:, c * HEAD_DIM:(c + 1) * HEAD_DIM]
        for j, (dirn, pos) in enumerate(items):
            cum = sums[j][0:CHUNK]
            last = 0 if dirn else CHUNK - 1
            tot = cum[last:last + 1]
            kd_ref[dirn, chunk_rows(pos), :] = (kks[j] * jnp.exp2(tot - cum)).astype(BF16)
            et_ref[dirn, pos] = jnp.broadcast_to(jnp.exp2(tot), et_ref.shape[2:])
        if not with_out:
            return
        qs = [q_ref[chunk_rows(pos), :] for _, pos in items]
        for j, (dirn, pos) in enumerate(items):
            qd_ref[dirn, chunk_rows(pos), :] = (qs[j] * jnp.exp2(sums[j][0:CHUNK])).astype(BF16)
        qbs = [qv.astype(BF16) for qv in qs]
        kbs = [kv.astype(BF16) for kv in kks]
        scs = [mk_ref[dirn, N_LEVELS] * _dot_nt(qbs[j], kbs[j]) for j, (dirn, _) in enumerate(items)]
        for lev in range(N_LEVELS):
            for j, (dirn, _) in enumerate(items):
                dl = level_decay(dirn, lev, fs[j], sums[j]).astype(BF16)
                scs[j] = scs[j] + mk_ref[dirn, lev] * _dot_nt(qbs[j] * dl, kbs[j] * dl)
        for j, (dirn, pos) in enumerate(items):
            sc_ref[dirn, chunk_rows(pos), :] = scs[j].astype(BF16)

    def scan(pos_f, pos_b, n, with_out):
        items = [(dirn, (pos_b - c) if dirn else (pos_f + c)) for c in range(n) for dirn in range(2)]
        vs = [i_ref[chunk_rows(pos), :] for _, pos in items]
        ups = [_dot(vs[j].T.astype(BF16), kd_ref[dirn, chunk_rows(pos), :]) for j, (dirn, pos) in enumerate(items)]
        sts = [st_refs[0][...], st_refs[1][...]]
        before = []
        for j, (dirn, pos) in enumerate(items):
            before.append(sts[dirn])
            sts[dirn] = sts[dirn] * et_ref[dirn, pos][0:1] + ups[j]
        for dirn in range(2):
            st_refs[dirn][...] = sts[dirn]
        if not with_out:
            return
        for j, (dirn, pos) in enumerate(items):
            orow = chunk_rows(pos)
            o_refs[dirn][orow, :] = (_dot_nt(qd_ref[dirn, orow, :], before[j].astype(BF16))
                                     + _dot(sc_ref[dirn, orow, :], vs[j].astype(BF16)))

    def loop(n, body):
        lax.fori_loop(0, n, lambda ci, c: (body(ci), c)[1], 0)

    ctx0 = n_lat_chunks
    prepare(ctx0, n_ctx_chunks, False)
    loop(n_lat_chunks // PREP_CHUNKS, lambda ci: prepare(ci * PREP_CHUNKS, PREP_CHUNKS, True))
    scan(ctx0, ctx0 + n_ctx_chunks - 1, n_ctx_chunks, False)
    loop(n_lat_chunks // SCAN_CHUNKS,
         lambda ci: scan(ci * SCAN_CHUNKS, n_lat_chunks - 1 - ci * SCAN_CHUNKS, SCAN_CHUNKS, True))

    def readout(i, carry):
        rows = pl.ds(pl.multiple_of(i * TOK, TOK), TOK)
        ot = of_ref[rows, :] + ob_ref[rows, :]
        on = ot * lax.rsqrt(jnp.mean(ot * ot, axis=-1, keepdims=True) + EPS) * ng_ref[...]
        gate = gt_ref[rows, :]
        y_ref[rows, :] = (on * (gate * _sigmoid(gate))).astype(BF16)
        return carry

    lax.fori_loop(0, n_lat_chunks * CHUNK // TOK, readout, 0, unroll=2)


def _hgrn_call(p3, lb_logits, norm_g, n_ctx, l):
    b, t, _ = p3.shape
    cm, mk = _hgrn_constants()
    cm = jnp.asarray(np.concatenate([cm] * 3, axis=-1), BF16)
    mk = jnp.asarray(mk, F32)
    lbl = lb_logits.reshape(2, 2, HEADS, 1, HEAD_DIM)
    hps = HGRN_HEADS_PER_STEP
    steps = HEADS // hps
    tok_spec = lambda grp: pl.BlockSpec((None, t, hps * HEAD_DIM), lambda i, h: (i, 0, grp * steps + h))
    full = lambda shape: pl.BlockSpec(shape, lambda i, h: (0,) * len(shape))
    return pl.pallas_call(
        functools.partial(_hgrn_kernel, n_ctx_chunks=n_ctx // CHUNK, n_lat_chunks=l // CHUNK),
        grid=(b, steps),
        in_specs=[pl.BlockSpec((2, 2, hps, 1, HEAD_DIM), lambda i, h: (0, 0, h, 0, 0)),
                  tok_spec(0), tok_spec(1), tok_spec(2), tok_spec(3), tok_spec(4),
                  full((2, 3 * CHUNK, 3 * CHUNK)),
                  full((2, N_LEVELS + 1, CHUNK, CHUNK)),
                  full((1, HEAD_DIM))],
        out_specs=pl.BlockSpec((None, l, hps * HEAD_DIM), lambda i, h: (i, 0, h)),
        out_shape=jax.ShapeDtypeStruct((b, l, MIX_W), BF16),
        scratch_shapes=[pltpu.VMEM((l, HEAD_DIM), F32), pltpu.VMEM((l, HEAD_DIM), F32),
                        pltpu.VMEM((HEAD_DIM, HEAD_DIM), F32), pltpu.VMEM((HEAD_DIM, HEAD_DIM), F32),
                        pltpu.VMEM((2, t, HEAD_DIM), BF16), pltpu.VMEM((2, t // CHUNK, 8, HEAD_DIM), F32),
                        pltpu.VMEM((2, l, HEAD_DIM), BF16), pltpu.VMEM((2, l, CHUNK), BF16)],
        compiler_params=_params(("parallel", "parallel")),
        name="hgrn",
    )(lbl, p3, p3, p3, p3, p3, cm, mk, norm_g)


def _rope_tables(l):
    pos = np.arange(l)
    nf = HEAD_DIM // 4
    inv = ROPE_THETA ** (-np.arange(nf, dtype=np.float64) / nf)
    ang_r = (pos // GRID_W)[:, None] * inv[None, :]
    ang_c = (pos % GRID_W)[:, None] * inv[None, :]
    cos = np.concatenate([np.cos(ang_r)] * 2 + [np.cos(ang_c)] * 2, axis=-1)
    sin = np.concatenate([-np.sin(ang_r), np.sin(ang_r), -np.sin(ang_c), np.sin(ang_c)], axis=-1)
    return jnp.asarray(cos, F32), jnp.asarray(sin, F32)


def _na_build_bias(rb_ref, tb_ref):
    lanes = 2 * GRID_W
    cq = lax.broadcasted_iota(jnp.int32, (GRID_W, lanes), 0)
    lane = lax.broadcasted_iota(jnp.int32, (GRID_W, lanes), 1)
    ck = lane % GRID_W
    cs = jnp.clip(cq - WIN_C // 2, 0, GRID_W - WIN_C)
    col_in = (ck >= cs) & (ck < cs + WIN_C)

    def toeplitz(d, lane0):
        row = jnp.broadcast_to(rb_ref[d:d + 1, :] * LOG2E, (GRID_W, lanes))
        return pltpu.roll(row, (lane0 - (WIN_C - 1)) % lanes, 1, stride=1, stride_axis=0)

    pair = [jnp.where(col_in, jnp.where(lane < GRID_W, toeplitz(d, 0), toeplitz(d + 1, GRID_W)), MASKED)
            for d in range(2 * WIN_R - 2)]
    for e in range(WIN_R):
        for p in range(WIN_R // 2):
            tb_ref[e, :, p * lanes:(p + 1) * lanes] = pair[2 * p - e + WIN_R - 1]


def _na_kernel(q_ref, k_ref, v_ref, cos_ref, sin_ref, rb_ref, gq_ref, gk_ref, perm_ref, o_ref,
               qs, ks, vs, kcs, vcs, tb_ref, **static):
    for hh in range(NA_HEADS_PER_STEP):
        lanes = pl.ds(hh * HEAD_DIM, HEAD_DIM)
        head = [r.at[:, lanes] for r in (q_ref, k_ref, v_ref)]
        _na_head(*head, cos_ref, sin_ref, rb_ref.at[hh], gq_ref, gk_ref, perm_ref, o_ref.at[:, lanes],
                 qs, ks, vs, kcs, vcs, tb_ref.at[hh], **static)


def _na_head(q_ref, k_ref, v_ref, cos_ref, sin_ref, rb_ref, gq_ref, gk_ref, perm_ref, o_ref,
             qs, ks, vs, kcs, vcs, tb_ref, *, n_ctx, grid_rows):
    @pl.when(pl.program_id(1) == 0)
    def _():
        _na_build_bias(rb_ref, tb_ref)

    ones = jnp.ones((2 * HEAD_DIM, HEAD_DIM), BF16)

    def normed(tv, g):
        sq = tv * tv
        hi = sq.astype(BF16)
        lo = (sq - hi.astype(F32)).astype(BF16)
        ssq = _dot(jnp.concatenate([hi, lo], axis=1), ones)
        return tv * lax.rsqrt(ssq * (1.0 / HEAD_DIM) + EPS) * g

    gq = gq_ref[...]
    gk = gk_ref[...]
    n_lat = grid_rows * GRID_W
    kcs[...] = normed(k_ref[n_lat:n_lat + n_ctx, :], gk).astype(BF16)
    vcs[...] = v_ref[n_lat:n_lat + n_ctx, :].astype(BF16)
    scale = HEAD_DIM ** -0.5 * LOG2E

    def prep(i, carry):
        rows = [pl.ds(pl.multiple_of((2 * i + n) * TOK, TOK), TOK) for n in range(2)]
        jobs = [(src_ref, g, dst, n) for n in range(2) for src_ref, g, dst in ((q_ref, gq, qs), (k_ref, gk, ks))]
        nrm = [normed(src_ref[rows[n], :], g) for src_ref, g, _, n in jobs]
        par = [_dot(tv.astype(BF16), perm_ref[...]) for tv in nrm]
        for (_, _, dst, n), tv, pv in zip(jobs, nrm, par):
            dst[rows[n], :] = (tv * cos_ref[rows[n], :] + pv * sin_ref[rows[n], :]).astype(BF16)
        for n in range(2):
            vs[rows[n], :] = v_ref[rows[n], :].astype(BF16)
        return carry

    lax.fori_loop(0, grid_rows * GRID_W // (2 * TOK), prep, 0, unroll=2)
    band = WIN_R * GRID_W

    def rows_step(i, carry):
        rr = [i * NA_ROWS + n for n in range(NA_ROWS)]
        rs = [jnp.clip(r - WIN_R // 2, 0, grid_rows - WIN_R) for r in rr]
        qrow = [pl.ds(pl.multiple_of(r * GRID_W, GRID_W), GRID_W) for r in rr]
        krow = [pl.ds(pl.multiple_of(s * GRID_W, GRID_W), band) for s in rs]
        qr = [qs[qw, :] for qw in qrow]
        kc = kcs[...]
        sb = [_dot_nt(qr[n], ks[krow[n], :]) * scale + tb_ref[rr[n] - rs[n]] for n in range(NA_ROWS)]
        sc = [_dot_nt(qr[n], kc) * scale for n in range(NA_ROWS)]
        m = [jnp.maximum(jnp.max(sb[n], axis=-1, keepdims=True), jnp.max(sc[n], axis=-1, keepdims=True))
             for n in range(NA_ROWS)]
        pb = [jnp.exp2(sb[n] - m[n]) for n in range(NA_ROWS)]
        pc = [jnp.exp2(sc[n] - m[n]) for n in range(NA_ROWS)]
        den = [jnp.sum(pb[n], axis=-1, keepdims=True) + jnp.sum(pc[n], axis=-1, keepdims=True)
               for n in range(NA_ROWS)]
        vc = vcs[...]
        o = [_dot(pb[n].astype(BF16), vs[krow[n], :]) + _dot(pc[n].astype(BF16), vc) for n in range(NA_ROWS)]
        for n in range(NA_ROWS):
            o_ref[qrow[n], :] = (o[n] / den[n]).astype(BF16)
        return carry

    lax.fori_loop(0, grid_rows // NA_ROWS, rows_step, 0)


def _na_call(p3, rel_bias, gq, gk, n_ctx, l):
    b, t, _ = p3.shape
    cos, sin = _rope_tables(l)
    nr, nc = rel_bias.shape[1:]
    rb = jnp.zeros((HEADS, 2 * WIN_R, 2 * GRID_W), F32).at[:, :nr, :nc].set(rel_bias)
    lanes = np.arange(HEAD_DIM)
    partner = np.where(lanes % (HEAD_DIM // 2) < HEAD_DIM // 4, lanes + HEAD_DIM // 4, lanes - HEAD_DIM // 4)
    perm = jnp.asarray(lanes[:, None] == partner[None, :], BF16)
    hps = NA_HEADS_PER_STEP
    steps = HEADS // hps
    col = lambda grp: (lambda h, i: (i, 0, grp * steps + h))
    tok_spec = lambda im: pl.BlockSpec((None, t, hps * HEAD_DIM), im)
    full = lambda shape: pl.BlockSpec(shape, lambda h, i: (0,) * len(shape))
    return pl.pallas_call(
        functools.partial(_na_kernel, n_ctx=n_ctx, grid_rows=l // GRID_W),
        grid=(steps, b),
        in_specs=[tok_spec(col(5)), tok_spec(col(6)), tok_spec(col(7)),
                  full((l, HEAD_DIM)), full((l, HEAD_DIM)),
                  pl.BlockSpec((hps, 2 * WIN_R, 2 * GRID_W), lambda h, i: (h, 0, 0)),
                  full((1, HEAD_DIM)), full((1, HEAD_DIM)), full((HEAD_DIM, HEAD_DIM))],
        out_specs=pl.BlockSpec((None, l, hps * HEAD_DIM), lambda h, i: (i, 0, h)),
        out_shape=jax.ShapeDtypeStruct((b, l, MIX_W), BF16),
        scratch_shapes=[pltpu.VMEM((l, HEAD_DIM), BF16), pltpu.VMEM((l, HEAD_DIM), BF16),
                        pltpu.VMEM((l, HEAD_DIM), BF16), pltpu.VMEM((n_ctx, HEAD_DIM), BF16),
                        pltpu.VMEM((n_ctx, HEAD_DIM), BF16),
                        pltpu.VMEM((hps, WIN_R, GRID_W, WIN_R * GRID_W), F32)],
        compiler_params=_params(("parallel", "arbitrary")),
        name="na",
    )(p3, p3, p3, cos, sin, rb, gq, gk, perm)


def _merge_kernel(ya_ref, yb_ref, ga_ref, gb_ref, x_ref, wa_ref, wb_ref, wo_ref, mod_ref, n2_ref,
                  xm_ref, h2_ref):
    tm = x_ref.shape[0]
    halves = [pl.ds(n * (tm // 2), tm // 2) for n in range(2)]
    za = [_dot(ya_ref[r, :], wa_ref[...]) for r in halves]
    zb = [_dot(yb_ref[r, :], wb_ref[...]) for r in halves]
    z = [(_sigmoid(ga_ref[r, :].astype(F32)) * za[n] + _sigmoid(gb_ref[r, :].astype(F32)) * zb[n]).astype(BF16)
         for n, r in enumerate(halves)]
    br = [_dot(zn, wo_ref[...]) for zn in z]
    for n, r in enumerate(halves):
        xm = x_ref[r, :] + mod_ref[0:1, :] * br[n]
        xm_ref[r, :] = xm
        y = xm * lax.rsqrt(jnp.mean(xm * xm, axis=-1, keepdims=True) + EPS) * n2_ref[...]
        h2_ref[r, :] = (y * (1.0 + mod_ref[2:3, :]) + mod_ref[1:2, :]).astype(BF16)


def _merge_call(ya, yb, pg3, x, wa, wb, wo, mod3, n2):
    b, l, d = x.shape
    tm = MERGE_ROW_TILE
    const = lambda shape: pl.BlockSpec(shape, lambda i, t: (0,) * len(shape), pipeline_mode=pl.Buffered(1))
    return pl.pallas_call(
        _merge_kernel,
        grid=(b, l // tm),
        in_specs=[pl.BlockSpec((None, tm, MIX_W), lambda i, t: (i, t, 0)),
                  pl.BlockSpec((None, tm, MIX_W), lambda i, t: (i, t, 0)),
                  pl.BlockSpec((None, tm, d), lambda i, t: (i, t, 0)),
                  pl.BlockSpec((None, tm, d), lambda i, t: (i, t, 1)),
                  pl.BlockSpec((None, tm, d), lambda i, t: (i, t, 0)),
                  const((MIX_W, d)), const((MIX_W, d)), const((d, d)),
                  pl.BlockSpec((None, 3, d), lambda i, t: (i, 0, 0)),
                  pl.BlockSpec((1, d), lambda i, t: (0, 0))],
        out_specs=[pl.BlockSpec((None, tm, d), lambda i, t: (i, t, 0)),
                   pl.BlockSpec((None, tm, d), lambda i, t: (i, t, 0))],
        out_shape=[jax.ShapeDtypeStruct((b, l, d), F32), jax.ShapeDtypeStruct((b, l, d), BF16)],
        compiler_params=_params(("parallel", "arbitrary")),
        name="merge",
    )(ya, yb, pg3, pg3, x, wa, wb, wo, mod3, n2)


def _ffn_kernel(h_ref, hp_ref, hn_ref, w1_ref, w3_ref, cw_ref, cb_ref, w2_ref, xm_ref, g2_ref, o_ref, a_ref,
                hc_ref, *, tiles_per_seq, halo):
    i = pl.program_id(0)
    s = pl.program_id(1)
    n_hid, tm, th = a_ref.shape

    @pl.when(s == 0)
    def _():
        hc_ref[0:tm, :] = h_ref[...]
        hc_ref[tm:tm + halo, :] = hp_ref[...]
        hc_ref[tm + halo:tm + 2 * halo, :] = hn_ref[...]

    @pl.when(s < n_hid)
    def _():
        t1_all = _dot(hc_ref[...], w1_ref[...])
        t1 = t1_all[0:tm]
        prev_row = t1_all[tm + halo - 1:tm + halo, :]
        next_row = t1_all[tm + halo:tm + halo + 1, :]
        prev_row = jnp.where(i % tiles_per_seq == 0, 0.0, prev_row)
        next_row = jnp.where(i % tiles_per_seq == tiles_per_seq - 1, 0.0, next_row)
        ridx = lax.broadcasted_iota(jnp.int32, (tm, 1), 0)
        up = jnp.where(ridx == 0, prev_row, pltpu.roll(t1, 1, 0))
        dn = jnp.where(ridx == tm - 1, next_row, pltpu.roll(t1, tm - 1, 0))
        u = up * cw_ref[0:1, :] + t1 * cw_ref[1:2, :] + dn * cw_ref[2:3, :] + cb_ref[...]
        a_ref[s] = ((u * _sigmoid(u)) * _dot(hc_ref[0:tm, :], w3_ref[...])).astype(BF16)

    @pl.when(s >= n_hid)
    def _():
        acc = _dot(a_ref[0], w2_ref[0:th, :])
        for k in range(1, n_hid):
            acc = acc + _dot(a_ref[k], w2_ref[k * th:(k + 1) * th, :])
        o_ref[...] = xm_ref[...] + g2_ref[...] * acc


def _ffn_hidden_tile(hid):
    return min(FFN_HIDDEN_TILE, hid)


def _ffn_call(h2, xm, w1, w3, cw, cb, w2, g2, l):
    m, d = h2.shape
    n_hid, _, th = w1.shape
    hid = n_hid * th
    tm = FFN_TOKEN_TILE
    tn = w2.shape[-1]
    halo = 16
    per = tm // halo
    nblk = m // halo
    hcol = lambda i, s: (0, jnp.minimum(s, n_hid - 1))
    htile = lambda i, s: (jnp.minimum(s, n_hid - 1), 0, 0)
    ocol = lambda s: jnp.maximum(s - n_hid, 0)
    return pl.pallas_call(
        functools.partial(_ffn_kernel, tiles_per_seq=l // tm, halo=halo),
        grid=(m // tm, n_hid + d // tn),
        in_specs=[pl.BlockSpec((tm, d), lambda i, s: (i, 0)),
                  pl.BlockSpec((halo, d), lambda i, s: (jnp.maximum(i * per - 1, 0), 0)),
                  pl.BlockSpec((halo, d), lambda i, s: (jnp.minimum((i + 1) * per, nblk - 1), 0)),
                  pl.BlockSpec((None, d, th), htile),
                  pl.BlockSpec((None, d, th), htile),
                  pl.BlockSpec((3, th), hcol),
                  pl.BlockSpec((1, th), hcol),
                  pl.BlockSpec((None, hid, tn), lambda i, s: (ocol(s), 0, 0)),
                  pl.BlockSpec((tm, tn), lambda i, s: (i, ocol(s))),
                  pl.BlockSpec((None, 1, tn), lambda i, s: (i // (l // tm), 0, ocol(s)))],
        out_specs=pl.BlockSpec((tm, tn), lambda i, s: (i, ocol(s))),
        out_shape=jax.ShapeDtypeStruct((m, d), F32),
        scratch_shapes=[pltpu.VMEM((n_hid, tm, th), BF16), pltpu.VMEM((tm + 2 * halo, d), BF16)],
        compiler_params=_params(("parallel", "arbitrary")),
        name="ffn",
    )(h2, h2, h2, w1, w3, cw, cb, w2, xm, g2)


def kernel(x, c, ctx, c_ctx, ada_w, ada_b, norm1_g, norm2_g, w_in, hgrn_lb_logits, hgrn_norm_g,
           na_q_norm_g, na_k_norm_g, na_rel_bias, w_branch_a, w_branch_b, w_out,
           ffn_w1, ffn_w3, ffn_conv_w, ffn_conv_b, ffn_w2):
    b, l, d = x.shape
    n_ctx = ctx.shape[1]
    assert ada_w.shape[0] == 1 and b + 1 <= 8 and n_ctx % TOK == 0 and l % (2 * TOK) == 0

    c_all = jnp.zeros((8, d), F32).at[:b].set(c).at[b].set(c_ctx)
    mod = _mod_call(c_all, ada_w[0], ada_b).reshape(8, N_MOD, d)
    sh1, sc1, g1, sh2, sc2, g2 = (mod[:b, n] for n in range(N_MOD))
    ss_ctx = jnp.broadcast_to(mod[b, 0:2][None], (b, 2, d))
    ss1 = jnp.stack([jnp.stack([sh1, sc1], axis=1), ss_ctx], axis=1)

    h = _prenorm_call(x, ctx, norm1_g, ss1)
    t = l + n_ctx
    h = h.reshape(b * t, d)
    n_mix = 8 * MIX_W
    th = _ffn_hidden_tile(ffn_w1.shape[-1])
    p, (w1, w3, w2) = _inproj_call(h, w_in[0], 0, n_mix, F32, "inproj",
                                   ((ffn_w1[0], th), (ffn_w3[0], th), (ffn_w2[0], FFN_OUT_TILE)))
    pg, (wa, wb, wo) = _inproj_call(h, w_in[0], n_mix, 2 * d, BF16, "inproj_gates",
                                    ((w_branch_a[0], d), (w_branch_b[0], d), (w_out[0], d)))
    wa, wb, wo = (wt.reshape(wt.shape[1:]) for wt in (wa, wb, wo))
    p3 = p.reshape(b, t, n_mix)
    pg3 = pg.reshape(b, t, 2 * d)

    y_a = _hgrn_call(p3, hgrn_lb_logits, hgrn_norm_g, n_ctx, l)
    y_b = _na_call(p3, na_rel_bias[0], na_q_norm_g, na_k_norm_g, n_ctx, l)

    mod3 = jnp.stack([g1, sh2, sc2], axis=1)
    x_mid, h2 = _merge_call(y_a, y_b, pg3, x, wa, wb, wo, mod3, norm2_g)

    out = _ffn_call(h2.reshape(b * l, d), x_mid.reshape(b * l, d), w1, w3, ffn_conv_w[0], ffn_conv_b, w2,
                    g2[:, None, :], l)
    return out.reshape(b, l, d)
```

```python
import functools

import numpy as np
import jax
import jax.numpy as jnp
from jax import lax
from jax.experimental import pallas as pl
from jax.experimental.pallas import tpu as pltpu

GRID_W = 64
HEADS = 8
HEAD_DIM = 128
MIX_W = HEADS * HEAD_DIM
CHUNK = 64
N_LEVELS = 6
MXU_LEVEL_HALVES = (4, 2)
PREP_CHUNKS = 16
SCAN_CHUNKS = 16
HGRN_HEADS_PER_STEP = 2
NA_HEADS_PER_STEP = 2
NA_ROWS = 16
WIN_R = 8
WIN_C = 16
ROPE_THETA = 10000.0
N_MOD = 6
EPS = 1e-6
LOG2E = 1.4426950408889634
TOK = 256
MOD_COL_TILE = 1024
INPROJ_ROW_TILE = 1024
INPROJ_COL_TILE = 1024
MERGE_ROW_TILE = 2 * TOK
FFN_TOKEN_TILE = 1024
FFN_HIDDEN_TILE = 512
FFN_OUT_TILE = 512
MASKED = -1e30
V7X_VMEM_LIMIT = 60 * 1024 * 1024

BF16 = jnp.bfloat16
F32 = jnp.float32


def _dot(a, b):
    return jnp.dot(a, b, preferred_element_type=F32)


def _dot_nt(a, b):
    return lax.dot_general(a, b, (((1,), (1,)), ((), ())), preferred_element_type=F32)


def _sigmoid(t):
    return 1.0 / (1.0 + jnp.exp(-t))


def _params(sem):
    return pltpu.CompilerParams(dimension_semantics=sem, vmem_limit_bytes=V7X_VMEM_LIMIT)


def _mod_kernel(c_ref, w_ref, b_ref, o_ref):
    cv = c_ref[...]
    s = (cv * _sigmoid(cv)).astype(BF16)
    o_ref[...] = _dot(s, w_ref[...].astype(BF16)) + b_ref[...]


def _mod_call(c_all, w, b):
    d, n = w.shape
    tn = min(MOD_COL_TILE, d)
    return pl.pallas_call(
        _mod_kernel,
        grid=(n // tn,),
        in_specs=[pl.BlockSpec((8, d), lambda j: (0, 0)),
                  pl.BlockSpec((d, tn), lambda j: (0, j)),
                  pl.BlockSpec((1, tn), lambda j: (0, j))],
        out_specs=pl.BlockSpec((8, tn), lambda j: (0, j)),
        out_shape=jax.ShapeDtypeStruct((8, n), F32),
        compiler_params=_params(("arbitrary",)),
        name="mod",
    )(c_all, w, b)


def _prenorm_kernel(x_ref, ctx_ref, g_ref, ss_ref, o_ref, *, n_lat_blk):
    t = pl.program_id(1)
    xv = jnp.where(t < n_lat_blk, x_ref[...], ctx_ref[...])
    y = xv * lax.rsqrt(jnp.mean(xv * xv, axis=-1, keepdims=True) + EPS) * g_ref[...]
    o_ref[...] = (y * (1.0 + ss_ref[1:2, :]) + ss_ref[0:1, :]).astype(BF16)


def _prenorm_call(x, ctx, g, ss):
    b, l, d = x.shape
    nlb = l // TOK
    nt = nlb + ctx.shape[1] // TOK
    return pl.pallas_call(
        functools.partial(_prenorm_kernel, n_lat_blk=nlb),
        grid=(b, nt),
        in_specs=[pl.BlockSpec((None, TOK, d), lambda i, t: (i, jnp.minimum(t, nlb - 1), 0)),
                  pl.BlockSpec((None, TOK, d), lambda i, t: (i, jnp.maximum(t - nlb, 0), 0)),
                  pl.BlockSpec((1, d), lambda i, t: (0, 0)),
                  pl.BlockSpec((None, None, 2, d), lambda i, t: (i, (t >= nlb).astype(jnp.int32), 0, 0))],
        out_specs=pl.BlockSpec((None, TOK, d), lambda i, t: (i, t, 0)),
        out_shape=jax.ShapeDtypeStruct((b, nt * TOK, d), BF16),
        compiler_params=_params(("parallel", "arbitrary")),
        name="prenorm",
    )(x, ctx, g, ss)


def _inproj_kernel(*refs, n_cast):
    h_ref, w_ref = refs[:2]
    cast_in = refs[2:2 + n_cast]
    o_ref = refs[2 + n_cast]
    cast_out = refs[3 + n_cast:3 + 2 * n_cast]
    wb_ref = refs[-1]

    @pl.when(pl.program_id(1) == 0)
    def _():
        wb_ref[...] = w_ref[...].astype(BF16)

    o_ref[...] = _dot(h_ref[...], wb_ref[...]).astype(o_ref.dtype)
    for src, dst in zip(cast_in, cast_out):
        ct = dst.shape[-1]
        for c in range(dst.shape[0]):
            dst[c] = src[:, c * ct:(c + 1) * ct].astype(BF16)


def _slab_rows(rows, steps):
    return next(r for r in range(32, rows + 1, 32) if rows % r == 0 and rows // r <= steps)


def _inproj_call(h, w, col0, n, out_dtype, name, to_cast=()):
    m, d = h.shape
    tm = INPROJ_ROW_TILE if m % INPROJ_ROW_TILE == 0 else TOK
    tn = min(INPROJ_COL_TILE, d)
    j0 = col0 // tn
    ni = m // tm
    steps = (n // tn) * ni
    in_slabs, out_slabs, out_shapes = [], [], []
    for a, ct in to_cast:
        rows, cols = a.shape
        r = _slab_rows(rows, steps)
        slab = lambda j, i, last=rows // r - 1: jnp.minimum(j * ni + i, last)
        in_slabs.append(pl.BlockSpec((r, cols), lambda j, i, slab=slab: (slab(j, i), 0)))
        out_slabs.append(pl.BlockSpec((cols // ct, r, ct), lambda j, i, slab=slab: (0, slab(j, i), 0)))
        out_shapes.append(jax.ShapeDtypeStruct((cols // ct, rows, ct), BF16))
    outs = pl.pallas_call(
        functools.partial(_inproj_kernel, n_cast=len(to_cast)),
        grid=(n // tn, ni),
        in_specs=[pl.BlockSpec((tm, d), lambda j, i: (i, 0)),
                  pl.BlockSpec((d, tn), lambda j, i: (0, j + j0))] + in_slabs,
        out_specs=[pl.BlockSpec((tm, tn), lambda j, i: (i, j))] + out_slabs,
        out_shape=[jax.ShapeDtypeStruct((m, n), out_dtype)] + out_shapes,
        scratch_shapes=[pltpu.VMEM((d, tn), BF16)],
        compiler_params=_params(("arbitrary", "arbitrary")),
        name=name,
    )(h, w, *[a for a, _ in to_cast])
    return outs[0], outs[1:]


def _hgrn_constants():
    u = np.arange(CHUNK)
    mats, masks = [], []
    for rev in (False, True):
        pos = CHUNK - 1 - u if rev else u
        incl = (pos[None, :] <= pos[:, None]).astype(np.float32)
        blocks, lvl_masks = [incl], []
        for lev in range(N_LEVELS):
            half = CHUNK >> (lev + 1)
            ref = (pos // (2 * half)) * (2 * half) + half - 1
            upto_ref = (pos[None, :] <= ref[:, None]).astype(np.float32)
            if half in MXU_LEVEL_HALVES:
                late = (pos % (2 * half)) >= half
                blocks.append(np.where(late[:, None], 1.0, -1.0).astype(np.float32) * (incl - upto_ref))
            same = (pos[:, None] // (2 * half)) == (pos[None, :] // (2 * half))
            late_q = (pos[:, None] % (2 * half)) >= half
            early_k = (pos[None, :] % (2 * half)) < half
            lvl_masks.append((same & late_q & early_k).astype(np.float32))
        lvl_masks.append(np.eye(CHUNK, dtype=np.float32))
        mats.append(np.concatenate(blocks, axis=0))
        masks.append(np.stack(lvl_masks))
    return np.stack(mats), np.stack(masks)


def _hgrn_kernel(lbl_ref, q_ref, ff_ref, fb_ref, i_ref, gt_ref, cm_ref, mk_ref, ng_ref, y_ref, *scratch, **chunks):
    for hh in range(HGRN_HEADS_PER_STEP):
        lanes = pl.ds(hh * HEAD_DIM, HEAD_DIM)
        head = [r.at[:, lanes] for r in (q_ref, ff_ref, fb_ref, i_ref, gt_ref)]
        _hgrn_head(lbl_ref.at[:, :, hh], *head, cm_ref, mk_ref, ng_ref, y_ref.at[:, lanes], *scratch, **chunks)


def _hgrn_head(lbl_ref, q_ref, ff_ref, fb_ref, i_ref, gt_ref, cm_ref, mk_ref, ng_ref, y_ref,
               of_ref, ob_ref, sf_ref, sb_ref, kd_ref, et_ref, qd_ref, sc_ref, *, n_ctx_chunks, n_lat_chunks):
    f_refs, o_refs, st_refs = (ff_ref, fb_ref), (of_ref, ob_ref), (sf_ref, sb_ref)
    lowers = []
    for dirn in range(2):
        la, lb_ = lbl_ref[dirn, 0], lbl_ref[dirn, 1]
        mx = jnp.maximum(la, lb_)
        ea, eb = jnp.exp(la - mx), jnp.exp(lb_ - mx)
        lowers.append(ea / (ea + eb))
        st_refs[dirn][...] = jnp.zeros(st_refs[dirn].shape, F32)
    row_odd = (lax.broadcasted_iota(jnp.int32, (CHUNK, 1), 0) % 2) == 1

    def chunk_rows(pos):
        return pl.ds(pl.multiple_of(pos * CHUNK, CHUNK), CHUNK)

    def level_decay(dirn, lev, f, sums):
        half = CHUNK >> (lev + 1)
        cum = sums[0:CHUNK]
        if half == 1:
            return jnp.where(row_odd, 1.0, f) if dirn else jnp.where(row_odd, f, 1.0)
        if half in MXU_LEVEL_HALVES:
            blk = 1 + MXU_LEVEL_HALVES.index(half)
            return jnp.exp2(sums[blk * CHUNK:(blk + 1) * CHUNK])
        parts = []
        for p in range(0, CHUNK, 2 * half):
            ref = cum[p + half - 1 + dirn:p + half + dirn]
            lo, hi = cum[p:p + half], cum[p + half:p + 2 * half]
            parts += [lo - ref, ref - hi] if dirn else [ref - lo, hi - ref]
        return jnp.exp2(jnp.concatenate(parts, axis=0))

    def prepare(pos0, n, with_out):
        items = [(dirn, pos0 + c) for c in range(n) for dirn in range(2)]
        fs, kks, splits = [], [], []
        for dirn, pos in items:
            lower = lowers[dirn]
            f = lower + (1.0 - lower) * _sigmoid(f_refs[dirn][chunk_rows(pos), :])
            g = jnp.log(f) * LOG2E
            g1 = g.astype(BF16)
            r1 = g - g1.astype(F32)
            g2 = r1.astype(BF16)
            g3 = (r1 - g2.astype(F32)).astype(BF16)
            fs.append(f)
            kks.append(1.0 - f)
            splits.append(jnp.concatenate([g1, g2, g3], axis=0))
        sums = [None] * len(items)
        for dirn in range(2):
            idx = [j for j, it in enumerate(items) if it[0] == dirn]
            wide = _dot(cm_ref[dirn], jnp.concatenate([splits[j] for j in idx], axis=1))
            for c, j in enumerate(idx):
                sums[j] = wide[:, c * HEAD_DIM:(c + 1) * HEAD_DIM]
        for j, (dirn, pos) in enumerate(items):
            cum = sums[j][0:CHUNK]
            last = 0 if dirn else CHUNK - 1
            tot = cum[last:last + 1]
            kd_ref[dirn, chunk_rows(pos), :] = (kks[j] * jnp.exp2(tot - cum)).astype(BF16)
            et_ref[dirn, pos] = jnp.broadcast_to(jnp.exp2(tot), et_ref.shape[2:])
        if not with_out:
            return
        for j, (dirn, pos) in enumerate(items):
            q = q_ref[chunk_rows(pos), :]
            qd_ref[dirn, chunk_rows(pos), :] = (q * jnp.exp2(sums[j][0:CHUNK])).astype(BF16)
            qb, kb = q.astype(BF16), kks[j].astype(BF16)
            terms = [mk_ref[dirn, N_LEVELS] * _dot_nt(qb, kb)]
            for lev in range(N_LEVELS):
                dl = level_decay(dirn, lev, fs[j], sums[j]).astype(BF16)
                terms.append(mk_ref[dirn, lev] * _dot_nt(qb * dl, kb * dl))
            while len(terms) > 1:
                terms = [a + b for a, b in zip(terms[::2], terms[1::2])] + terms[len(terms) & ~1:]
            sc_ref[dirn, chunk_rows(pos), :] = terms[0].astype(BF16)

    def scan(pos_f, pos_b, n, with_out):
        items = [(dirn, (pos_b - c) if dirn else (pos_f + c)) for c in range(n) for dirn in range(2)]
        vs = [i_ref[chunk_rows(pos), :] for _, pos in items]
        ups = [_dot(vs[j].T.astype(BF16), kd_ref[dirn, chunk_rows(pos), :]) for j, (dirn, pos) in enumerate(items)]
        sts = [st_refs[0][...], st_refs[1][...]]
        before = []
        for j, (dirn, pos) in enumerate(items):
            before.append(sts[dirn])
            sts[dirn] = sts[dirn] * et_ref[dirn, pos][0:1] + ups[j]
        for dirn in range(2):
            st_refs[dirn][...] = sts[dirn]
        if not with_out:
            return
        for j, (dirn, pos) in enumerate(items):
            orow = chunk_rows(pos)
            o_refs[dirn][orow, :] = (_dot_nt(qd_ref[dirn, orow, :], before[j].astype(BF16))
                                     + _dot(sc_ref[dirn, orow, :], vs[j].astype(BF16)))

    def loop(n, body):
        lax.fori_loop(0, n, lambda ci, c: (body(ci), c)[1], 0)

    ctx0 = n_lat_chunks
    prepare(ctx0, n_ctx_chunks, False)
    loop(n_lat_chunks // PREP_CHUNKS, lambda ci: prepare(ci * PREP_CHUNKS, PREP_CHUNKS, True))
    scan(ctx0, ctx0 + n_ctx_chunks - 1, n_ctx_chunks, False)
    loop(n_lat_chunks // SCAN_CHUNKS,
         lambda ci: scan(ci * SCAN_CHUNKS, n_lat_chunks - 1 - ci * SCAN_CHUNKS, SCAN_CHUNKS, True))

    def readout(i, carry):
        rows = pl.ds(pl.multiple_of(i * TOK, TOK), TOK)
        ot = of_ref[rows, :] + ob_ref[rows, :]
        on = ot * lax.rsqrt(jnp.mean(ot * ot, axis=-1, keepdims=True) + EPS) * ng_ref[...]
        gate = gt_ref[rows, :]
        y_ref[rows, :] = (on * (gate * _sigmoid(gate))).astype(BF16)
        return carry

    lax.fori_loop(0, n_lat_chunks * CHUNK // TOK, readout, 0, unroll=2)


def _hgrn_call(p3, lb_logits, norm_g, n_ctx, l):
    b, t, _ = p3.shape
    cm, mk = _hgrn_constants()
    cm = jnp.asarray(np.concatenate([cm] * 3, axis=-1), BF16)
    mk = jnp.asarray(mk, F32)
    lbl = lb_logits.reshape(2, 2, HEADS, 1, HEAD_DIM)
    hps = HGRN_HEADS_PER_STEP
    steps = HEADS // hps
    tok_spec = lambda grp: pl.BlockSpec((None, t, hps * HEAD_DIM), lambda i, h: (i, 0, grp * steps + h))
    full = lambda shape: pl.BlockSpec(shape, lambda i, h: (0,) * len(shape))
    return pl.pallas_call(
        functools.partial(_hgrn_kernel, n_ctx_chunks=n_ctx // CHUNK, n_lat_chunks=l // CHUNK),
        grid=(b, steps),
        in_specs=[pl.BlockSpec((2, 2, hps, 1, HEAD_DIM), lambda i, h: (0, 0, h, 0, 0)),
                  tok_spec(0), tok_spec(1), tok_spec(2), tok_spec(3), tok_spec(4),
                  full((2, 3 * CHUNK, 3 * CHUNK)),
                  full((2, N_LEVELS + 1, CHUNK, CHUNK)),
                  full((1, HEAD_DIM))],
        out_specs=pl.BlockSpec((None, l, hps * HEAD_DIM), lambda i, h: (i, 0, h)),
        out_shape=jax.ShapeDtypeStruct((b, l, MIX_W), BF16),
        scratch_shapes=[pltpu.VMEM((l, HEAD_DIM), F32), pltpu.VMEM((l, HEAD_DIM), F32),
                        pltpu.VMEM((HEAD_DIM, HEAD_DIM), F32), pltpu.VMEM((HEAD_DIM, HEAD_DIM), F32),
                        pltpu.VMEM((2, t, HEAD_DIM), BF16), pltpu.VMEM((2, t // CHUNK, 8, HEAD_DIM), F32),
                        pltpu.VMEM((2, l, HEAD_DIM), BF16), pltpu.VMEM((2, l, CHUNK), BF16)],
        compiler_params=_params(("parallel", "parallel")),
        name="hgrn",
    )(lbl, p3, p3, p3, p3, p3, cm, mk, norm_g)


def _rope_tables(l):
    pos = np.arange(l)
    nf = HEAD_DIM // 4
    inv = ROPE_THETA ** (-np.arange(nf, dtype=np.float64) / nf)
    ang_r = (pos // GRID_W)[:, None] * inv[None, :]
    ang_c = (pos % GRID_W)[:, None] * inv[None, :]
    cos = np.concatenate([np.cos(ang_r)] * 2 + [np.cos(ang_c)] * 2, axis=-1)
    sin = np.concatenate([-np.sin(ang_r), np.sin(ang_r), -np.sin(ang_c), np.sin(ang_c)], axis=-1)
    return jnp.asarray(cos, F32), jnp.asarray(sin, F32)


def _na_build_bias(rb_ref, tb_ref):
    lanes = 2 * GRID_W
    cq = lax.broadcasted_iota(jnp.int32, (GRID_W, lanes), 0)
    lane = lax.broadcasted_iota(jnp.int32, (GRID_W, lanes), 1)
    ck = lane % GRID_W
    cs = jnp.clip(cq - WIN_C // 2, 0, GRID_W - WIN_C)
    col_in = (ck >= cs) & (ck < cs + WIN_C)

    def toeplitz(d, lane0):
        row = jnp.broadcast_to(rb_ref[d:d + 1, :] * LOG2E, (GRID_W, lanes))
        return pltpu.roll(row, (lane0 - (WIN_C - 1)) % lanes, 1, stride=1, stride_axis=0)

    pair = [jnp.where(col_in, jnp.where(lane < GRID_W, toeplitz(d, 0), toeplitz(d + 1, GRID_W)), MASKED)
            for d in range(2 * WIN_R - 2)]
    for e in range(WIN_R):
        for p in range(WIN_R // 2):
            tb_ref[e, :, p * lanes:(p + 1) * lanes] = pair[2 * p - e + WIN_R - 1]


def _na_kernel(q_ref, k_ref, v_ref, cos_ref, sin_ref, rb_ref, gq_ref, gk_ref, perm_ref, o_ref,
               qs, ks, vs, kcs, vcs, tb_ref, **static):
    for hh in range(NA_HEADS_PER_STEP):
        lanes = pl.ds(hh * HEAD_DIM, HEAD_DIM)
        head = [r.at[:, lanes] for r in (q_ref, k_ref, v_ref)]
        _na_head(*head, cos_ref, sin_ref, rb_ref.at[hh], gq_ref, gk_ref, perm_ref, o_ref.at[:, lanes],
                 qs, ks, vs, kcs, vcs, tb_ref.at[hh], **static)


def _na_head(q_ref, k_ref, v_ref, cos_ref, sin_ref, rb_ref, gq_ref, gk_ref, perm_ref, o_ref,
             qs, ks, vs, kcs, vcs, tb_ref, *, n_ctx, grid_rows):
    @pl.when(pl.program_id(1) == 0)
    def _():
        _na_build_bias(rb_ref, tb_ref)

    ones = jnp.ones((2 * HEAD_DIM, HEAD_DIM), BF16)

    def normed(tv, g):
        sq = tv * tv
        hi = sq.astype(BF16)
        lo = (sq - hi.astype(F32)).astype(BF16)
        ssq = _dot(jnp.concatenate([hi, lo], axis=1), ones)
        return tv * lax.rsqrt(ssq * (1.0 / HEAD_DIM) + EPS) * g

    gq = gq_ref[...]
    gk = gk_ref[...]
    n_lat = grid_rows * GRID_W
    kcs[...] = normed(k_ref[n_lat:n_lat + n_ctx, :], gk).astype(BF16)
    vcs[...] = v_ref[n_lat:n_lat + n_ctx, :].astype(BF16)
    scale = HEAD_DIM ** -0.5 * LOG2E

    def prep(i, carry):
        rows = [pl.ds(pl.multiple_of((2 * i + n) * TOK, TOK), TOK) for n in range(2)]
        jobs = [(src_ref, g, dst, n) for n in range(2) for src_ref, g, dst in ((q_ref, gq, qs), (k_ref, gk, ks))]
        nrm = [normed(src_ref[rows[n], :], g) for src_ref, g, _, n in jobs]
        par = [_dot(tv.astype(BF16), perm_ref[...]) for tv in nrm]
        for (_, _, dst, n), tv, pv in zip(jobs, nrm, par):
            dst[rows[n], :] = (tv * cos_ref[rows[n], :] + pv * sin_ref[rows[n], :]).astype(BF16)
        for n in range(2):
            vs[rows[n], :] = v_ref[rows[n], :].astype(BF16)
        return carry

    lax.fori_loop(0, grid_rows * GRID_W // (2 * TOK), prep, 0, unroll=2)
    band = WIN_R * GRID_W

    def rows_step(i, carry):
        rr = [i * NA_ROWS + n for n in range(NA_ROWS)]
        rs = [jnp.clip(r - WIN_R // 2, 0, grid_rows - WIN_R) for r in rr]
        qrow = [pl.ds(pl.multiple_of(r * GRID_W, GRID_W), GRID_W) for r in rr]
        krow = [pl.ds(pl.multiple_of(s * GRID_W, GRID_W), band) for s in rs]
        qr = [qs[qw, :] for qw in qrow]
        kc = kcs[...]
        sb = [_dot_nt(qr[n], ks[krow[n], :]) * scale + tb_ref[rr[n] - rs[n]] for n in range(NA_ROWS)]
        sc = [_dot_nt(qr[n], kc) * scale for n in range(NA_ROWS)]
        m = [jnp.maximum(jnp.max(sb[n], axis=-1, keepdims=True), jnp.max(sc[n], axis=-1, keepdims=True))
             for n in range(NA_ROWS)]
        pb = [jnp.exp2(sb[n] - m[n]) for n in range(NA_ROWS)]
        pc = [jnp.exp2(sc[n] - m[n]) for n in range(NA_ROWS)]
        den = [jnp.sum(pb[n], axis=-1, keepdims=True) + jnp.sum(pc[n], axis=-1, keepdims=True)
               for n in range(NA_ROWS)]
        vc = vcs[...]
        o = [_dot(pb[n].astype(BF16), vs[krow[n], :]) + _dot(pc[n].astype(BF16), vc) for n in range(NA_ROWS)]
        for n in range(NA_ROWS):
            o_ref[qrow[n], :] = (o[n] / den[n]).astype(BF16)
        return carry

    lax.fori_loop(0, grid_rows // NA_ROWS, rows_step, 0)


def _na_call(p3, rel_bias, gq, gk, n_ctx, l):
    b, t, _ = p3.shape
    cos, sin = _rope_tables(l)
    nr, nc = rel_bias.shape[1:]
    rb = jnp.zeros((HEADS, 2 * WIN_R, 2 * GRID_W), F32).at[:, :nr, :nc].set(rel_bias)
    lanes = np.arange(HEAD_DIM)
    partner = np.where(lanes % (HEAD_DIM // 2) < HEAD_DIM // 4, lanes + HEAD_DIM // 4, lanes - HEAD_DIM // 4)
    perm = jnp.asarray(lanes[:, None] == partner[None, :], BF16)
    hps = NA_HEADS_PER_STEP
    steps = HEADS // hps
    col = lambda grp: (lambda h, i: (i, 0, grp * steps + h))
    tok_spec = lambda im: pl.BlockSpec((None, t, hps * HEAD_DIM), im)
    full = lambda shape: pl.BlockSpec(shape, lambda h, i: (0,) * len(shape))
    return pl.pallas_call(
        functools.partial(_na_kernel, n_ctx=n_ctx, grid_rows=l // GRID_W),
        grid=(steps, b),
        in_specs=[tok_spec(col(5)), tok_spec(col(6)), tok_spec(col(7)),
                  full((l, HEAD_DIM)), full((l, HEAD_DIM)),
                  pl.BlockSpec((hps, 2 * WIN_R, 2 * GRID_W), lambda h, i: (h, 0, 0)),
                  full((1, HEAD_DIM)), full((1, HEAD_DIM)), full((HEAD_DIM, HEAD_DIM))],
        out_specs=pl.BlockSpec((None, l, hps * HEAD_DIM), lambda h, i: (i, 0, h)),
        out_shape=jax.ShapeDtypeStruct((b, l, MIX_W), BF16),
        scratch_shapes=[pltpu.VMEM((l, HEAD_DIM), BF16), pltpu.VMEM((l, HEAD_DIM), BF16),
                        pltpu.VMEM((l, HEAD_DIM), BF16), pltpu.VMEM((n_ctx, HEAD_DIM), BF16),
                        pltpu.VMEM((n_ctx, HEAD_DIM), BF16),
                        pltpu.VMEM((hps, WIN_R, GRID_W, WIN_R * GRID_W), F32)],
        compiler_params=_params(("parallel", "arbitrary")),
        name="na",
    )(p3, p3, p3, cos, sin, rb, gq, gk, perm)


def _merge_kernel(ya_ref, yb_ref, ga_ref, gb_ref, x_ref, wa_ref, wb_ref, wo_ref, mod_ref, n2_ref,
                  xm_ref, h2_ref):
    tm = x_ref.shape[0]
    halves = [pl.ds(n * (tm // 2), tm // 2) for n in range(2)]
    za = [_dot(ya_ref[r, :], wa_ref[...]) for r in halves]
    zb = [_dot(yb_ref[r, :], wb_ref[...]) for r in halves]
    z = [(_sigmoid(ga_ref[r, :].astype(F32)) * za[n] + _sigmoid(gb_ref[r, :].astype(F32)) * zb[n]).astype(BF16)
         for n, r in enumerate(halves)]
    br = [_dot(zn, wo_ref[...]) for zn in z]
    for n, r in enumerate(halves):
        xm = x_ref[r, :] + mod_ref[0:1, :] * br[n]
        xm_ref[r, :] = xm
        y = xm * lax.rsqrt(jnp.mean(xm * xm, axis=-1, keepdims=True) + EPS) * n2_ref[...]
        h2_ref[r, :] = (y * (1.0 + mod_ref[2:3, :]) + mod_ref[1:2, :]).astype(BF16)


def _merge_call(ya, yb, pg3, x, wa, wb, wo, mod3, n2):
    b, l, d = x.shape
    tm = MERGE_ROW_TILE
    const = lambda shape: pl.BlockSpec(shape, lambda i, t: (0,) * len(shape), pipeline_mode=pl.Buffered(1))
    return pl.pallas_call(
        _merge_kernel,
        grid=(b, l // tm),
        in_specs=[pl.BlockSpec((None, tm, MIX_W), lambda i, t: (i, t, 0)),
                  pl.BlockSpec((None, tm, MIX_W), lambda i, t: (i, t, 0)),
                  pl.BlockSpec((None, tm, d), lambda i, t: (i, t, 0)),
                  pl.BlockSpec((None, tm, d), lambda i, t: (i, t, 1)),
                  pl.BlockSpec((None, tm, d), lambda i, t: (i, t, 0)),
                  const((MIX_W, d)), const((MIX_W, d)), const((d, d)),
                  pl.BlockSpec((None, 3, d), lambda i, t: (i, 0, 0)),
                  pl.BlockSpec((1, d), lambda i, t: (0, 0))],
        out_specs=[pl.BlockSpec((None, tm, d), lambda i, t: (i, t, 0)),
                   pl.BlockSpec((None, tm, d), lambda i, t: (i, t, 0))],
        out_shape=[jax.ShapeDtypeStruct((b, l, d), F32), jax.ShapeDtypeStruct((b, l, d), BF16)],
        compiler_params=_params(("parallel", "arbitrary")),
        name="merge",
    )(ya, yb, pg3, pg3, x, wa, wb, wo, mod3, n2)


def _ffn_kernel(h_ref, hp_ref, hn_ref, w1_ref, w3_ref, cw_ref, cb_ref, w2_ref, xm_ref, g2_ref, o_ref, a_ref,
                hc_ref, *, tiles_per_seq, halo):
    i = pl.program_id(0)
    s = pl.program_id(1)
    n_hid, tm, th = a_ref.shape

    @pl.when(s == 0)
    def _():
        hc_ref[0:tm, :] = h_ref[...]
        hc_ref[tm:tm + halo, :] = hp_ref[...]
        hc_ref[tm + halo:tm + 2 * halo, :] = hn_ref[...]

    @pl.when(s < n_hid)
    def _():
        t1_all = _dot(hc_ref[...], w1_ref[...])
        t1 = t1_all[0:tm]
        prev_row = t1_all[tm + halo - 1:tm + halo, :]
        next_row = t1_all[tm + halo:tm + halo + 1, :]
        prev_row = jnp.where(i % tiles_per_seq == 0, 0.0, prev_row)
        next_row = jnp.where(i % tiles_per_seq == tiles_per_seq - 1, 0.0, next_row)
        ridx = lax.broadcasted_iota(jnp.int32, (tm, 1), 0)
        up = jnp.where(ridx == 0, prev_row, pltpu.roll(t1, 1, 0))
        dn = jnp.where(ridx == tm - 1, next_row, pltpu.roll(t1, tm - 1, 0))
        u = up * cw_ref[0:1, :] + t1 * cw_ref[1:2, :] + dn * cw_ref[2:3, :] + cb_ref[...]
        a_ref[s] = ((u * _sigmoid(u)) * _dot(hc_ref[0:tm, :], w3_ref[...])).astype(BF16)

    @pl.when(s >= n_hid)
    def _():
        acc = _dot(a_ref[0], w2_ref[0:th, :])
        for k in range(1, n_hid):
            acc = acc + _dot(a_ref[k], w2_ref[k * th:(k + 1) * th, :])
        o_ref[...] = xm_ref[...] + g2_ref[...] * acc


def _ffn_hidden_tile(hid):
    return min(FFN_HIDDEN_TILE, hid)


def _ffn_call(h2, xm, w1, w3, cw, cb, w2, g2, l):
    m, d = h2.shape
    n_hid, _, th = w1.shape
    hid = n_hid * th
    tm = FFN_TOKEN_TILE
    tn = w2.shape[-1]
    halo = 16
    per = tm // halo
    nblk = m // halo
    hcol = lambda i, s: (0, jnp.minimum(s, n_hid - 1))
    htile = lambda i, s: (jnp.minimum(s, n_hid - 1), 0, 0)
    ocol = lambda s: jnp.maximum(s - n_hid, 0)
    return pl.pallas_call(
        functools.partial(_ffn_kernel, tiles_per_seq=l // tm, halo=halo),
        grid=(m // tm, n_hid + d // tn),
        in_specs=[pl.BlockSpec((tm, d), lambda i, s: (i, 0)),
                  pl.BlockSpec((halo, d), lambda i, s: (jnp.maximum(i * per - 1, 0), 0)),
                  pl.BlockSpec((halo, d), lambda i, s: (jnp.minimum((i + 1) * per, nblk - 1), 0)),
                  pl.BlockSpec((None, d, th), htile),
                  pl.BlockSpec((None, d, th), htile),
                  pl.BlockSpec((3, th), hcol),
                  pl.BlockSpec((1, th), hcol),
                  pl.BlockSpec((None, hid, tn), lambda i, s: (ocol(s), 0, 0)),
                  pl.BlockSpec((tm, tn), lambda i, s: (i, ocol(s))),
                  pl.BlockSpec((None, 1, tn), lambda i, s: (i // (l // tm), 0, ocol(s)))],
        out_specs=pl.BlockSpec((tm, tn), lambda i, s: (i, ocol(s))),
        out_shape=jax.ShapeDtypeStruct((m, d), F32),
        scratch_shapes=[pltpu.VMEM((n_hid, tm, th), BF16), pltpu.VMEM((tm + 2 * halo, d), BF16)],
        compiler_params=_params(("parallel", "arbitrary")),
        name="ffn",
    )(h2, h2, h2, w1, w3, cw, cb, w2, xm, g2)


def kernel(x, c, ctx, c_ctx, ada_w, ada_b, norm1_g, norm2_g, w_in, hgrn_lb_logits, hgrn_norm_g,
           na_q_norm_g, na_k_norm_g, na_rel_bias, w_branch_a, w_branch_b, w_out,
           ffn_w1, ffn_w3, ffn_conv_w, ffn_conv_b, ffn_w2):
    b, l, d = x.shape
    n_ctx = ctx.shape[1]
    assert ada_w.shape[0] == 1 and b + 1 <= 8 and n_ctx % TOK == 0 and l % (2 * TOK) == 0

    c_all = jnp.zeros((8, d), F32).at[:b].set(c).at[b].set(c_ctx)
    mod = _mod_call(c_all, ada_w[0], ada_b).reshape(8, N_MOD, d)
    sh1, sc1, g1, sh2, sc2, g2 = (mod[:b, n] for n in range(N_MOD))
    ss_ctx = jnp.broadcast_to(mod[b, 0:2][None], (b, 2, d))
    ss1 = jnp.stack([jnp.stack([sh1, sc1], axis=1), ss_ctx], axis=1)

    h = _prenorm_call(x, ctx, norm1_g, ss1)
    t = l + n_ctx
    h = h.reshape(b * t, d)
    n_mix = 8 * MIX_W
    th = _ffn_hidden_tile(ffn_w1.shape[-1])
    p, (w1, w3, w2) = _inproj_call(h, w_in[0], 0, n_mix, F32, "inproj",
                                   ((ffn_w1[0], th), (ffn_w3[0], th), (ffn_w2[0], FFN_OUT_TILE)))
    pg, (wa, wb, wo) = _inproj_call(h, w_in[0], n_mix, 2 * d, BF16, "inproj_gates",
                                    ((w_branch_a[0], d), (w_branch_b[0], d), (w_out[0], d)))
    wa, wb, wo = (wt.reshape(wt.shape[1:]) for wt in (wa, wb, wo))
    p3 = p.reshape(b, t, n_mix)
    pg3 = pg.reshape(b, t, 2 * d)

    y_a = _hgrn_call(p3, hgrn_lb_logits, hgrn_norm_g, n_ctx, l)
    y_b = _na_call(p3, na_rel_bias[0], na_q_norm_g, na_k_norm_g, n_ctx, l)

    mod3 = jnp.stack([g1, sh2, sc2], axis=1)
    x_mid, h2 = _merge_call(y_a, y_b, pg3, x, wa, wb, wo, mod3, norm2_g)

    out = _ffn_call(h2.reshape(b * l, d), x_mid.reshape(b * l, d), w1, w3, ffn_conv_w[0], ffn_conv_b, w2,
                    g2[:, None, :], l)
    return out.reshape(b, l, d)
```

```python
import functools

import numpy as np
import jax
import jax.numpy as jnp
from jax import lax
from jax.experimental import pallas as pl
from jax.experimental.pallas import tpu as pltpu

GRID_W = 64
HEADS = 8
HEAD_DIM = 128
MIX_W = HEADS * HEAD_DIM
CHUNK = 64
N_LEVELS = 6
MXU_LEVEL_HALVES = (4, 2)
PREP_CHUNKS = 16
PREP_GROUP = 4
SCAN_CHUNKS = 16
SCAN_GROUP = 4
HGRN_HEADS_PER_STEP = 2
NA_HEADS_PER_STEP = 2
NA_ROWS = 16
WIN_R = 8
WIN_C = 16
ROPE_THETA = 10000.0
N_MOD = 6
EPS = 1e-6
LOG2E = 1.4426950408889634
TOK = 256
MOD_COL_TILE = 1024
INPROJ_ROW_TILE = 1024
INPROJ_COL_TILE = 1024
MERGE_ROW_TILE = 2 * TOK
FFN_TOKEN_TILE = 1024
FFN_HIDDEN_TILE = 512
FFN_OUT_TILE = 512
MASKED = -1e30
V7X_VMEM_LIMIT = 60 * 1024 * 1024

BF16 = jnp.bfloat16
F32 = jnp.float32


def _dot(a, b):
    return jnp.dot(a, b, preferred_element_type=F32)


def _dot_nt(a, b):
    return lax.dot_general(a, b, (((1,), (1,)), ((), ())), preferred_element_type=F32)


def _sigmoid(t):
    return 1.0 / (1.0 + jnp.exp(-t))


def _params(sem):
    return pltpu.CompilerParams(dimension_semantics=sem, vmem_limit_bytes=V7X_VMEM_LIMIT)


def _mod_kernel(c_ref, w_ref, b_ref, o_ref):
    cv = c_ref[...]
    s = (cv * _sigmoid(cv)).astype(BF16)
    o_ref[...] = _dot(s, w_ref[...].astype(BF16)) + b_ref[...]


def _mod_call(c_all, w, b):
    d, n = w.shape
    tn = min(MOD_COL_TILE, d)
    return pl.pallas_call(
        _mod_kernel,
        grid=(n // tn,),
        in_specs=[pl.BlockSpec((8, d), lambda j: (0, 0)),
                  pl.BlockSpec((d, tn), lambda j: (0, j)),
                  pl.BlockSpec((1, tn), lambda j: (0, j))],
        out_specs=pl.BlockSpec((8, tn), lambda j: (0, j)),
        out_shape=jax.ShapeDtypeStruct((8, n), F32),
        compiler_params=_params(("arbitrary",)),
        name="mod",
    )(c_all, w, b)


def _prenorm_kernel(x_ref, ctx_ref, g_ref, ss_ref, o_ref, *, n_lat_blk):
    t = pl.program_id(1)
    xv = jnp.where(t < n_lat_blk, x_ref[...], ctx_ref[...])
    y = xv * lax.rsqrt(jnp.mean(xv * xv, axis=-1, keepdims=True) + EPS) * g_ref[...]
    o_ref[...] = (y * (1.0 + ss_ref[1:2, :]) + ss_ref[0:1, :]).astype(BF16)


def _prenorm_call(x, ctx, g, ss):
    b, l, d = x.shape
    nlb = l // TOK
    nt = nlb + ctx.shape[1] // TOK
    return pl.pallas_call(
        functools.partial(_prenorm_kernel, n_lat_blk=nlb),
        grid=(b, nt),
        in_specs=[pl.BlockSpec((None, TOK, d), lambda i, t: (i, jnp.minimum(t, nlb - 1), 0)),
                  pl.BlockSpec((None, TOK, d), lambda i, t: (i, jnp.maximum(t - nlb, 0), 0)),
                  pl.BlockSpec((1, d), lambda i, t: (0, 0)),
                  pl.BlockSpec((None, None, 2, d), lambda i, t: (i, (t >= nlb).astype(jnp.int32), 0, 0))],
        out_specs=pl.BlockSpec((None, TOK, d), lambda i, t: (i, t, 0)),
        out_shape=jax.ShapeDtypeStruct((b, nt * TOK, d), BF16),
        compiler_params=_params(("parallel", "arbitrary")),
        name="prenorm",
    )(x, ctx, g, ss)


def _inproj_kernel(*refs, n_cast):
    h_ref, w_ref = refs[:2]
    cast_in = refs[2:2 + n_cast]
    o_ref = refs[2 + n_cast]
    cast_out = refs[3 + n_cast:3 + 2 * n_cast]
    wb_ref = refs[-1]

    @pl.when(pl.program_id(1) == 0)
    def _():
        wb_ref[...] = w_ref[...].astype(BF16)

    o_ref[...] = _dot(h_ref[...], wb_ref[...]).astype(o_ref.dtype)
    for src, dst in zip(cast_in, cast_out):
        ct = dst.shape[-1]
        for c in range(dst.shape[0]):
            dst[c] = src[:, c * ct:(c + 1) * ct].astype(BF16)


def _slab_rows(rows, steps):
    return next(r for r in range(32, rows + 1, 32) if rows % r == 0 and rows // r <= steps)


def _inproj_call(h, w, col0, n, out_dtype, name, to_cast=()):
    m, d = h.shape
    tm = INPROJ_ROW_TILE if m % INPROJ_ROW_TILE == 0 else TOK
    tn = min(INPROJ_COL_TILE, d)
    j0 = col0 // tn
    ni = m // tm
    steps = (n // tn) * ni
    in_slabs, out_slabs, out_shapes = [], [], []
    for a, ct in to_cast:
        rows, cols = a.shape
        r = _slab_rows(rows, steps)
        slab = lambda j, i, last=rows // r - 1: jnp.minimum(j * ni + i, last)
        in_slabs.append(pl.BlockSpec((r, cols), lambda j, i, slab=slab: (slab(j, i), 0)))
        out_slabs.append(pl.BlockSpec((cols // ct, r, ct), lambda j, i, slab=slab: (0, slab(j, i), 0)))
        out_shapes.append(jax.ShapeDtypeStruct((cols // ct, rows, ct), BF16))
    outs = pl.pallas_call(
        functools.partial(_inproj_kernel, n_cast=len(to_cast)),
        grid=(n // tn, ni),
        in_specs=[pl.BlockSpec((tm, d), lambda j, i: (i, 0)),
                  pl.BlockSpec((d, tn), lambda j, i: (0, j + j0))] + in_slabs,
        out_specs=[pl.BlockSpec((tm, tn), lambda j, i: (i, j))] + out_slabs,
        out_shape=[jax.ShapeDtypeStruct((m, n), out_dtype)] + out_shapes,
        scratch_shapes=[pltpu.VMEM((d, tn), BF16)],
        compiler_params=_params(("arbitrary", "arbitrary")),
        name=name,
    )(h, w, *[a for a, _ in to_cast])
    return outs[0], outs[1:]


def _hgrn_constants():
    u = np.arange(CHUNK)
    mats, masks = [], []
    for rev in (False, True):
        pos = CHUNK - 1 - u if rev else u
        incl = (pos[None, :] <= pos[:, None]).astype(np.float32)
        blocks, lvl_masks = [incl], []
        for lev in range(N_LEVELS):
            half = CHUNK >> (lev + 1)
            ref = (pos // (2 * half)) * (2 * half) + half - 1
            upto_ref = (pos[None, :] <= ref[:, None]).astype(np.float32)
            if half in MXU_LEVEL_HALVES:
                late = (pos % (2 * half)) >= half
                blocks.append(np.where(late[:, None], 1.0, -1.0).astype(np.float32) * (incl - upto_ref))
            same = (pos[:, None] // (2 * half)) == (pos[None, :] // (2 * half))
            late_q = (pos[:, None] % (2 * half)) >= half
            early_k = (pos[None, :] % (2 * half)) < half
            lvl_masks.append((same & late_q & early_k).astype(np.float32))
        lvl_masks.append(np.eye(CHUNK, dtype=np.float32))
        mats.append(np.concatenate(blocks, axis=0))
        masks.append(np.stack(lvl_masks))
    return np.stack(mats), np.stack(masks)


def _hgrn_kernel(lbl_ref, q_ref, ff_ref, fb_ref, i_ref, gt_ref, cm_ref, mk_ref, ng_ref, y_ref, *scratch, **chunks):
    for hh in range(HGRN_HEADS_PER_STEP):
        lanes = pl.ds(hh * HEAD_DIM, HEAD_DIM)
        head = [r.at[:, lanes] for r in (q_ref, ff_ref, fb_ref, i_ref, gt_ref)]
        _hgrn_head(lbl_ref.at[:, :, hh], *head, cm_ref, mk_ref, ng_ref, y_ref.at[:, lanes], *scratch, **chunks)


def _hgrn_head(lbl_ref, q_ref, ff_ref, fb_ref, i_ref, gt_ref, cm_ref, mk_ref, ng_ref, y_ref,
               of_ref, ob_ref, sf_ref, sb_ref, kd_ref, et_ref, qd_ref, sc_ref, *, n_ctx_chunks, n_lat_chunks):
    f_refs, o_refs, st_refs = (ff_ref, fb_ref), (of_ref, ob_ref), (sf_ref, sb_ref)
    lowers = []
    for dirn in range(2):
        la, lb_ = lbl_ref[dirn, 0], lbl_ref[dirn, 1]
        mx = jnp.maximum(la, lb_)
        ea, eb = jnp.exp(la - mx), jnp.exp(lb_ - mx)
        lowers.append(ea / (ea + eb))
        st_refs[dirn][...] = jnp.zeros(st_refs[dirn].shape, F32)
    row_odd = (lax.broadcasted_iota(jnp.int32, (CHUNK, 1), 0) % 2) == 1

    def chunk_rows(pos):
        return pl.ds(pl.multiple_of(pos * CHUNK, CHUNK), CHUNK)

    def level_decay(dirn, lev, f, sums):
        half = CHUNK >> (lev + 1)
        cum = sums[0:CHUNK]
        if half == 1:
            return jnp.where(row_odd, 1.0, f) if dirn else jnp.where(row_odd, f, 1.0)
        if half in MXU_LEVEL_HALVES:
            blk = 1 + MXU_LEVEL_HALVES.index(half)
            return jnp.exp2(sums[blk * CHUNK:(blk + 1) * CHUNK])
        parts = []
        for p in range(0, CHUNK, 2 * half):
            ref = cum[p + half - 1 + dirn:p + half + dirn]
            lo, hi = cum[p:p + half], cum[p + half:p + 2 * half]
            parts += [lo - ref, ref - hi] if dirn else [ref - lo, hi - ref]
        return jnp.exp2(jnp.concatenate(parts, axis=0))

    def prepare(pos0, n, with_out):
        items = [(dirn, pos0 + c) for c in range(n) for dirn in range(2)]
        fs, kks, splits = [], [], []
        for dirn, pos in items:
            lower = lowers[dirn]
            f = lower + (1.0 - lower) * _sigmoid(f_refs[dirn][chunk_rows(pos), :])
            g = jnp.log(f) * LOG2E
            g1 = g.astype(BF16)
            r1 = g - g1.astype(F32)
            g2 = r1.astype(BF16)
            g3 = (r1 - g2.astype(F32)).astype(BF16)
            fs.append(f)
            kks.append(1.0 - f)
            splits.append(jnp.concatenate([g1, g2, g3], axis=0))
        sums = [None] * len(items)
        for dirn in range(2):
            idx = [j for j, it in enumerate(items) if it[0] == dirn]
            wide = _dot(cm_ref[dirn], jnp.concatenate([splits[j] for j in idx], axis=1))
            for c, j in enumerate(idx):
                sums[j] = wide[:, c * HEAD_DIM:(c + 1) * HEAD_DIM]
        for j, (dirn, pos) in enumerate(items):
            cum = sums[j][0:CHUNK]
            last = 0 if dirn else CHUNK - 1
            tot = cum[last:last + 1]
            kd_ref[dirn, chunk_rows(pos), :] = (kks[j] * jnp.exp2(tot - cum)).astype(BF16)
            et_ref[dirn, pos] = jnp.broadcast_to(jnp.exp2(tot), et_ref.shape[2:])
        if not with_out:
            return
        for j, (dirn, pos) in enumerate(items):
            q = q_ref[chunk_rows(pos), :]
            qd_ref[dirn, chunk_rows(pos), :] = (q * jnp.exp2(sums[j][0:CHUNK])).astype(BF16)
            qb, kb = q.astype(BF16), kks[j].astype(BF16)
            terms = [mk_ref[dirn, N_LEVELS] * _dot_nt(qb, kb)]
            for lev in range(N_LEVELS):
                dl = level_decay(dirn, lev, fs[j], sums[j]).astype(BF16)
                terms.append(mk_ref[dirn, lev] * _dot_nt(qb * dl, kb * dl))
            while len(terms) > 1:
                terms = [a + b for a, b in zip(terms[::2], terms[1::2])] + terms[len(terms) & ~1:]
            sc_ref[dirn, chunk_rows(pos), :] = terms[0].astype(BF16)

    def scan(pos_f, pos_b, n, with_out):
        items = [(dirn, (pos_b - c) if dirn else (pos_f + c)) for c in range(n) for dirn in range(2)]
        vs = [i_ref[chunk_rows(pos), :] for _, pos in items]
        ups = [_dot(vs[j].T.astype(BF16), kd_ref[dirn, chunk_rows(pos), :]) for j, (dirn, pos) in enumerate(items)]
        sts = [st_refs[0][...], st_refs[1][...]]
        before = []
        for j, (dirn, pos) in enumerate(items):
            before.append(sts[dirn])
            sts[dirn] = sts[dirn] * et_ref[dirn, pos][0:1] + ups[j]
        for dirn in range(2):
            st_refs[dirn][...] = sts[dirn]
        if not with_out:
            return
        for j, (dirn, pos) in enumerate(items):
            orow = chunk_rows(pos)
            o_refs[dirn][orow, :] = (_dot_nt(qd_ref[dirn, orow, :], before[j].astype(BF16))
                                     + _dot(sc_ref[dirn, orow, :], vs[j].astype(BF16)))

    def loop(n, body):
        lax.fori_loop(0, n, lambda ci, c: (body(ci), c)[1], 0)

    ctx0 = n_lat_chunks
    prepare(ctx0, n_ctx_chunks, False)
    loop(n_lat_chunks // PREP_CHUNKS,
         lambda ci: [prepare(ci * PREP_CHUNKS + g, PREP_GROUP, True) for g in range(0, PREP_CHUNKS, PREP_GROUP)])
    scan(ctx0, ctx0 + n_ctx_chunks - 1, n_ctx_chunks, False)
    loop(n_lat_chunks // SCAN_CHUNKS,
         lambda ci: [scan(ci * SCAN_CHUNKS + g, n_lat_chunks - 1 - ci * SCAN_CHUNKS - g, SCAN_GROUP, True)
                     for g in range(0, SCAN_CHUNKS, SCAN_GROUP)])

    def readout(i, carry):
        rows = pl.ds(pl.multiple_of(i * TOK, TOK), TOK)
        ot = of_ref[rows, :] + ob_ref[rows, :]
        on = ot * lax.rsqrt(jnp.mean(ot * ot, axis=-1, keepdims=True) + EPS) * ng_ref[...]
        gate = gt_ref[rows, :]
        y_ref[rows, :] = (on * (gate * _sigmoid(gate))).astype(BF16)
        return carry

    lax.fori_loop(0, n_lat_chunks * CHUNK // TOK, readout, 0, unroll=2)


def _hgrn_call(p3, lb_logits, norm_g, n_ctx, l):
    b, t, _ = p3.shape
    cm, mk = _hgrn_constants()
    cm = jnp.asarray(np.concatenate([cm] * 3, axis=-1), BF16)
    mk = jnp.asarray(mk, F32)
    lbl = lb_logits.reshape(2, 2, HEADS, 1, HEAD_DIM)
    hps = HGRN_HEADS_PER_STEP
    steps = HEADS // hps
    tok_spec = lambda grp: pl.BlockSpec((None, t, hps * HEAD_DIM), lambda i, h: (i, 0, grp * steps + h))
    full = lambda shape: pl.BlockSpec(shape, lambda i, h: (0,) * len(shape))
    return pl.pallas_call(
        functools.partial(_hgrn_kernel, n_ctx_chunks=n_ctx // CHUNK, n_lat_chunks=l // CHUNK),
        grid=(b, steps),
        in_specs=[pl.BlockSpec((2, 2, hps, 1, HEAD_DIM), lambda i, h: (0, 0, h, 0, 0)),
                  tok_spec(0), tok_spec(1), tok_spec(2), tok_spec(3), tok_spec(4),
                  full((2, 3 * CHUNK, 3 * CHUNK)),
                  full((2, N_LEVELS + 1, CHUNK, CHUNK)),
                  full((1, HEAD_DIM))],
        out_specs=pl.BlockSpec((None, l, hps * HEAD_DIM), lambda i, h: (i, 0, h)),
        out_shape=jax.ShapeDtypeStruct((b, l, MIX_W), BF16),
        scratch_shapes=[pltpu.VMEM((l, HEAD_DIM), F32), pltpu.VMEM((l, HEAD_DIM), F32),
                        pltpu.VMEM((HEAD_DIM, HEAD_DIM), F32), pltpu.VMEM((HEAD_DIM, HEAD_DIM), F32),
                        pltpu.VMEM((2, t, HEAD_DIM), BF16), pltpu.VMEM((2, t // CHUNK, 8, HEAD_DIM), F32),
                        pltpu.VMEM((2, l, HEAD_DIM), BF16), pltpu.VMEM((2, l, CHUNK), BF16)],
        compiler_params=_params(("parallel", "parallel")),
        name="hgrn",
    )(lbl, p3, p3, p3, p3, p3, cm, mk, norm_g)


def _rope_tables(l):
    pos = np.arange(l)
    nf = HEAD_DIM // 4
    inv = ROPE_THETA ** (-np.arange(nf, dtype=np.float64) / nf)
    ang_r = (pos // GRID_W)[:, None] * inv[None, :]
    ang_c = (pos % GRID_W)[:, None] * inv[None, :]
    cos = np.concatenate([np.cos(ang_r)] * 2 + [np.cos(ang_c)] * 2, axis=-1)
    sin = np.concatenate([-np.sin(ang_r), np.sin(ang_r), -np.sin(ang_c), np.sin(ang_c)], axis=-1)
    return jnp.asarray(cos, F32), jnp.asarray(sin, F32)


def _na_build_bias(rb_ref, tb_ref):
    lanes = 2 * GRID_W
    cq = lax.broadcasted_iota(jnp.int32, (GRID_W, lanes), 0)
    lane = lax.broadcasted_iota(jnp.int32, (GRID_W, lanes), 1)
    ck = lane % GRID_W
    cs = jnp.clip(cq - WIN_C // 2, 0, GRID_W - WIN_C)
    col_in = (ck >= cs) & (ck < cs + WIN_C)

    def toeplitz(d, lane0):
        row = jnp.broadcast_to(rb_ref[d:d + 1, :] * LOG2E, (GRID_W, lanes))
        return pltpu.roll(row, (lane0 - (WIN_C - 1)) % lanes, 1, stride=1, stride_axis=0)

    pair = [jnp.where(col_in, jnp.where(lane < GRID_W, toeplitz(d, 0), toeplitz(d + 1, GRID_W)), MASKED)
            for d in range(2 * WIN_R - 2)]
    for e in range(WIN_R):
        for p in range(WIN_R // 2):
            tb_ref[e, :, p * lanes:(p + 1) * lanes] = pair[2 * p - e + WIN_R - 1]


def _na_kernel(q_ref, k_ref, v_ref, cos_ref, sin_ref, rb_ref, gq_ref, gk_ref, perm_ref, o_ref,
               qs, ks, vs, kcs, vcs, tb_ref, **static):
    for hh in range(NA_HEADS_PER_STEP):
        lanes = pl.ds(hh * HEAD_DIM, HEAD_DIM)
        head = [r.at[:, lanes] for r in (q_ref, k_ref, v_ref)]
        _na_head(*head, cos_ref, sin_ref, rb_ref.at[hh], gq_ref, gk_ref, perm_ref, o_ref.at[:, lanes],
                 qs, ks, vs, kcs, vcs, tb_ref.at[hh], **static)


def _na_head(q_ref, k_ref, v_ref, cos_ref, sin_ref, rb_ref, gq_ref, gk_ref, perm_ref, o_ref,
             qs, ks, vs, kcs, vcs, tb_ref, *, n_ctx, grid_rows):
    @pl.when(pl.program_id(1) == 0)
    def _():
        _na_build_bias(rb_ref, tb_ref)

    ones = jnp.ones((2 * HEAD_DIM, HEAD_DIM), BF16)

    def normed(tv, g):
        sq = tv * tv
        hi = sq.astype(BF16)
        lo = (sq - hi.astype(F32)).astype(BF16)
        ssq = _dot(jnp.concatenate([hi, lo], axis=1), ones)
        return tv * lax.rsqrt(ssq * (1.0 / HEAD_DIM) + EPS) * g

    gq = gq_ref[...]
    gk = gk_ref[...]
    n_lat = grid_rows * GRID_W
    kcs[...] = normed(k_ref[n_lat:n_lat + n_ctx, :], gk).astype(BF16)
    vcs[...] = v_ref[n_lat:n_lat + n_ctx, :].astype(BF16)
    scale = HEAD_DIM ** -0.5 * LOG2E

    def prep(i, carry):
        rows = [pl.ds(pl.multiple_of((2 * i + n) * TOK, TOK), TOK) for n in range(2)]
        jobs = [(src_ref, g, dst, n) for n in range(2) for src_ref, g, dst in ((q_ref, gq, qs), (k_ref, gk, ks))]
        nrm = [normed(src_ref[rows[n], :], g) for src_ref, g, _, n in jobs]
        par = [_dot(tv.astype(BF16), perm_ref[...]) for tv in nrm]
        for (_, _, dst, n), tv, pv in zip(jobs, nrm, par):
            dst[rows[n], :] = (tv * cos_ref[rows[n], :] + pv * sin_ref[rows[n], :]).astype(BF16)
        for n in range(2):
            vs[rows[n], :] = v_ref[rows[n], :].astype(BF16)
        return carry

    lax.fori_loop(0, grid_rows * GRID_W // (2 * TOK), prep, 0, unroll=2)
    band = WIN_R * GRID_W

    def rows_step(i, carry):
        rr = [i * NA_ROWS + n for n in range(NA_ROWS)]
        rs = [jnp.clip(r - WIN_R // 2, 0, grid_rows - WIN_R) for r in rr]
        qrow = [pl.ds(pl.multiple_of(r * GRID_W, GRID_W), GRID_W) for r in rr]
        krow = [pl.ds(pl.multiple_of(s * GRID_W, GRID_W), band) for s in rs]
        qr = [qs[qw, :] for qw in qrow]
        kc = kcs[...]
        sb = [_dot_nt(qr[n], ks[krow[n], :]) * scale + tb_ref[rr[n] - rs[n]] for n in range(NA_ROWS)]
        sc = [_dot_nt(qr[n], kc) * scale for n in range(NA_ROWS)]
        m = [jnp.maximum(jnp.max(sb[n], axis=-1, keepdims=True), jnp.max(sc[n], axis=-1, keepdims=True))
             for n in range(NA_ROWS)]
        pb = [jnp.exp2(sb[n] - m[n]) for n in range(NA_ROWS)]
        pc = [jnp.exp2(sc[n] - m[n]) for n in range(NA_ROWS)]
        den = [jnp.sum(pb[n], axis=-1, keepdims=True) + jnp.sum(pc[n], axis=-1, keepdims=True)
               for n in range(NA_ROWS)]
        vc = vcs[...]
        o = [_dot(pb[n].astype(BF16), vs[krow[n], :]) + _dot(pc[n].astype(BF16), vc) for n in range(NA_ROWS)]
        for n in range(NA_ROWS):
            o_ref[qrow[n], :] = (o[n] / den[n]).astype(BF16)
        return carry

    lax.fori_loop(0, grid_rows // NA_ROWS, rows_step, 0)


def _na_call(p3, rel_bias, gq, gk, n_ctx, l):
    b, t, _ = p3.shape
    cos, sin = _rope_tables(l)
    nr, nc = rel_bias.shape[1:]
    rb = jnp.zeros((HEADS, 2 * WIN_R, 2 * GRID_W), F32).at[:, :nr, :nc].set(rel_bias)
    lanes = np.arange(HEAD_DIM)
    partner = np.where(lanes % (HEAD_DIM // 2) < HEAD_DIM // 4, lanes + HEAD_DIM // 4, lanes - HEAD_DIM // 4)
    perm = jnp.asarray(lanes[:, None] == partner[None, :], BF16)
    hps = NA_HEADS_PER_STEP
    steps = HEADS // hps
    col = lambda grp: (lambda h, i: (i, 0, grp * steps + h))
    tok_spec = lambda im: pl.BlockSpec((None, t, hps * HEAD_DIM), im)
    full = lambda shape: pl.BlockSpec(shape, lambda h, i: (0,) * len(shape))
    return pl.pallas_call(
        functools.partial(_na_kernel, n_ctx=n_ctx, grid_rows=l // GRID_W),
        grid=(steps, b),
        in_specs=[tok_spec(col(5)), tok_spec(col(6)), tok_spec(col(7)),
                  full((l, HEAD_DIM)), full((l, HEAD_DIM)),
                  pl.BlockSpec((hps, 2 * WIN_R, 2 * GRID_W), lambda h, i: (h, 0, 0)),
                  full((1, HEAD_DIM)), full((1, HEAD_DIM)), full((HEAD_DIM, HEAD_DIM))],
        out_specs=pl.BlockSpec((None, l, hps * HEAD_DIM), lambda h, i: (i, 0, h)),
        out_shape=jax.ShapeDtypeStruct((b, l, MIX_W), BF16),
        scratch_shapes=[pltpu.VMEM((l, HEAD_DIM), BF16), pltpu.VMEM((l, HEAD_DIM), BF16),
                        pltpu.VMEM((l, HEAD_DIM), BF16), pltpu.VMEM((n_ctx, HEAD_DIM), BF16),
                        pltpu.VMEM((n_ctx, HEAD_DIM), BF16),
                        pltpu.VMEM((hps, WIN_R, GRID_W, WIN_R * GRID_W), F32)],
        compiler_params=_params(("parallel", "arbitrary")),
        name="na",
    )(p3, p3, p3, cos, sin, rb, gq, gk, perm)


def _merge_kernel(ya_ref, yb_ref, ga_ref, gb_ref, x_ref, wa_ref, wb_ref, wo_ref, mod_ref, n2_ref,
                  xm_ref, h2_ref):
    tm = x_ref.shape[0]
    halves = [pl.ds(n * (tm // 2), tm // 2) for n in range(2)]
    za = [_dot(ya_ref[r, :], wa_ref[...]) for r in halves]
    zb = [_dot(yb_ref[r, :], wb_ref[...]) for r in halves]
    z = [(_sigmoid(ga_ref[r, :].astype(F32)) * za[n] + _sigmoid(gb_ref[r, :].astype(F32)) * zb[n]).astype(BF16)
         for n, r in enumerate(halves)]
    br = [_dot(zn, wo_ref[...]) for zn in z]
    for n, r in enumerate(halves):
        xm = x_ref[r, :] + mod_ref[0:1, :] * br[n]
        xm_ref[r, :] = xm
        y = xm * lax.rsqrt(jnp.mean(xm * xm, axis=-1, keepdims=True) + EPS) * n2_ref[...]
        h2_ref[r, :] = (y * (1.0 + mod_ref[2:3, :]) + mod_ref[1:2, :]).astype(BF16)


def _merge_call(ya, yb, pg3, x, wa, wb, wo, mod3, n2):
    b, l, d = x.shape
    tm = MERGE_ROW_TILE
    const = lambda shape: pl.BlockSpec(shape, lambda i, t: (0,) * len(shape), pipeline_mode=pl.Buffered(1))
    return pl.pallas_call(
        _merge_kernel,
        grid=(b, l // tm),
        in_specs=[pl.BlockSpec((None, tm, MIX_W), lambda i, t: (i, t, 0)),
                  pl.BlockSpec((None, tm, MIX_W), lambda i, t: (i, t, 0)),
                  pl.BlockSpec((None, tm, d), lambda i, t: (i, t, 0)),
                  pl.BlockSpec((None, tm, d), lambda i, t: (i, t, 1)),
                  pl.BlockSpec((None, tm, d), lambda i, t: (i, t, 0)),
                  const((MIX_W, d)), const((MIX_W, d)), const((d, d)),
                  pl.BlockSpec((None, 3, d), lambda i, t: (i, 0, 0)),
                  pl.BlockSpec((1, d), lambda i, t: (0, 0))],
        out_specs=[pl.BlockSpec((None, tm, d), lambda i, t: (i, t, 0)),
                   pl.BlockSpec((None, tm, d), lambda i, t: (i, t, 0))],
        out_shape=[jax.ShapeDtypeStruct((b, l, d), F32), jax.ShapeDtypeStruct((b, l, d), BF16)],
        compiler_params=_params(("parallel", "arbitrary")),
        name="merge",
    )(ya, yb, pg3, pg3, x, wa, wb, wo, mod3, n2)


def _ffn_kernel(h_ref, hp_ref, hn_ref, w1_ref, w3_ref, cw_ref, cb_ref, w2_ref, xm_ref, g2_ref, o_ref, a_ref,
                hc_ref, *, tiles_per_seq, halo):
    i = pl.program_id(0)
    s = pl.program_id(1)
    n_hid, tm, th = a_ref.shape

    @pl.when(s == 0)
    def _():
        hc_ref[0:tm, :] = h_ref[...]
        hc_ref[tm:tm + halo, :] = hp_ref[...]
        hc_ref[tm + halo:tm + 2 * halo, :] = hn_ref[...]

    @pl.when(s < n_hid)
    def _():
        t1_all = _dot(hc_ref[...], w1_ref[...])
        t1 = t1_all[0:tm]
        prev_row = t1_all[tm + halo - 1:tm + halo, :]
        next_row = t1_all[tm + halo:tm + halo + 1, :]
        prev_row = jnp.where(i % tiles_per_seq == 0, 0.0, prev_row)
        next_row = jnp.where(i % tiles_per_seq == tiles_per_seq - 1, 0.0, next_row)
        ridx = lax.broadcasted_iota(jnp.int32, (tm, 1), 0)
        up = jnp.where(ridx == 0, prev_row, pltpu.roll(t1, 1, 0))
        dn = jnp.where(ridx == tm - 1, next_row, pltpu.roll(t1, tm - 1, 0))
        u = up * cw_ref[0:1, :] + t1 * cw_ref[1:2, :] + dn * cw_ref[2:3, :] + cb_ref[...]
        a_ref[s] = ((u * _sigmoid(u)) * _dot(hc_ref[0:tm, :], w3_ref[...])).astype(BF16)

    @pl.when(s >= n_hid)
    def _():
        acc = _dot(a_ref[0], w2_ref[0:th, :])
        for k in range(1, n_hid):
            acc = acc + _dot(a_ref[k], w2_ref[k * th:(k + 1) * th, :])
        o_ref[...] = xm_ref[...] + g2_ref[...] * acc


def _ffn_hidden_tile(hid):
    return min(FFN_HIDDEN_TILE, hid)


def _ffn_call(h2, xm, w1, w3, cw, cb, w2, g2, l):
    m, d = h2.shape
    n_hid, _, th = w1.shape
    hid = n_hid * th
    tm = FFN_TOKEN_TILE
    tn = w2.shape[-1]
    halo = 16
    per = tm // halo
    nblk = m // halo
    hcol = lambda i, s: (0, jnp.minimum(s, n_hid - 1))
    htile = lambda i, s: (jnp.minimum(s, n_hid - 1), 0, 0)
    ocol = lambda s: jnp.maximum(s - n_hid, 0)
    return pl.pallas_call(
        functools.partial(_ffn_kernel, tiles_per_seq=l // tm, halo=halo),
        grid=(m // tm, n_hid + d // tn),
        in_specs=[pl.BlockSpec((tm, d), lambda i, s: (i, 0)),
                  pl.BlockSpec((halo, d), lambda i, s: (jnp.maximum(i * per - 1, 0), 0)),
                  pl.BlockSpec((halo, d), lambda i, s: (jnp.minimum((i + 1) * per, nblk - 1), 0)),
                  pl.BlockSpec((None, d, th), htile),
                  pl.BlockSpec((None, d, th), htile),
                  pl.BlockSpec((3, th), hcol),
                  pl.BlockSpec((1, th), hcol),
                  pl.BlockSpec((None, hid, tn), lambda i, s: (ocol(s), 0, 0)),
                  pl.BlockSpec((tm, tn), lambda i, s: (i, ocol(s))),
                  pl.BlockSpec((None, 1, tn), lambda i, s: (i // (l // tm), 0, ocol(s)))],
        out_specs=pl.BlockSpec((tm, tn), lambda i, s: (i, ocol(s))),
        out_shape=jax.ShapeDtypeStruct((m, d), F32),
        scratch_shapes=[pltpu.VMEM((n_hid, tm, th), BF16), pltpu.VMEM((tm + 2 * halo, d), BF16)],
        compiler_params=_params(("parallel", "arbitrary")),
        name="ffn",
    )(h2, h2, h2, w1, w3, cw, cb, w2, xm, g2)


def kernel(x, c, ctx, c_ctx, ada_w, ada_b, norm1_g, norm2_g, w_in, hgrn_lb_logits, hgrn_norm_g,
           na_q_norm_g, na_k_norm_g, na_rel_bias, w_branch_a, w_branch_b, w_out,
           ffn_w1, ffn_w3, ffn_conv_w, ffn_conv_b, ffn_w2):
    b, l, d = x.shape
    n_ctx = ctx.shape[1]
    assert ada_w.shape[0] == 1 and b + 1 <= 8 and n_ctx % TOK == 0 and l % (2 * TOK) == 0

    c_all = jnp.zeros((8, d), F32).at[:b].set(c).at[b].set(c_ctx)
    mod = _mod_call(c_all, ada_w[0], ada_b).reshape(8, N_MOD, d)
    sh1, sc1, g1, sh2, sc2, g2 = (mod[:b, n] for n in range(N_MOD))
    ss_ctx = jnp.broadcast_to(mod[b, 0:2][None], (b, 2, d))
    ss1 = jnp.stack([jnp.stack([sh1, sc1], axis=1), ss_ctx], axis=1)

    h = _prenorm_call(x, ctx, norm1_g, ss1)
    t = l + n_ctx
    h = h.reshape(b * t, d)
    n_mix = 8 * MIX_W
    th = _ffn_hidden_tile(ffn_w1.shape[-1])
    p, (w1, w3, w2) = _inproj_call(h, w_in[0], 0, n_mix, F32, "inproj",
                                   ((ffn_w1[0], th), (ffn_w3[0], th), (ffn_w2[0], FFN_OUT_TILE)))
    pg, (wa, wb, wo) = _inproj_call(h, w_in[0], n_mix, 2 * d, BF16, "inproj_gates",
                                    ((w_branch_a[0], d), (w_branch_b[0], d), (w_out[0], d)))
    wa, wb, wo = (wt.reshape(wt.shape[1:]) for wt in (wa, wb, wo))
    p3 = p.reshape(b, t, n_mix)
    pg3 = pg.reshape(b, t, 2 * d)

    y_a = _hgrn_call(p3, hgrn_lb_logits, hgrn_norm_g, n_ctx, l)
    y_b = _na_call(p3, na_rel_bias[0], na_q_norm_g, na_k_norm_g, n_ctx, l)

    mod3 = jnp.stack([g1, sh2, sc2], axis=1)
    x_mid, h2 = _merge_call(y_a, y_b, pg3, x, wa, wb, wo, mod3, norm2_g)

    out = _ffn_call(h2.reshape(b * l, d), x_mid.reshape(b * l, d), w1, w3, ffn_conv_w[0], ffn_conv_b, w2,
                    g2[:, None, :], l)
    return out.reshape(b, l, d)
```

```python
import functools

import numpy as np
import jax
import jax.numpy as jnp
from jax import lax
from jax.experimental import pallas as pl
from jax.experimental.pallas import tpu as pltpu

GRID_W = 64
HEADS = 8
HEAD_DIM = 128
MIX_W = HEADS * HEAD_DIM
CHUNK = 64
N_LEVELS = 6
MXU_LEVEL_HALVES = (4, 2)
PREP_CHUNKS = 32
PREP_GROUP = 4
SCAN_CHUNKS = 32
SCAN_GROUP = 4
HGRN_HEADS_PER_STEP = 2
NA_HEADS_PER_STEP = 2
NA_ROWS = 16
WIN_R = 8
WIN_C = 16
ROPE_THETA = 10000.0
N_MOD = 6
EPS = 1e-6
LOG2E = 1.4426950408889634
TOK = 256
MOD_COL_TILE = 1024
INPROJ_ROW_TILE = 1024
INPROJ_COL_TILE = 1024
MERGE_ROW_TILE = 2 * TOK
FFN_TOKEN_TILE = 1024
FFN_HIDDEN_TILE = 512
FFN_OUT_TILE = 512
MASKED = -1e30
V7X_VMEM_LIMIT = 60 * 1024 * 1024

BF16 = jnp.bfloat16
F32 = jnp.float32


def _dot(a, b):
    return jnp.dot(a, b, preferred_element_type=F32)


def _dot_nt(a, b):
    return lax.dot_general(a, b, (((1,), (1,)), ((), ())), preferred_element_type=F32)


def _sigmoid(t):
    return 1.0 / (1.0 + jnp.exp(-t))


def _params(sem):
    return pltpu.CompilerParams(dimension_semantics=sem, vmem_limit_bytes=V7X_VMEM_LIMIT)


def _mod_kernel(c_ref, w_ref, b_ref, o_ref):
    cv = c_ref[...]
    s = (cv * _sigmoid(cv)).astype(BF16)
    o_ref[...] = _dot(s, w_ref[...].astype(BF16)) + b_ref[...]


def _mod_call(c_all, w, b):
    d, n = w.shape
    tn = min(MOD_COL_TILE, d)
    return pl.pallas_call(
        _mod_kernel,
        grid=(n // tn,),
        in_specs=[pl.BlockSpec((8, d), lambda j: (0, 0)),
                  pl.BlockSpec((d, tn), lambda j: (0, j)),
                  pl.BlockSpec((1, tn), lambda j: (0, j))],
        out_specs=pl.BlockSpec((8, tn), lambda j: (0, j)),
        out_shape=jax.ShapeDtypeStruct((8, n), F32),
        compiler_params=_params(("arbitrary",)),
        name="mod",
    )(c_all, w, b)


def _prenorm_kernel(x_ref, ctx_ref, g_ref, ss_ref, o_ref, *, n_lat_blk):
    t = pl.program_id(1)
    xv = jnp.where(t < n_lat_blk, x_ref[...], ctx_ref[...])
    y = xv * lax.rsqrt(jnp.mean(xv * xv, axis=-1, keepdims=True) + EPS) * g_ref[...]
    o_ref[...] = (y * (1.0 + ss_ref[1:2, :]) + ss_ref[0:1, :]).astype(BF16)


def _prenorm_call(x, ctx, g, ss):
    b, l, d = x.shape
    nlb = l // TOK
    nt = nlb + ctx.shape[1] // TOK
    return pl.pallas_call(
        functools.partial(_prenorm_kernel, n_lat_blk=nlb),
        grid=(b, nt),
        in_specs=[pl.BlockSpec((None, TOK, d), lambda i, t: (i, jnp.minimum(t, nlb - 1), 0)),
                  pl.BlockSpec((None, TOK, d), lambda i, t: (i, jnp.maximum(t - nlb, 0), 0)),
                  pl.BlockSpec((1, d), lambda i, t: (0, 0)),
                  pl.BlockSpec((None, None, 2, d), lambda i, t: (i, (t >= nlb).astype(jnp.int32), 0, 0))],
        out_specs=pl.BlockSpec((None, TOK, d), lambda i, t: (i, t, 0)),
        out_shape=jax.ShapeDtypeStruct((b, nt * TOK, d), BF16),
        compiler_params=_params(("parallel", "arbitrary")),
        name="prenorm",
    )(x, ctx, g, ss)


def _inproj_kernel(*refs, n_cast):
    h_ref, w_ref = refs[:2]
    cast_in = refs[2:2 + n_cast]
    o_ref = refs[2 + n_cast]
    cast_out = refs[3 + n_cast:3 + 2 * n_cast]
    wb_ref = refs[-1]

    @pl.when(pl.program_id(1) == 0)
    def _():
        wb_ref[...] = w_ref[...].astype(BF16)

    o_ref[...] = _dot(h_ref[...], wb_ref[...]).astype(o_ref.dtype)
    for src, dst in zip(cast_in, cast_out):
        ct = dst.shape[-1]
        for c in range(dst.shape[0]):
            dst[c] = src[:, c * ct:(c + 1) * ct].astype(BF16)


def _slab_rows(rows, steps):
    return next(r for r in range(32, rows + 1, 32) if rows % r == 0 and rows // r <= steps)


def _inproj_call(h, w, col0, n, out_dtype, name, to_cast=()):
    m, d = h.shape
    tm = INPROJ_ROW_TILE if m % INPROJ_ROW_TILE == 0 else TOK
    tn = min(INPROJ_COL_TILE, d)
    j0 = col0 // tn
    ni = m // tm
    steps = (n // tn) * ni
    in_slabs, out_slabs, out_shapes = [], [], []
    for a, ct in to_cast:
        rows, cols = a.shape
        r = _slab_rows(rows, steps)
        slab = lambda j, i, last=rows // r - 1: jnp.minimum(j * ni + i, last)
        in_slabs.append(pl.BlockSpec((r, cols), lambda j, i, slab=slab: (slab(j, i), 0)))
        out_slabs.append(pl.BlockSpec((cols // ct, r, ct), lambda j, i, slab=slab: (0, slab(j, i), 0)))
        out_shapes.append(jax.ShapeDtypeStruct((cols // ct, rows, ct), BF16))
    outs = pl.pallas_call(
        functools.partial(_inproj_kernel, n_cast=len(to_cast)),
        grid=(n // tn, ni),
        in_specs=[pl.BlockSpec((tm, d), lambda j, i: (i, 0)),
                  pl.BlockSpec((d, tn), lambda j, i: (0, j + j0))] + in_slabs,
        out_specs=[pl.BlockSpec((tm, tn), lambda j, i: (i, j))] + out_slabs,
        out_shape=[jax.ShapeDtypeStruct((m, n), out_dtype)] + out_shapes,
        scratch_shapes=[pltpu.VMEM((d, tn), BF16)],
        compiler_params=_params(("arbitrary", "arbitrary")),
        name=name,
    )(h, w, *[a for a, _ in to_cast])
    return outs[0], outs[1:]


def _hgrn_constants():
    u = np.arange(CHUNK)
    mats, masks = [], []
    for rev in (False, True):
        pos = CHUNK - 1 - u if rev else u
        incl = (pos[None, :] <= pos[:, None]).astype(np.float32)
        blocks, lvl_masks = [incl], []
        for lev in range(N_LEVELS):
            half = CHUNK >> (lev + 1)
            ref = (pos // (2 * half)) * (2 * half) + half - 1
            upto_ref = (pos[None, :] <= ref[:, None]).astype(np.float32)
            if half in MXU_LEVEL_HALVES:
                late = (pos % (2 * half)) >= half
                blocks.append(np.where(late[:, None], 1.0, -1.0).astype(np.float32) * (incl - upto_ref))
            same = (pos[:, None] // (2 * half)) == (pos[None, :] // (2 * half))
            late_q = (pos[:, None] % (2 * half)) >= half
            early_k = (pos[None, :] % (2 * half)) < half
            lvl_masks.append((same & late_q & early_k).astype(np.float32))
        lvl_masks.append(np.eye(CHUNK, dtype=np.float32))
        mats.append(np.concatenate(blocks, axis=0))
        masks.append(np.stack(lvl_masks))
    return np.stack(mats), np.stack(masks)


def _hgrn_kernel(lbl_ref, q_ref, ff_ref, fb_ref, i_ref, gt_ref, cm_ref, mk_ref, ng_ref, y_ref, *scratch, **chunks):
    for hh in range(HGRN_HEADS_PER_STEP):
        lanes = pl.ds(hh * HEAD_DIM, HEAD_DIM)
        head = [r.at[:, lanes] for r in (q_ref, ff_ref, fb_ref, i_ref, gt_ref)]
        _hgrn_head(lbl_ref.at[:, :, hh], *head, cm_ref, mk_ref, ng_ref, y_ref.at[:, lanes], *scratch, **chunks)


def _hgrn_head(lbl_ref, q_ref, ff_ref, fb_ref, i_ref, gt_ref, cm_ref, mk_ref, ng_ref, y_ref,
               of_ref, ob_ref, sf_ref, sb_ref, kd_ref, et_ref, qd_ref, sc_ref, *, n_ctx_chunks, n_lat_chunks):
    f_refs, o_refs, st_refs = (ff_ref, fb_ref), (of_ref, ob_ref), (sf_ref, sb_ref)
    lowers = []
    for dirn in range(2):
        la, lb_ = lbl_ref[dirn, 0], lbl_ref[dirn, 1]
        mx = jnp.maximum(la, lb_)
        ea, eb = jnp.exp(la - mx), jnp.exp(lb_ - mx)
        lowers.append(ea / (ea + eb))
        st_refs[dirn][...] = jnp.zeros(st_refs[dirn].shape, F32)
    row_odd = (lax.broadcasted_iota(jnp.int32, (CHUNK, 1), 0) % 2) == 1

    def chunk_rows(pos):
        return pl.ds(pl.multiple_of(pos * CHUNK, CHUNK), CHUNK)

    def level_decay(dirn, lev, f, sums):
        half = CHUNK >> (lev + 1)
        cum = sums[0:CHUNK]
        if half == 1:
            return jnp.where(row_odd, 1.0, f) if dirn else jnp.where(row_odd, f, 1.0)
        if half in MXU_LEVEL_HALVES:
            blk = 1 + MXU_LEVEL_HALVES.index(half)
            return jnp.exp2(sums[blk * CHUNK:(blk + 1) * CHUNK])
        parts = []
        for p in range(0, CHUNK, 2 * half):
            ref = cum[p + half - 1 + dirn:p + half + dirn]
            lo, hi = cum[p:p + half], cum[p + half:p + 2 * half]
            parts += [lo - ref, ref - hi] if dirn else [ref - lo, hi - ref]
        return jnp.exp2(jnp.concatenate(parts, axis=0))

    def prepare(pos0, n, with_out):
        items = [(dirn, pos0 + c) for c in range(n) for dirn in range(2)]
        fs, kks, splits = [], [], []
        for dirn, pos in items:
            lower = lowers[dirn]
            f = lower + (1.0 - lower) * _sigmoid(f_refs[dirn][chunk_rows(pos), :])
            g = jnp.log(f) * LOG2E
            g1 = g.astype(BF16)
            r1 = g - g1.astype(F32)
            g2 = r1.astype(BF16)
            g3 = (r1 - g2.astype(F32)).astype(BF16)
            fs.append(f)
            kks.append(1.0 - f)
            splits.append(jnp.concatenate([g1, g2, g3], axis=0))
        sums = [None] * len(items)
        for dirn in range(2):
            idx = [j for j, it in enumerate(items) if it[0] == dirn]
            wide = _dot(cm_ref[dirn], jnp.concatenate([splits[j] for j in idx], axis=1))
            for c, j in enumerate(idx):
                sums[j] = wide[:, c * HEAD_DIM:(c + 1) * HEAD_DIM]
        for j, (dirn, pos) in enumerate(items):
            cum = sums[j][0:CHUNK]
            last = 0 if dirn else CHUNK - 1
            tot = cum[last:last + 1]
            kd_ref[dirn, chunk_rows(pos), :] = (kks[j] * jnp.exp2(tot - cum)).astype(BF16)
            et_ref[dirn, pos] = jnp.broadcast_to(jnp.exp2(tot), et_ref.shape[2:])
        if not with_out:
            return
        for j, (dirn, pos) in enumerate(items):
            q = q_ref[chunk_rows(pos), :]
            qd_ref[dirn, chunk_rows(pos), :] = (q * jnp.exp2(sums[j][0:CHUNK])).astype(BF16)
            qb, kb = q.astype(BF16), kks[j].astype(BF16)
            terms = [mk_ref[dirn, N_LEVELS] * _dot_nt(qb, kb)]
            for lev in range(N_LEVELS):
                dl = level_decay(dirn, lev, fs[j], sums[j]).astype(BF16)
                terms.append(mk_ref[dirn, lev] * _dot_nt(qb * dl, kb * dl))
            while len(terms) > 1:
                terms = [a + b for a, b in zip(terms[::2], terms[1::2])] + terms[len(terms) & ~1:]
            sc_ref[dirn, chunk_rows(pos), :] = terms[0].astype(BF16)

    def scan(pos_f, pos_b, n, with_out):
        items = [(dirn, (pos_b - c) if dirn else (pos_f + c)) for c in range(n) for dirn in range(2)]
        vs = [i_ref[chunk_rows(pos), :] for _, pos in items]
        ups = [_dot(vs[j].T.astype(BF16), kd_ref[dirn, chunk_rows(pos), :]) for j, (dirn, pos) in enumerate(items)]
        sts = [st_refs[0][...], st_refs[1][...]]
        before = []
        for j, (dirn, pos) in enumerate(items):
            before.append(sts[dirn])
            sts[dirn] = sts[dirn] * et_ref[dirn, pos][0:1] + ups[j]
        for dirn in range(2):
            st_refs[dirn][...] = sts[dirn]
        if not with_out:
            return
        for j, (dirn, pos) in enumerate(items):
            orow = chunk_rows(pos)
            o_refs[dirn][orow, :] = (_dot_nt(qd_ref[dirn, orow, :], before[j].astype(BF16))
                                     + _dot(sc_ref[dirn, orow, :], vs[j].astype(BF16)))

    def loop(n, body):
        lax.fori_loop(0, n, lambda ci, c: (body(ci), c)[1], 0)

    ctx0 = n_lat_chunks
    prepare(ctx0, n_ctx_chunks, False)
    loop(n_lat_chunks // PREP_CHUNKS,
         lambda ci: [prepare(ci * PREP_CHUNKS + g, PREP_GROUP, True) for g in range(0, PREP_CHUNKS, PREP_GROUP)])
    scan(ctx0, ctx0 + n_ctx_chunks - 1, n_ctx_chunks, False)
    loop(n_lat_chunks // SCAN_CHUNKS,
         lambda ci: [scan(ci * SCAN_CHUNKS + g, n_lat_chunks - 1 - ci * SCAN_CHUNKS - g, SCAN_GROUP, True)
                     for g in range(0, SCAN_CHUNKS, SCAN_GROUP)])

    def readout(i, carry):
        rows = pl.ds(pl.multiple_of(i * TOK, TOK), TOK)
        ot = of_ref[rows, :] + ob_ref[rows, :]
        on = ot * lax.rsqrt(jnp.mean(ot * ot, axis=-1, keepdims=True) + EPS) * ng_ref[...]
        gate = gt_ref[rows, :]
        y_ref[rows, :] = (on * (gate * _sigmoid(gate))).astype(BF16)
        return carry

    lax.fori_loop(0, n_lat_chunks * CHUNK // TOK, readout, 0, unroll=2)


def _hgrn_call(p3, lb_logits, norm_g, n_ctx, l):
    b, t, _ = p3.shape
    cm, mk = _hgrn_constants()
    cm = jnp.asarray(np.concatenate([cm] * 3, axis=-1), BF16)
    mk = jnp.asarray(mk, F32)
    lbl = lb_logits.reshape(2, 2, HEADS, 1, HEAD_DIM)
    hps = HGRN_HEADS_PER_STEP
    steps = HEADS // hps
    tok_spec = lambda grp: pl.BlockSpec((None, t, hps * HEAD_DIM), lambda i, h: (i, 0, grp * steps + h))
    full = lambda shape: pl.BlockSpec(shape, lambda i, h: (0,) * len(shape))
    return pl.pallas_call(
        functools.partial(_hgrn_kernel, n_ctx_chunks=n_ctx // CHUNK, n_lat_chunks=l // CHUNK),
        grid=(b, steps),
        in_specs=[pl.BlockSpec((2, 2, hps, 1, HEAD_DIM), lambda i, h: (0, 0, h, 0, 0)),
                  tok_spec(0), tok_spec(1), tok_spec(2), tok_spec(3), tok_spec(4),
                  full((2, 3 * CHUNK, 3 * CHUNK)),
                  full((2, N_LEVELS + 1, CHUNK, CHUNK)),
                  full((1, HEAD_DIM))],
        out_specs=pl.BlockSpec((None, l, hps * HEAD_DIM), lambda i, h: (i, 0, h)),
        out_shape=jax.ShapeDtypeStruct((b, l, MIX_W), BF16),
        scratch_shapes=[pltpu.VMEM((l, HEAD_DIM), F32), pltpu.VMEM((l, HEAD_DIM), F32),
                        pltpu.VMEM((HEAD_DIM, HEAD_DIM), F32), pltpu.VMEM((HEAD_DIM, HEAD_DIM), F32),
                        pltpu.VMEM((2, t, HEAD_DIM), BF16), pltpu.VMEM((2, t // CHUNK, 8, HEAD_DIM), F32),
                        pltpu.VMEM((2, l, HEAD_DIM), BF16), pltpu.VMEM((2, l, CHUNK), BF16)],
        compiler_params=_params(("parallel", "parallel")),
        name="hgrn",
    )(lbl, p3, p3, p3, p3, p3, cm, mk, norm_g)


def _rope_tables(l):
    pos = np.arange(l)
    nf = HEAD_DIM // 4
    inv = ROPE_THETA ** (-np.arange(nf, dtype=np.float64) / nf)
    ang_r = (pos // GRID_W)[:, None] * inv[None, :]
    ang_c = (pos % GRID_W)[:, None] * inv[None, :]
    cos = np.concatenate([np.cos(ang_r)] * 2 + [np.cos(ang_c)] * 2, axis=-1)
    sin = np.concatenate([-np.sin(ang_r), np.sin(ang_r), -np.sin(ang_c), np.sin(ang_c)], axis=-1)
    return jnp.asarray(cos, F32), jnp.asarray(sin, F32)


def _na_build_bias(rb_ref, tb_ref):
    lanes = 2 * GRID_W
    cq = lax.broadcasted_iota(jnp.int32, (GRID_W, lanes), 0)
    lane = lax.broadcasted_iota(jnp.int32, (GRID_W, lanes), 1)
    ck = lane % GRID_W
    cs = jnp.clip(cq - WIN_C // 2, 0, GRID_W - WIN_C)
    col_in = (ck >= cs) & (ck < cs + WIN_C)

    def toeplitz(d, lane0):
        row = jnp.broadcast_to(rb_ref[d:d + 1, :] * LOG2E, (GRID_W, lanes))
        return pltpu.roll(row, (lane0 - (WIN_C - 1)) % lanes, 1, stride=1, stride_axis=0)

    pair = [jnp.where(col_in, jnp.where(lane < GRID_W, toeplitz(d, 0), toeplitz(d + 1, GRID_W)), MASKED)
            for d in range(2 * WIN_R - 2)]
    for e in range(WIN_R):
        for p in range(WIN_R // 2):
            tb_ref[e, :, p * lanes:(p + 1) * lanes] = pair[2 * p - e + WIN_R - 1]


def _na_kernel(q_ref, k_ref, v_ref, cos_ref, sin_ref, rb_ref, gq_ref, gk_ref, perm_ref, o_ref,
               qs, ks, vs, kcs, vcs, tb_ref, **static):
    for hh in range(NA_HEADS_PER_STEP):
        lanes = pl.ds(hh * HEAD_DIM, HEAD_DIM)
        head = [r.at[:, lanes] for r in (q_ref, k_ref, v_ref)]
        _na_head(*head, cos_ref, sin_ref, rb_ref.at[hh], gq_ref, gk_ref, perm_ref, o_ref.at[:, lanes],
                 qs, ks, vs, kcs, vcs, tb_ref.at[hh], **static)


def _na_head(q_ref, k_ref, v_ref, cos_ref, sin_ref, rb_ref, gq_ref, gk_ref, perm_ref, o_ref,
             qs, ks, vs, kcs, vcs, tb_ref, *, n_ctx, grid_rows):
    @pl.when(pl.program_id(1) == 0)
    def _():
        _na_build_bias(rb_ref, tb_ref)

    ones = jnp.ones((2 * HEAD_DIM, HEAD_DIM), BF16)

    def normed(tv, g):
        sq = tv * tv
        hi = sq.astype(BF16)
        lo = (sq - hi.astype(F32)).astype(BF16)
        ssq = _dot(jnp.concatenate([hi, lo], axis=1), ones)
        return tv * lax.rsqrt(ssq * (1.0 / HEAD_DIM) + EPS) * g

    gq = gq_ref[...]
    gk = gk_ref[...]
    n_lat = grid_rows * GRID_W
    kcs[...] = normed(k_ref[n_lat:n_lat + n_ctx, :], gk).astype(BF16)
    vcs[...] = v_ref[n_lat:n_lat + n_ctx, :].astype(BF16)
    scale = HEAD_DIM ** -0.5 * LOG2E

    def prep(i, carry):
        rows = [pl.ds(pl.multiple_of((2 * i + n) * TOK, TOK), TOK) for n in range(2)]
        jobs = [(src_ref, g, dst, n) for n in range(2) for src_ref, g, dst in ((q_ref, gq, qs), (k_ref, gk, ks))]
        nrm = [normed(src_ref[rows[n], :], g) for src_ref, g, _, n in jobs]
        par = [_dot(tv.astype(BF16), perm_ref[...]) for tv in nrm]
        for (_, _, dst, n), tv, pv in zip(jobs, nrm, par):
            dst[rows[n], :] = (tv * cos_ref[rows[n], :] + pv * sin_ref[rows[n], :]).astype(BF16)
        for n in range(2):
            vs[rows[n], :] = v_ref[rows[n], :].astype(BF16)
        return carry

    lax.fori_loop(0, grid_rows * GRID_W // (2 * TOK), prep, 0, unroll=2)
    band = WIN_R * GRID_W

    def rows_step(i, carry):
        rr = [i * NA_ROWS + n for n in range(NA_ROWS)]
        rs = [jnp.clip(r - WIN_R // 2, 0, grid_rows - WIN_R) for r in rr]
        qrow = [pl.ds(pl.multiple_of(r * GRID_W, GRID_W), GRID_W) for r in rr]
        krow = [pl.ds(pl.multiple_of(s * GRID_W, GRID_W), band) for s in rs]
        qr = [qs[qw, :] for qw in qrow]
        kc = kcs[...]
        sb = [_dot_nt(qr[n], ks[krow[n], :]) * scale + tb_ref[rr[n] - rs[n]] for n in range(NA_ROWS)]
        sc = [_dot_nt(qr[n], kc) * scale for n in range(NA_ROWS)]
        m = [jnp.maximum(jnp.max(sb[n], axis=-1, keepdims=True), jnp.max(sc[n], axis=-1, keepdims=True))
             for n in range(NA_ROWS)]
        pb = [jnp.exp2(sb[n] - m[n]) for n in range(NA_ROWS)]
        pc = [jnp.exp2(sc[n] - m[n]) for n in range(NA_ROWS)]
        den = [jnp.sum(pb[n], axis=-1, keepdims=True) + jnp.sum(pc[n], axis=-1, keepdims=True)
               for n in range(NA_ROWS)]
        vc = vcs[...]
        o = [_dot(pb[n].astype(BF16), vs[krow[n], :]) + _dot(pc[n].astype(BF16), vc) for n in range(NA_ROWS)]
        for n in range(NA_ROWS):
            o_ref[qrow[n], :] = (o[n] / den[n]).astype(BF16)
        return carry

    lax.fori_loop(0, grid_rows // NA_ROWS, rows_step, 0)


def _na_call(p3, rel_bias, gq, gk, n_ctx, l):
    b, t, _ = p3.shape
    cos, sin = _rope_tables(l)
    nr, nc = rel_bias.shape[1:]
    rb = jnp.zeros((HEADS, 2 * WIN_R, 2 * GRID_W), F32).at[:, :nr, :nc].set(rel_bias)
    lanes = np.arange(HEAD_DIM)
    partner = np.where(lanes % (HEAD_DIM // 2) < HEAD_DIM // 4, lanes + HEAD_DIM // 4, lanes - HEAD_DIM // 4)
    perm = jnp.asarray(lanes[:, None] == partner[None, :], BF16)
    hps = NA_HEADS_PER_STEP
    steps = HEADS // hps
    col = lambda grp: (lambda h, i: (i, 0, grp * steps + h))
    tok_spec = lambda im: pl.BlockSpec((None, t, hps * HEAD_DIM), im)
    full = lambda shape: pl.BlockSpec(shape, lambda h, i: (0,) * len(shape))
    return pl.pallas_call(
        functools.partial(_na_kernel, n_ctx=n_ctx, grid_rows=l // GRID_W),
        grid=(steps, b),
        in_specs=[tok_spec(col(5)), tok_spec(col(6)), tok_spec(col(7)),
                  full((l, HEAD_DIM)), full((l, HEAD_DIM)),
                  pl.BlockSpec((hps, 2 * WIN_R, 2 * GRID_W), lambda h, i: (h, 0, 0)),
                  full((1, HEAD_DIM)), full((1, HEAD_DIM)), full((HEAD_DIM, HEAD_DIM))],
        out_specs=pl.BlockSpec((None, l, hps * HEAD_DIM), lambda h, i: (i, 0, h)),
        out_shape=jax.ShapeDtypeStruct((b, l, MIX_W), BF16),
        scratch_shapes=[pltpu.VMEM((l, HEAD_DIM), BF16), pltpu.VMEM((l, HEAD_DIM), BF16),
                        pltpu.VMEM((l, HEAD_DIM), BF16), pltpu.VMEM((n_ctx, HEAD_DIM), BF16),
                        pltpu.VMEM((n_ctx, HEAD_DIM), BF16),
                        pltpu.VMEM((hps, WIN_R, GRID_W, WIN_R * GRID_W), F32)],
        compiler_params=_params(("parallel", "arbitrary")),
        name="na",
    )(p3, p3, p3, cos, sin, rb, gq, gk, perm)


def _merge_kernel(ya_ref, yb_ref, ga_ref, gb_ref, x_ref, wa_ref, wb_ref, wo_ref, mod_ref, n2_ref,
                  xm_ref, h2_ref):
    tm = x_ref.shape[0]
    halves = [pl.ds(n * (tm // 2), tm // 2) for n in range(2)]
    za = [_dot(ya_ref[r, :], wa_ref[...]) for r in halves]
    zb = [_dot(yb_ref[r, :], wb_ref[...]) for r in halves]
    z = [(_sigmoid(ga_ref[r, :].astype(F32)) * za[n] + _sigmoid(gb_ref[r, :].astype(F32)) * zb[n]).astype(BF16)
         for n, r in enumerate(halves)]
    br = [_dot(zn, wo_ref[...]) for zn in z]
    for n, r in enumerate(halves):
        xm = x_ref[r, :] + mod_ref[0:1, :] * br[n]
        xm_ref[r, :] = xm
        y = xm * lax.rsqrt(jnp.mean(xm * xm, axis=-1, keepdims=True) + EPS) * n2_ref[...]
        h2_ref[r, :] = (y * (1.0 + mod_ref[2:3, :]) + mod_ref[1:2, :]).astype(BF16)


def _merge_call(ya, yb, pg3, x, wa, wb, wo, mod3, n2):
    b, l, d = x.shape
    tm = MERGE_ROW_TILE
    const = lambda shape: pl.BlockSpec(shape, lambda i, t: (0,) * len(shape), pipeline_mode=pl.Buffered(1))
    return pl.pallas_call(
        _merge_kernel,
        grid=(b, l // tm),
        in_specs=[pl.BlockSpec((None, tm, MIX_W), lambda i, t: (i, t, 0)),
                  pl.BlockSpec((None, tm, MIX_W), lambda i, t: (i, t, 0)),
                  pl.BlockSpec((None, tm, d), lambda i, t: (i, t, 0)),
                  pl.BlockSpec((None, tm, d), lambda i, t: (i, t, 1)),
                  pl.BlockSpec((None, tm, d), lambda i, t: (i, t, 0)),
                  const((MIX_W, d)), const((MIX_W, d)), const((d, d)),
                  pl.BlockSpec((None, 3, d), lambda i, t: (i, 0, 0)),
                  pl.BlockSpec((1, d), lambda i, t: (0, 0))],
        out_specs=[pl.BlockSpec((None, tm, d), lambda i, t: (i, t, 0)),
                   pl.BlockSpec((None, tm, d), lambda i, t: (i, t, 0))],
        out_shape=[jax.ShapeDtypeStruct((b, l, d), F32), jax.ShapeDtypeStruct((b, l, d), BF16)],
        compiler_params=_params(("parallel", "arbitrary")),
        name="merge",
    )(ya, yb, pg3, pg3, x, wa, wb, wo, mod3, n2)


def _ffn_kernel(h_ref, hp_ref, hn_ref, w1_ref, w3_ref, cw_ref, cb_ref, w2_ref, xm_ref, g2_ref, o_ref, a_ref,
                hc_ref, *, tiles_per_seq, halo):
    i = pl.program_id(0)
    s = pl.program_id(1)
    n_hid, tm, th = a_ref.shape

    @pl.when(s == 0)
    def _():
        hc_ref[0:tm, :] = h_ref[...]
        hc_ref[tm:tm + halo, :] = hp_ref[...]
        hc_ref[tm + halo:tm + 2 * halo, :] = hn_ref[...]

    @pl.when(s < n_hid)
    def _():
        t1_all = _dot(hc_ref[...], w1_ref[...])
        t1 = t1_all[0:tm]
        prev_row = t1_all[tm + halo - 1:tm + halo, :]
        next_row = t1_all[tm + halo:tm + halo + 1, :]
        prev_row = jnp.where(i % tiles_per_seq == 0, 0.0, prev_row)
        next_row = jnp.where(i % tiles_per_seq == tiles_per_seq - 1, 0.0, next_row)
        ridx = lax.broadcasted_iota(jnp.int32, (tm, 1), 0)
        up = jnp.where(ridx == 0, prev_row, pltpu.roll(t1, 1, 0))
        dn = jnp.where(ridx == tm - 1, next_row, pltpu.roll(t1, tm - 1, 0))
        u = up * cw_ref[0:1, :] + t1 * cw_ref[1:2, :] + dn * cw_ref[2:3, :] + cb_ref[...]
        a_ref[s] = ((u * _sigmoid(u)) * _dot(hc_ref[0:tm, :], w3_ref[...])).astype(BF16)

    @pl.when(s >= n_hid)
    def _():
        acc = _dot(a_ref[0], w2_ref[0:th, :])
        for k in range(1, n_hid):
            acc = acc + _dot(a_ref[k], w2_ref[k * th:(k + 1) * th, :])
        o_ref[...] = xm_ref[...] + g2_ref[...] * acc


def _ffn_hidden_tile(hid):
    return min(FFN_HIDDEN_TILE, hid)


def _ffn_call(h2, xm, w1, w3, cw, cb, w2, g2, l):
    m, d = h2.shape
    n_hid, _, th = w1.shape
    hid = n_hid * th
    tm = FFN_TOKEN_TILE
    tn = w2.shape[-1]
    halo = 16
    per = tm // halo
    nblk = m // halo
    hcol = lambda i, s: (0, jnp.minimum(s, n_hid - 1))
    htile = lambda i, s: (jnp.minimum(s, n_hid - 1), 0, 0)
    ocol = lambda s: jnp.maximum(s - n_hid, 0)
    return pl.pallas_call(
        functools.partial(_ffn_kernel, tiles_per_seq=l // tm, halo=halo),
        grid=(m // tm, n_hid + d // tn),
        in_specs=[pl.BlockSpec((tm, d), lambda i, s: (i, 0)),
                  pl.BlockSpec((halo, d), lambda i, s: (jnp.maximum(i * per - 1, 0), 0)),
                  pl.BlockSpec((halo, d), lambda i, s: (jnp.minimum((i + 1) * per, nblk - 1), 0)),
                  pl.BlockSpec((None, d, th), htile),
                  pl.BlockSpec((None, d, th), htile),
                  pl.BlockSpec((3, th), hcol),
                  pl.BlockSpec((1, th), hcol),
                  pl.BlockSpec((None, hid, tn), lambda i, s: (ocol(s), 0, 0)),
                  pl.BlockSpec((tm, tn), lambda i, s: (i, ocol(s))),
                  pl.BlockSpec((None, 1, tn), lambda i, s: (i // (l // tm), 0, ocol(s)))],
        out_specs=pl.BlockSpec((tm, tn), lambda i, s: (i, ocol(s))),
        out_shape=jax.ShapeDtypeStruct((m, d), F32),
        scratch_shapes=[pltpu.VMEM((n_hid, tm, th), BF16), pltpu.VMEM((tm + 2 * halo, d), BF16)],
        compiler_params=_params(("parallel", "arbitrary")),
        name="ffn",
    )(h2, h2, h2, w1, w3, cw, cb, w2, xm, g2)


def kernel(x, c, ctx, c_ctx, ada_w, ada_b, norm1_g, norm2_g, w_in, hgrn_lb_logits, hgrn_norm_g,
           na_q_norm_g, na_k_norm_g, na_rel_bias, w_branch_a, w_branch_b, w_out,
           ffn_w1, ffn_w3, ffn_conv_w, ffn_conv_b, ffn_w2):
    b, l, d = x.shape
    n_ctx = ctx.shape[1]
    assert ada_w.shape[0] == 1 and b + 1 <= 8 and n_ctx % TOK == 0 and l % (2 * TOK) == 0

    c_all = jnp.zeros((8, d), F32).at[:b].set(c).at[b].set(c_ctx)
    mod = _mod_call(c_all, ada_w[0], ada_b).reshape(8, N_MOD, d)
    sh1, sc1, g1, sh2, sc2, g2 = (mod[:b, n] for n in range(N_MOD))
    ss_ctx = jnp.broadcast_to(mod[b, 0:2][None], (b, 2, d))
    ss1 = jnp.stack([jnp.stack([sh1, sc1], axis=1), ss_ctx], axis=1)

    h = _prenorm_call(x, ctx, norm1_g, ss1)
    t = l + n_ctx
    h = h.reshape(b * t, d)
    n_mix = 8 * MIX_W
    th = _ffn_hidden_tile(ffn_w1.shape[-1])
    p, (w1, w3, w2) = _inproj_call(h, w_in[0], 0, n_mix, F32, "inproj",
                                   ((ffn_w1[0], th), (ffn_w3[0], th), (ffn_w2[0], FFN_OUT_TILE)))
    pg, (wa, wb, wo) = _inproj_call(h, w_in[0], n_mix, 2 * d, BF16, "inproj_gates",
                                    ((w_branch_a[0], d), (w_branch_b[0], d), (w_out[0], d)))
    wa, wb, wo = (wt.reshape(wt.shape[1:]) for wt in (wa, wb, wo))
    p3 = p.reshape(b, t, n_mix)
    pg3 = pg.reshape(b, t, 2 * d)

    y_a = _hgrn_call(p3, hgrn_lb_logits, hgrn_norm_g, n_ctx, l)
    y_b = _na_call(p3, na_rel_bias[0], na_q_norm_g, na_k_norm_g, n_ctx, l)

    mod3 = jnp.stack([g1, sh2, sc2], axis=1)
    x_mid, h2 = _merge_call(y_a, y_b, pg3, x, wa, wb, wo, mod3, norm2_g)

    out = _ffn_call(h2.reshape(b * l, d), x_mid.reshape(b * l, d), w1, w3, ffn_conv_w[0], ffn_conv_b, w2,
                    g2[:, None, :], l)
    return out.reshape(b, l, d)
```

```python
import functools

import numpy as np
import jax
import jax.numpy as jnp
from jax import lax
from jax.experimental import pallas as pl
from jax.experimental.pallas import tpu as pltpu

GRID_W = 64
HEADS = 8
HEAD_DIM = 128
MIX_W = HEADS * HEAD_DIM
CHUNK = 64
N_LEVELS = 6
MXU_LEVEL_HALVES = (4, 2)
PREP_GROUP = 4
SCAN_GROUP = 4
HGRN_HEADS_PER_STEP = 2
NA_HEADS_PER_STEP = 2
NA_ROWS = 16
WIN_R = 8
WIN_C = 16
ROPE_THETA = 10000.0
N_MOD = 6
EPS = 1e-6
LOG2E = 1.4426950408889634
TOK = 256
MOD_COL_TILE = 1024
INPROJ_ROW_TILE = 1024
INPROJ_COL_TILE = 1024
MERGE_ROW_TILE = 2 * TOK
FFN_TOKEN_TILE = 1024
FFN_HIDDEN_TILE = 512
FFN_OUT_TILE = 512
MASKED = -1e30
V7X_VMEM_LIMIT = 60 * 1024 * 1024

BF16 = jnp.bfloat16
F32 = jnp.float32


def _dot(a, b):
    return jnp.dot(a, b, preferred_element_type=F32)


def _dot_nt(a, b):
    return lax.dot_general(a, b, (((1,), (1,)), ((), ())), preferred_element_type=F32)


def _sigmoid(t):
    return 1.0 / (1.0 + jnp.exp(-t))


def _params(sem):
    return pltpu.CompilerParams(dimension_semantics=sem, vmem_limit_bytes=V7X_VMEM_LIMIT)


def _mod_kernel(c_ref, w_ref, b_ref, o_ref):
    cv = c_ref[...]
    s = (cv * _sigmoid(cv)).astype(BF16)
    o_ref[...] = _dot(s, w_ref[...].astype(BF16)) + b_ref[...]


def _mod_call(c_all, w, b):
    d, n = w.shape
    tn = min(MOD_COL_TILE, d)
    return pl.pallas_call(
        _mod_kernel,
        grid=(n // tn,),
        in_specs=[pl.BlockSpec((8, d), lambda j: (0, 0)),
                  pl.BlockSpec((d, tn), lambda j: (0, j)),
                  pl.BlockSpec((1, tn), lambda j: (0, j))],
        out_specs=pl.BlockSpec((8, tn), lambda j: (0, j)),
        out_shape=jax.ShapeDtypeStruct((8, n), F32),
        compiler_params=_params(("arbitrary",)),
        name="mod",
    )(c_all, w, b)


def _prenorm_kernel(x_ref, ctx_ref, g_ref, ss_ref, o_ref, *, n_lat_blk):
    t = pl.program_id(1)
    xv = jnp.where(t < n_lat_blk, x_ref[...], ctx_ref[...])
    y = xv * lax.rsqrt(jnp.mean(xv * xv, axis=-1, keepdims=True) + EPS) * g_ref[...]
    o_ref[...] = (y * (1.0 + ss_ref[1:2, :]) + ss_ref[0:1, :]).astype(BF16)


def _prenorm_call(x, ctx, g, ss):
    b, l, d = x.shape
    nlb = l // TOK
    nt = nlb + ctx.shape[1] // TOK
    return pl.pallas_call(
        functools.partial(_prenorm_kernel, n_lat_blk=nlb),
        grid=(b, nt),
        in_specs=[pl.BlockSpec((None, TOK, d), lambda i, t: (i, jnp.minimum(t, nlb - 1), 0)),
                  pl.BlockSpec((None, TOK, d), lambda i, t: (i, jnp.maximum(t - nlb, 0), 0)),
                  pl.BlockSpec((1, d), lambda i, t: (0, 0)),
                  pl.BlockSpec((None, None, 2, d), lambda i, t: (i, (t >= nlb).astype(jnp.int32), 0, 0))],
        out_specs=pl.BlockSpec((None, TOK, d), lambda i, t: (i, t, 0)),
        out_shape=jax.ShapeDtypeStruct((b, nt * TOK, d), BF16),
        compiler_params=_params(("parallel", "arbitrary")),
        name="prenorm",
    )(x, ctx, g, ss)


def _inproj_kernel(*refs, n_cast):
    h_ref, w_ref = refs[:2]
    cast_in = refs[2:2 + n_cast]
    o_ref = refs[2 + n_cast]
    cast_out = refs[3 + n_cast:3 + 2 * n_cast]
    wb_ref = refs[-1]

    @pl.when(pl.program_id(1) == 0)
    def _():
        wb_ref[...] = w_ref[...].astype(BF16)

    o_ref[...] = _dot(h_ref[...], wb_ref[...]).astype(o_ref.dtype)
    for src, dst in zip(cast_in, cast_out):
        ct = dst.shape[-1]
        for c in range(dst.shape[0]):
            dst[c] = src[:, c * ct:(c + 1) * ct].astype(BF16)


def _slab_rows(rows, steps):
    return next(r for r in range(32, rows + 1, 32) if rows % r == 0 and rows // r <= steps)


def _inproj_call(h, w, col0, n, out_dtype, name, to_cast=()):
    m, d = h.shape
    tm = INPROJ_ROW_TILE if m % INPROJ_ROW_TILE == 0 else TOK
    tn = min(INPROJ_COL_TILE, d)
    j0 = col0 // tn
    ni = m // tm
    steps = (n // tn) * ni
    in_slabs, out_slabs, out_shapes = [], [], []
    for a, ct in to_cast:
        rows, cols = a.shape
        r = _slab_rows(rows, steps)
        slab = lambda j, i, last=rows // r - 1: jnp.minimum(j * ni + i, last)
        in_slabs.append(pl.BlockSpec((r, cols), lambda j, i, slab=slab: (slab(j, i), 0)))
        out_slabs.append(pl.BlockSpec((cols // ct, r, ct), lambda j, i, slab=slab: (0, slab(j, i), 0)))
        out_shapes.append(jax.ShapeDtypeStruct((cols // ct, rows, ct), BF16))
    outs = pl.pallas_call(
        functools.partial(_inproj_kernel, n_cast=len(to_cast)),
        grid=(n // tn, ni),
        in_specs=[pl.BlockSpec((tm, d), lambda j, i: (i, 0)),
                  pl.BlockSpec((d, tn), lambda j, i: (0, j + j0))] + in_slabs,
        out_specs=[pl.BlockSpec((tm, tn), lambda j, i: (i, j))] + out_slabs,
        out_shape=[jax.ShapeDtypeStruct((m, n), out_dtype)] + out_shapes,
        scratch_shapes=[pltpu.VMEM((d, tn), BF16)],
        compiler_params=_params(("arbitrary", "arbitrary")),
        name=name,
    )(h, w, *[a for a, _ in to_cast])
    return outs[0], outs[1:]


def _hgrn_constants():
    u = np.arange(CHUNK)
    mats, masks = [], []
    for rev in (False, True):
        pos = CHUNK - 1 - u if rev else u
        incl = (pos[None, :] <= pos[:, None]).astype(np.float32)
        blocks, lvl_masks = [incl], []
        for lev in range(N_LEVELS):
            half = CHUNK >> (lev + 1)
            ref = (pos // (2 * half)) * (2 * half) + half - 1
            upto_ref = (pos[None, :] <= ref[:, None]).astype(np.float32)
            if half in MXU_LEVEL_HALVES:
                late = (pos % (2 * half)) >= half
                blocks.append(np.where(late[:, None], 1.0, -1.0).astype(np.float32) * (incl - upto_ref))
            same = (pos[:, None] // (2 * half)) == (pos[None, :] // (2 * half))
            late_q = (pos[:, None] % (2 * half)) >= half
            early_k = (pos[None, :] % (2 * half)) < half
            lvl_masks.append((same & late_q & early_k).astype(np.float32))
        lvl_masks.append(np.eye(CHUNK, dtype=np.float32))
        mats.append(np.concatenate(blocks, axis=0))
        masks.append(np.stack(lvl_masks))
    return np.stack(mats), np.stack(masks)


def _hgrn_kernel(lbl_ref, q_ref, ff_ref, fb_ref, i_ref, gt_ref, cm_ref, mk_ref, ng_ref, y_ref, *scratch, **chunks):
    for hh in range(HGRN_HEADS_PER_STEP):
        lanes = pl.ds(hh * HEAD_DIM, HEAD_DIM)
        head = [r.at[:, lanes] for r in (q_ref, ff_ref, fb_ref, i_ref, gt_ref)]
        _hgrn_head(lbl_ref.at[:, :, hh], *head, cm_ref, mk_ref, ng_ref, y_ref.at[:, lanes], *scratch, **chunks)


def _hgrn_head(lbl_ref, q_ref, ff_ref, fb_ref, i_ref, gt_ref, cm_ref, mk_ref, ng_ref, y_ref,
               of_ref, ob_ref, sf_ref, sb_ref, kd_ref, et_ref, qd_ref, sc_ref, *, n_ctx_chunks, n_lat_chunks):
    f_refs, o_refs, st_refs = (ff_ref, fb_ref), (of_ref, ob_ref), (sf_ref, sb_ref)
    lowers = []
    for dirn in range(2):
        la, lb_ = lbl_ref[dirn, 0], lbl_ref[dirn, 1]
        mx = jnp.maximum(la, lb_)
        ea, eb = jnp.exp(la - mx), jnp.exp(lb_ - mx)
        lowers.append(ea / (ea + eb))
        st_refs[dirn][...] = jnp.zeros(st_refs[dirn].shape, F32)
    row_odd = (lax.broadcasted_iota(jnp.int32, (CHUNK, 1), 0) % 2) == 1

    def chunk_rows(pos):
        return pl.ds(pos * CHUNK, CHUNK)

    def level_decay(dirn, lev, f, sums):
        half = CHUNK >> (lev + 1)
        cum = sums[0:CHUNK]
        if half == 1:
            return jnp.where(row_odd, 1.0, f) if dirn else jnp.where(row_odd, f, 1.0)
        if half in MXU_LEVEL_HALVES:
            blk = 1 + MXU_LEVEL_HALVES.index(half)
            return jnp.exp2(sums[blk * CHUNK:(blk + 1) * CHUNK])
        parts = []
        for p in range(0, CHUNK, 2 * half):
            ref = cum[p + half - 1 + dirn:p + half + dirn]
            lo, hi = cum[p:p + half], cum[p + half:p + 2 * half]
            parts += [lo - ref, ref - hi] if dirn else [ref - lo, hi - ref]
        return jnp.exp2(jnp.concatenate(parts, axis=0))

    def prepare(pos0, n, with_out):
        items = [(dirn, pos0 + c) for c in range(n) for dirn in range(2)]
        fs, kks, splits = [], [], []
        for dirn, pos in items:
            lower = lowers[dirn]
            f = lower + (1.0 - lower) * _sigmoid(f_refs[dirn][chunk_rows(pos), :])
            g = jnp.log(f) * LOG2E
            g1 = g.astype(BF16)
            r1 = g - g1.astype(F32)
            g2 = r1.astype(BF16)
            g3 = (r1 - g2.astype(F32)).astype(BF16)
            fs.append(f)
            kks.append(1.0 - f)
            splits.append(jnp.concatenate([g1, g2, g3], axis=0))
        sums = [None] * len(items)
        for dirn in range(2):
            idx = [j for j, it in enumerate(items) if it[0] == dirn]
            wide = _dot(cm_ref[dirn], jnp.concatenate([splits[j] for j in idx], axis=1))
            for c, j in enumerate(idx):
                sums[j] = wide[:, c * HEAD_DIM:(c + 1) * HEAD_DIM]
        for j, (dirn, pos) in enumerate(items):
            cum = sums[j][0:CHUNK]
            last = 0 if dirn else CHUNK - 1
            tot = cum[last:last + 1]
            kd_ref[dirn, chunk_rows(pos), :] = (kks[j] * jnp.exp2(tot - cum)).astype(BF16)
            et_ref[dirn, pos] = jnp.broadcast_to(jnp.exp2(tot), et_ref.shape[2:])
        if not with_out:
            return
        for j, (dirn, pos) in enumerate(items):
            q = q_ref[chunk_rows(pos), :]
            qd_ref[dirn, chunk_rows(pos), :] = (q * jnp.exp2(sums[j][0:CHUNK])).astype(BF16)
            qb, kb = q.astype(BF16), kks[j].astype(BF16)
            terms = [mk_ref[dirn, N_LEVELS] * _dot_nt(qb, kb)]
            for lev in range(N_LEVELS):
                dl = level_decay(dirn, lev, fs[j], sums[j]).astype(BF16)
                terms.append(mk_ref[dirn, lev] * _dot_nt(qb * dl, kb * dl))
            while len(terms) > 1:
                terms = [a + b for a, b in zip(terms[::2], terms[1::2])] + terms[len(terms) & ~1:]
            sc_ref[dirn, chunk_rows(pos), :] = terms[0].astype(BF16)

    def scan(pos_f, pos_b, n, with_out):
        items = [(dirn, (pos_b - c) if dirn else (pos_f + c)) for c in range(n) for dirn in range(2)]
        vs = [i_ref[chunk_rows(pos), :] for _, pos in items]
        ups = [_dot(vs[j].T.astype(BF16), kd_ref[dirn, chunk_rows(pos), :]) for j, (dirn, pos) in enumerate(items)]
        sts = [st_refs[0][...], st_refs[1][...]]
        before = []
        for j, (dirn, pos) in enumerate(items):
            before.append(sts[dirn])
            sts[dirn] = sts[dirn] * et_ref[dirn, pos][0:1] + ups[j]
        for dirn in range(2):
            st_refs[dirn][...] = sts[dirn]
        if not with_out:
            return
        for j, (dirn, pos) in enumerate(items):
            orow = chunk_rows(pos)
            o_refs[dirn][orow, :] = (_dot_nt(qd_ref[dirn, orow, :], before[j].astype(BF16))
                                     + _dot(sc_ref[dirn, orow, :], vs[j].astype(BF16)))

    ctx0 = n_lat_chunks
    prepare(ctx0, n_ctx_chunks, False)
    for g in range(0, n_lat_chunks, PREP_GROUP):
        prepare(g, PREP_GROUP, True)
    scan(ctx0, ctx0 + n_ctx_chunks - 1, n_ctx_chunks, False)
    for g in range(0, n_lat_chunks, SCAN_GROUP):
        scan(g, n_lat_chunks - 1 - g, SCAN_GROUP, True)

    def readout(i, carry):
        rows = pl.ds(pl.multiple_of(i * TOK, TOK), TOK)
        ot = of_ref[rows, :] + ob_ref[rows, :]
        on = ot * lax.rsqrt(jnp.mean(ot * ot, axis=-1, keepdims=True) + EPS) * ng_ref[...]
        gate = gt_ref[rows, :]
        y_ref[rows, :] = (on * (gate * _sigmoid(gate))).astype(BF16)
        return carry

    lax.fori_loop(0, n_lat_chunks * CHUNK // TOK, readout, 0, unroll=4)


def _hgrn_call(p3, lb_logits, norm_g, n_ctx, l):
    b, t, _ = p3.shape
    cm, mk = _hgrn_constants()
    cm = jnp.asarray(np.concatenate([cm] * 3, axis=-1), BF16)
    mk = jnp.asarray(mk, F32)
    lbl = lb_logits.reshape(2, 2, HEADS, 1, HEAD_DIM)
    hps = HGRN_HEADS_PER_STEP
    steps = HEADS // hps
    assert n_ctx % CHUNK == 0 and (l // CHUNK) % PREP_GROUP == 0 and (l // CHUNK) % SCAN_GROUP == 0
    tok_spec = lambda grp: pl.BlockSpec((None, t, hps * HEAD_DIM), lambda i, h: (i, 0, grp * steps + h))
    full = lambda shape: pl.BlockSpec(shape, lambda i, h: (0,) * len(shape))
    return pl.pallas_call(
        functools.partial(_hgrn_kernel, n_ctx_chunks=n_ctx // CHUNK, n_lat_chunks=l // CHUNK),
        grid=(b, steps),
        in_specs=[pl.BlockSpec((2, 2, hps, 1, HEAD_DIM), lambda i, h: (0, 0, h, 0, 0)),
                  tok_spec(0), tok_spec(1), tok_spec(2), tok_spec(3), tok_spec(4),
                  full((2, 3 * CHUNK, 3 * CHUNK)),
                  full((2, N_LEVELS + 1, CHUNK, CHUNK)),
                  full((1, HEAD_DIM))],
        out_specs=pl.BlockSpec((None, l, hps * HEAD_DIM), lambda i, h: (i, 0, h)),
        out_shape=jax.ShapeDtypeStruct((b, l, MIX_W), BF16),
        scratch_shapes=[pltpu.VMEM((l, HEAD_DIM), F32), pltpu.VMEM((l, HEAD_DIM), F32),
                        pltpu.VMEM((HEAD_DIM, HEAD_DIM), F32), pltpu.VMEM((HEAD_DIM, HEAD_DIM), F32),
                        pltpu.VMEM((2, t, HEAD_DIM), BF16), pltpu.VMEM((2, t // CHUNK, 8, HEAD_DIM), F32),
                        pltpu.VMEM((2, l, HEAD_DIM), BF16), pltpu.VMEM((2, l, CHUNK), BF16)],
        compiler_params=_params(("parallel", "parallel")),
        name="hgrn",
    )(lbl, p3, p3, p3, p3, p3, cm, mk, norm_g)


def _rope_tables(l):
    pos = np.arange(l)
    nf = HEAD_DIM // 4
    inv = ROPE_THETA ** (-np.arange(nf, dtype=np.float64) / nf)
    ang_r = (pos // GRID_W)[:, None] * inv[None, :]
    ang_c = (pos % GRID_W)[:, None] * inv[None, :]
    cos = np.concatenate([np.cos(ang_r)] * 2 + [np.cos(ang_c)] * 2, axis=-1)
    sin = np.concatenate([-np.sin(ang_r), np.sin(ang_r), -np.sin(ang_c), np.sin(ang_c)], axis=-1)
    return jnp.asarray(cos, F32), jnp.asarray(sin, F32)


def _na_build_bias(rb_ref, tb_ref):
    lanes = 2 * GRID_W
    cq = lax.broadcasted_iota(jnp.int32, (GRID_W, lanes), 0)
    lane = lax.broadcasted_iota(jnp.int32, (GRID_W, lanes), 1)
    ck = lane % GRID_W
    cs = jnp.clip(cq - WIN_C // 2, 0, GRID_W - WIN_C)
    col_in = (ck >= cs) & (ck < cs + WIN_C)

    def toeplitz(d, lane0):
        row = jnp.broadcast_to(rb_ref[d:d + 1, :] * LOG2E, (GRID_W, lanes))
        return pltpu.roll(row, (lane0 - (WIN_C - 1)) % lanes, 1, stride=1, stride_axis=0)

    pair = [jnp.where(col_in, jnp.where(lane < GRID_W, toeplitz(d, 0), toeplitz(d + 1, GRID_W)), MASKED)
            for d in range(2 * WIN_R - 2)]
    for e in range(WIN_R):
        for p in range(WIN_R // 2):
            tb_ref[e, :, p * lanes:(p + 1) * lanes] = pair[2 * p - e + WIN_R - 1]


def _na_kernel(q_ref, k_ref, v_ref, cos_ref, sin_ref, rb_ref, gq_ref, gk_ref, perm_ref, o_ref,
               qs, ks, vs, kcs, vcs, tb_ref, **static):
    for hh in range(NA_HEADS_PER_STEP):
        lanes = pl.ds(hh * HEAD_DIM, HEAD_DIM)
        head = [r.at[:, lanes] for r in (q_ref, k_ref, v_ref)]
        _na_head(*head, cos_ref, sin_ref, rb_ref.at[hh], gq_ref, gk_ref, perm_ref, o_ref.at[:, lanes],
                 qs, ks, vs, kcs, vcs, tb_ref.at[hh], **static)


def _na_head(q_ref, k_ref, v_ref, cos_ref, sin_ref, rb_ref, gq_ref, gk_ref, perm_ref, o_ref,
             qs, ks, vs, kcs, vcs, tb_ref, *, n_ctx, grid_rows):
    @pl.when(pl.program_id(1) == 0)
    def _():
        _na_build_bias(rb_ref, tb_ref)

    ones = jnp.ones((2 * HEAD_DIM, HEAD_DIM), BF16)

    def normed(tv, g):
        sq = tv * tv
        hi = sq.astype(BF16)
        lo = (sq - hi.astype(F32)).astype(BF16)
        ssq = _dot(jnp.concatenate([hi, lo], axis=1), ones)
        return tv * lax.rsqrt(ssq * (1.0 / HEAD_DIM) + EPS) * g

    gq = gq_ref[...]
    gk = gk_ref[...]
    n_lat = grid_rows * GRID_W
    kcs[...] = normed(k_ref[n_lat:n_lat + n_ctx, :], gk).astype(BF16)
    vcs[...] = v_ref[n_lat:n_lat + n_ctx, :].astype(BF16)
    scale = HEAD_DIM ** -0.5 * LOG2E

    def prep(i, carry):
        rows = [pl.ds(pl.multiple_of((2 * i + n) * TOK, TOK), TOK) for n in range(2)]
        jobs = [(src_ref, g, dst, n) for n in range(2) for src_ref, g, dst in ((q_ref, gq, qs), (k_ref, gk, ks))]
        nrm = [normed(src_ref[rows[n], :], g) for src_ref, g, _, n in jobs]
        par = [_dot(tv.astype(BF16), perm_ref[...]) for tv in nrm]
        for (_, _, dst, n), tv, pv in zip(jobs, nrm, par):
            dst[rows[n], :] = (tv * cos_ref[rows[n], :] + pv * sin_ref[rows[n], :]).astype(BF16)
        for n in range(2):
            vs[rows[n], :] = v_ref[rows[n], :].astype(BF16)
        return carry

    lax.fori_loop(0, grid_rows * GRID_W // (2 * TOK), prep, 0, unroll=4)
    band = WIN_R * GRID_W

    def rows_step(i, carry):
        rr = [i * NA_ROWS + n for n in range(NA_ROWS)]
        rs = [jnp.clip(r - WIN_R // 2, 0, grid_rows - WIN_R) for r in rr]
        qrow = [pl.ds(pl.multiple_of(r * GRID_W, GRID_W), GRID_W) for r in rr]
        krow = [pl.ds(pl.multiple_of(s * GRID_W, GRID_W), band) for s in rs]
        qr = [qs[qw, :] for qw in qrow]
        kc = kcs[...]
        sb = [_dot_nt(qr[n], ks[krow[n], :]) * scale + tb_ref[rr[n] - rs[n]] for n in range(NA_ROWS)]
        sc = [_dot_nt(qr[n], kc) * scale for n in range(NA_ROWS)]
        m = [jnp.maximum(jnp.max(sb[n], axis=-1, keepdims=True), jnp.max(sc[n], axis=-1, keepdims=True))
             for n in range(NA_ROWS)]
        pb = [jnp.exp2(sb[n] - m[n]) for n in range(NA_ROWS)]
        pc = [jnp.exp2(sc[n] - m[n]) for n in range(NA_ROWS)]
        den = [jnp.sum(pb[n], axis=-1, keepdims=True) + jnp.sum(pc[n], axis=-1, keepdims=True)
               for n in range(NA_ROWS)]
        vc = vcs[...]
        o = [_dot(pb[n].astype(BF16), vs[krow[n], :]) + _dot(pc[n].astype(BF16), vc) for n in range(NA_ROWS)]
        for n in range(NA_ROWS):
            o_ref[qrow[n], :] = (o[n] / den[n]).astype(BF16)
        return carry

    lax.fori_loop(0, grid_rows // NA_ROWS, rows_step, 0)


def _na_call(p3, rel_bias, gq, gk, n_ctx, l):
    b, t, _ = p3.shape
    cos, sin = _rope_tables(l)
    nr, nc = rel_bias.shape[1:]
    rb = jnp.zeros((HEADS, 2 * WIN_R, 2 * GRID_W), F32).at[:, :nr, :nc].set(rel_bias)
    lanes = np.arange(HEAD_DIM)
    partner = np.where(lanes % (HEAD_DIM // 2) < HEAD_DIM // 4, lanes + HEAD_DIM // 4, lanes - HEAD_DIM // 4)
    perm = jnp.asarray(lanes[:, None] == partner[None, :], BF16)
    hps = NA_HEADS_PER_STEP
    steps = HEADS // hps
    col = lambda grp: (lambda h, i: (i, 0, grp * steps + h))
    tok_spec = lambda im: pl.BlockSpec((None, t, hps * HEAD_DIM), im)
    full = lambda shape: pl.BlockSpec(shape, lambda h, i: (0,) * len(shape))
    return pl.pallas_call(
        functools.partial(_na_kernel, n_ctx=n_ctx, grid_rows=l // GRID_W),
        grid=(steps, b),
        in_specs=[tok_spec(col(5)), tok_spec(col(6)), tok_spec(col(7)),
                  full((l, HEAD_DIM)), full((l, HEAD_DIM)),
                  pl.BlockSpec((hps, 2 * WIN_R, 2 * GRID_W), lambda h, i: (h, 0, 0)),
                  full((1, HEAD_DIM)), full((1, HEAD_DIM)), full((HEAD_DIM, HEAD_DIM))],
        out_specs=pl.BlockSpec((None, l, hps * HEAD_DIM), lambda h, i: (i, 0, h)),
        out_shape=jax.ShapeDtypeStruct((b, l, MIX_W), BF16),
        scratch_shapes=[pltpu.VMEM((l, HEAD_DIM), BF16), pltpu.VMEM((l, HEAD_DIM), BF16),
                        pltpu.VMEM((l, HEAD_DIM), BF16), pltpu.VMEM((n_ctx, HEAD_DIM), BF16),
                        pltpu.VMEM((n_ctx, HEAD_DIM), BF16),
                        pltpu.VMEM((hps, WIN_R, GRID_W, WIN_R * GRID_W), F32)],
        compiler_params=_params(("parallel", "arbitrary")),
        name="na",
    )(p3, p3, p3, cos, sin, rb, gq, gk, perm)


def _merge_kernel(ya_ref, yb_ref, ga_ref, gb_ref, x_ref, wa_ref, wb_ref, wo_ref, mod_ref, n2_ref,
                  xm_ref, h2_ref):
    tm = x_ref.shape[0]
    halves = [pl.ds(n * (tm // 2), tm // 2) for n in range(2)]
    za = [_dot(ya_ref[r, :], wa_ref[...]) for r in halves]
    zb = [_dot(yb_ref[r, :], wb_ref[...]) for r in halves]
    z = [(_sigmoid(ga_ref[r, :].astype(F32)) * za[n] + _sigmoid(gb_ref[r, :].astype(F32)) * zb[n]).astype(BF16)
         for n, r in enumerate(halves)]
    br = [_dot(zn, wo_ref[...]) for zn in z]
    for n, r in enumerate(halves):
        xm = x_ref[r, :] + mod_ref[0:1, :] * br[n]
        xm_ref[r, :] = xm
        y = xm * lax.rsqrt(jnp.mean(xm * xm, axis=-1, keepdims=True) + EPS) * n2_ref[...]
        h2_ref[r, :] = (y * (1.0 + mod_ref[2:3, :]) + mod_ref[1:2, :]).astype(BF16)


def _merge_call(ya, yb, pg3, x, wa, wb, wo, mod3, n2):
    b, l, d = x.shape
    tm = MERGE_ROW_TILE
    const = lambda shape: pl.BlockSpec(shape, lambda i, t: (0,) * len(shape), pipeline_mode=pl.Buffered(1))
    return pl.pallas_call(
        _merge_kernel,
        grid=(b, l // tm),
        in_specs=[pl.BlockSpec((None, tm, MIX_W), lambda i, t: (i, t, 0)),
                  pl.BlockSpec((None, tm, MIX_W), lambda i, t: (i, t, 0)),
                  pl.BlockSpec((None, tm, d), lambda i, t: (i, t, 0)),
                  pl.BlockSpec((None, tm, d), lambda i, t: (i, t, 1)),
                  pl.BlockSpec((None, tm, d), lambda i, t: (i, t, 0)),
                  const((MIX_W, d)), const((MIX_W, d)), const((d, d)),
                  pl.BlockSpec((None, 3, d), lambda i, t: (i, 0, 0)),
                  pl.BlockSpec((1, d), lambda i, t: (0, 0))],
        out_specs=[pl.BlockSpec((None, tm, d), lambda i, t: (i, t, 0)),
                   pl.BlockSpec((None, tm, d), lambda i, t: (i, t, 0))],
        out_shape=[jax.ShapeDtypeStruct((b, l, d), F32), jax.ShapeDtypeStruct((b, l, d), BF16)],
        compiler_params=_params(("parallel", "arbitrary")),
        name="merge",
    )(ya, yb, pg3, pg3, x, wa, wb, wo, mod3, n2)


def _ffn_kernel(h_ref, hp_ref, hn_ref, w1_ref, w3_ref, cw_ref, cb_ref, w2_ref, xm_ref, g2_ref, o_ref, a_ref,
                hc_ref, *, tiles_per_seq, halo):
    i = pl.program_id(0)
    s = pl.program_id(1)
    n_hid, tm, th = a_ref.shape

    @pl.when(s == 0)
    def _():
        hc_ref[0:tm, :] = h_ref[...]
        hc_ref[tm:tm + halo, :] = hp_ref[...]
        hc_ref[tm + halo:tm + 2 * halo, :] = hn_ref[...]

    @pl.when(s < n_hid)
    def _():
        t1_all = _dot(hc_ref[...], w1_ref[...])
        t1 = t1_all[0:tm]
        prev_row = t1_all[tm + halo - 1:tm + halo, :]
        next_row = t1_all[tm + halo:tm + halo + 1, :]
        prev_row = jnp.where(i % tiles_per_seq == 0, 0.0, prev_row)
        next_row = jnp.where(i % tiles_per_seq == tiles_per_seq - 1, 0.0, next_row)
        ridx = lax.broadcasted_iota(jnp.int32, (tm, 1), 0)
        up = jnp.where(ridx == 0, prev_row, pltpu.roll(t1, 1, 0))
        dn = jnp.where(ridx == tm - 1, next_row, pltpu.roll(t1, tm - 1, 0))
        u = up * cw_ref[0:1, :] + t1 * cw_ref[1:2, :] + dn * cw_ref[2:3, :] + cb_ref[...]
        a_ref[s] = ((u * _sigmoid(u)) * _dot(hc_ref[0:tm, :], w3_ref[...])).astype(BF16)

    @pl.when(s >= n_hid)
    def _():
        acc = _dot(a_ref[0], w2_ref[0:th, :])
        for k in range(1, n_hid):
            acc = acc + _dot(a_ref[k], w2_ref[k * th:(k + 1) * th, :])
        o_ref[...] = xm_ref[...] + g2_ref[...] * acc


def _ffn_hidden_tile(hid):
    return min(FFN_HIDDEN_TILE, hid)


def _ffn_call(h2, xm, w1, w3, cw, cb, w2, g2, l):
    m, d = h2.shape
    n_hid, _, th = w1.shape
    hid = n_hid * th
    tm = FFN_TOKEN_TILE
    tn = w2.shape[-1]
    halo = 16
    per = tm // halo
    nblk = m // halo
    hcol = lambda i, s: (0, jnp.minimum(s, n_hid - 1))
    htile = lambda i, s: (jnp.minimum(s, n_hid - 1), 0, 0)
    ocol = lambda s: jnp.maximum(s - n_hid, 0)
    return pl.pallas_call(
        functools.partial(_ffn_kernel, tiles_per_seq=l // tm, halo=halo),
        grid=(m // tm, n_hid + d // tn),
        in_specs=[pl.BlockSpec((tm, d), lambda i, s: (i, 0)),
                  pl.BlockSpec((halo, d), lambda i, s: (jnp.maximum(i * per - 1, 0), 0)),
                  pl.BlockSpec((halo, d), lambda i, s: (jnp.minimum((i + 1) * per, nblk - 1), 0)),
                  pl.BlockSpec((None, d, th), htile),
                  pl.BlockSpec((None, d, th), htile),
                  pl.BlockSpec((3, th), hcol),
                  pl.BlockSpec((1, th), hcol),
                  pl.BlockSpec((None, hid, tn), lambda i, s: (ocol(s), 0, 0)),
                  pl.BlockSpec((tm, tn), lambda i, s: (i, ocol(s))),
                  pl.BlockSpec((None, 1, tn), lambda i, s: (i // (l // tm), 0, ocol(s)))],
        out_specs=pl.BlockSpec((tm, tn), lambda i, s: (i, ocol(s))),
        out_shape=jax.ShapeDtypeStruct((m, d), F32),
        scratch_shapes=[pltpu.VMEM((n_hid, tm, th), BF16), pltpu.VMEM((tm + 2 * halo, d), BF16)],
        compiler_params=_params(("parallel", "arbitrary")),
        name="ffn",
    )(h2, h2, h2, w1, w3, cw, cb, w2, xm, g2)


def kernel(x, c, ctx, c_ctx, ada_w, ada_b, norm1_g, norm2_g, w_in, hgrn_lb_logits, hgrn_norm_g,
           na_q_norm_g, na_k_norm_g, na_rel_bias, w_branch_a, w_branch_b, w_out,
           ffn_w1, ffn_w3, ffn_conv_w, ffn_conv_b, ffn_w2):
    b, l, d = x.shape
    n_ctx = ctx.shape[1]
    assert ada_w.shape[0] == 1 and b + 1 <= 8 and n_ctx % TOK == 0 and l % (2 * TOK) == 0

    c_all = jnp.zeros((8, d), F32).at[:b].set(c).at[b].set(c_ctx)
    mod = _mod_call(c_all, ada_w[0], ada_b).reshape(8, N_MOD, d)
    sh1, sc1, g1, sh2, sc2, g2 = (mod[:b, n] for n in range(N_MOD))
    ss_ctx = jnp.broadcast_to(mod[b, 0:2][None], (b, 2, d))
    ss1 = jnp.stack([jnp.stack([sh1, sc1], axis=1), ss_ctx], axis=1)

    h = _prenorm_call(x, ctx, norm1_g, ss1)
    t = l + n_ctx
    h = h.reshape(b * t, d)
    n_mix = 8 * MIX_W
    th = _ffn_hidden_tile(ffn_w1.shape[-1])
    p, (w1, w3, w2) = _inproj_call(h, w_in[0], 0, n_mix, F32, "inproj",
                                   ((ffn_w1[0], th), (ffn_w3[0], th), (ffn_w2[0], FFN_OUT_TILE)))
    pg, (wa, wb, wo) = _inproj_call(h, w_in[0], n_mix, 2 * d, BF16, "inproj_gates",
                                    ((w_branch_a[0], d), (w_branch_b[0], d), (w_out[0], d)))
    wa, wb, wo = (wt.reshape(wt.shape[1:]) for wt in (wa, wb, wo))
    p3 = p.reshape(b, t, n_mix)
    pg3 = pg.reshape(b, t, 2 * d)

    y_a = _hgrn_call(p3, hgrn_lb_logits, hgrn_norm_g, n_ctx, l)
    y_b = _na_call(p3, na_rel_bias[0], na_q_norm_g, na_k_norm_g, n_ctx, l)

    mod3 = jnp.stack([g1, sh2, sc2], axis=1)
    x_mid, h2 = _merge_call(y_a, y_b, pg3, x, wa, wb, wo, mod3, norm2_g)

    out = _ffn_call(h2.reshape(b * l, d), x_mid.reshape(b * l, d), w1, w3, ffn_conv_w[0], ffn_conv_b, w2,
                    g2[:, None, :], l)
    return out.reshape(b, l, d)
```

```python
import functools

import numpy as np
import jax
import jax.numpy as jnp
from jax import lax
from jax.experimental import pallas as pl
from jax.experimental.pallas import tpu as pltpu

GRID_W = 64
HEADS = 8
HEAD_DIM = 128
MIX_W = HEADS * HEAD_DIM
CHUNK = 64
N_LEVELS = 6
MXU_LEVEL_HALVES = (4, 2)
PREP_GROUP = 4
SCAN_GROUP = 4
HGRN_HEADS_PER_STEP = 2
NA_HEADS_PER_STEP = 2
NA_ROWS = 16
WIN_R = 8
WIN_C = 16
ROPE_THETA = 10000.0
N_MOD = 6
EPS = 1e-6
LOG2E = 1.4426950408889634
TOK = 256
MOD_COL_TILE = 1024
INPROJ_ROW_TILE = 1024
INPROJ_COL_TILE = 1024
MERGE_ROW_TILE = 2 * TOK
FFN_TOKEN_TILE = 1024
FFN_HIDDEN_TILE = 512
FFN_OUT_TILE = 512
MASKED = -1e30
V7X_VMEM_LIMIT = 60 * 1024 * 1024

BF16 = jnp.bfloat16
F32 = jnp.float32


def _dot(a, b):
    return jnp.dot(a, b, preferred_element_type=F32)


def _dot_nt(a, b):
    return lax.dot_general(a, b, (((1,), (1,)), ((), ())), preferred_element_type=F32)


def _sigmoid(t):
    return 1.0 / (1.0 + jnp.exp(-t))


def _params(sem):
    return pltpu.CompilerParams(dimension_semantics=sem, vmem_limit_bytes=V7X_VMEM_LIMIT)


def _mod_kernel(c_ref, w_ref, b_ref, o_ref):
    cv = c_ref[...]
    s = (cv * _sigmoid(cv)).astype(BF16)
    o_ref[...] = _dot(s, w_ref[...].astype(BF16)) + b_ref[...]


def _mod_call(c_all, w, b):
    d, n = w.shape
    tn = min(MOD_COL_TILE, d)
    return pl.pallas_call(
        _mod_kernel,
        grid=(n // tn,),
        in_specs=[pl.BlockSpec((8, d), lambda j: (0, 0)),
                  pl.BlockSpec((d, tn), lambda j: (0, j)),
                  pl.BlockSpec((1, tn), lambda j: (0, j))],
        out_specs=pl.BlockSpec((8, tn), lambda j: (0, j)),
        out_shape=jax.ShapeDtypeStruct((8, n), F32),
        compiler_params=_params(("arbitrary",)),
        name="mod",
    )(c_all, w, b)


def _prenorm_kernel(x_ref, ctx_ref, g_ref, ss_ref, o_ref, *, n_lat_blk):
    t = pl.program_id(1)
    xv = jnp.where(t < n_lat_blk, x_ref[...], ctx_ref[...])
    y = xv * lax.rsqrt(jnp.mean(xv * xv, axis=-1, keepdims=True) + EPS) * g_ref[...]
    o_ref[...] = (y * (1.0 + ss_ref[1:2, :]) + ss_ref[0:1, :]).astype(BF16)


def _prenorm_call(x, ctx, g, ss):
    b, l, d = x.shape
    nlb = l // TOK
    nt = nlb + ctx.shape[1] // TOK
    return pl.pallas_call(
        functools.partial(_prenorm_kernel, n_lat_blk=nlb),
        grid=(b, nt),
        in_specs=[pl.BlockSpec((None, TOK, d), lambda i, t: (i, jnp.minimum(t, nlb - 1), 0)),
                  pl.BlockSpec((None, TOK, d), lambda i, t: (i, jnp.maximum(t - nlb, 0), 0)),
                  pl.BlockSpec((1, d), lambda i, t: (0, 0)),
                  pl.BlockSpec((None, N_MOD, d), lambda i, t: (jnp.where(t >= nlb, b, i), 0, 0))],
        out_specs=pl.BlockSpec((None, TOK, d), lambda i, t: (i, t, 0)),
        out_shape=jax.ShapeDtypeStruct((b, nt * TOK, d), BF16),
        compiler_params=_params(("parallel", "arbitrary")),
        name="prenorm",
    )(x, ctx, g, ss)


def _inproj_kernel(*refs, n_cast):
    h_ref, w_ref = refs[:2]
    cast_in = refs[2:2 + n_cast]
    o_ref = refs[2 + n_cast]
    cast_out = refs[3 + n_cast:3 + 2 * n_cast]
    wb_ref = refs[-1]

    @pl.when(pl.program_id(1) == 0)
    def _():
        wb_ref[...] = w_ref[...].astype(BF16)

    o_ref[...] = _dot(h_ref[...], wb_ref[...]).astype(o_ref.dtype)
    for src, dst in zip(cast_in, cast_out):
        ct = dst.shape[-1]
        for c in range(dst.shape[0]):
            dst[c] = src[:, c * ct:(c + 1) * ct].astype(BF16)


def _slab_rows(rows, steps):
    return next(r for r in range(32, rows + 1, 32) if rows % r == 0 and rows // r <= steps)


def _inproj_call(h, w, col0, n, out_dtype, name, to_cast=()):
    m, d = h.shape
    tm = INPROJ_ROW_TILE if m % INPROJ_ROW_TILE == 0 else TOK
    tn = min(INPROJ_COL_TILE, d)
    j0 = col0 // tn
    ni = m // tm
    steps = (n // tn) * ni
    in_slabs, out_slabs, out_shapes = [], [], []
    for a, ct in to_cast:
        rows, cols = a.shape
        r = _slab_rows(rows, steps)
        slab = lambda j, i, last=rows // r - 1: jnp.minimum(j * ni + i, last)
        in_slabs.append(pl.BlockSpec((r, cols), lambda j, i, slab=slab: (slab(j, i), 0)))
        out_slabs.append(pl.BlockSpec((cols // ct, r, ct), lambda j, i, slab=slab: (0, slab(j, i), 0)))
        out_shapes.append(jax.ShapeDtypeStruct((cols // ct, rows, ct), BF16))
    outs = pl.pallas_call(
        functools.partial(_inproj_kernel, n_cast=len(to_cast)),
        grid=(n // tn, ni),
        in_specs=[pl.BlockSpec((tm, d), lambda j, i: (i, 0)),
                  pl.BlockSpec((d, tn), lambda j, i: (0, j + j0))] + in_slabs,
        out_specs=[pl.BlockSpec((tm, tn), lambda j, i: (i, j))] + out_slabs,
        out_shape=[jax.ShapeDtypeStruct((m, n), out_dtype)] + out_shapes,
        scratch_shapes=[pltpu.VMEM((d, tn), BF16)],
        compiler_params=_params(("arbitrary", "arbitrary")),
        name=name,
    )(h, w, *[a for a, _ in to_cast])
    return outs[0], outs[1:]


def _hgrn_constants():
    u = np.arange(CHUNK)
    mats, masks = [], []
    for rev in (False, True):
        pos = CHUNK - 1 - u if rev else u
        incl = (pos[None, :] <= pos[:, None]).astype(np.float32)
        blocks, lvl_masks = [incl], []
        for lev in range(N_LEVELS):
            half = CHUNK >> (lev + 1)
            ref = (pos // (2 * half)) * (2 * half) + half - 1
            upto_ref = (pos[None, :] <= ref[:, None]).astype(np.float32)
            if half in MXU_LEVEL_HALVES:
                late = (pos % (2 * half)) >= half
                blocks.append(np.where(late[:, None], 1.0, -1.0).astype(np.float32) * (incl - upto_ref))
            same = (pos[:, None] // (2 * half)) == (pos[None, :] // (2 * half))
            late_q = (pos[:, None] % (2 * half)) >= half
            early_k = (pos[None, :] % (2 * half)) < half
            lvl_masks.append((same & late_q & early_k).astype(np.float32))
        lvl_masks.append(np.eye(CHUNK, dtype=np.float32))
        mats.append(np.concatenate(blocks, axis=0))
        masks.append(np.stack(lvl_masks))
    return np.stack(mats), np.stack(masks)


def _hgrn_kernel(lbl_ref, q_ref, ff_ref, fb_ref, i_ref, gt_ref, cm_ref, mk_ref, ng_ref, y_ref, *scratch, **chunks):
    for hh in range(HGRN_HEADS_PER_STEP):
        lanes = pl.ds(hh * HEAD_DIM, HEAD_DIM)
        head = [r.at[:, lanes] for r in (q_ref, ff_ref, fb_ref, i_ref, gt_ref)]
        _hgrn_head(lbl_ref.at[:, :, hh], *head, cm_ref, mk_ref, ng_ref, y_ref.at[:, lanes], *scratch, **chunks)


def _hgrn_head(lbl_ref, q_ref, ff_ref, fb_ref, i_ref, gt_ref, cm_ref, mk_ref, ng_ref, y_ref,
               of_ref, ob_ref, sf_ref, sb_ref, kd_ref, et_ref, qd_ref, sc_ref, *, n_ctx_chunks, n_lat_chunks):
    f_refs, o_refs, st_refs = (ff_ref, fb_ref), (of_ref, ob_ref), (sf_ref, sb_ref)
    lowers = []
    for dirn in range(2):
        la, lb_ = lbl_ref[dirn, 0], lbl_ref[dirn, 1]
        mx = jnp.maximum(la, lb_)
        ea, eb = jnp.exp(la - mx), jnp.exp(lb_ - mx)
        lowers.append(ea / (ea + eb))
        st_refs[dirn][...] = jnp.zeros(st_refs[dirn].shape, F32)
    row_odd = (lax.broadcasted_iota(jnp.int32, (CHUNK, 1), 0) % 2) == 1

    def chunk_rows(pos):
        return pl.ds(pos * CHUNK, CHUNK)

    def level_decay(dirn, lev, f, sums):
        half = CHUNK >> (lev + 1)
        cum = sums[0:CHUNK]
        if half == 1:
            return jnp.where(row_odd, 1.0, f) if dirn else jnp.where(row_odd, f, 1.0)
        if half in MXU_LEVEL_HALVES:
            blk = 1 + MXU_LEVEL_HALVES.index(half)
            return jnp.exp2(sums[blk * CHUNK:(blk + 1) * CHUNK])
        parts = []
        for p in range(0, CHUNK, 2 * half):
            ref = cum[p + half - 1 + dirn:p + half + dirn]
            lo, hi = cum[p:p + half], cum[p + half:p + 2 * half]
            parts += [lo - ref, ref - hi] if dirn else [ref - lo, hi - ref]
        return jnp.exp2(jnp.concatenate(parts, axis=0))

    def prepare(pos0, n, with_out):
        items = [(dirn, pos0 + c) for c in range(n) for dirn in range(2)]
        fs, kks, splits = [], [], []
        for dirn, pos in items:
            lower = lowers[dirn]
            f = lower + (1.0 - lower) * _sigmoid(f_refs[dirn][chunk_rows(pos), :])
            g = jnp.log(f) * LOG2E
            g1 = g.astype(BF16)
            r1 = g - g1.astype(F32)
            g2 = r1.astype(BF16)
            g3 = (r1 - g2.astype(F32)).astype(BF16)
            fs.append(f)
            kks.append(1.0 - f)
            splits.append(jnp.concatenate([g1, g2, g3], axis=0))
        sums = [None] * len(items)
        for dirn in range(2):
            idx = [j for j, it in enumerate(items) if it[0] == dirn]
            wide = _dot(cm_ref[dirn], jnp.concatenate([splits[j] for j in idx], axis=1))
            for c, j in enumerate(idx):
                sums[j] = wide[:, c * HEAD_DIM:(c + 1) * HEAD_DIM]
        for j, (dirn, pos) in enumerate(items):
            cum = sums[j][0:CHUNK]
            last = 0 if dirn else CHUNK - 1
            tot = cum[last:last + 1]
            kd_ref[dirn, chunk_rows(pos), :] = (kks[j] * jnp.exp2(tot - cum)).astype(BF16)
            et_ref[dirn, pos] = jnp.broadcast_to(jnp.exp2(tot), et_ref.shape[2:])
        if not with_out:
            return
        for j, (dirn, pos) in enumerate(items):
            q = q_ref[chunk_rows(pos), :]
            qd_ref[dirn, chunk_rows(pos), :] = (q * jnp.exp2(sums[j][0:CHUNK])).astype(BF16)
            qb, kb = q.astype(BF16), kks[j].astype(BF16)
            terms = [mk_ref[dirn, N_LEVELS] * _dot_nt(qb, kb)]
            for lev in range(N_LEVELS):
                dl = level_decay(dirn, lev, fs[j], sums[j]).astype(BF16)
                terms.append(mk_ref[dirn, lev] * _dot_nt(qb * dl, kb * dl))
            while len(terms) > 1:
                terms = [a + b for a, b in zip(terms[::2], terms[1::2])] + terms[len(terms) & ~1:]
            sc_ref[dirn, chunk_rows(pos), :] = terms[0].astype(BF16)

    def scan(pos_f, pos_b, n, with_out):
        items = [(dirn, (pos_b - c) if dirn else (pos_f + c)) for c in range(n) for dirn in range(2)]
        vs = [i_ref[chunk_rows(pos), :] for _, pos in items]
        ups = [_dot(vs[j].T.astype(BF16), kd_ref[dirn, chunk_rows(pos), :]) for j, (dirn, pos) in enumerate(items)]
        sts = [st_refs[0][...], st_refs[1][...]]
        before = []
        for j, (dirn, pos) in enumerate(items):
            before.append(sts[dirn])
            sts[dirn] = sts[dirn] * et_ref[dirn, pos][0:1] + ups[j]
        for dirn in range(2):
            st_refs[dirn][...] = sts[dirn]
        if not with_out:
            return
        for j, (dirn, pos) in enumerate(items):
            orow = chunk_rows(pos)
            o_refs[dirn][orow, :] = (_dot_nt(qd_ref[dirn, orow, :], before[j].astype(BF16))
                                     + _dot(sc_ref[dirn, orow, :], vs[j].astype(BF16)))

    ctx0 = n_lat_chunks
    prepare(ctx0, n_ctx_chunks, False)
    for g in range(0, n_lat_chunks, PREP_GROUP):
        prepare(g, PREP_GROUP, True)
    scan(ctx0, ctx0 + n_ctx_chunks - 1, n_ctx_chunks, False)
    for g in range(0, n_lat_chunks, SCAN_GROUP):
        scan(g, n_lat_chunks - 1 - g, SCAN_GROUP, True)

    def readout(i, carry):
        rows = pl.ds(pl.multiple_of(i * TOK, TOK), TOK)
        ot = of_ref[rows, :] + ob_ref[rows, :]
        on = ot * lax.rsqrt(jnp.mean(ot * ot, axis=-1, keepdims=True) + EPS) * ng_ref[...]
        gate = gt_ref[rows, :]
        y_ref[rows, :] = (on * (gate * _sigmoid(gate))).astype(BF16)
        return carry

    lax.fori_loop(0, n_lat_chunks * CHUNK // TOK, readout, 0, unroll=4)


def _hgrn_call(p3, lb_logits, norm_g, n_ctx, l):
    b, t, _ = p3.shape
    cm, mk = _hgrn_constants()
    cm = jnp.asarray(np.concatenate([cm] * 3, axis=-1), BF16)
    mk = jnp.asarray(mk, F32)
    lbl = lb_logits.reshape(2, 2, HEADS, 1, HEAD_DIM)
    hps = HGRN_HEADS_PER_STEP
    steps = HEADS // hps
    assert n_ctx % CHUNK == 0 and (l // CHUNK) % PREP_GROUP == 0 and (l // CHUNK) % SCAN_GROUP == 0
    tok_spec = lambda grp: pl.BlockSpec((None, t, hps * HEAD_DIM), lambda i, h: (i, 0, grp * steps + h))
    full = lambda shape: pl.BlockSpec(shape, lambda i, h: (0,) * len(shape))
    return pl.pallas_call(
        functools.partial(_hgrn_kernel, n_ctx_chunks=n_ctx // CHUNK, n_lat_chunks=l // CHUNK),
        grid=(b, steps),
        in_specs=[pl.BlockSpec((2, 2, hps, 1, HEAD_DIM), lambda i, h: (0, 0, h, 0, 0)),
                  tok_spec(0), tok_spec(1), tok_spec(2), tok_spec(3), tok_spec(4),
                  full((2, 3 * CHUNK, 3 * CHUNK)),
                  full((2, N_LEVELS + 1, CHUNK, CHUNK)),
                  full((1, HEAD_DIM))],
        out_specs=pl.BlockSpec((None, l, hps * HEAD_DIM), lambda i, h: (i, 0, h)),
        out_shape=jax.ShapeDtypeStruct((b, l, MIX_W), BF16),
        scratch_shapes=[pltpu.VMEM((l, HEAD_DIM), F32), pltpu.VMEM((l, HEAD_DIM), F32),
                        pltpu.VMEM((HEAD_DIM, HEAD_DIM), F32), pltpu.VMEM((HEAD_DIM, HEAD_DIM), F32),
                        pltpu.VMEM((2, t, HEAD_DIM), BF16), pltpu.VMEM((2, t // CHUNK, 8, HEAD_DIM), F32),
                        pltpu.VMEM((2, l, HEAD_DIM), BF16), pltpu.VMEM((2, l, CHUNK), BF16)],
        compiler_params=_params(("parallel", "parallel")),
        name="hgrn",
    )(lbl, p3, p3, p3, p3, p3, cm, mk, norm_g)


def _rope_tables(l):
    pos = np.arange(l)
    nf = HEAD_DIM // 4
    inv = ROPE_THETA ** (-np.arange(nf, dtype=np.float64) / nf)
    ang_r = (pos // GRID_W)[:, None] * inv[None, :]
    ang_c = (pos % GRID_W)[:, None] * inv[None, :]
    cos = np.concatenate([np.cos(ang_r)] * 2 + [np.cos(ang_c)] * 2, axis=-1)
    sin = np.concatenate([-np.sin(ang_r), np.sin(ang_r), -np.sin(ang_c), np.sin(ang_c)], axis=-1)
    return jnp.asarray(cos, F32), jnp.asarray(sin, F32)


def _na_build_bias(rb_ref, tb_ref):
    lanes = 2 * GRID_W
    cq = lax.broadcasted_iota(jnp.int32, (GRID_W, lanes), 0)
    lane = lax.broadcasted_iota(jnp.int32, (GRID_W, lanes), 1)
    ck = lane % GRID_W
    cs = jnp.clip(cq - WIN_C // 2, 0, GRID_W - WIN_C)
    col_in = (ck >= cs) & (ck < cs + WIN_C)

    def toeplitz(d, lane0):
        row = jnp.broadcast_to(rb_ref[d:d + 1, :] * LOG2E, (GRID_W, lanes))
        return pltpu.roll(row, (lane0 - (WIN_C - 1)) % lanes, 1, stride=1, stride_axis=0)

    pair = [jnp.where(col_in, jnp.where(lane < GRID_W, toeplitz(d, 0), toeplitz(d + 1, GRID_W)), MASKED)
            for d in range(2 * WIN_R - 2)]
    for e in range(WIN_R):
        for p in range(WIN_R // 2):
            tb_ref[e, :, p * lanes:(p + 1) * lanes] = pair[2 * p - e + WIN_R - 1]


def _na_kernel(q_ref, k_ref, v_ref, cos_ref, sin_ref, rb_ref, gq_ref, gk_ref, perm_ref, o_ref,
               qs, ks, vs, kcs, vcs, tb_ref, **static):
    for hh in range(NA_HEADS_PER_STEP):
        lanes = pl.ds(hh * HEAD_DIM, HEAD_DIM)
        head = [r.at[:, lanes] for r in (q_ref, k_ref, v_ref)]
        _na_head(*head, cos_ref, sin_ref, rb_ref.at[hh], gq_ref, gk_ref, perm_ref, o_ref.at[:, lanes],
                 qs, ks, vs, kcs, vcs, tb_ref.at[hh], **static)


def _na_head(q_ref, k_ref, v_ref, cos_ref, sin_ref, rb_ref, gq_ref, gk_ref, perm_ref, o_ref,
             qs, ks, vs, kcs, vcs, tb_ref, *, n_ctx, grid_rows):
    @pl.when(pl.program_id(1) == 0)
    def _():
        _na_build_bias(rb_ref, tb_ref)

    ones = jnp.ones((2 * HEAD_DIM, HEAD_DIM), BF16)

    def normed(tv, g):
        sq = tv * tv
        hi = sq.astype(BF16)
        lo = (sq - hi.astype(F32)).astype(BF16)
        ssq = _dot(jnp.concatenate([hi, lo], axis=1), ones)
        return tv * lax.rsqrt(ssq * (1.0 / HEAD_DIM) + EPS) * g

    gq = gq_ref[...]
    gk = gk_ref[...]
    n_lat = grid_rows * GRID_W
    kcs[...] = normed(k_ref[n_lat:n_lat + n_ctx, :], gk).astype(BF16)
    vcs[...] = v_ref[n_lat:n_lat + n_ctx, :].astype(BF16)
    scale = HEAD_DIM ** -0.5 * LOG2E

    def prep(i, carry):
        rows = [pl.ds(pl.multiple_of((2 * i + n) * TOK, TOK), TOK) for n in range(2)]
        jobs = [(src_ref, g, dst, n) for n in range(2) for src_ref, g, dst in ((q_ref, gq, qs), (k_ref, gk, ks))]
        nrm = [normed(src_ref[rows[n], :], g) for src_ref, g, _, n in jobs]
        par = [_dot(tv.astype(BF16), perm_ref[...]) for tv in nrm]
        for (_, _, dst, n), tv, pv in zip(jobs, nrm, par):
            dst[rows[n], :] = (tv * cos_ref[rows[n], :] + pv * sin_ref[rows[n], :]).astype(BF16)
        for n in range(2):
            vs[rows[n], :] = v_ref[rows[n], :].astype(BF16)
        return carry

    lax.fori_loop(0, grid_rows * GRID_W // (2 * TOK), prep, 0, unroll=4)
    band = WIN_R * GRID_W

    def rows_step(i, carry):
        rr = [i * NA_ROWS + n for n in range(NA_ROWS)]
        rs = [jnp.clip(r - WIN_R // 2, 0, grid_rows - WIN_R) for r in rr]
        qrow = [pl.ds(pl.multiple_of(r * GRID_W, GRID_W), GRID_W) for r in rr]
        krow = [pl.ds(pl.multiple_of(s * GRID_W, GRID_W), band) for s in rs]
        qr = [qs[qw, :] for qw in qrow]
        kc = kcs[...]
        sb = [_dot_nt(qr[n], ks[krow[n], :]) * scale + tb_ref[rr[n] - rs[n]] for n in range(NA_ROWS)]
        sc = [_dot_nt(qr[n], kc) * scale for n in range(NA_ROWS)]
        m = [jnp.maximum(jnp.max(sb[n], axis=-1, keepdims=True), jnp.max(sc[n], axis=-1, keepdims=True))
             for n in range(NA_ROWS)]
        pb = [jnp.exp2(sb[n] - m[n]) for n in range(NA_ROWS)]
        pc = [jnp.exp2(sc[n] - m[n]) for n in range(NA_ROWS)]
        den = [jnp.sum(pb[n], axis=-1, keepdims=True) + jnp.sum(pc[n], axis=-1, keepdims=True)
               for n in range(NA_ROWS)]
        vc = vcs[...]
        o = [_dot(pb[n].astype(BF16), vs[krow[n], :]) + _dot(pc[n].astype(BF16), vc) for n in range(NA_ROWS)]
        for n in range(NA_ROWS):
            o_ref[qrow[n], :] = (o[n] / den[n]).astype(BF16)
        return carry

    lax.fori_loop(0, grid_rows // NA_ROWS, rows_step, 0)


def _na_call(p3, rel_bias, gq, gk, n_ctx, l):
    b, t, _ = p3.shape
    cos, sin = _rope_tables(l)
    nr, nc = rel_bias.shape[1:]
    rb = jnp.zeros((HEADS, 2 * WIN_R, 2 * GRID_W), F32).at[:, :nr, :nc].set(rel_bias)
    lanes = np.arange(HEAD_DIM)
    partner = np.where(lanes % (HEAD_DIM // 2) < HEAD_DIM // 4, lanes + HEAD_DIM // 4, lanes - HEAD_DIM // 4)
    perm = jnp.asarray(lanes[:, None] == partner[None, :], BF16)
    hps = NA_HEADS_PER_STEP
    steps = HEADS // hps
    col = lambda grp: (lambda h, i: (i, 0, grp * steps + h))
    tok_spec = lambda im: pl.BlockSpec((None, t, hps * HEAD_DIM), im)
    full = lambda shape: pl.BlockSpec(shape, lambda h, i: (0,) * len(shape))
    return pl.pallas_call(
        functools.partial(_na_kernel, n_ctx=n_ctx, grid_rows=l // GRID_W),
        grid=(steps, b),
        in_specs=[tok_spec(col(5)), tok_spec(col(6)), tok_spec(col(7)),
                  full((l, HEAD_DIM)), full((l, HEAD_DIM)),
                  pl.BlockSpec((hps, 2 * WIN_R, 2 * GRID_W), lambda h, i: (h, 0, 0)),
                  full((1, HEAD_DIM)), full((1, HEAD_DIM)), full((HEAD_DIM, HEAD_DIM))],
        out_specs=pl.BlockSpec((None, l, hps * HEAD_DIM), lambda h, i: (i, 0, h)),
        out_shape=jax.ShapeDtypeStruct((b, l, MIX_W), BF16),
        scratch_shapes=[pltpu.VMEM((l, HEAD_DIM), BF16), pltpu.VMEM((l, HEAD_DIM), BF16),
                        pltpu.VMEM((l, HEAD_DIM), BF16), pltpu.VMEM((n_ctx, HEAD_DIM), BF16),
                        pltpu.VMEM((n_ctx, HEAD_DIM), BF16),
                        pltpu.VMEM((hps, WIN_R, GRID_W, WIN_R * GRID_W), F32)],
        compiler_params=_params(("parallel", "arbitrary")),
        name="na",
    )(p3, p3, p3, cos, sin, rb, gq, gk, perm)


def _merge_kernel(ya_ref, yb_ref, ga_ref, gb_ref, x_ref, wa_ref, wb_ref, wo_ref, mod_ref, n2_ref,
                  xm_ref, h2_ref):
    tm = x_ref.shape[0]
    halves = [pl.ds(n * (tm // 2), tm // 2) for n in range(2)]
    za = [_dot(ya_ref[r, :], wa_ref[...]) for r in halves]
    zb = [_dot(yb_ref[r, :], wb_ref[...]) for r in halves]
    z = [(_sigmoid(ga_ref[r, :].astype(F32)) * za[n] + _sigmoid(gb_ref[r, :].astype(F32)) * zb[n]).astype(BF16)
         for n, r in enumerate(halves)]
    br = [_dot(zn, wo_ref[...]) for zn in z]
    for n, r in enumerate(halves):
        xm = x_ref[r, :] + mod_ref[2:3, :] * br[n]
        xm_ref[r, :] = xm
        y = xm * lax.rsqrt(jnp.mean(xm * xm, axis=-1, keepdims=True) + EPS) * n2_ref[...]
        h2_ref[r, :] = (y * (1.0 + mod_ref[4:5, :]) + mod_ref[3:4, :]).astype(BF16)


def _merge_call(ya, yb, pg3, x, wa, wb, wo, mod3, n2):
    b, l, d = x.shape
    tm = MERGE_ROW_TILE
    const = lambda shape: pl.BlockSpec(shape, lambda i, t: (0,) * len(shape), pipeline_mode=pl.Buffered(1))
    return pl.pallas_call(
        _merge_kernel,
        grid=(b, l // tm),
        in_specs=[pl.BlockSpec((None, tm, MIX_W), lambda i, t: (i, t, 0)),
                  pl.BlockSpec((None, tm, MIX_W), lambda i, t: (i, t, 0)),
                  pl.BlockSpec((None, tm, d), lambda i, t: (i, t, 0)),
                  pl.BlockSpec((None, tm, d), lambda i, t: (i, t, 1)),
                  pl.BlockSpec((None, tm, d), lambda i, t: (i, t, 0)),
                  const((MIX_W, d)), const((MIX_W, d)), const((d, d)),
                  pl.BlockSpec((None, N_MOD, d), lambda i, t: (i, 0, 0)),
                  pl.BlockSpec((1, d), lambda i, t: (0, 0))],
        out_specs=[pl.BlockSpec((None, tm, d), lambda i, t: (i, t, 0)),
                   pl.BlockSpec((None, tm, d), lambda i, t: (i, t, 0))],
        out_shape=[jax.ShapeDtypeStruct((b, l, d), F32), jax.ShapeDtypeStruct((b, l, d), BF16)],
        compiler_params=_params(("parallel", "arbitrary")),
        name="merge",
    )(ya, yb, pg3, pg3, x, wa, wb, wo, mod3, n2)


def _ffn_kernel(h_ref, hp_ref, hn_ref, w1_ref, w3_ref, cw_ref, cb_ref, w2_ref, xm_ref, g2_ref, o_ref, a_ref,
                hc_ref, *, tiles_per_seq, halo):
    i = pl.program_id(0)
    s = pl.program_id(1)
    n_hid, tm, th = a_ref.shape

    @pl.when(s == 0)
    def _():
        hc_ref[0:tm, :] = h_ref[...]
        hc_ref[tm:tm + halo, :] = hp_ref[...]
        hc_ref[tm + halo:tm + 2 * halo, :] = hn_ref[...]

    @pl.when(s < n_hid)
    def _():
        t1_all = _dot(hc_ref[...], w1_ref[...])
        t1 = t1_all[0:tm]
        prev_row = t1_all[tm + halo - 1:tm + halo, :]
        next_row = t1_all[tm + halo:tm + halo + 1, :]
        prev_row = jnp.where(i % tiles_per_seq == 0, 0.0, prev_row)
        next_row = jnp.where(i % tiles_per_seq == tiles_per_seq - 1, 0.0, next_row)
        ridx = lax.broadcasted_iota(jnp.int32, (tm, 1), 0)
        up = jnp.where(ridx == 0, prev_row, pltpu.roll(t1, 1, 0))
        dn = jnp.where(ridx == tm - 1, next_row, pltpu.roll(t1, tm - 1, 0))
        u = up * cw_ref[0:1, :] + t1 * cw_ref[1:2, :] + dn * cw_ref[2:3, :] + cb_ref[...]
        a_ref[s] = ((u * _sigmoid(u)) * _dot(hc_ref[0:tm, :], w3_ref[...])).astype(BF16)

    @pl.when(s >= n_hid)
    def _():
        acc = _dot(a_ref[0], w2_ref[0:th, :])
        for k in range(1, n_hid):
            acc = acc + _dot(a_ref[k], w2_ref[k * th:(k + 1) * th, :])
        o_ref[...] = xm_ref[...] + g2_ref[N_MOD - 1:N_MOD, :] * acc


def _ffn_hidden_tile(hid):
    return min(FFN_HIDDEN_TILE, hid)


def _ffn_call(h2, xm, w1, w3, cw, cb, w2, g2, l):
    m, d = h2.shape
    n_hid, _, th = w1.shape
    hid = n_hid * th
    tm = FFN_TOKEN_TILE
    tn = w2.shape[-1]
    halo = 16
    per = tm // halo
    nblk = m // halo
    hcol = lambda i, s: (0, jnp.minimum(s, n_hid - 1))
    htile = lambda i, s: (jnp.minimum(s, n_hid - 1), 0, 0)
    ocol = lambda s: jnp.maximum(s - n_hid, 0)
    return pl.pallas_call(
        functools.partial(_ffn_kernel, tiles_per_seq=l // tm, halo=halo),
        grid=(m // tm, n_hid + d // tn),
        in_specs=[pl.BlockSpec((tm, d), lambda i, s: (i, 0)),
                  pl.BlockSpec((halo, d), lambda i, s: (jnp.maximum(i * per - 1, 0), 0)),
                  pl.BlockSpec((halo, d), lambda i, s: (jnp.minimum((i + 1) * per, nblk - 1), 0)),
                  pl.BlockSpec((None, d, th), htile),
                  pl.BlockSpec((None, d, th), htile),
                  pl.BlockSpec((3, th), hcol),
                  pl.BlockSpec((1, th), hcol),
                  pl.BlockSpec((None, hid, tn), lambda i, s: (ocol(s), 0, 0)),
                  pl.BlockSpec((tm, tn), lambda i, s: (i, ocol(s))),
                  pl.BlockSpec((None, N_MOD, tn), lambda i, s: (i // (l // tm), 0, ocol(s)))],
        out_specs=pl.BlockSpec((tm, tn), lambda i, s: (i, ocol(s))),
        out_shape=jax.ShapeDtypeStruct((m, d), F32),
        scratch_shapes=[pltpu.VMEM((n_hid, tm, th), BF16), pltpu.VMEM((tm + 2 * halo, d), BF16)],
        compiler_params=_params(("parallel", "arbitrary")),
        name="ffn",
    )(h2, h2, h2, w1, w3, cw, cb, w2, xm, g2)


def kernel(x, c, ctx, c_ctx, ada_w, ada_b, norm1_g, norm2_g, w_in, hgrn_lb_logits, hgrn_norm_g,
           na_q_norm_g, na_k_norm_g, na_rel_bias, w_branch_a, w_branch_b, w_out,
           ffn_w1, ffn_w3, ffn_conv_w, ffn_conv_b, ffn_w2):
    b, l, d = x.shape
    n_ctx = ctx.shape[1]
    assert ada_w.shape[0] == 1 and b + 1 <= 8 and n_ctx % TOK == 0 and l % (2 * TOK) == 0

    c_all = jnp.zeros((8, d), F32).at[:b].set(c).at[b].set(c_ctx)
    mod = _mod_call(c_all, ada_w[0], ada_b).reshape(8, N_MOD, d)

    h = _prenorm_call(x, ctx, norm1_g, mod)
    t = l + n_ctx
    h = h.reshape(b * t, d)
    n_mix = 8 * MIX_W
    th = _ffn_hidden_tile(ffn_w1.shape[-1])
    p, (w1, w3, w2) = _inproj_call(h, w_in[0], 0, n_mix, F32, "inproj",
                                   ((ffn_w1[0], th), (ffn_w3[0], th), (ffn_w2[0], FFN_OUT_TILE)))
    pg, (wa, wb, wo) = _inproj_call(h, w_in[0], n_mix, 2 * d, BF16, "inproj_gates",
                                    ((w_branch_a[0], d), (w_branch_b[0], d), (w_out[0], d)))
    wa, wb, wo = (wt.reshape(wt.shape[1:]) for wt in (wa, wb, wo))
    p3 = p.reshape(b, t, n_mix)
    pg3 = pg.reshape(b, t, 2 * d)

    y_a = _hgrn_call(p3, hgrn_lb_logits, hgrn_norm_g, n_ctx, l)
    y_b = _na_call(p3, na_rel_bias[0], na_q_norm_g, na_k_norm_g, n_ctx, l)

    x_mid, h2 = _merge_call(y_a, y_b, pg3, x, wa, wb, wo, mod, norm2_g)

    out = _ffn_call(h2.reshape(b * l, d), x_mid.reshape(b * l, d), w1, w3, ffn_conv_w[0], ffn_conv_b, w2,
                    mod, l)
    return out.reshape(b, l, d)
```
